```python
import jax, jax.numpy as jnp
from jax import lax
import numpy as np

D_MODEL = 1024
BATCH = 8
SEQ = 8192
DEPTH = 1

D_MIX = D_MODEL
CONV_CH = D_MIX // 2
CONV_WIDTH = 31
RET_HEADS = 4
RET_DIM = (D_MIX - CONV_CH) // RET_HEADS
RET_WIDTH = RET_HEADS * RET_DIM
RET_CHUNK = 128
ROPE_BASE = 10000.0
IN_COLS = 2 * CONV_CH + 4 * RET_WIDTH
N_EXPERTS = 32
TOP_K = 4
D_EXPERT = D_MODEL
SWIGLU_LIMIT = 7.0
SWIGLU_ALPHA = 1.702
RMS_EPS = 1e-6
LN_EPS = 1e-5
N_MOD = 6

kernel_name = "hybrid_conv_retention_moe_adaln"


def rmsnorm(x, g):
    xf = x.astype(jnp.float32)
    y = xf * lax.rsqrt(jnp.mean(xf * xf, axis=-1, keepdims=True) + RMS_EPS)
    return (y * g.astype(jnp.float32)).astype(x.dtype)


def layernorm(x, g, b, eps):
    xf = x.astype(jnp.float32)
    mu = jnp.mean(xf, axis=-1, keepdims=True)
    var = jnp.mean(jnp.square(xf - mu), axis=-1, keepdims=True)
    return (xf - mu) * lax.rsqrt(var + eps) * g.astype(jnp.float32) + b.astype(jnp.float32)


def rotary(t, positions):
    half = t.shape[-1] // 2
    inv_freq = ROPE_BASE ** (-jnp.arange(half, dtype=jnp.float32) / half)
    ang = positions.astype(jnp.float32)[..., None] * inv_freq
    cos = jnp.cos(ang)[:, :, None, :]
    sin = jnp.sin(ang)[:, :, None, :]
    tf = t.astype(jnp.float32)
    t1, t2 = tf[..., :half], tf[..., half:]
    return jnp.concatenate([t1 * cos - t2 * sin, t1 * sin + t2 * cos], axis=-1)


def chunkwise_retention(q, k, v):
    B, S, H, Dk = q.shape
    Dv = v.shape[-1]
    C = RET_CHUNK
    N = S // C
    log_gamma = jnp.log(1.0 - jnp.power(2.0, -5.0 - jnp.arange(H, dtype=jnp.float32)))
    idx = jnp.arange(C, dtype=jnp.float32)
    diff = idx[:, None] - idx[None, :]
    causal = diff >= 0
    decay_mask = jnp.where(causal[None], jnp.exp(log_gamma[:, None, None] * jnp.where(causal, diff, 0.0)[None]), 0.0)
    q_decay = jnp.exp(log_gamma[:, None] * (idx + 1.0))[..., None]
    k_decay = jnp.exp(log_gamma[:, None] * (C - 1.0 - idx))[..., None]
    chunk_decay = jnp.exp(log_gamma * C)[:, None, None]

    def to_chunks(t):
        return t.reshape(B, N, C, H, t.shape[-1]).transpose(1, 0, 3, 2, 4)

    def step(state, inp):
        qc, kc, vc = inp
        scores = jnp.einsum('bhnd,bhmd->bhnm', qc, kc) * decay_mask
        inner = jnp.einsum('bhnm,bhmv->bhnv', scores, vc)
        cross = jnp.einsum('bhnd,bhdv->bhnv', qc * q_decay, state)
        state = chunk_decay * state + jnp.einsum('bhmd,bhmv->bhdv', kc * k_decay, vc)
        return state, inner + cross

    state0 = jnp.zeros((B, H, Dk, Dv), jnp.float32)
    _, out = lax.scan(step, state0, (to_chunks(q), to_chunks(k), to_chunks(v)))
    return out.transpose(1, 0, 3, 2, 4).reshape(B, S, H, Dv)


def hybrid_mixer(h, positions, w_in, w_dw, b_dw, g_conv_ln, b_conv_ln, g_ret_norm, w_out):
    B, S, _ = h.shape
    proj = h @ w_in
    cuts = [CONV_CH, 2 * CONV_CH, 2 * CONV_CH + RET_WIDTH,
            2 * CONV_CH + 2 * RET_WIDTH, 2 * CONV_CH + 3 * RET_WIDTH]
    a, b, q, k, v, g = jnp.split(proj, cuts, axis=-1)

    u = a * jax.nn.sigmoid(b)
    u = lax.conv_general_dilated(u, w_dw[:, None, :].astype(u.dtype), window_strides=(1,),
                                 padding=((CONV_WIDTH - 1, 0),),
                                 dimension_numbers=('NWC', 'WIO', 'NWC'),
                                 feature_group_count=CONV_CH) + b_dw
    conv_out = jax.nn.silu(layernorm(u, g_conv_ln, b_conv_ln, LN_EPS)).astype(h.dtype)

    q = rotary(q.reshape(B, S, RET_HEADS, RET_DIM), positions)
    k = rotary(k.reshape(B, S, RET_HEADS, RET_DIM), positions) * (RET_DIM ** -0.5)
    v = v.reshape(B, S, RET_HEADS, RET_DIM).astype(jnp.float32)
    r = chunkwise_retention(q, k, v)
    mu = jnp.mean(r, axis=-1, keepdims=True)
    var = jnp.mean(jnp.square(r - mu), axis=-1, keepdims=True)
    r = ((r - mu) * lax.rsqrt(var + LN_EPS)).reshape(B, S, RET_WIDTH) * g_ret_norm.astype(jnp.float32)
    ret_out = (jax.nn.silu(g.astype(jnp.float32)) * r).astype(h.dtype)

    return jnp.concatenate([conv_out, ret_out], axis=-1) @ w_out


def moe_ffn(h, w_router, b_router, w_gu, b_gu, w_down, b_down):
    def row(hr):
        S = hr.shape[0]
        logits = (hr @ w_router).astype(jnp.float32) + b_router.astype(jnp.float32)
        top_vals, top_idx = lax.top_k(logits, TOP_K)
        weights = jax.nn.softmax(top_vals, axis=-1).astype(hr.dtype)
        flat_e = top_idx.reshape(-1)
        order = jnp.argsort(flat_e)
        e_sorted = flat_e[order]
        xs = hr[order // TOP_K]
        group_sizes = jnp.bincount(flat_e, length=N_EXPERTS).astype(jnp.int32)
        gu = lax.ragged_dot(xs, w_gu, group_sizes) + b_gu[e_sorted]
        gate = jnp.minimum(gu[:, :D_EXPERT], SWIGLU_LIMIT)
        lin = jnp.clip(gu[:, D_EXPERT:], -SWIGLU_LIMIT, SWIGLU_LIMIT)
        act = gate * jax.nn.sigmoid(SWIGLU_ALPHA * gate) * (lin + 1.0)
        out = lax.ragged_dot(act, w_down, group_sizes) + b_down[e_sorted]
        out = jnp.zeros_like(out).at[order].set(out).reshape(S, TOP_K, -1)
        return jnp.einsum('skd,sk->sd', out, weights)
    return lax.map(row, h)


def setup_inputs(seed: int = 0) -> dict:
    key = jax.random.key(seed)
    ks = jax.random.split(key, 24)
    f32 = jnp.float32
    nrm = lambda k, shape, scale: jax.random.normal(k, shape, f32) * scale
    gain = lambda k, shape: 1.0 + 0.01 * jax.random.normal(k, shape, f32)
    L = DEPTH
    return {
        "x": jax.random.normal(ks[0], (BATCH, SEQ, D_MODEL), f32),
        "c": jax.random.normal(ks[1], (BATCH, D_MODEL), f32),
        "positions": (jax.random.randint(ks[2], (BATCH, 1), 0, 4096, jnp.int32)
                      + jnp.arange(SEQ, dtype=jnp.int32)[None, :]),
        "w_ada": nrm(ks[3], (L, D_MODEL, N_MOD * D_MODEL), D_MODEL ** -0.5),
        "b_ada": nrm(ks[4], (L, N_MOD * D_MODEL), 0.02),
        "g_mix": gain(ks[5], (L, D_MODEL)),
        "w_in": nrm(ks[6], (L, D_MODEL, IN_COLS), D_MODEL ** -0.5),
        "w_dw": nrm(ks[7], (L, CONV_WIDTH, CONV_CH), CONV_WIDTH ** -0.5),
        "b_dw": nrm(ks[8], (L, CONV_CH), 0.02),
        "g_conv_ln": gain(ks[9], (L, CONV_CH)),
        "b_conv_ln": nrm(ks[10], (L, CONV_CH), 0.02),
        "g_ret_norm": gain(ks[11], (L, RET_WIDTH)),
        "w_out": nrm(ks[12], (L, D_MIX, D_MODEL), D_MIX ** -0.5),
        "g_ffn": gain(ks[13], (L, D_MODEL)),
        "w_router": nrm(ks[14], (L, D_MODEL, N_EXPERTS), D_MODEL ** -0.5),
        "b_router": nrm(ks[15], (L, N_EXPERTS), 0.01),
        "w_gu": nrm(ks[16], (L, N_EXPERTS, D_MODEL, 2 * D_EXPERT), D_MODEL ** -0.5),
        "b_gu": nrm(ks[17], (L, N_EXPERTS, 2 * D_EXPERT), 0.02),
        "w_down": nrm(ks[18], (L, N_EXPERTS, D_EXPERT, D_MODEL), D_EXPERT ** -0.5),
        "b_down": nrm(ks[19], (L, N_EXPERTS, D_MODEL), 0.02),
        "g_final": gain(ks[20], (D_MODEL,)),
    }


def reference(x, c, positions, w_ada, b_ada, g_mix, w_in, w_dw, b_dw, g_conv_ln, b_conv_ln,
              g_ret_norm, w_out, g_ffn, w_router, b_router, w_gu, b_gu, w_down, b_down, g_final):
    c_act = jax.nn.silu(c)
    for l in range(DEPTH):
        mod = c_act @ w_ada[l] + b_ada[l]
        sh_m, sc_m, gt_m, sh_f, sc_f, gt_f = [m[:, None, :] for m in jnp.split(mod, N_MOD, axis=-1)]
        h = rmsnorm(x, g_mix[l]) * (1.0 + sc_m) + sh_m
        x = x + gt_m * hybrid_mixer(h, positions, w_in[l], w_dw[l], b_dw[l], g_conv_ln[l],
                                    b_conv_ln[l], g_ret_norm[l], w_out[l])
        h = rmsnorm(x, g_ffn[l]) * (1.0 + sc_f) + sh_f
        x = x + gt_f * moe_ffn(h, w_router[l], b_router[l], w_gu[l], b_gu[l], w_down[l], b_down[l])
    return rmsnorm(x, g_final)
```

```python
import functools

import numpy as np
import jax
import jax.numpy as jnp
from jax import lax
from jax.experimental import pallas as pl
from jax.experimental.pallas import tpu as pltpu

F32 = jnp.float32
BF16 = jnp.bfloat16
U32 = jnp.uint32
I32 = jnp.int32

CONV_WIDTH = 31
CONV_HALO = 32
RET_HEADS = 4
RET_DIM = 128
RET_CHUNK = 128
ROPE_BASE = 10000.0
N_EXPERTS = 32
TOP_K = 4
SWIGLU_LIMIT = 7.0
SWIGLU_ALPHA = 1.702
RMS_EPS = 1e-6
LN_EPS = 1e-5
N_MOD = 6

VMEM_LIMIT_BYTES = 56 * 1024 * 1024


def _split_bf16(a):
    hi = a.astype(BF16)
    lo = (a - hi.astype(F32)).astype(BF16)
    return hi, lo


def _dot3(a, b_hi, b_lo):
    a_hi, a_lo = _split_bf16(a)
    d = functools.partial(jnp.dot, preferred_element_type=F32)
    return d(a_hi, b_hi) + (d(a_hi, b_lo) + d(a_lo, b_hi))


def _pack_bf16_pair(lo, hi):
    lo_bits = lax.bitcast_convert_type(lo.astype(BF16).astype(F32), U32)
    hi_bits = lax.bitcast_convert_type(hi.astype(BF16).astype(F32), U32)
    return (lo_bits >> 16) | (hi_bits & jnp.uint32(0xFFFF0000))


def _unpack_bf16_pair(p):
    lo = lax.bitcast_convert_type(p << 16, F32)
    hi = lax.bitcast_convert_type(p & jnp.uint32(0xFFFF0000), F32)
    return lo, hi


def _mod_kernel(c_ref, whi_ref, wlo_ref, b_ref, o_ref):
    c = c_ref[...]
    c_act = c * jax.nn.sigmoid(c)
    o_ref[...] = _dot3(c_act, whi_ref[...], wlo_ref[...]) + b_ref[...]


def _mod_call(c, w_ada, b_ada):
    B, D = c.shape
    n = w_ada.shape[1]
    bn = 1024
    w_hi, w_lo = _split_bf16(w_ada)
    return pl.pallas_call(
        _mod_kernel,
        grid=(n // bn,),
        in_specs=[
            pl.BlockSpec((B, D), lambda j: (0, 0)),
            pl.BlockSpec((D, bn), lambda j: (0, j)),
            pl.BlockSpec((D, bn), lambda j: (0, j)),
            pl.BlockSpec((1, bn), lambda j: (0, j)),
        ],
        out_specs=pl.BlockSpec((B, bn), lambda j: (0, j)),
        out_shape=jax.ShapeDtypeStruct((B, n), F32),
        name="adaln_mod",
    )(c, w_hi, w_lo, b_ada.reshape(1, n))


def _retention_tables():
    h = np.arange(RET_HEADS, dtype=np.float32)
    log_gamma = np.log(1.0 - np.power(2.0, -5.0 - h)).astype(np.float32)
    idx = np.arange(RET_CHUNK, dtype=np.float32)
    diff = idx[:, None] - idx[None, :]
    causal = diff >= 0
    mask = np.where(causal[None], np.exp(log_gamma[:, None, None] * np.where(causal, diff, 0.0)[None]), 0.0)
    q_decay = np.exp(log_gamma[:, None] * (idx + 1.0))[..., None]
    k_decay = np.exp(log_gamma[:, None] * (RET_CHUNK - 1.0 - idx))[..., None]
    chunk_decay = np.exp(log_gamma * RET_CHUNK)
    return (mask.astype(np.float32), q_decay.astype(np.float32), k_decay.astype(np.float32),
            [float(v) for v in chunk_decay.astype(np.float32)])


def _mixer_kernel(chunk_decay, ts, x_ref, pos_ref, mod_ref, gmix_ref, win_ref, wdw_ref, bdw_ref,
                  gcl_ref, bcl_ref, gret_ref, wout_ref, invf_ref, dmask_ref, qdec_ref, kdec_ref,
                  o_ref, proj_ref, uext_ref, state_ref, cat_ref):
    s = pl.program_id(1)
    conv_ch = wdw_ref.shape[1]
    ret_w = RET_HEADS * RET_DIM

    @pl.when(s == 0)
    def _():
        uext_ref[0:CONV_HALO, :] = jnp.zeros((CONV_HALO, conv_ch), F32)
        state_ref[...] = jnp.zeros_like(state_ref)

    x = x_ref[...]
    sh = mod_ref[0:1, :]
    sc = mod_ref[1:2, :]
    y = x * lax.rsqrt(jnp.mean(x * x, axis=-1, keepdims=True) + RMS_EPS) * gmix_ref[...]
    h = (y * (1.0 + sc) + sh).astype(BF16)
    proj_ref[...] = jnp.dot(h, win_ref[...], preferred_element_type=F32)

    a = proj_ref[:, 0:conv_ch]
    b = proj_ref[:, conv_ch:2 * conv_ch]
    uext_ref[CONV_HALO:CONV_HALO + ts, :] = a * jax.nn.sigmoid(b)
    rb = 128
    for r0 in range(0, ts, rb):
        acc = jnp.zeros((rb, conv_ch), F32) + bdw_ref[...]
        for j in range(CONV_WIDTH):
            off = r0 + CONV_HALO - (CONV_WIDTH - 1) + j
            acc = acc + wdw_ref[j:j + 1, :] * uext_ref[off:off + rb, :]
        mu = jnp.mean(acc, axis=-1, keepdims=True)
        d = acc - mu
        var = jnp.mean(d * d, axis=-1, keepdims=True)
        ln = d * lax.rsqrt(var + LN_EPS) * gcl_ref[...] + bcl_ref[...]
        cat_ref[r0:r0 + rb, 0:conv_ch] = (ln * jax.nn.sigmoid(ln)).astype(BF16)
    uext_ref[0:CONV_HALO, :] = uext_ref[ts:ts + CONV_HALO, :]

    ang = pos_ref[...].astype(F32) * invf_ref[...]
    cos2 = jnp.cos(ang)
    sin = jnp.sin(ang)
    lane = lax.broadcasted_iota(I32, ang.shape, 1)
    sin2 = jnp.where(lane < RET_DIM // 2, -sin, sin)
    q0 = 2 * conv_ch
    k0 = q0 + ret_w
    v0 = k0 + ret_w
    g0 = v0 + ret_w
    scale = RET_DIM ** -0.5
    for hd in range(RET_HEADS):
        c0 = hd * RET_DIM
        for n in range(ts // RET_CHUNK):
            r0 = n * RET_CHUNK
            rows = slice(r0, r0 + RET_CHUNK)
            cs = cos2[rows]
            sn = sin2[rows]
            q = proj_ref[rows, q0 + c0:q0 + c0 + RET_DIM]
            k = proj_ref[rows, k0 + c0:k0 + c0 + RET_DIM]
            v = proj_ref[rows, v0 + c0:v0 + c0 + RET_DIM].astype(BF16)
            g = proj_ref[rows, g0 + c0:g0 + c0 + RET_DIM]
            qr = q * cs + pltpu.roll(q, RET_DIM // 2, 1) * sn
            kr = (k * cs + pltpu.roll(k, RET_DIM // 2, 1) * sn) * scale
            st = state_ref[hd]
            scores = lax.dot_general(qr.astype(BF16), kr.astype(BF16), (((1,), (1,)), ((), ())),
                                     preferred_element_type=F32) * dmask_ref[hd]
            inner = jnp.dot(scores.astype(BF16), v, preferred_element_type=F32)
            cross = jnp.dot((qr * qdec_ref[hd]).astype(BF16), st.astype(BF16), preferred_element_type=F32)
            kv = lax.dot_general((kr * kdec_ref[hd]).astype(BF16), v, (((0,), (0,)), ((), ())),
                                 preferred_element_type=F32)
            state_ref[hd] = chunk_decay[hd] * st + kv
            r = inner + cross
            mu = jnp.mean(r, axis=-1, keepdims=True)
            d = r - mu
            var = jnp.mean(d * d, axis=-1, keepdims=True)
            rn = d * lax.rsqrt(var + LN_EPS) * gret_ref[:, c0:c0 + RET_DIM]
            cat_ref[rows, conv_ch + c0:conv_ch + c0 + RET_DIM] = (g * jax.nn.sigmoid(g) * rn).astype(BF16)

    out = jnp.dot(cat_ref[...], wout_ref[...], preferred_element_type=F32)
    o_ref[...] = x + mod_ref[2:3, :] * out


def _mixer_call(x, positions, mod, g_mix, w_in, w_dw, b_dw, g_conv_ln, b_conv_ln, g_ret_norm, w_out):
    B, S, D = x.shape
    in_cols = w_in.shape[1]
    conv_ch = w_dw.shape[1]
    ts = min(512, S)
    mask, q_decay, k_decay, chunk_decay = _retention_tables()
    half = RET_DIM // 2
    inv_freq = (ROPE_BASE ** (-np.arange(half, dtype=np.float32) / half)).astype(np.float32)
    inv_freq2 = np.concatenate([inv_freq, inv_freq])[None, :]
    w_dw_p = jnp.zeros((CONV_HALO, conv_ch), F32).at[:CONV_WIDTH].set(w_dw)
    full = lambda shape: pl.BlockSpec(shape, lambda b, s: (0,) * len(shape))
    return pl.pallas_call(
        functools.partial(_mixer_kernel, chunk_decay, ts),
        grid=(B, S // ts),
        in_specs=[
            pl.BlockSpec((None, ts, D), lambda b, s: (b, s, 0)),
            pl.BlockSpec((None, ts, 1), lambda b, s: (b, s, 0)),
            pl.BlockSpec((None, N_MOD, D), lambda b, s: (b, 0, 0)),
            full((1, D)),
            full((D, in_cols)),
            full((CONV_HALO, conv_ch)),
            full((1, conv_ch)),
            full((1, conv_ch)),
            full((1, conv_ch)),
            full((1, RET_HEADS * RET_DIM)),
            full((conv_ch + RET_HEADS * RET_DIM, D)),
            full((1, RET_DIM)),
            full((RET_HEADS, RET_CHUNK, RET_CHUNK)),
            full((RET_HEADS, RET_CHUNK, 1)),
            full((RET_HEADS, RET_CHUNK, 1)),
        ],
        out_specs=pl.BlockSpec((None, ts, D), lambda b, s: (b, s, 0)),
        out_shape=jax.ShapeDtypeStruct((B, S, D), F32),
        scratch_shapes=[
            pltpu.VMEM((ts, in_cols), F32),
            pltpu.VMEM((CONV_HALO + ts, conv_ch), F32),
            pltpu.VMEM((RET_HEADS, RET_DIM, RET_DIM), F32),
            pltpu.VMEM((ts, conv_ch + RET_HEADS * RET_DIM), BF16),
        ],
        compiler_params=pltpu.CompilerParams(
            dimension_semantics=("arbitrary", "arbitrary"), vmem_limit_bytes=VMEM_LIMIT_BYTES),
        name="hybrid_mixer",
    )(x, positions.reshape(B, S, 1), mod, g_mix.reshape(1, D), w_in.astype(BF16), w_dw_p,
      b_dw.reshape(1, -1), g_conv_ln.reshape(1, -1), b_conv_ln.reshape(1, -1), g_ret_norm.reshape(1, -1),
      w_out.astype(BF16), jnp.asarray(inv_freq2), jnp.asarray(mask), jnp.asarray(q_decay),
      jnp.asarray(k_decay))


def _route_kernel(x_ref, mod_ref, gffn_ref, wrhi_ref, wrlo_ref, br_ref, ltri_ref,
                  h2p_ref, idx_ref, wts_ref, rank_ref, cnt_ref, carry_ref):
    i = pl.program_id(0)

    @pl.when(i == 0)
    def _():
        carry_ref[...] = jnp.zeros_like(carry_ref)

    x = x_ref[...]
    half = x.shape[1] // 2
    y = x * lax.rsqrt(jnp.mean(x * x, axis=-1, keepdims=True) + RMS_EPS) * gffn_ref[...]
    h2 = y * (1.0 + mod_ref[4:5, :]) + mod_ref[3:4, :]
    h2p_ref[...] = _pack_bf16_pair(h2[:, :half], h2[:, half:])

    logits = _dot3(h2, wrhi_ref[...], wrlo_ref[...]) + br_ref[...]
    eid = lax.broadcasted_iota(I32, logits.shape, 1)
    vals, sels, idxs = [], [], []
    l = logits
    for _ in range(TOP_K):
        m = jnp.max(l, axis=-1, keepdims=True)
        ik = jnp.min(jnp.where(l == m, eid, N_EXPERTS), axis=-1, keepdims=True)
        sel = eid == ik
        vals.append(m)
        sels.append(sel)
        idxs.append(ik)
        l = jnp.where(sel, -jnp.inf, l)
    exps = [jnp.exp(v - vals[0]) for v in vals]
    denom = exps[0] + exps[1] + exps[2] + exps[3]
    member = jnp.zeros(logits.shape, F32)
    for sel in sels:
        member = member + sel.astype(F32)
    before = jnp.dot(ltri_ref[...], member.astype(BF16), preferred_element_type=F32) + carry_ref[...]
    ranks = [jnp.sum(jnp.where(sel, before, 0.0), axis=-1, keepdims=True) for sel in sels]
    carry_ref[...] = carry_ref[...] + jnp.sum(member, axis=0, keepdims=True)
    idx_ref[...] = jnp.concatenate(idxs, axis=1)
    wts_ref[...] = jnp.concatenate([e / denom for e in exps], axis=1)
    rank_ref[...] = jnp.concatenate(ranks, axis=1).astype(I32)
    cnt_ref[...] = carry_ref[...].astype(I32)


def _route_call(x1, mod, g_ffn, w_router, b_router, tokens_per_batch):
    T, D = x1.shape
    tr = min(512, tokens_per_batch)
    per_b = tokens_per_batch // tr
    wr_hi, wr_lo = _split_bf16(w_router)
    ltri = jnp.asarray(np.tril(np.ones((tr, tr), np.float32), -1), BF16)
    full = lambda shape: pl.BlockSpec(shape, lambda i: (0,) * len(shape))
    return pl.pallas_call(
        _route_kernel,
        grid=(T // tr,),
        in_specs=[
            pl.BlockSpec((tr, D), lambda i: (i, 0)),
            pl.BlockSpec((None, N_MOD, D), lambda i: (i // per_b, 0, 0)),
            full((1, D)),
            full((D, N_EXPERTS)),
            full((D, N_EXPERTS)),
            full((1, N_EXPERTS)),
            full((tr, tr)),
        ],
        out_specs=[
            pl.BlockSpec((tr, D // 2), lambda i: (i, 0)),
            pl.BlockSpec((tr, TOP_K), lambda i: (i, 0)),
            pl.BlockSpec((tr, TOP_K), lambda i: (i, 0)),
            pl.BlockSpec((tr, TOP_K), lambda i: (i, 0)),
            full((1, N_EXPERTS)),
        ],
        out_shape=[
            jax.ShapeDtypeStruct((T, D // 2), U32),
            jax.ShapeDtypeStruct((T, TOP_K), I32),
            jax.ShapeDtypeStruct((T, TOP_K), F32),
            jax.ShapeDtypeStruct((T, TOP_K), I32),
            jax.ShapeDtypeStruct((1, N_EXPERTS), I32),
        ],
        scratch_shapes=[pltpu.VMEM((1, N_EXPERTS), F32)],
        compiler_params=pltpu.CompilerParams(
            dimension_semantics=("arbitrary",), vmem_limit_bytes=VMEM_LIMIT_BYTES),
        name="moe_route",
    )(x1, mod, g_ffn.reshape(1, D), wr_hi, wr_lo, b_router.reshape(1, N_EXPERTS), ltri)


def _row_move_kernel(n, src_of, dst_of, idx_ref, src_ref, dst_ref, sem):
    base = pl.program_id(0) * n

    def copy(j):
        return pltpu.make_async_copy(src_ref.at[pl.ds(src_of(base + j, idx_ref[j]), 1)],
                                     dst_ref.at[pl.ds(dst_of(base + j, idx_ref[j]), 1)], sem)

    def start(j, c):
        copy(j).start()
        return c

    def wait(j, c):
        copy(j).wait()
        return c

    lax.fori_loop(0, n, start, 0)
    lax.fori_loop(0, n, wait, 0)


def _row_move_call(idx_flat, src, n_dst_rows, src_of, dst_of, name):
    n_idx = idx_flat.shape[0]
    n = min(4096, n_idx)
    return pl.pallas_call(
        functools.partial(_row_move_kernel, n, src_of, dst_of),
        grid=(n_idx // n,),
        in_specs=[
            pl.BlockSpec((n,), lambda i: (i,), memory_space=pltpu.SMEM),
            pl.BlockSpec(memory_space=pl.ANY),
        ],
        out_specs=pl.BlockSpec(memory_space=pl.ANY),
        out_shape=jax.ShapeDtypeStruct((n_dst_rows, src.shape[1]), src.dtype),
        scratch_shapes=[pltpu.SemaphoreType.DMA(())],
        compiler_params=pltpu.CompilerParams(dimension_semantics=("arbitrary",)),
        name=name,
    )(idx_flat, src)


def _ffn_kernel(te_ref, tv_ref, xs_ref, wgu_ref, bgu_ref, wd_ref, bd_ref, y_ref):
    i = pl.program_id(0)
    valid = tv_ref[i]
    tm, half = xs_ref.shape
    d_exp = wd_ref.shape[0]

    @pl.when(valid > 0)
    def _():
        keep = lax.broadcasted_iota(I32, (tm, half), 0) < valid
        lo, hi = _unpack_bf16_pair(xs_ref[...])
        x_lo = jnp.where(keep, lo, 0.0).astype(BF16)
        x_hi = jnp.where(keep, hi, 0.0).astype(BF16)
        d = functools.partial(jnp.dot, preferred_element_type=F32)
        nb = 256
        acc = jnp.zeros((tm, wd_ref.shape[1]), F32) + bd_ref[...]
        for c in range(0, d_exp, nb):
            gate = (d(x_lo, wgu_ref[0:half, c:c + nb]) + d(x_hi, wgu_ref[half:, c:c + nb])
                    + bgu_ref[:, c:c + nb])
            lin = (d(x_lo, wgu_ref[0:half, d_exp + c:d_exp + c + nb])
                   + d(x_hi, wgu_ref[half:, d_exp + c:d_exp + c + nb]) + bgu_ref[:, d_exp + c:d_exp + c + nb])
            gate = jnp.minimum(gate, SWIGLU_LIMIT)
            lin = jnp.clip(lin, -SWIGLU_LIMIT, SWIGLU_LIMIT)
            act = gate * jax.nn.sigmoid(SWIGLU_ALPHA * gate) * (lin + 1.0)
            acc = acc + d(act.astype(BF16), wd_ref[c:c + nb, :])
        y_ref[...] = _pack_bf16_pair(acc[:, :half], acc[:, half:])


def _ffn_call(tile_expert, tile_valid, xs, w_gu, b_gu, w_down, b_down, tm):
    R, half = xs.shape
    E, D, two_f = w_gu.shape
    d_exp = w_down.shape[1]
    grid_spec = pltpu.PrefetchScalarGridSpec(
        num_scalar_prefetch=2,
        grid=(R // tm,),
        in_specs=[
            pl.BlockSpec((tm, half), lambda i, te, tv: (i, 0)),
            pl.BlockSpec((None, D, two_f), lambda i, te, tv: (te[i], 0, 0)),
            pl.BlockSpec((None, 1, two_f), lambda i, te, tv: (te[i], 0, 0)),
            pl.BlockSpec((None, d_exp, D), lambda i, te, tv: (te[i], 0, 0)),
            pl.BlockSpec((None, 1, D), lambda i, te, tv: (te[i], 0, 0)),
        ],
        out_specs=pl.BlockSpec((tm, half), lambda i, te, tv: (i, 0)),
    )
    return pl.pallas_call(
        _ffn_kernel,
        grid_spec=grid_spec,
        out_shape=jax.ShapeDtypeStruct((R, half), U32),
        compiler_params=pltpu.CompilerParams(
            dimension_semantics=("arbitrary",), vmem_limit_bytes=VMEM_LIMIT_BYTES),
        name="moe_ffn",
    )(tile_expert, tile_valid, xs, w_gu.astype(BF16), b_gu.reshape(E, 1, two_f),
      w_down.astype(BF16), b_down.reshape(E, 1, D))


def _final_kernel(x_ref, yp_ref, wts_ref, mod_ref, gfin_ref, o_ref):
    x = x_ref[...]
    half = x.shape[1] // 2
    w = wts_ref[...]
    lo = jnp.zeros((x.shape[0], half), F32)
    hi = jnp.zeros((x.shape[0], half), F32)
    for k in range(TOP_K):
        l, h = _unpack_bf16_pair(yp_ref[k])
        lo = lo + w[:, k:k + 1] * l
        hi = hi + w[:, k:k + 1] * h
    gate = mod_ref[5:6, :]
    x_lo = x[:, :half] + gate[:, :half] * lo
    x_hi = x[:, half:] + gate[:, half:] * hi
    ms = (jnp.sum(x_lo * x_lo, axis=-1, keepdims=True) + jnp.sum(x_hi * x_hi, axis=-1, keepdims=True)) / x.shape[1]
    inv = lax.rsqrt(ms + RMS_EPS)
    o_ref[:, :half] = x_lo * inv * gfin_ref[:, :half]
    o_ref[:, half:] = x_hi * inv * gfin_ref[:, half:]


def _final_call(x1, yp, wts, mod, g_final, tokens_per_batch):
    T, D = x1.shape
    tq = min(512, tokens_per_batch)
    per_b = tokens_per_batch // tq
    return pl.pallas_call(
        _final_kernel,
        grid=(T // tq,),
        in_specs=[
            pl.BlockSpec((tq, D), lambda i: (i, 0)),
            pl.BlockSpec((TOP_K, tq, D // 2), lambda i: (0, i, 0)),
            pl.BlockSpec((tq, TOP_K), lambda i: (i, 0)),
            pl.BlockSpec((None, N_MOD, D), lambda i: (i // per_b, 0, 0)),
            pl.BlockSpec((1, D), lambda i: (0, 0)),
        ],
        out_specs=pl.BlockSpec((tq, D), lambda i: (i, 0)),
        out_shape=jax.ShapeDtypeStruct((T, D), F32),
        compiler_params=pltpu.CompilerParams(
            dimension_semantics=("arbitrary",), vmem_limit_bytes=VMEM_LIMIT_BYTES),
        name="moe_combine_final",
    )(x1, yp, wts, mod, g_final.reshape(1, D))


def _group_layout(counts, n_tiles, tm):
    padded = ((counts + tm - 1) // tm) * tm
    ends = jnp.cumsum(padded)
    starts = ends - padded
    tile_row = jnp.arange(n_tiles, dtype=I32) * tm
    te = jnp.minimum(jnp.searchsorted(ends, tile_row, side="right"), N_EXPERTS - 1).astype(I32)
    tv = jnp.clip(counts[te] - (tile_row - starts[te]), 0, tm).astype(I32)
    return starts, te, tv


def kernel(x, c, positions, w_ada, b_ada, g_mix, w_in, w_dw, b_dw, g_conv_ln, b_conv_ln, g_ret_norm,
           w_out, g_ffn, w_router, b_router, w_gu, b_gu, w_down, b_down, g_final):
    B, S, D = x.shape
    T = B * S
    assert w_ada.shape[0] == 1, "single-layer block: the final norm directly follows layer 0"
    xt = x
    for l in range(1):
        mod = _mod_call(c, w_ada[l], b_ada[l]).reshape(B, N_MOD, D)
        x1 = _mixer_call(xt, positions, mod, g_mix[l], w_in[l], w_dw[l], b_dw[l], g_conv_ln[l],
                         b_conv_ln[l], g_ret_norm[l], w_out[l]).reshape(T, D)
        h2p, idx, wts, rank, counts = _route_call(x1, mod, g_ffn[l], w_router[l], b_router[l], S)

        tm = 512 if T * TOP_K >= 512 * N_EXPERTS * 4 else 128
        n_tiles = (T * TOP_K) // tm + N_EXPERTS
        starts, te, tv = _group_layout(counts[0], n_tiles, tm)
        pos = (starts[idx] + rank).astype(I32).reshape(-1)
        xs = _row_move_call(pos, h2p, n_tiles * tm,
                            lambda p, r: p // TOP_K, lambda p, r: r, "moe_scatter_rows")
        ys = _ffn_call(te, tv, xs, w_gu[l], b_gu[l], w_down[l], b_down[l], tm)
        yp = _row_move_call(pos, ys, TOP_K * T,
                            lambda p, r: r, lambda p, r: (p % TOP_K) * T + p // TOP_K, "moe_unsort_rows")
        xt = _final_call(x1, yp.reshape(TOP_K, T, D // 2), wts, mod, g_final, S)
    return xt.reshape(B, S, D)
```

```python
import functools

import numpy as np
import jax
import jax.numpy as jnp
from jax import lax
from jax.experimental import pallas as pl
from jax.experimental.pallas import tpu as pltpu
from jax.experimental.pallas import tpu_sc as plsc

F32 = jnp.float32
BF16 = jnp.bfloat16
U32 = jnp.uint32
I32 = jnp.int32

CONV_WIDTH = 31
CONV_HALO = 32
RET_HEADS = 4
RET_DIM = 128
RET_CHUNK = 128
ROPE_BASE = 10000.0
N_EXPERTS = 32
TOP_K = 4
SWIGLU_LIMIT = 7.0
SWIGLU_ALPHA = 1.702
RMS_EPS = 1e-6
LN_EPS = 1e-5
N_MOD = 6

VMEM_LIMIT_BYTES = 56 * 1024 * 1024


def _split_bf16(a):
    hi = a.astype(BF16)
    lo = (a - hi.astype(F32)).astype(BF16)
    return hi, lo


def _dot3(a, b_hi, b_lo):
    a_hi, a_lo = _split_bf16(a)
    d = functools.partial(jnp.dot, preferred_element_type=F32)
    return d(a_hi, b_hi) + (d(a_hi, b_lo) + d(a_lo, b_hi))


def _pack_bf16_pair(lo, hi):
    lo_bits = lax.bitcast_convert_type(lo.astype(BF16).astype(F32), U32)
    hi_bits = lax.bitcast_convert_type(hi.astype(BF16).astype(F32), U32)
    return (lo_bits >> 16) | (hi_bits & jnp.uint32(0xFFFF0000))


def _unpack_bf16_pair(p):
    lo = lax.bitcast_convert_type(p << 16, F32)
    hi = lax.bitcast_convert_type(p & jnp.uint32(0xFFFF0000), F32)
    return lo, hi


def _mod_kernel(c_ref, whi_ref, wlo_ref, b_ref, o_ref):
    c = c_ref[...]
    c_act = c * jax.nn.sigmoid(c)
    o_ref[...] = _dot3(c_act, whi_ref[...], wlo_ref[...]) + b_ref[...]


def _mod_call(c, w_ada, b_ada):
    B, D = c.shape
    n = w_ada.shape[1]
    bn = 1024
    w_hi, w_lo = _split_bf16(w_ada)
    return pl.pallas_call(
        _mod_kernel,
        grid=(n // bn,),
        in_specs=[
            pl.BlockSpec((B, D), lambda j: (0, 0)),
            pl.BlockSpec((D, bn), lambda j: (0, j)),
            pl.BlockSpec((D, bn), lambda j: (0, j)),
            pl.BlockSpec((1, bn), lambda j: (0, j)),
        ],
        out_specs=pl.BlockSpec((B, bn), lambda j: (0, j)),
        out_shape=jax.ShapeDtypeStruct((B, n), F32),
        name="adaln_mod",
    )(c, w_hi, w_lo, b_ada.reshape(1, n))


def _retention_tables():
    h = np.arange(RET_HEADS, dtype=np.float32)
    log_gamma = np.log(1.0 - np.power(2.0, -5.0 - h)).astype(np.float32)
    idx = np.arange(RET_CHUNK, dtype=np.float32)
    diff = idx[:, None] - idx[None, :]
    causal = diff >= 0
    mask = np.where(causal[None], np.exp(log_gamma[:, None, None] * np.where(causal, diff, 0.0)[None]), 0.0)
    q_decay = np.exp(log_gamma[:, None] * (idx + 1.0))[..., None]
    k_decay = np.exp(log_gamma[:, None] * (RET_CHUNK - 1.0 - idx))[..., None]
    chunk_decay = np.exp(log_gamma * RET_CHUNK)
    return (mask.astype(np.float32), q_decay.astype(np.float32), k_decay.astype(np.float32),
            [float(v) for v in chunk_decay.astype(np.float32)])


def _mixer_kernel(chunk_decay, ts, x_ref, pos_ref, mod_ref, gmix_ref, win_ref, wdw_ref, bdw_ref,
                  gcl_ref, bcl_ref, gret_ref, wout_ref, invf_ref, dmask_ref, qdec_ref, kdec_ref,
                  o_ref, proj_ref, uext_ref, state_ref, cat_ref):
    s = pl.program_id(1)
    conv_ch = wdw_ref.shape[1]
    ret_w = RET_HEADS * RET_DIM

    @pl.when(s == 0)
    def _():
        uext_ref[0:CONV_HALO, :] = jnp.zeros((CONV_HALO, conv_ch), F32)
        state_ref[...] = jnp.zeros_like(state_ref)

    x = x_ref[...]
    sh = mod_ref[0:1, :]
    sc = mod_ref[1:2, :]
    y = x * lax.rsqrt(jnp.mean(x * x, axis=-1, keepdims=True) + RMS_EPS) * gmix_ref[...]
    h = (y * (1.0 + sc) + sh).astype(BF16)
    proj_ref[...] = jnp.dot(h, win_ref[...], preferred_element_type=F32)

    a = proj_ref[:, 0:conv_ch]
    b = proj_ref[:, conv_ch:2 * conv_ch]
    uext_ref[CONV_HALO:CONV_HALO + ts, :] = a * jax.nn.sigmoid(b)
    rb = 128
    for r0 in range(0, ts, rb):
        acc = jnp.zeros((rb, conv_ch), F32) + bdw_ref[...]
        for j in range(CONV_WIDTH):
            off = r0 + CONV_HALO - (CONV_WIDTH - 1) + j
            acc = acc + wdw_ref[j:j + 1, :] * uext_ref[off:off + rb, :]
        mu = jnp.mean(acc, axis=-1, keepdims=True)
        d = acc - mu
        var = jnp.mean(d * d, axis=-1, keepdims=True)
        ln = d * lax.rsqrt(var + LN_EPS) * gcl_ref[...] + bcl_ref[...]
        cat_ref[r0:r0 + rb, 0:conv_ch] = (ln * jax.nn.sigmoid(ln)).astype(BF16)
    uext_ref[0:CONV_HALO, :] = uext_ref[ts:ts + CONV_HALO, :]

    ang = pos_ref[...].astype(F32) * invf_ref[...]
    cos2 = jnp.cos(ang)
    sin = jnp.sin(ang)
    lane = lax.broadcasted_iota(I32, ang.shape, 1)
    sin2 = jnp.where(lane < RET_DIM // 2, -sin, sin)
    q0 = 2 * conv_ch
    k0 = q0 + ret_w
    v0 = k0 + ret_w
    g0 = v0 + ret_w
    scale = RET_DIM ** -0.5
    for hd in range(RET_HEADS):
        c0 = hd * RET_DIM
        for n in range(ts // RET_CHUNK):
            r0 = n * RET_CHUNK
            rows = slice(r0, r0 + RET_CHUNK)
            cs = cos2[rows]
            sn = sin2[rows]
            q = proj_ref[rows, q0 + c0:q0 + c0 + RET_DIM]
            k = proj_ref[rows, k0 + c0:k0 + c0 + RET_DIM]
            v = proj_ref[rows, v0 + c0:v0 + c0 + RET_DIM].astype(BF16)
            g = proj_ref[rows, g0 + c0:g0 + c0 + RET_DIM]
            qr = q * cs + pltpu.roll(q, RET_DIM // 2, 1) * sn
            kr = (k * cs + pltpu.roll(k, RET_DIM // 2, 1) * sn) * scale
            st = state_ref[hd]
            scores = lax.dot_general(qr.astype(BF16), kr.astype(BF16), (((1,), (1,)), ((), ())),
                                     preferred_element_type=F32) * dmask_ref[hd]
            inner = jnp.dot(scores.astype(BF16), v, preferred_element_type=F32)
            cross = jnp.dot((qr * qdec_ref[hd]).astype(BF16), st.astype(BF16), preferred_element_type=F32)
            kv = lax.dot_general((kr * kdec_ref[hd]).astype(BF16), v, (((0,), (0,)), ((), ())),
                                 preferred_element_type=F32)
            state_ref[hd] = chunk_decay[hd] * st + kv
            r = inner + cross
            mu = jnp.mean(r, axis=-1, keepdims=True)
            d = r - mu
            var = jnp.mean(d * d, axis=-1, keepdims=True)
            rn = d * lax.rsqrt(var + LN_EPS) * gret_ref[:, c0:c0 + RET_DIM]
            cat_ref[rows, conv_ch + c0:conv_ch + c0 + RET_DIM] = (g * jax.nn.sigmoid(g) * rn).astype(BF16)

    out = jnp.dot(cat_ref[...], wout_ref[...], preferred_element_type=F32)
    o_ref[...] = x + mod_ref[2:3, :] * out


def _mixer_call(x, positions, mod, g_mix, w_in, w_dw, b_dw, g_conv_ln, b_conv_ln, g_ret_norm, w_out):
    B, S, D = x.shape
    in_cols = w_in.shape[1]
    conv_ch = w_dw.shape[1]
    ts = min(512, S)
    mask, q_decay, k_decay, chunk_decay = _retention_tables()
    half = RET_DIM // 2
    inv_freq = (ROPE_BASE ** (-np.arange(half, dtype=np.float32) / half)).astype(np.float32)
    inv_freq2 = np.concatenate([inv_freq, inv_freq])[None, :]
    w_dw_p = jnp.zeros((CONV_HALO, conv_ch), F32).at[:CONV_WIDTH].set(w_dw)
    full = lambda shape: pl.BlockSpec(shape, lambda b, s: (0,) * len(shape))
    return pl.pallas_call(
        functools.partial(_mixer_kernel, chunk_decay, ts),
        grid=(B, S // ts),
        in_specs=[
            pl.BlockSpec((None, ts, D), lambda b, s: (b, s, 0)),
            pl.BlockSpec((None, ts, 1), lambda b, s: (b, s, 0)),
            pl.BlockSpec((None, N_MOD, D), lambda b, s: (b, 0, 0)),
            full((1, D)),
            full((D, in_cols)),
            full((CONV_HALO, conv_ch)),
            full((1, conv_ch)),
            full((1, conv_ch)),
            full((1, conv_ch)),
            full((1, RET_HEADS * RET_DIM)),
            full((conv_ch + RET_HEADS * RET_DIM, D)),
            full((1, RET_DIM)),
            full((RET_HEADS, RET_CHUNK, RET_CHUNK)),
            full((RET_HEADS, RET_CHUNK, 1)),
            full((RET_HEADS, RET_CHUNK, 1)),
        ],
        out_specs=pl.BlockSpec((None, ts, D), lambda b, s: (b, s, 0)),
        out_shape=jax.ShapeDtypeStruct((B, S, D), F32),
        scratch_shapes=[
            pltpu.VMEM((ts, in_cols), F32),
            pltpu.VMEM((CONV_HALO + ts, conv_ch), F32),
            pltpu.VMEM((RET_HEADS, RET_DIM, RET_DIM), F32),
            pltpu.VMEM((ts, conv_ch + RET_HEADS * RET_DIM), BF16),
        ],
        compiler_params=pltpu.CompilerParams(
            dimension_semantics=("arbitrary", "arbitrary"), vmem_limit_bytes=VMEM_LIMIT_BYTES),
        name="hybrid_mixer",
    )(x, positions.reshape(B, S, 1), mod, g_mix.reshape(1, D), w_in.astype(BF16), w_dw_p,
      b_dw.reshape(1, -1), g_conv_ln.reshape(1, -1), b_conv_ln.reshape(1, -1), g_ret_norm.reshape(1, -1),
      w_out.astype(BF16), jnp.asarray(inv_freq2), jnp.asarray(mask), jnp.asarray(q_decay),
      jnp.asarray(k_decay))


def _route_kernel(x_ref, mod_ref, gffn_ref, wrhi_ref, wrlo_ref, br_ref, ltri_ref,
                  h2p_ref, idx_ref, wts_ref, rank_ref, cnt_ref, carry_ref):
    i = pl.program_id(0)

    @pl.when(i == 0)
    def _():
        carry_ref[...] = jnp.zeros_like(carry_ref)

    x = x_ref[...]
    half = x.shape[1] // 2
    y = x * lax.rsqrt(jnp.mean(x * x, axis=-1, keepdims=True) + RMS_EPS) * gffn_ref[...]
    h2 = y * (1.0 + mod_ref[4:5, :]) + mod_ref[3:4, :]
    h2p_ref[...] = _pack_bf16_pair(h2[:, :half], h2[:, half:])

    logits = _dot3(h2, wrhi_ref[...], wrlo_ref[...]) + br_ref[...]
    eid = lax.broadcasted_iota(I32, logits.shape, 1)
    vals, sels, idxs = [], [], []
    l = logits
    for _ in range(TOP_K):
        m = jnp.max(l, axis=-1, keepdims=True)
        ik = jnp.min(jnp.where(l == m, eid, N_EXPERTS), axis=-1, keepdims=True)
        sel = eid == ik
        vals.append(m)
        sels.append(sel)
        idxs.append(ik)
        l = jnp.where(sel, -jnp.inf, l)
    exps = [jnp.exp(v - vals[0]) for v in vals]
    denom = exps[0] + exps[1] + exps[2] + exps[3]
    member = jnp.zeros(logits.shape, F32)
    for sel in sels:
        member = member + sel.astype(F32)
    before = jnp.dot(ltri_ref[...], member.astype(BF16), preferred_element_type=F32) + carry_ref[...]
    ranks = [jnp.sum(jnp.where(sel, before, 0.0), axis=-1, keepdims=True) for sel in sels]
    carry_ref[...] = carry_ref[...] + jnp.sum(member, axis=0, keepdims=True)
    idx_ref[...] = jnp.concatenate(idxs, axis=1)
    wts_ref[...] = jnp.concatenate([e / denom for e in exps], axis=1)
    rank_ref[...] = jnp.concatenate(ranks, axis=1).astype(I32)
    cnt_ref[...] = carry_ref[...].astype(I32)


def _route_call(x1, mod, g_ffn, w_router, b_router, tokens_per_batch):
    T, D = x1.shape
    tr = min(512, tokens_per_batch)
    per_b = tokens_per_batch // tr
    wr_hi, wr_lo = _split_bf16(w_router)
    ltri = jnp.asarray(np.tril(np.ones((tr, tr), np.float32), -1), BF16)
    full = lambda shape: pl.BlockSpec(shape, lambda i: (0,) * len(shape))
    return pl.pallas_call(
        _route_kernel,
        grid=(T // tr,),
        in_specs=[
            pl.BlockSpec((tr, D), lambda i: (i, 0)),
            pl.BlockSpec((None, N_MOD, D), lambda i: (i // per_b, 0, 0)),
            full((1, D)),
            full((D, N_EXPERTS)),
            full((D, N_EXPERTS)),
            full((1, N_EXPERTS)),
            full((tr, tr)),
        ],
        out_specs=[
            pl.BlockSpec((tr, D // 2), lambda i: (i, 0)),
            pl.BlockSpec((tr, TOP_K), lambda i: (i, 0)),
            pl.BlockSpec((tr, TOP_K), lambda i: (i, 0)),
            pl.BlockSpec((tr, TOP_K), lambda i: (i, 0)),
            full((1, N_EXPERTS)),
        ],
        out_shape=[
            jax.ShapeDtypeStruct((T, D // 2), U32),
            jax.ShapeDtypeStruct((T, TOP_K), I32),
            jax.ShapeDtypeStruct((T, TOP_K), F32),
            jax.ShapeDtypeStruct((T, TOP_K), I32),
            jax.ShapeDtypeStruct((1, N_EXPERTS), I32),
        ],
        scratch_shapes=[pltpu.VMEM((1, N_EXPERTS), F32)],
        compiler_params=pltpu.CompilerParams(
            dimension_semantics=("arbitrary",), vmem_limit_bytes=VMEM_LIMIT_BYTES),
        name="moe_route",
    )(x1, mod, g_ffn.reshape(1, D), wr_hi, wr_lo, b_router.reshape(1, N_EXPERTS), ltri)


SC_ROWS = 128


V7X_SC_CORES = 2
V7X_SC_SUBCORES = 16


def _sc_workers():
    return V7X_SC_CORES, V7X_SC_SUBCORES


def _sc_mesh():
    return plsc.VectorSubcoreMesh(core_axis_name="c", subcore_axis_name="s",
                                  num_cores=V7X_SC_CORES, num_subcores=V7X_SC_SUBCORES)


def _sc_scatter_call(h2p, pos_km, n_rows):
    T, W = h2p.shape
    nc, ns = _sc_workers()
    n = SC_ROWS
    per_w = T // (nc * ns * n)

    def body(h2p_hbm, pos_hbm, xs_hbm, i0, i1, i2, i3, rows_v, sem):
        wid = lax.axis_index("s") * nc + lax.axis_index("c")
        idx_refs = (i0, i1, i2, i3)

        @pl.loop(0, per_w)
        def _(j):
            t0 = (wid * per_w + j) * n
            pltpu.sync_copy(h2p_hbm.at[pl.ds(t0, n)], rows_v)
            for k in range(TOP_K):
                pltpu.sync_copy(pos_hbm.at[pl.ds(k * T + t0, n)], idx_refs[k])
            copies = [pltpu.async_copy(rows_v, xs_hbm.at[idx_refs[k]], sem) for k in range(TOP_K)]
            for cp in copies:
                cp.wait()

    return pl.kernel(
        body,
        out_type=jax.ShapeDtypeStruct((n_rows, W), h2p.dtype),
        mesh=_sc_mesh(),
        scratch_types=[pltpu.VMEM((n,), I32)] * TOP_K + [pltpu.VMEM((n, W), h2p.dtype), pltpu.SemaphoreType.DMA],
        name="moe_scatter_rows",
    )(h2p, pos_km)


def _sc_gather_call(ys, pos_km):
    P = pos_km.shape[0]
    W = ys.shape[1]
    nc, ns = _sc_workers()
    n = SC_ROWS
    per_w = P // (nc * ns * n)

    def body(ys_hbm, pos_hbm, yp_hbm, idx_v, rows_v, sem):
        wid = lax.axis_index("s") * nc + lax.axis_index("c")

        @pl.loop(0, per_w)
        def _(j):
            p0 = (wid * per_w + j) * n
            pltpu.sync_copy(pos_hbm.at[pl.ds(p0, n)], idx_v)
            pltpu.async_copy(ys_hbm.at[idx_v], rows_v, sem).wait()
            pltpu.sync_copy(rows_v, yp_hbm.at[pl.ds(p0, n)])

    return pl.kernel(
        body,
        out_type=jax.ShapeDtypeStruct((P, W), ys.dtype),
        mesh=_sc_mesh(),
        scratch_types=[pltpu.VMEM((n,), I32), pltpu.VMEM((n, W), ys.dtype), pltpu.SemaphoreType.DMA],
        name="moe_gather_rows",
    )(ys, pos_km)


def _ffn_kernel(te_ref, tv_ref, xs_ref, wgu_ref, bgu_ref, wd_ref, bd_ref, y_ref):
    i = pl.program_id(0)
    valid = tv_ref[i]
    tm, half = xs_ref.shape
    d_exp = wd_ref.shape[0]

    @pl.when(valid > 0)
    def _():
        keep = lax.broadcasted_iota(I32, (tm, half), 0) < valid
        lo, hi = _unpack_bf16_pair(xs_ref[...])
        x_lo = jnp.where(keep, lo, 0.0).astype(BF16)
        x_hi = jnp.where(keep, hi, 0.0).astype(BF16)
        d = functools.partial(jnp.dot, preferred_element_type=F32)
        nb = 256
        acc = jnp.zeros((tm, wd_ref.shape[1]), F32) + bd_ref[...]
        for c in range(0, d_exp, nb):
            gate = (d(x_lo, wgu_ref[0:half, c:c + nb]) + d(x_hi, wgu_ref[half:, c:c + nb])
                    + bgu_ref[:, c:c + nb])
            lin = (d(x_lo, wgu_ref[0:half, d_exp + c:d_exp + c + nb])
                   + d(x_hi, wgu_ref[half:, d_exp + c:d_exp + c + nb]) + bgu_ref[:, d_exp + c:d_exp + c + nb])
            gate = jnp.minimum(gate, SWIGLU_LIMIT)
            lin = jnp.clip(lin, -SWIGLU_LIMIT, SWIGLU_LIMIT)
            act = gate * jax.nn.sigmoid(SWIGLU_ALPHA * gate) * (lin + 1.0)
            acc = acc + d(act.astype(BF16), wd_ref[c:c + nb, :])
        y_ref[...] = _pack_bf16_pair(acc[:, :half], acc[:, half:])


def _ffn_call(tile_expert, tile_valid, xs, w_gu, b_gu, w_down, b_down, tm):
    R, half = xs.shape
    E, D, two_f = w_gu.shape
    d_exp = w_down.shape[1]
    grid_spec = pltpu.PrefetchScalarGridSpec(
        num_scalar_prefetch=2,
        grid=(R // tm,),
        in_specs=[
            pl.BlockSpec((tm, half), lambda i, te, tv: (i, 0)),
            pl.BlockSpec((None, D, two_f), lambda i, te, tv: (te[i], 0, 0)),
            pl.BlockSpec((None, 1, two_f), lambda i, te, tv: (te[i], 0, 0)),
            pl.BlockSpec((None, d_exp, D), lambda i, te, tv: (te[i], 0, 0)),
            pl.BlockSpec((None, 1, D), lambda i, te, tv: (te[i], 0, 0)),
        ],
        out_specs=pl.BlockSpec((tm, half), lambda i, te, tv: (i, 0)),
    )
    return pl.pallas_call(
        _ffn_kernel,
        grid_spec=grid_spec,
        out_shape=jax.ShapeDtypeStruct((R, half), U32),
        compiler_params=pltpu.CompilerParams(
            dimension_semantics=("arbitrary",), vmem_limit_bytes=VMEM_LIMIT_BYTES),
        name="moe_ffn",
    )(tile_expert, tile_valid, xs, w_gu.astype(BF16), b_gu.reshape(E, 1, two_f),
      w_down.astype(BF16), b_down.reshape(E, 1, D))


def _final_kernel(x_ref, yp_ref, wts_ref, mod_ref, gfin_ref, o_ref):
    x = x_ref[...]
    half = x.shape[1] // 2
    w = wts_ref[...]
    lo = jnp.zeros((x.shape[0], half), F32)
    hi = jnp.zeros((x.shape[0], half), F32)
    for k in range(TOP_K):
        l, h = _unpack_bf16_pair(yp_ref[k])
        lo = lo + w[:, k:k + 1] * l
        hi = hi + w[:, k:k + 1] * h
    gate = mod_ref[5:6, :]
    x_lo = x[:, :half] + gate[:, :half] * lo
    x_hi = x[:, half:] + gate[:, half:] * hi
    ms = (jnp.sum(x_lo * x_lo, axis=-1, keepdims=True) + jnp.sum(x_hi * x_hi, axis=-1, keepdims=True)) / x.shape[1]
    inv = lax.rsqrt(ms + RMS_EPS)
    o_ref[:, :half] = x_lo * inv * gfin_ref[:, :half]
    o_ref[:, half:] = x_hi * inv * gfin_ref[:, half:]


def _final_call(x1, yp, wts, mod, g_final, tokens_per_batch):
    T, D = x1.shape
    tq = min(512, tokens_per_batch)
    per_b = tokens_per_batch // tq
    return pl.pallas_call(
        _final_kernel,
        grid=(T // tq,),
        in_specs=[
            pl.BlockSpec((tq, D), lambda i: (i, 0)),
            pl.BlockSpec((TOP_K, tq, D // 2), lambda i: (0, i, 0)),
            pl.BlockSpec((tq, TOP_K), lambda i: (i, 0)),
            pl.BlockSpec((None, N_MOD, D), lambda i: (i // per_b, 0, 0)),
            pl.BlockSpec((1, D), lambda i: (0, 0)),
        ],
        out_specs=pl.BlockSpec((tq, D), lambda i: (i, 0)),
        out_shape=jax.ShapeDtypeStruct((T, D), F32),
        compiler_params=pltpu.CompilerParams(
            dimension_semantics=("arbitrary",), vmem_limit_bytes=VMEM_LIMIT_BYTES),
        name="moe_combine_final",
    )(x1, yp, wts, mod, g_final.reshape(1, D))


def _group_layout(counts, n_tiles, tm):
    padded = ((counts + tm - 1) // tm) * tm
    ends = jnp.cumsum(padded)
    starts = ends - padded
    tile_row = jnp.arange(n_tiles, dtype=I32) * tm
    te = jnp.minimum(jnp.searchsorted(ends, tile_row, side="right"), N_EXPERTS - 1).astype(I32)
    tv = jnp.clip(counts[te] - (tile_row - starts[te]), 0, tm).astype(I32)
    return starts, te, tv


def kernel(x, c, positions, w_ada, b_ada, g_mix, w_in, w_dw, b_dw, g_conv_ln, b_conv_ln, g_ret_norm,
           w_out, g_ffn, w_router, b_router, w_gu, b_gu, w_down, b_down, g_final):
    B, S, D = x.shape
    T = B * S
    assert w_ada.shape[0] == 1, "single-layer block: the final norm directly follows layer 0"
    xt = x
    for l in range(1):
        mod = _mod_call(c, w_ada[l], b_ada[l]).reshape(B, N_MOD, D)
        x1 = _mixer_call(xt, positions, mod, g_mix[l], w_in[l], w_dw[l], b_dw[l], g_conv_ln[l],
                         b_conv_ln[l], g_ret_norm[l], w_out[l]).reshape(T, D)
        h2p, idx, wts, rank, counts = _route_call(x1, mod, g_ffn[l], w_router[l], b_router[l], S)

        tm = 512 if T * TOP_K >= 512 * N_EXPERTS * 4 else 128
        n_tiles = (T * TOP_K) // tm + N_EXPERTS
        starts, te, tv = _group_layout(counts[0], n_tiles, tm)
        pos_km = (starts[idx] + rank).astype(I32).T.reshape(-1)
        xs = _sc_scatter_call(h2p, pos_km, n_tiles * tm)
        ys = _ffn_call(te, tv, xs, w_gu[l], b_gu[l], w_down[l], b_down[l], tm)
        yp = _sc_gather_call(ys, pos_km)
        xt = _final_call(x1, yp.reshape(TOP_K, T, D // 2), wts, mod, g_final, S)
    return xt.reshape(B, S, D)
```

```python
import functools

import numpy as np
import jax
import jax.numpy as jnp
from jax import lax
from jax.experimental import pallas as pl
from jax.experimental.pallas import tpu as pltpu
from jax.experimental.pallas import tpu_sc as plsc

F32 = jnp.float32
BF16 = jnp.bfloat16
U32 = jnp.uint32
I32 = jnp.int32

CONV_WIDTH = 31
CONV_HALO = 32
RET_HEADS = 4
RET_DIM = 128
RET_CHUNK = 128
ROPE_BASE = 10000.0
N_EXPERTS = 32
TOP_K = 4
SWIGLU_LIMIT = 7.0
SWIGLU_ALPHA = 1.702
RMS_EPS = 1e-6
LN_EPS = 1e-5
N_MOD = 6

VMEM_LIMIT_BYTES = 56 * 1024 * 1024


def _split_bf16(a):
    hi = a.astype(BF16)
    lo = (a - hi.astype(F32)).astype(BF16)
    return hi, lo


def _dot3(a, b_hi, b_lo):
    a_hi, a_lo = _split_bf16(a)
    d = functools.partial(jnp.dot, preferred_element_type=F32)
    return d(a_hi, b_hi) + (d(a_hi, b_lo) + d(a_lo, b_hi))


def _pack_bf16_pair(lo, hi):
    lo_bits = lax.bitcast_convert_type(lo.astype(BF16).astype(F32), U32)
    hi_bits = lax.bitcast_convert_type(hi.astype(BF16).astype(F32), U32)
    return (lo_bits >> 16) | (hi_bits & jnp.uint32(0xFFFF0000))


def _unpack_bf16_pair(p):
    lo = lax.bitcast_convert_type(p << 16, F32)
    hi = lax.bitcast_convert_type(p & jnp.uint32(0xFFFF0000), F32)
    return lo, hi


def _mod_kernel(c_ref, whi_ref, wlo_ref, b_ref, o_ref):
    c = c_ref[...]
    c_act = c * jax.nn.sigmoid(c)
    o_ref[...] = _dot3(c_act, whi_ref[...], wlo_ref[...]) + b_ref[...]


def _mod_call(c, w_ada, b_ada):
    B, D = c.shape
    n = w_ada.shape[1]
    bn = 1024
    w_hi, w_lo = _split_bf16(w_ada)
    return pl.pallas_call(
        _mod_kernel,
        grid=(n // bn,),
        in_specs=[
            pl.BlockSpec((B, D), lambda j: (0, 0)),
            pl.BlockSpec((D, bn), lambda j: (0, j)),
            pl.BlockSpec((D, bn), lambda j: (0, j)),
            pl.BlockSpec((1, bn), lambda j: (0, j)),
        ],
        out_specs=pl.BlockSpec((B, bn), lambda j: (0, j)),
        out_shape=jax.ShapeDtypeStruct((B, n), F32),
        name="adaln_mod",
    )(c, w_hi, w_lo, b_ada.reshape(1, n))


def _retention_tables():
    h = np.arange(RET_HEADS, dtype=np.float32)
    log_gamma = np.log(1.0 - np.power(2.0, -5.0 - h)).astype(np.float32)
    idx = np.arange(RET_CHUNK, dtype=np.float32)
    diff = idx[:, None] - idx[None, :]
    causal = diff >= 0
    mask = np.where(causal[None], np.exp(log_gamma[:, None, None] * np.where(causal, diff, 0.0)[None]), 0.0)
    q_decay = np.exp(log_gamma[:, None] * (idx + 1.0))[..., None]
    k_decay = np.exp(log_gamma[:, None] * (RET_CHUNK - 1.0 - idx))[..., None]
    chunk_decay = np.exp(log_gamma * RET_CHUNK)
    return (mask.astype(np.float32), q_decay.astype(np.float32), k_decay.astype(np.float32),
            [float(v) for v in chunk_decay.astype(np.float32)])


def _mixer_kernel(chunk_decay, ts, x_ref, pos_ref, mod_ref, gmix_ref, win_ref, wdw_ref, bdw_ref,
                  gcl_ref, bcl_ref, gret_ref, wout_ref, invf_ref, dmask_ref, qdec_ref, kdec_ref,
                  o_ref, proj_ref, uext_ref, state_ref, cat_ref):
    s = pl.program_id(1)
    conv_ch = wdw_ref.shape[1]
    ret_w = RET_HEADS * RET_DIM

    @pl.when(s == 0)
    def _():
        uext_ref[0:CONV_HALO, :] = jnp.zeros((CONV_HALO, conv_ch), F32)
        state_ref[...] = jnp.zeros_like(state_ref)

    x = x_ref[...]
    sh = mod_ref[0:1, :]
    sc = mod_ref[1:2, :]
    y = x * lax.rsqrt(jnp.mean(x * x, axis=-1, keepdims=True) + RMS_EPS) * gmix_ref[...]
    h = (y * (1.0 + sc) + sh).astype(BF16)
    proj_ref[...] = jnp.dot(h, win_ref[...], preferred_element_type=F32)

    a = proj_ref[:, 0:conv_ch]
    b = proj_ref[:, conv_ch:2 * conv_ch]
    uext_ref[CONV_HALO:CONV_HALO + ts, :] = a * jax.nn.sigmoid(b)
    cb = 64
    lead = CONV_HALO - (CONV_WIDTH - 1)
    span = cb + CONV_HALO
    for r0 in range(0, ts, cb):
        for c0 in range(0, conv_ch, 128):
            xw = uext_ref[r0:r0 + span, c0:c0 + 128]
            acc = jnp.zeros((cb, 128), F32) + bdw_ref[:, c0:c0 + 128]
            for r in range(8):
                xr = xw if r == 0 else pltpu.roll(xw, span - r, 0)
                for q in range((lead + CONV_WIDTH - 1 - r) // 8 + 1):
                    j = 8 * q + r - lead
                    if 0 <= j < CONV_WIDTH:
                        acc = acc + wdw_ref[j:j + 1, c0:c0 + 128] * xr[8 * q:8 * q + cb]
            proj_ref[r0:r0 + cb, c0:c0 + 128] = acc
    rb = 128
    for r0 in range(0, ts, rb):
        acc = proj_ref[r0:r0 + rb, 0:conv_ch]
        mu = jnp.mean(acc, axis=-1, keepdims=True)
        d = acc - mu
        var = jnp.mean(d * d, axis=-1, keepdims=True)
        ln = d * lax.rsqrt(var + LN_EPS) * gcl_ref[...] + bcl_ref[...]
        cat_ref[r0:r0 + rb, 0:conv_ch] = (ln * jax.nn.sigmoid(ln)).astype(BF16)
    uext_ref[0:CONV_HALO, :] = uext_ref[ts:ts + CONV_HALO, :]

    ang = pos_ref[...].astype(F32) * invf_ref[...]
    cos2 = jnp.cos(ang)
    sin = jnp.sin(ang)
    lane = lax.broadcasted_iota(I32, ang.shape, 1)
    sin2 = jnp.where(lane < RET_DIM // 2, -sin, sin)
    q0 = 2 * conv_ch
    k0 = q0 + ret_w
    v0 = k0 + ret_w
    g0 = v0 + ret_w
    scale = RET_DIM ** -0.5
    for hd in range(RET_HEADS):
        c0 = hd * RET_DIM
        for n in range(ts // RET_CHUNK):
            r0 = n * RET_CHUNK
            rows = slice(r0, r0 + RET_CHUNK)
            cs = cos2[rows]
            sn = sin2[rows]
            q = proj_ref[rows, q0 + c0:q0 + c0 + RET_DIM]
            k = proj_ref[rows, k0 + c0:k0 + c0 + RET_DIM]
            v = proj_ref[rows, v0 + c0:v0 + c0 + RET_DIM].astype(BF16)
            g = proj_ref[rows, g0 + c0:g0 + c0 + RET_DIM]
            qr = q * cs + pltpu.roll(q, RET_DIM // 2, 1) * sn
            kr = (k * cs + pltpu.roll(k, RET_DIM // 2, 1) * sn) * scale
            st = state_ref[hd]
            scores = lax.dot_general(qr.astype(BF16), kr.astype(BF16), (((1,), (1,)), ((), ())),
                                     preferred_element_type=F32) * dmask_ref[hd]
            inner = jnp.dot(scores.astype(BF16), v, preferred_element_type=F32)
            cross = jnp.dot((qr * qdec_ref[hd]).astype(BF16), st.astype(BF16), preferred_element_type=F32)
            kv = lax.dot_general((kr * kdec_ref[hd]).astype(BF16), v, (((0,), (0,)), ((), ())),
                                 preferred_element_type=F32)
            state_ref[hd] = chunk_decay[hd] * st + kv
            r = inner + cross
            mu = jnp.mean(r, axis=-1, keepdims=True)
            d = r - mu
            var = jnp.mean(d * d, axis=-1, keepdims=True)
            rn = d * lax.rsqrt(var + LN_EPS) * gret_ref[:, c0:c0 + RET_DIM]
            cat_ref[rows, conv_ch + c0:conv_ch + c0 + RET_DIM] = (g * jax.nn.sigmoid(g) * rn).astype(BF16)

    out = jnp.dot(cat_ref[...], wout_ref[...], preferred_element_type=F32)
    o_ref[...] = x + mod_ref[2:3, :] * out


def _mixer_call(x, positions, mod, g_mix, w_in, w_dw, b_dw, g_conv_ln, b_conv_ln, g_ret_norm, w_out):
    B, S, D = x.shape
    in_cols = w_in.shape[1]
    conv_ch = w_dw.shape[1]
    ts = min(512, S)
    mask, q_decay, k_decay, chunk_decay = _retention_tables()
    half = RET_DIM // 2
    inv_freq = (ROPE_BASE ** (-np.arange(half, dtype=np.float32) / half)).astype(np.float32)
    inv_freq2 = np.concatenate([inv_freq, inv_freq])[None, :]
    w_dw_p = jnp.zeros((CONV_HALO, conv_ch), F32).at[:CONV_WIDTH].set(w_dw)
    full = lambda shape: pl.BlockSpec(shape, lambda b, s: (0,) * len(shape))
    return pl.pallas_call(
        functools.partial(_mixer_kernel, chunk_decay, ts),
        grid=(B, S // ts),
        in_specs=[
            pl.BlockSpec((None, ts, D), lambda b, s: (b, s, 0)),
            pl.BlockSpec((None, ts, 1), lambda b, s: (b, s, 0)),
            pl.BlockSpec((None, N_MOD, D), lambda b, s: (b, 0, 0)),
            full((1, D)),
            full((D, in_cols)),
            full((CONV_HALO, conv_ch)),
            full((1, conv_ch)),
            full((1, conv_ch)),
            full((1, conv_ch)),
            full((1, RET_HEADS * RET_DIM)),
            full((conv_ch + RET_HEADS * RET_DIM, D)),
            full((1, RET_DIM)),
            full((RET_HEADS, RET_CHUNK, RET_CHUNK)),
            full((RET_HEADS, RET_CHUNK, 1)),
            full((RET_HEADS, RET_CHUNK, 1)),
        ],
        out_specs=pl.BlockSpec((None, ts, D), lambda b, s: (b, s, 0)),
        out_shape=jax.ShapeDtypeStruct((B, S, D), F32),
        scratch_shapes=[
            pltpu.VMEM((ts, in_cols), F32),
            pltpu.VMEM((CONV_HALO + ts, conv_ch), F32),
            pltpu.VMEM((RET_HEADS, RET_DIM, RET_DIM), F32),
            pltpu.VMEM((ts, conv_ch + RET_HEADS * RET_DIM), BF16),
        ],
        compiler_params=pltpu.CompilerParams(
            dimension_semantics=("arbitrary", "arbitrary"), vmem_limit_bytes=VMEM_LIMIT_BYTES),
        name="hybrid_mixer",
    )(x, positions.reshape(B, S, 1), mod, g_mix.reshape(1, D), w_in.astype(BF16), w_dw_p,
      b_dw.reshape(1, -1), g_conv_ln.reshape(1, -1), b_conv_ln.reshape(1, -1), g_ret_norm.reshape(1, -1),
      w_out.astype(BF16), jnp.asarray(inv_freq2), jnp.asarray(mask), jnp.asarray(q_decay),
      jnp.asarray(k_decay))


def _route_kernel(x_ref, mod_ref, gffn_ref, wrhi_ref, wrlo_ref, br_ref, ltri_ref,
                  h2p_ref, idx_ref, wts_ref, rank_ref, cnt_ref, carry_ref):
    i = pl.program_id(0)

    @pl.when(i == 0)
    def _():
        carry_ref[...] = jnp.zeros_like(carry_ref)

    x = x_ref[...]
    half = x.shape[1] // 2
    y = x * lax.rsqrt(jnp.mean(x * x, axis=-1, keepdims=True) + RMS_EPS) * gffn_ref[...]
    h2 = y * (1.0 + mod_ref[4:5, :]) + mod_ref[3:4, :]
    h2p_ref[...] = _pack_bf16_pair(h2[:, :half], h2[:, half:])

    logits = _dot3(h2, wrhi_ref[...], wrlo_ref[...]) + br_ref[...]
    eid = lax.broadcasted_iota(I32, logits.shape, 1)
    vals, sels, idxs = [], [], []
    l = logits
    for _ in range(TOP_K):
        m = jnp.max(l, axis=-1, keepdims=True)
        ik = jnp.min(jnp.where(l == m, eid, N_EXPERTS), axis=-1, keepdims=True)
        sel = eid == ik
        vals.append(m)
        sels.append(sel)
        idxs.append(ik)
        l = jnp.where(sel, -jnp.inf, l)
    exps = [jnp.exp(v - vals[0]) for v in vals]
    denom = exps[0] + exps[1] + exps[2] + exps[3]
    member = jnp.zeros(logits.shape, F32)
    for sel in sels:
        member = member + sel.astype(F32)
    before = jnp.dot(ltri_ref[...], member.astype(BF16), preferred_element_type=F32) + carry_ref[...]
    ranks = [jnp.sum(jnp.where(sel, before, 0.0), axis=-1, keepdims=True) for sel in sels]
    carry_ref[...] = carry_ref[...] + jnp.sum(member, axis=0, keepdims=True)
    idx_ref[...] = jnp.concatenate(idxs, axis=1)
    wts_ref[...] = jnp.concatenate([e / denom for e in exps], axis=1)
    rank_ref[...] = jnp.concatenate(ranks, axis=1).astype(I32)
    cnt_ref[...] = carry_ref[...].astype(I32)


def _route_call(x1, mod, g_ffn, w_router, b_router, tokens_per_batch):
    T, D = x1.shape
    tr = min(512, tokens_per_batch)
    per_b = tokens_per_batch // tr
    wr_hi, wr_lo = _split_bf16(w_router)
    ltri = jnp.asarray(np.tril(np.ones((tr, tr), np.float32), -1), BF16)
    full = lambda shape: pl.BlockSpec(shape, lambda i: (0,) * len(shape))
    return pl.pallas_call(
        _route_kernel,
        grid=(T // tr,),
        in_specs=[
            pl.BlockSpec((tr, D), lambda i: (i, 0)),
            pl.BlockSpec((None, N_MOD, D), lambda i: (i // per_b, 0, 0)),
            full((1, D)),
            full((D, N_EXPERTS)),
            full((D, N_EXPERTS)),
            full((1, N_EXPERTS)),
            full((tr, tr)),
        ],
        out_specs=[
            pl.BlockSpec((tr, D // 2), lambda i: (i, 0)),
            pl.BlockSpec((tr, TOP_K), lambda i: (i, 0)),
            pl.BlockSpec((tr, TOP_K), lambda i: (i, 0)),
            pl.BlockSpec((tr, TOP_K), lambda i: (i, 0)),
            full((1, N_EXPERTS)),
        ],
        out_shape=[
            jax.ShapeDtypeStruct((T, D // 2), U32),
            jax.ShapeDtypeStruct((T, TOP_K), I32),
            jax.ShapeDtypeStruct((T, TOP_K), F32),
            jax.ShapeDtypeStruct((T, TOP_K), I32),
            jax.ShapeDtypeStruct((1, N_EXPERTS), I32),
        ],
        scratch_shapes=[pltpu.VMEM((1, N_EXPERTS), F32)],
        compiler_params=pltpu.CompilerParams(
            dimension_semantics=("arbitrary",), vmem_limit_bytes=VMEM_LIMIT_BYTES),
        name="moe_route",
    )(x1, mod, g_ffn.reshape(1, D), wr_hi, wr_lo, b_router.reshape(1, N_EXPERTS), ltri)


SC_ROWS = 128


V7X_SC_CORES = 2
V7X_SC_SUBCORES = 16


def _sc_workers():
    return V7X_SC_CORES, V7X_SC_SUBCORES


def _sc_mesh():
    return plsc.VectorSubcoreMesh(core_axis_name="c", subcore_axis_name="s",
                                  num_cores=V7X_SC_CORES, num_subcores=V7X_SC_SUBCORES)


def _sc_scatter_call(h2p, pos_km, n_rows):
    T, W = h2p.shape
    nc, ns = _sc_workers()
    n = SC_ROWS
    per_w = T // (nc * ns * n)

    def body(h2p_hbm, pos_hbm, xs_hbm, i0, i1, i2, i3, rows_v, sem):
        wid = lax.axis_index("s") * nc + lax.axis_index("c")
        idx_refs = (i0, i1, i2, i3)

        @pl.loop(0, per_w)
        def _(j):
            t0 = (wid * per_w + j) * n
            pltpu.sync_copy(h2p_hbm.at[pl.ds(t0, n)], rows_v)
            for k in range(TOP_K):
                pltpu.sync_copy(pos_hbm.at[pl.ds(k * T + t0, n)], idx_refs[k])
            copies = [pltpu.async_copy(rows_v, xs_hbm.at[idx_refs[k]], sem) for k in range(TOP_K)]
            for cp in copies:
                cp.wait()

    return pl.kernel(
        body,
        out_type=jax.ShapeDtypeStruct((n_rows, W), h2p.dtype),
        mesh=_sc_mesh(),
        scratch_types=[pltpu.VMEM((n,), I32)] * TOP_K + [pltpu.VMEM((n, W), h2p.dtype), pltpu.SemaphoreType.DMA],
        name="moe_scatter_rows",
    )(h2p, pos_km)


def _sc_gather_call(ys, pos_km):
    P = pos_km.shape[0]
    W = ys.shape[1]
    nc, ns = _sc_workers()
    n = SC_ROWS
    per_w = P // (nc * ns * n)

    def body(ys_hbm, pos_hbm, yp_hbm, idx_v, rows_v, sem):
        wid = lax.axis_index("s") * nc + lax.axis_index("c")

        @pl.loop(0, per_w)
        def _(j):
            p0 = (wid * per_w + j) * n
            pltpu.sync_copy(pos_hbm.at[pl.ds(p0, n)], idx_v)
            pltpu.async_copy(ys_hbm.at[idx_v], rows_v, sem).wait()
            pltpu.sync_copy(rows_v, yp_hbm.at[pl.ds(p0, n)])

    return pl.kernel(
        body,
        out_type=jax.ShapeDtypeStruct((P, W), ys.dtype),
        mesh=_sc_mesh(),
        scratch_types=[pltpu.VMEM((n,), I32), pltpu.VMEM((n, W), ys.dtype), pltpu.SemaphoreType.DMA],
        name="moe_gather_rows",
    )(ys, pos_km)


def _ffn_kernel(te_ref, tv_ref, xs_ref, wgu_ref, bgu_ref, wd_ref, bd_ref, y_ref):
    i = pl.program_id(0)
    valid = tv_ref[i]
    tm, half = xs_ref.shape
    d_exp = wd_ref.shape[0]

    @pl.when(valid > 0)
    def _():
        keep = lax.broadcasted_iota(I32, (tm, half), 0) < valid
        lo, hi = _unpack_bf16_pair(xs_ref[...])
        x_lo = jnp.where(keep, lo, 0.0).astype(BF16)
        x_hi = jnp.where(keep, hi, 0.0).astype(BF16)
        d = functools.partial(jnp.dot, preferred_element_type=F32)
        nb = 256
        acc = jnp.zeros((tm, wd_ref.shape[1]), F32) + bd_ref[...]
        for c in range(0, d_exp, nb):
            gate = (d(x_lo, wgu_ref[0:half, c:c + nb]) + d(x_hi, wgu_ref[half:, c:c + nb])
                    + bgu_ref[:, c:c + nb])
            lin = (d(x_lo, wgu_ref[0:half, d_exp + c:d_exp + c + nb])
                   + d(x_hi, wgu_ref[half:, d_exp + c:d_exp + c + nb]) + bgu_ref[:, d_exp + c:d_exp + c + nb])
            gate = jnp.minimum(gate, SWIGLU_LIMIT)
            lin = jnp.clip(lin, -SWIGLU_LIMIT, SWIGLU_LIMIT)
            act = gate * jax.nn.sigmoid(SWIGLU_ALPHA * gate) * (lin + 1.0)
            acc = acc + d(act.astype(BF16), wd_ref[c:c + nb, :])
        y_ref[...] = _pack_bf16_pair(acc[:, :half], acc[:, half:])


def _ffn_call(tile_expert, tile_valid, xs, w_gu, b_gu, w_down, b_down, tm):
    R, half = xs.shape
    E, D, two_f = w_gu.shape
    d_exp = w_down.shape[1]
    grid_spec = pltpu.PrefetchScalarGridSpec(
        num_scalar_prefetch=2,
        grid=(R // tm,),
        in_specs=[
            pl.BlockSpec((tm, half), lambda i, te, tv: (i, 0)),
            pl.BlockSpec((None, D, two_f), lambda i, te, tv: (te[i], 0, 0)),
            pl.BlockSpec((None, 1, two_f), lambda i, te, tv: (te[i], 0, 0)),
            pl.BlockSpec((None, d_exp, D), lambda i, te, tv: (te[i], 0, 0)),
            pl.BlockSpec((None, 1, D), lambda i, te, tv: (te[i], 0, 0)),
        ],
        out_specs=pl.BlockSpec((tm, half), lambda i, te, tv: (i, 0)),
    )
    return pl.pallas_call(
        _ffn_kernel,
        grid_spec=grid_spec,
        out_shape=jax.ShapeDtypeStruct((R, half), U32),
        compiler_params=pltpu.CompilerParams(
            dimension_semantics=("arbitrary",), vmem_limit_bytes=VMEM_LIMIT_BYTES),
        name="moe_ffn",
    )(tile_expert, tile_valid, xs, w_gu.astype(BF16), b_gu.reshape(E, 1, two_f),
      w_down.astype(BF16), b_down.reshape(E, 1, D))


def _final_kernel(x_ref, yp_ref, wts_ref, mod_ref, gfin_ref, o_ref):
    x = x_ref[...]
    half = x.shape[1] // 2
    w = wts_ref[...]
    lo = jnp.zeros((x.shape[0], half), F32)
    hi = jnp.zeros((x.shape[0], half), F32)
    for k in range(TOP_K):
        l, h = _unpack_bf16_pair(yp_ref[k])
        lo = lo + w[:, k:k + 1] * l
        hi = hi + w[:, k:k + 1] * h
    gate = mod_ref[5:6, :]
    x_lo = x[:, :half] + gate[:, :half] * lo
    x_hi = x[:, half:] + gate[:, half:] * hi
    ms = (jnp.sum(x_lo * x_lo, axis=-1, keepdims=True) + jnp.sum(x_hi * x_hi, axis=-1, keepdims=True)) / x.shape[1]
    inv = lax.rsqrt(ms + RMS_EPS)
    o_ref[:, :half] = x_lo * inv * gfin_ref[:, :half]
    o_ref[:, half:] = x_hi * inv * gfin_ref[:, half:]


def _final_call(x1, yp, wts, mod, g_final, tokens_per_batch):
    T, D = x1.shape
    tq = min(512, tokens_per_batch)
    per_b = tokens_per_batch // tq
    return pl.pallas_call(
        _final_kernel,
        grid=(T // tq,),
        in_specs=[
            pl.BlockSpec((tq, D), lambda i: (i, 0)),
            pl.BlockSpec((TOP_K, tq, D // 2), lambda i: (0, i, 0)),
            pl.BlockSpec((tq, TOP_K), lambda i: (i, 0)),
            pl.BlockSpec((None, N_MOD, D), lambda i: (i // per_b, 0, 0)),
            pl.BlockSpec((1, D), lambda i: (0, 0)),
        ],
        out_specs=pl.BlockSpec((tq, D), lambda i: (i, 0)),
        out_shape=jax.ShapeDtypeStruct((T, D), F32),
        compiler_params=pltpu.CompilerParams(
            dimension_semantics=("arbitrary",), vmem_limit_bytes=VMEM_LIMIT_BYTES),
        name="moe_combine_final",
    )(x1, yp, wts, mod, g_final.reshape(1, D))


def _group_layout(counts, n_tiles, tm):
    padded = ((counts + tm - 1) // tm) * tm
    ends = jnp.cumsum(padded)
    starts = ends - padded
    tile_row = jnp.arange(n_tiles, dtype=I32) * tm
    te = jnp.minimum(jnp.sum(tile_row[:, None] >= ends[None, :], axis=1), N_EXPERTS - 1).astype(I32)
    tv = jnp.clip(counts[te] - (tile_row - starts[te]), 0, tm).astype(I32)
    return starts, te, tv


def kernel(x, c, positions, w_ada, b_ada, g_mix, w_in, w_dw, b_dw, g_conv_ln, b_conv_ln, g_ret_norm,
           w_out, g_ffn, w_router, b_router, w_gu, b_gu, w_down, b_down, g_final):
    B, S, D = x.shape
    T = B * S
    assert w_ada.shape[0] == 1, "single-layer block: the final norm directly follows layer 0"
    xt = x
    for l in range(1):
        mod = _mod_call(c, w_ada[l], b_ada[l]).reshape(B, N_MOD, D)
        x1 = _mixer_call(xt, positions, mod, g_mix[l], w_in[l], w_dw[l], b_dw[l], g_conv_ln[l],
                         b_conv_ln[l], g_ret_norm[l], w_out[l]).reshape(T, D)
        h2p, idx, wts, rank, counts = _route_call(x1, mod, g_ffn[l], w_router[l], b_router[l], S)

        tm = 512 if T * TOP_K >= 512 * N_EXPERTS * 4 else 128
        n_tiles = (T * TOP_K) // tm + N_EXPERTS
        starts, te, tv = _group_layout(counts[0], n_tiles, tm)
        pos_km = (starts[idx] + rank).astype(I32).T.reshape(-1)
        xs = _sc_scatter_call(h2p, pos_km, n_tiles * tm)
        ys = _ffn_call(te, tv, xs, w_gu[l], b_gu[l], w_down[l], b_down[l], tm)
        yp = _sc_gather_call(ys, pos_km)
        xt = _final_call(x1, yp.reshape(TOP_K, T, D // 2), wts, mod, g_final, S)
    return xt.reshape(B, S, D)
```

```python
import functools

import numpy as np
import jax
import jax.numpy as jnp
from jax import lax
from jax.experimental import pallas as pl
from jax.experimental.pallas import tpu as pltpu
from jax.experimental.pallas import tpu_sc as plsc

F32 = jnp.float32
BF16 = jnp.bfloat16
U32 = jnp.uint32
I32 = jnp.int32

CONV_WIDTH = 31
CONV_HALO = 32
RET_HEADS = 4
RET_DIM = 128
RET_CHUNK = 128
ROPE_BASE = 10000.0
N_EXPERTS = 32
TOP_K = 4
SWIGLU_LIMIT = 7.0
SWIGLU_ALPHA = 1.702
RMS_EPS = 1e-6
LN_EPS = 1e-5
N_MOD = 6

VMEM_LIMIT_BYTES = 56 * 1024 * 1024
MOE_TOKEN_GROUPS = 2


def _split_bf16(a):
    hi = a.astype(BF16)
    lo = (a - hi.astype(F32)).astype(BF16)
    return hi, lo


def _dot3(a, b_hi, b_lo):
    a_hi, a_lo = _split_bf16(a)
    d = functools.partial(jnp.dot, preferred_element_type=F32)
    return d(a_hi, b_hi) + (d(a_hi, b_lo) + d(a_lo, b_hi))


def _pack_bf16_pair(lo, hi):
    lo_bits = lax.bitcast_convert_type(lo.astype(BF16).astype(F32), U32)
    hi_bits = lax.bitcast_convert_type(hi.astype(BF16).astype(F32), U32)
    return (lo_bits >> 16) | (hi_bits & jnp.uint32(0xFFFF0000))


def _unpack_bf16_pair(p):
    lo = lax.bitcast_convert_type(p << 16, F32)
    hi = lax.bitcast_convert_type(p & jnp.uint32(0xFFFF0000), F32)
    return lo, hi


def _mod_kernel(c_ref, whi_ref, wlo_ref, b_ref, o_ref):
    c = c_ref[...]
    c_act = c * jax.nn.sigmoid(c)
    o_ref[...] = _dot3(c_act, whi_ref[...], wlo_ref[...]) + b_ref[...]


def _mod_call(c, w_ada, b_ada):
    B, D = c.shape
    n = w_ada.shape[1]
    bn = 1024
    w_hi, w_lo = _split_bf16(w_ada)
    return pl.pallas_call(
        _mod_kernel,
        grid=(n // bn,),
        in_specs=[
            pl.BlockSpec((B, D), lambda j: (0, 0)),
            pl.BlockSpec((D, bn), lambda j: (0, j)),
            pl.BlockSpec((D, bn), lambda j: (0, j)),
            pl.BlockSpec((1, bn), lambda j: (0, j)),
        ],
        out_specs=pl.BlockSpec((B, bn), lambda j: (0, j)),
        out_shape=jax.ShapeDtypeStruct((B, n), F32),
        name="adaln_mod",
    )(c, w_hi, w_lo, b_ada.reshape(1, n))


def _retention_tables():
    h = np.arange(RET_HEADS, dtype=np.float32)
    log_gamma = np.log(1.0 - np.power(2.0, -5.0 - h)).astype(np.float32)
    idx = np.arange(RET_CHUNK, dtype=np.float32)
    diff = idx[:, None] - idx[None, :]
    causal = diff >= 0
    mask = np.where(causal[None], np.exp(log_gamma[:, None, None] * np.where(causal, diff, 0.0)[None]), 0.0)
    q_decay = np.exp(log_gamma[:, None] * (idx + 1.0))[..., None]
    k_decay = np.exp(log_gamma[:, None] * (RET_CHUNK - 1.0 - idx))[..., None]
    chunk_decay = np.exp(log_gamma * RET_CHUNK)
    return (mask.astype(np.float32), q_decay.astype(np.float32), k_decay.astype(np.float32),
            [float(v) for v in chunk_decay.astype(np.float32)])


def _mixer_kernel(chunk_decay, ts, x_ref, pos_ref, mod_ref, gmix_ref, win_ref, wdw_ref, bdw_ref,
                  gcl_ref, bcl_ref, gret_ref, wout_ref, invf_ref, dmask_ref, qdec_ref, kdec_ref,
                  o_ref, proj_ref, uext_ref, state_ref, cat_ref):
    s = pl.program_id(1)
    conv_ch = wdw_ref.shape[1]
    ret_w = RET_HEADS * RET_DIM

    @pl.when(s == 0)
    def _():
        uext_ref[0:CONV_HALO, :] = jnp.zeros((CONV_HALO, conv_ch), F32)
        state_ref[...] = jnp.zeros_like(state_ref)

    x = x_ref[...]
    sh = mod_ref[0:1, :]
    sc = mod_ref[1:2, :]
    y = x * lax.rsqrt(jnp.mean(x * x, axis=-1, keepdims=True) + RMS_EPS) * gmix_ref[...]
    h = (y * (1.0 + sc) + sh).astype(BF16)
    proj_ref[...] = jnp.dot(h, win_ref[...], preferred_element_type=F32)

    a = proj_ref[:, 0:conv_ch]
    b = proj_ref[:, conv_ch:2 * conv_ch]
    uext_ref[CONV_HALO:CONV_HALO + ts, :] = a * jax.nn.sigmoid(b)
    cb = 64
    lead = CONV_HALO - (CONV_WIDTH - 1)
    span = cb + CONV_HALO
    for r0 in range(0, ts, cb):
        for c0 in range(0, conv_ch, 128):
            xw = uext_ref[r0:r0 + span, c0:c0 + 128]
            acc = jnp.zeros((cb, 128), F32) + bdw_ref[:, c0:c0 + 128]
            for r in range(8):
                xr = xw if r == 0 else pltpu.roll(xw, span - r, 0)
                for q in range((lead + CONV_WIDTH - 1 - r) // 8 + 1):
                    j = 8 * q + r - lead
                    if 0 <= j < CONV_WIDTH:
                        acc = acc + wdw_ref[j:j + 1, c0:c0 + 128] * xr[8 * q:8 * q + cb]
            proj_ref[r0:r0 + cb, c0:c0 + 128] = acc
    rb = 128
    for r0 in range(0, ts, rb):
        acc = proj_ref[r0:r0 + rb, 0:conv_ch]
        mu = jnp.mean(acc, axis=-1, keepdims=True)
        d = acc - mu
        var = jnp.mean(d * d, axis=-1, keepdims=True)
        ln = d * lax.rsqrt(var + LN_EPS) * gcl_ref[...] + bcl_ref[...]
        cat_ref[r0:r0 + rb, 0:conv_ch] = (ln * jax.nn.sigmoid(ln)).astype(BF16)
    uext_ref[0:CONV_HALO, :] = uext_ref[ts:ts + CONV_HALO, :]

    ang = pos_ref[...].astype(F32) * invf_ref[...]
    cos2 = jnp.cos(ang)
    sin = jnp.sin(ang)
    lane = lax.broadcasted_iota(I32, ang.shape, 1)
    sin2 = jnp.where(lane < RET_DIM // 2, -sin, sin)
    q0 = 2 * conv_ch
    k0 = q0 + ret_w
    v0 = k0 + ret_w
    g0 = v0 + ret_w
    scale = RET_DIM ** -0.5
    for hd in range(RET_HEADS):
        c0 = hd * RET_DIM
        for n in range(ts // RET_CHUNK):
            r0 = n * RET_CHUNK
            rows = slice(r0, r0 + RET_CHUNK)
            cs = cos2[rows]
            sn = sin2[rows]
            q = proj_ref[rows, q0 + c0:q0 + c0 + RET_DIM]
            k = proj_ref[rows, k0 + c0:k0 + c0 + RET_DIM]
            v = proj_ref[rows, v0 + c0:v0 + c0 + RET_DIM].astype(BF16)
            g = proj_ref[rows, g0 + c0:g0 + c0 + RET_DIM]
            qr = q * cs + pltpu.roll(q, RET_DIM // 2, 1) * sn
            kr = (k * cs + pltpu.roll(k, RET_DIM // 2, 1) * sn) * scale
            st = state_ref[hd]
            scores = lax.dot_general(qr.astype(BF16), kr.astype(BF16), (((1,), (1,)), ((), ())),
                                     preferred_element_type=F32) * dmask_ref[hd]
            inner = jnp.dot(scores.astype(BF16), v, preferred_element_type=F32)
            cross = jnp.dot((qr * qdec_ref[hd]).astype(BF16), st.astype(BF16), preferred_element_type=F32)
            kv = lax.dot_general((kr * kdec_ref[hd]).astype(BF16), v, (((0,), (0,)), ((), ())),
                                 preferred_element_type=F32)
            state_ref[hd] = chunk_decay[hd] * st + kv
            r = inner + cross
            mu = jnp.mean(r, axis=-1, keepdims=True)
            d = r - mu
            var = jnp.mean(d * d, axis=-1, keepdims=True)
            rn = d * lax.rsqrt(var + LN_EPS) * gret_ref[:, c0:c0 + RET_DIM]
            cat_ref[rows, conv_ch + c0:conv_ch + c0 + RET_DIM] = (g * jax.nn.sigmoid(g) * rn).astype(BF16)

    out = jnp.dot(cat_ref[...], wout_ref[...], preferred_element_type=F32)
    o_ref[...] = x + mod_ref[2:3, :] * out


def _mixer_call(x, positions, mod, g_mix, w_in, w_dw, b_dw, g_conv_ln, b_conv_ln, g_ret_norm, w_out):
    B, S, D = x.shape
    in_cols = w_in.shape[1]
    conv_ch = w_dw.shape[1]
    ts = min(512, S)
    mask, q_decay, k_decay, chunk_decay = _retention_tables()
    half = RET_DIM // 2
    inv_freq = (ROPE_BASE ** (-np.arange(half, dtype=np.float32) / half)).astype(np.float32)
    inv_freq2 = np.concatenate([inv_freq, inv_freq])[None, :]
    w_dw_p = jnp.zeros((CONV_HALO, conv_ch), F32).at[:CONV_WIDTH].set(w_dw)
    full = lambda shape: pl.BlockSpec(shape, lambda b, s: (0,) * len(shape))
    return pl.pallas_call(
        functools.partial(_mixer_kernel, chunk_decay, ts),
        grid=(B, S // ts),
        in_specs=[
            pl.BlockSpec((None, ts, D), lambda b, s: (b, s, 0)),
            pl.BlockSpec((None, ts, 1), lambda b, s: (b, s, 0)),
            pl.BlockSpec((None, N_MOD, D), lambda b, s: (b, 0, 0)),
            full((1, D)),
            full((D, in_cols)),
            full((CONV_HALO, conv_ch)),
            full((1, conv_ch)),
            full((1, conv_ch)),
            full((1, conv_ch)),
            full((1, RET_HEADS * RET_DIM)),
            full((conv_ch + RET_HEADS * RET_DIM, D)),
            full((1, RET_DIM)),
            full((RET_HEADS, RET_CHUNK, RET_CHUNK)),
            full((RET_HEADS, RET_CHUNK, 1)),
            full((RET_HEADS, RET_CHUNK, 1)),
        ],
        out_specs=pl.BlockSpec((None, ts, D), lambda b, s: (b, s, 0)),
        out_shape=jax.ShapeDtypeStruct((B, S, D), F32),
        scratch_shapes=[
            pltpu.VMEM((ts, in_cols), F32),
            pltpu.VMEM((CONV_HALO + ts, conv_ch), F32),
            pltpu.VMEM((RET_HEADS, RET_DIM, RET_DIM), F32),
            pltpu.VMEM((ts, conv_ch + RET_HEADS * RET_DIM), BF16),
        ],
        compiler_params=pltpu.CompilerParams(
            dimension_semantics=("arbitrary", "arbitrary"), vmem_limit_bytes=VMEM_LIMIT_BYTES),
        name="hybrid_mixer",
    )(x, positions.reshape(B, S, 1), mod, g_mix.reshape(1, D), w_in.astype(BF16), w_dw_p,
      b_dw.reshape(1, -1), g_conv_ln.reshape(1, -1), b_conv_ln.reshape(1, -1), g_ret_norm.reshape(1, -1),
      w_out.astype(BF16), jnp.asarray(inv_freq2), jnp.asarray(mask), jnp.asarray(q_decay),
      jnp.asarray(k_decay))


def _route_kernel(x_ref, mod_ref, gffn_ref, wrhi_ref, wrlo_ref, br_ref, ltri_ref,
                  h2p_ref, idx_ref, wts_ref, rank_ref, cnt_ref, carry_ref):
    i = pl.program_id(0)

    @pl.when(i == 0)
    def _():
        carry_ref[...] = jnp.zeros_like(carry_ref)

    x = x_ref[...]
    half = x.shape[1] // 2
    y = x * lax.rsqrt(jnp.mean(x * x, axis=-1, keepdims=True) + RMS_EPS) * gffn_ref[...]
    h2 = y * (1.0 + mod_ref[4:5, :]) + mod_ref[3:4, :]
    h2p_ref[...] = _pack_bf16_pair(h2[:, :half], h2[:, half:])

    logits = _dot3(h2, wrhi_ref[...], wrlo_ref[...]) + br_ref[...]
    eid = lax.broadcasted_iota(I32, logits.shape, 1)
    vals, sels, idxs = [], [], []
    l = logits
    for _ in range(TOP_K):
        m = jnp.max(l, axis=-1, keepdims=True)
        ik = jnp.min(jnp.where(l == m, eid, N_EXPERTS), axis=-1, keepdims=True)
        sel = eid == ik
        vals.append(m)
        sels.append(sel)
        idxs.append(ik)
        l = jnp.where(sel, -jnp.inf, l)
    exps = [jnp.exp(v - vals[0]) for v in vals]
    denom = exps[0] + exps[1] + exps[2] + exps[3]
    member = jnp.zeros(logits.shape, F32)
    for sel in sels:
        member = member + sel.astype(F32)
    before = jnp.dot(ltri_ref[...], member.astype(BF16), preferred_element_type=F32) + carry_ref[...]
    ranks = [jnp.sum(jnp.where(sel, before, 0.0), axis=-1, keepdims=True) for sel in sels]
    carry_ref[...] = carry_ref[...] + jnp.sum(member, axis=0, keepdims=True)
    idx_ref[...] = jnp.concatenate(idxs, axis=1)
    wts_ref[...] = jnp.concatenate([e / denom for e in exps], axis=1)
    rank_ref[...] = jnp.concatenate(ranks, axis=1).astype(I32)
    cnt_ref[...] = carry_ref[...].astype(I32)


def _route_call(x1, mod, g_ffn, w_router, b_router, tokens_per_batch, group, n_groups):
    D = x1.shape[1]
    T = x1.shape[0] // n_groups
    tr = min(512, tokens_per_batch)
    per_b = tokens_per_batch // tr
    first = group * (T // tr)
    wr_hi, wr_lo = _split_bf16(w_router)
    ltri = jnp.asarray(np.tril(np.ones((tr, tr), np.float32), -1), BF16)
    full = lambda shape: pl.BlockSpec(shape, lambda i: (0,) * len(shape))
    return pl.pallas_call(
        _route_kernel,
        grid=(T // tr,),
        in_specs=[
            pl.BlockSpec((tr, D), lambda i: (first + i, 0)),
            pl.BlockSpec((None, N_MOD, D), lambda i: ((first + i) // per_b, 0, 0)),
            full((1, D)),
            full((D, N_EXPERTS)),
            full((D, N_EXPERTS)),
            full((1, N_EXPERTS)),
            full((tr, tr)),
        ],
        out_specs=[
            pl.BlockSpec((tr, D // 2), lambda i: (i, 0)),
            pl.BlockSpec((tr, TOP_K), lambda i: (i, 0)),
            pl.BlockSpec((tr, TOP_K), lambda i: (i, 0)),
            pl.BlockSpec((tr, TOP_K), lambda i: (i, 0)),
            full((1, N_EXPERTS)),
        ],
        out_shape=[
            jax.ShapeDtypeStruct((T, D // 2), U32),
            jax.ShapeDtypeStruct((T, TOP_K), I32),
            jax.ShapeDtypeStruct((T, TOP_K), F32),
            jax.ShapeDtypeStruct((T, TOP_K), I32),
            jax.ShapeDtypeStruct((1, N_EXPERTS), I32),
        ],
        scratch_shapes=[pltpu.VMEM((1, N_EXPERTS), F32)],
        compiler_params=pltpu.CompilerParams(
            dimension_semantics=("arbitrary",), vmem_limit_bytes=VMEM_LIMIT_BYTES),
        name="moe_route",
    )(x1, mod, g_ffn.reshape(1, D), wr_hi, wr_lo, b_router.reshape(1, N_EXPERTS), ltri)


SC_ROWS = 128


V7X_SC_CORES = 2
V7X_SC_SUBCORES = 16


def _sc_workers():
    return V7X_SC_CORES, V7X_SC_SUBCORES


def _sc_mesh():
    return plsc.VectorSubcoreMesh(core_axis_name="c", subcore_axis_name="s",
                                  num_cores=V7X_SC_CORES, num_subcores=V7X_SC_SUBCORES)


def _sc_scatter_call(h2p, pos_km, n_rows):
    T, W = h2p.shape
    nc, ns = _sc_workers()
    n = SC_ROWS
    per_w = T // (nc * ns * n)

    def body(h2p_hbm, pos_hbm, xs_hbm, i0, i1, i2, i3, rows_v, sem):
        wid = lax.axis_index("s") * nc + lax.axis_index("c")
        idx_refs = (i0, i1, i2, i3)

        @pl.loop(0, per_w)
        def _(j):
            t0 = (wid * per_w + j) * n
            pltpu.sync_copy(h2p_hbm.at[pl.ds(t0, n)], rows_v)
            for k in range(TOP_K):
                pltpu.sync_copy(pos_hbm.at[pl.ds(k * T + t0, n)], idx_refs[k])
            copies = [pltpu.async_copy(rows_v, xs_hbm.at[idx_refs[k]], sem) for k in range(TOP_K)]
            for cp in copies:
                cp.wait()

    return pl.kernel(
        body,
        out_type=jax.ShapeDtypeStruct((n_rows, W), h2p.dtype),
        mesh=_sc_mesh(),
        scratch_types=[pltpu.VMEM((n,), I32)] * TOP_K + [pltpu.VMEM((n, W), h2p.dtype), pltpu.SemaphoreType.DMA],
        name="moe_scatter_rows",
    )(h2p, pos_km)


def _sc_gather_call(ys, pos_km):
    P = pos_km.shape[0]
    W = ys.shape[1]
    nc, ns = _sc_workers()
    n = SC_ROWS
    per_w = P // (nc * ns * n)

    def body(ys_hbm, pos_hbm, yp_hbm, idx_v, rows_v, sem):
        wid = lax.axis_index("s") * nc + lax.axis_index("c")

        @pl.loop(0, per_w)
        def _(j):
            p0 = (wid * per_w + j) * n
            pltpu.sync_copy(pos_hbm.at[pl.ds(p0, n)], idx_v)
            pltpu.async_copy(ys_hbm.at[idx_v], rows_v, sem).wait()
            pltpu.sync_copy(rows_v, yp_hbm.at[pl.ds(p0, n)])

    return pl.kernel(
        body,
        out_type=jax.ShapeDtypeStruct((P, W), ys.dtype),
        mesh=_sc_mesh(),
        scratch_types=[pltpu.VMEM((n,), I32), pltpu.VMEM((n, W), ys.dtype), pltpu.SemaphoreType.DMA],
        name="moe_gather_rows",
    )(ys, pos_km)


def _ffn_kernel(te_ref, tv_ref, xs_ref, wgu_ref, bgu_ref, wd_ref, bd_ref, y_ref, wgu_bf, wd_bf):
    i = pl.program_id(0)
    valid = tv_ref[i]
    tm, half = xs_ref.shape
    d_exp = wd_ref.shape[0]

    @pl.when((i == 0) | (te_ref[i] != te_ref[jnp.maximum(i - 1, 0)]))
    def _():
        wgu_bf[...] = wgu_ref[...].astype(BF16)
        wd_bf[...] = wd_ref[...].astype(BF16)

    @pl.when(valid > 0)
    def _():
        keep = lax.broadcasted_iota(I32, (tm, half), 0) < valid
        lo, hi = _unpack_bf16_pair(xs_ref[...])
        x_lo = jnp.where(keep, lo, 0.0).astype(BF16)
        x_hi = jnp.where(keep, hi, 0.0).astype(BF16)
        d = functools.partial(jnp.dot, preferred_element_type=F32)
        nb = 256
        acc = jnp.zeros((tm, wd_ref.shape[1]), F32) + bd_ref[...]
        for c in range(0, d_exp, nb):
            gate = (d(x_lo, wgu_bf[0:half, c:c + nb]) + d(x_hi, wgu_bf[half:, c:c + nb])
                    + bgu_ref[:, c:c + nb])
            lin = (d(x_lo, wgu_bf[0:half, d_exp + c:d_exp + c + nb])
                   + d(x_hi, wgu_bf[half:, d_exp + c:d_exp + c + nb]) + bgu_ref[:, d_exp + c:d_exp + c + nb])
            gate = jnp.minimum(gate, SWIGLU_LIMIT)
            lin = jnp.clip(lin, -SWIGLU_LIMIT, SWIGLU_LIMIT)
            act = gate * jax.nn.sigmoid(SWIGLU_ALPHA * gate) * (lin + 1.0)
            acc = acc + d(act.astype(BF16), wd_bf[c:c + nb, :])
        y_ref[...] = _pack_bf16_pair(acc[:, :half], acc[:, half:])


def _ffn_call(tile_expert, tile_valid, xs, w_gu, b_gu, w_down, b_down, tm):
    R, half = xs.shape
    E, D, two_f = w_gu.shape
    d_exp = w_down.shape[1]
    grid_spec = pltpu.PrefetchScalarGridSpec(
        num_scalar_prefetch=2,
        grid=(R // tm,),
        in_specs=[
            pl.BlockSpec((tm, half), lambda i, te, tv: (i, 0)),
            pl.BlockSpec((None, D, two_f), lambda i, te, tv: (te[i], 0, 0)),
            pl.BlockSpec((None, 1, two_f), lambda i, te, tv: (te[i], 0, 0)),
            pl.BlockSpec((None, d_exp, D), lambda i, te, tv: (te[i], 0, 0)),
            pl.BlockSpec((None, 1, D), lambda i, te, tv: (te[i], 0, 0)),
        ],
        out_specs=pl.BlockSpec((tm, half), lambda i, te, tv: (i, 0)),
        scratch_shapes=[pltpu.VMEM((D, two_f), BF16), pltpu.VMEM((d_exp, D), BF16)],
    )
    return pl.pallas_call(
        _ffn_kernel,
        grid_spec=grid_spec,
        out_shape=jax.ShapeDtypeStruct((R, half), U32),
        compiler_params=pltpu.CompilerParams(
            dimension_semantics=("arbitrary",), vmem_limit_bytes=VMEM_LIMIT_BYTES),
        name="moe_ffn",
    )(tile_expert, tile_valid, xs, w_gu, b_gu.reshape(E, 1, two_f), w_down, b_down.reshape(E, 1, D))


def _final_kernel(x_ref, yp_ref, wts_ref, mod_ref, gfin_ref, *rest):
    o_ref = rest[-1]
    x = x_ref[...]
    half = x.shape[1] // 2
    w = wts_ref[...]
    lo = jnp.zeros((x.shape[0], half), F32)
    hi = jnp.zeros((x.shape[0], half), F32)
    for k in range(TOP_K):
        l, h = _unpack_bf16_pair(yp_ref[k])
        lo = lo + w[:, k:k + 1] * l
        hi = hi + w[:, k:k + 1] * h
    gate = mod_ref[5:6, :]
    x_lo = x[:, :half] + gate[:, :half] * lo
    x_hi = x[:, half:] + gate[:, half:] * hi
    ms = (jnp.sum(x_lo * x_lo, axis=-1, keepdims=True) + jnp.sum(x_hi * x_hi, axis=-1, keepdims=True)) / x.shape[1]
    inv = lax.rsqrt(ms + RMS_EPS)
    o_ref[:, :half] = x_lo * inv * gfin_ref[:, :half]
    o_ref[:, half:] = x_hi * inv * gfin_ref[:, half:]


def _final_call(x1, yp, wts, mod, g_final, tokens_per_batch, group, n_groups, prev_out):
    T_all, D = x1.shape
    T = T_all // n_groups
    tq = min(512, tokens_per_batch)
    per_b = tokens_per_batch // tq
    first = group * (T // tq)
    in_specs = [
        pl.BlockSpec((tq, D), lambda i: (first + i, 0)),
        pl.BlockSpec((TOP_K, tq, D // 2), lambda i: (0, i, 0)),
        pl.BlockSpec((tq, TOP_K), lambda i: (i, 0)),
        pl.BlockSpec((None, N_MOD, D), lambda i: ((first + i) // per_b, 0, 0)),
        pl.BlockSpec((1, D), lambda i: (0, 0)),
    ]
    args = [x1, yp, wts, mod, g_final.reshape(1, D)]
    aliases = {}
    if prev_out is not None:
        in_specs.append(pl.BlockSpec(memory_space=pl.ANY))
        args.append(prev_out)
        aliases = {len(args) - 1: 0}
    return pl.pallas_call(
        _final_kernel,
        grid=(T // tq,),
        in_specs=in_specs,
        out_specs=pl.BlockSpec((tq, D), lambda i: (first + i, 0)),
        out_shape=jax.ShapeDtypeStruct((T_all, D), F32),
        input_output_aliases=aliases,
        compiler_params=pltpu.CompilerParams(
            dimension_semantics=("arbitrary",), vmem_limit_bytes=VMEM_LIMIT_BYTES),
        name="moe_combine_final",
    )(*args)


def _group_layout(counts, n_tiles, tm):
    padded = ((counts + tm - 1) // tm) * tm
    ends = jnp.cumsum(padded)
    starts = ends - padded
    tile_row = jnp.arange(n_tiles, dtype=I32) * tm
    te = jnp.minimum(jnp.sum(tile_row[:, None] >= ends[None, :], axis=1), N_EXPERTS - 1).astype(I32)
    tv = jnp.clip(counts[te] - (tile_row - starts[te]), 0, tm).astype(I32)
    return starts, te, tv


def kernel(x, c, positions, w_ada, b_ada, g_mix, w_in, w_dw, b_dw, g_conv_ln, b_conv_ln, g_ret_norm,
           w_out, g_ffn, w_router, b_router, w_gu, b_gu, w_down, b_down, g_final):
    B, S, D = x.shape
    T = B * S
    assert w_ada.shape[0] == 1, "single-layer block: the final norm directly follows layer 0"
    xt = x
    for l in range(1):
        mod = _mod_call(c, w_ada[l], b_ada[l]).reshape(B, N_MOD, D)
        x1 = _mixer_call(xt, positions, mod, g_mix[l], w_in[l], w_dw[l], b_dw[l], g_conv_ln[l],
                         b_conv_ln[l], g_ret_norm[l], w_out[l]).reshape(T, D)
        n_groups = MOE_TOKEN_GROUPS if B % MOE_TOKEN_GROUPS == 0 else 1
        Tg = T // n_groups
        tm = 512 if Tg * TOP_K >= 512 * N_EXPERTS * 4 else 128
        n_tiles = (Tg * TOP_K) // tm + N_EXPERTS
        routed = [_route_call(x1, mod, g_ffn[l], w_router[l], b_router[l], S, g, n_groups)
                  for g in range(n_groups)]
        out = None
        for g, (h2p, idx, wts, rank, counts) in enumerate(routed):
            starts, te, tv = _group_layout(counts[0], n_tiles, tm)
            pos_km = (starts[idx] + rank).astype(I32).T.reshape(-1)
            xs = _sc_scatter_call(h2p, pos_km, n_tiles * tm)
            ys = _ffn_call(te, tv, xs, w_gu[l], b_gu[l], w_down[l], b_down[l], tm)
            yp = _sc_gather_call(ys, pos_km)
            out = _final_call(x1, yp.reshape(TOP_K, Tg, D // 2), wts, mod, g_final, S, g, n_groups, out)
        xt = out
    return xt.reshape(B, S, D)
```

```python
import functools

import numpy as np
import jax
import jax.numpy as jnp
from jax import lax
from jax.experimental import pallas as pl
from jax.experimental.pallas import tpu as pltpu
from jax.experimental.pallas import tpu_sc as plsc

F32 = jnp.float32
BF16 = jnp.bfloat16
U32 = jnp.uint32
I32 = jnp.int32

CONV_WIDTH = 31
CONV_HALO = 32
RET_HEADS = 4
RET_DIM = 128
RET_CHUNK = 128
ROPE_BASE = 10000.0
N_EXPERTS = 32
TOP_K = 4
SWIGLU_LIMIT = 7.0
SWIGLU_ALPHA = 1.702
RMS_EPS = 1e-6
LN_EPS = 1e-5
N_MOD = 6

VMEM_LIMIT_BYTES = 56 * 1024 * 1024
MOE_TOKEN_GROUPS = 2


def _split_bf16(a):
    hi = a.astype(BF16)
    lo = (a - hi.astype(F32)).astype(BF16)
    return hi, lo


def _dot3(a, b_hi, b_lo):
    a_hi, a_lo = _split_bf16(a)
    d = functools.partial(jnp.dot, preferred_element_type=F32)
    return d(a_hi, b_hi) + (d(a_hi, b_lo) + d(a_lo, b_hi))


def _pack_bf16_pair(lo, hi):
    lo_bits = lax.bitcast_convert_type(lo.astype(BF16).astype(F32), U32)
    hi_bits = lax.bitcast_convert_type(hi.astype(BF16).astype(F32), U32)
    return (lo_bits >> 16) | (hi_bits & jnp.uint32(0xFFFF0000))


def _unpack_bf16_pair(p):
    lo = lax.bitcast_convert_type(p << 16, F32)
    hi = lax.bitcast_convert_type(p & jnp.uint32(0xFFFF0000), F32)
    return lo, hi


def _mod_kernel(c_ref, whi_ref, wlo_ref, b_ref, o_ref):
    c = c_ref[...]
    c_act = c * jax.nn.sigmoid(c)
    o_ref[...] = _dot3(c_act, whi_ref[...], wlo_ref[...]) + b_ref[...]


def _mod_call(c, w_ada, b_ada):
    B, D = c.shape
    n = w_ada.shape[1]
    bn = 1024
    w_hi, w_lo = _split_bf16(w_ada)
    return pl.pallas_call(
        _mod_kernel,
        grid=(n // bn,),
        in_specs=[
            pl.BlockSpec((B, D), lambda j: (0, 0)),
            pl.BlockSpec((D, bn), lambda j: (0, j)),
            pl.BlockSpec((D, bn), lambda j: (0, j)),
            pl.BlockSpec((1, bn), lambda j: (0, j)),
        ],
        out_specs=pl.BlockSpec((B, bn), lambda j: (0, j)),
        out_shape=jax.ShapeDtypeStruct((B, n), F32),
        name="adaln_mod",
    )(c, w_hi, w_lo, b_ada.reshape(1, n))


def _retention_tables():
    h = np.arange(RET_HEADS, dtype=np.float32)
    log_gamma = np.log(1.0 - np.power(2.0, -5.0 - h)).astype(np.float32)
    idx = np.arange(RET_CHUNK, dtype=np.float32)
    diff = idx[:, None] - idx[None, :]
    causal = diff >= 0
    mask = np.where(causal[None], np.exp(log_gamma[:, None, None] * np.where(causal, diff, 0.0)[None]), 0.0)
    q_decay = np.exp(log_gamma[:, None] * (idx + 1.0))[..., None]
    k_decay = np.exp(log_gamma[:, None] * (RET_CHUNK - 1.0 - idx))[..., None]
    chunk_decay = np.exp(log_gamma * RET_CHUNK)
    return (mask.astype(np.float32), q_decay.astype(np.float32), k_decay.astype(np.float32),
            [float(v) for v in chunk_decay.astype(np.float32)])


def _mixer_kernel(chunk_decay, ts, x_ref, pos_ref, mod_ref, gmix_ref, win_ref, wdw_ref, bdw_ref,
                  gcl_ref, bcl_ref, gret_ref, wout_ref, invf_ref, dmask_ref, qdec_ref, kdec_ref,
                  o_ref, proj_ref, uext_ref, state_ref, cat_ref):
    s = pl.program_id(1)
    conv_ch = wdw_ref.shape[1]
    ret_w = RET_HEADS * RET_DIM

    @pl.when(s == 0)
    def _():
        uext_ref[0:CONV_HALO, :] = jnp.zeros((CONV_HALO, conv_ch), F32)
        state_ref[...] = jnp.zeros_like(state_ref)

    x = x_ref[...]
    sh = mod_ref[0:1, :]
    sc = mod_ref[1:2, :]
    y = x * lax.rsqrt(jnp.mean(x * x, axis=-1, keepdims=True) + RMS_EPS) * gmix_ref[...]
    h = (y * (1.0 + sc) + sh).astype(BF16)
    proj_ref[...] = jnp.dot(h, win_ref[...], preferred_element_type=F32)

    a = proj_ref[:, 0:conv_ch]
    b = proj_ref[:, conv_ch:2 * conv_ch]
    uext_ref[CONV_HALO:CONV_HALO + ts, :] = a * jax.nn.sigmoid(b)
    cb = 64
    lead = CONV_HALO - (CONV_WIDTH - 1)
    span = cb + CONV_HALO
    for r0 in range(0, ts, cb):
        for c0 in range(0, conv_ch, 128):
            xw = uext_ref[r0:r0 + span, c0:c0 + 128]
            acc = jnp.zeros((cb, 128), F32) + bdw_ref[:, c0:c0 + 128]
            for r in range(8):
                xr = xw if r == 0 else pltpu.roll(xw, span - r, 0)
                for q in range((lead + CONV_WIDTH - 1 - r) // 8 + 1):
                    j = 8 * q + r - lead
                    if 0 <= j < CONV_WIDTH:
                        acc = acc + wdw_ref[j:j + 1, c0:c0 + 128] * xr[8 * q:8 * q + cb]
            proj_ref[r0:r0 + cb, c0:c0 + 128] = acc
    rb = 128
    for r0 in range(0, ts, rb):
        acc = proj_ref[r0:r0 + rb, 0:conv_ch]
        mu = jnp.mean(acc, axis=-1, keepdims=True)
        d = acc - mu
        var = jnp.mean(d * d, axis=-1, keepdims=True)
        ln = d * lax.rsqrt(var + LN_EPS) * gcl_ref[...] + bcl_ref[...]
        cat_ref[r0:r0 + rb, 0:conv_ch] = (ln * jax.nn.sigmoid(ln)).astype(BF16)
    uext_ref[0:CONV_HALO, :] = uext_ref[ts:ts + CONV_HALO, :]

    ang = pos_ref[...].astype(F32) * invf_ref[...]
    cos2 = jnp.cos(ang)
    sin = jnp.sin(ang)
    lane = lax.broadcasted_iota(I32, ang.shape, 1)
    sin2 = jnp.where(lane < RET_DIM // 2, -sin, sin)
    q0 = 2 * conv_ch
    k0 = q0 + ret_w
    v0 = k0 + ret_w
    g0 = v0 + ret_w
    scale = RET_DIM ** -0.5
    for hd in range(RET_HEADS):
        c0 = hd * RET_DIM
        for n in range(ts // RET_CHUNK):
            r0 = n * RET_CHUNK
            rows = slice(r0, r0 + RET_CHUNK)
            cs = cos2[rows]
            sn = sin2[rows]
            q = proj_ref[rows, q0 + c0:q0 + c0 + RET_DIM]
            k = proj_ref[rows, k0 + c0:k0 + c0 + RET_DIM]
            v = proj_ref[rows, v0 + c0:v0 + c0 + RET_DIM].astype(BF16)
            g = proj_ref[rows, g0 + c0:g0 + c0 + RET_DIM]
            qr = q * cs + pltpu.roll(q, RET_DIM // 2, 1) * sn
            kr = (k * cs + pltpu.roll(k, RET_DIM // 2, 1) * sn) * scale
            st = state_ref[hd]
            scores = lax.dot_general(qr.astype(BF16), kr.astype(BF16), (((1,), (1,)), ((), ())),
                                     preferred_element_type=F32) * dmask_ref[hd]
            inner = jnp.dot(scores.astype(BF16), v, preferred_element_type=F32)
            cross = jnp.dot((qr * qdec_ref[hd]).astype(BF16), st.astype(BF16), preferred_element_type=F32)
            kv = lax.dot_general((kr * kdec_ref[hd]).astype(BF16), v, (((0,), (0,)), ((), ())),
                                 preferred_element_type=F32)
            state_ref[hd] = chunk_decay[hd] * st + kv
            r = inner + cross
            mu = jnp.mean(r, axis=-1, keepdims=True)
            d = r - mu
            var = jnp.mean(d * d, axis=-1, keepdims=True)
            rn = d * lax.rsqrt(var + LN_EPS) * gret_ref[:, c0:c0 + RET_DIM]
            cat_ref[rows, conv_ch + c0:conv_ch + c0 + RET_DIM] = (g * jax.nn.sigmoid(g) * rn).astype(BF16)

    out = jnp.dot(cat_ref[...], wout_ref[...], preferred_element_type=F32)
    o_ref[...] = x + mod_ref[2:3, :] * out


def _mixer_call(x, positions, mod, g_mix, w_in, w_dw, b_dw, g_conv_ln, b_conv_ln, g_ret_norm, w_out):
    B, S, D = x.shape
    in_cols = w_in.shape[1]
    conv_ch = w_dw.shape[1]
    ts = min(512, S)
    mask, q_decay, k_decay, chunk_decay = _retention_tables()
    half = RET_DIM // 2
    inv_freq = (ROPE_BASE ** (-np.arange(half, dtype=np.float32) / half)).astype(np.float32)
    inv_freq2 = np.concatenate([inv_freq, inv_freq])[None, :]
    w_dw_p = jnp.zeros((CONV_HALO, conv_ch), F32).at[:CONV_WIDTH].set(w_dw)
    full = lambda shape: pl.BlockSpec(shape, lambda b, s: (0,) * len(shape))
    return pl.pallas_call(
        functools.partial(_mixer_kernel, chunk_decay, ts),
        grid=(B, S // ts),
        in_specs=[
            pl.BlockSpec((None, ts, D), lambda b, s: (b, s, 0)),
            pl.BlockSpec((None, ts, 1), lambda b, s: (b, s, 0)),
            pl.BlockSpec((None, N_MOD, D), lambda b, s: (b, 0, 0)),
            full((1, D)),
            full((D, in_cols)),
            full((CONV_HALO, conv_ch)),
            full((1, conv_ch)),
            full((1, conv_ch)),
            full((1, conv_ch)),
            full((1, RET_HEADS * RET_DIM)),
            full((conv_ch + RET_HEADS * RET_DIM, D)),
            full((1, RET_DIM)),
            full((RET_HEADS, RET_CHUNK, RET_CHUNK)),
            full((RET_HEADS, RET_CHUNK, 1)),
            full((RET_HEADS, RET_CHUNK, 1)),
        ],
        out_specs=pl.BlockSpec((None, ts, D), lambda b, s: (b, s, 0)),
        out_shape=jax.ShapeDtypeStruct((B, S, D), F32),
        scratch_shapes=[
            pltpu.VMEM((ts, in_cols), F32),
            pltpu.VMEM((CONV_HALO + ts, conv_ch), F32),
            pltpu.VMEM((RET_HEADS, RET_DIM, RET_DIM), F32),
            pltpu.VMEM((ts, conv_ch + RET_HEADS * RET_DIM), BF16),
        ],
        compiler_params=pltpu.CompilerParams(
            dimension_semantics=("arbitrary", "arbitrary"), vmem_limit_bytes=VMEM_LIMIT_BYTES),
        name="hybrid_mixer",
    )(x, positions.reshape(B, S, 1), mod, g_mix.reshape(1, D), w_in.astype(BF16), w_dw_p,
      b_dw.reshape(1, -1), g_conv_ln.reshape(1, -1), b_conv_ln.reshape(1, -1), g_ret_norm.reshape(1, -1),
      w_out.astype(BF16), jnp.asarray(inv_freq2), jnp.asarray(mask), jnp.asarray(q_decay),
      jnp.asarray(k_decay))


def _route_kernel(x_ref, mod_ref, gffn_ref, wrhi_ref, wrlo_ref, br_ref, ltri_ref,
                  h2p_ref, idx_ref, wts_ref, rank_ref, cnt_ref, carry_ref):
    i = pl.program_id(0)

    @pl.when(i == 0)
    def _():
        carry_ref[...] = jnp.zeros_like(carry_ref)

    x = x_ref[...]
    half = x.shape[1] // 2
    y = x * lax.rsqrt(jnp.mean(x * x, axis=-1, keepdims=True) + RMS_EPS) * gffn_ref[...]
    h2 = y * (1.0 + mod_ref[4:5, :]) + mod_ref[3:4, :]
    h2p_ref[...] = _pack_bf16_pair(h2[:, :half], h2[:, half:])

    logits = _dot3(h2, wrhi_ref[...], wrlo_ref[...]) + br_ref[...]
    eid = lax.broadcasted_iota(I32, logits.shape, 1)
    vals, sels, idxs = [], [], []
    l = logits
    for _ in range(TOP_K):
        m = jnp.max(l, axis=-1, keepdims=True)
        ik = jnp.min(jnp.where(l == m, eid, N_EXPERTS), axis=-1, keepdims=True)
        sel = eid == ik
        vals.append(m)
        sels.append(sel)
        idxs.append(ik)
        l = jnp.where(sel, -jnp.inf, l)
    exps = [jnp.exp(v - vals[0]) for v in vals]
    denom = exps[0] + exps[1] + exps[2] + exps[3]
    member = jnp.zeros(logits.shape, F32)
    for sel in sels:
        member = member + sel.astype(F32)
    before = jnp.dot(ltri_ref[...], member.astype(BF16), preferred_element_type=F32) + carry_ref[...]
    ranks = [jnp.sum(jnp.where(sel, before, 0.0), axis=-1, keepdims=True) for sel in sels]
    carry_ref[...] = carry_ref[...] + jnp.sum(member, axis=0, keepdims=True)
    idx_ref[...] = jnp.concatenate(idxs, axis=1)
    wts_ref[...] = jnp.concatenate([e / denom for e in exps], axis=1)
    rank_ref[...] = jnp.concatenate(ranks, axis=1).astype(I32)
    cnt_ref[...] = carry_ref[...].astype(I32)


def _route_call(x1, mod, g_ffn, w_router, b_router, tokens_per_batch, group, n_groups):
    D = x1.shape[1]
    T = x1.shape[0] // n_groups
    tr = min(512, tokens_per_batch)
    per_b = tokens_per_batch // tr
    first = group * (T // tr)
    wr_hi, wr_lo = _split_bf16(w_router)
    ltri = jnp.asarray(np.tril(np.ones((tr, tr), np.float32), -1), BF16)
    full = lambda shape: pl.BlockSpec(shape, lambda i: (0,) * len(shape))
    return pl.pallas_call(
        _route_kernel,
        grid=(T // tr,),
        in_specs=[
            pl.BlockSpec((tr, D), lambda i: (first + i, 0)),
            pl.BlockSpec((None, N_MOD, D), lambda i: ((first + i) // per_b, 0, 0)),
            full((1, D)),
            full((D, N_EXPERTS)),
            full((D, N_EXPERTS)),
            full((1, N_EXPERTS)),
            full((tr, tr)),
        ],
        out_specs=[
            pl.BlockSpec((tr, D // 2), lambda i: (i, 0)),
            pl.BlockSpec((tr, TOP_K), lambda i: (i, 0)),
            pl.BlockSpec((tr, TOP_K), lambda i: (i, 0)),
            pl.BlockSpec((tr, TOP_K), lambda i: (i, 0)),
            full((1, N_EXPERTS)),
        ],
        out_shape=[
            jax.ShapeDtypeStruct((T, D // 2), U32),
            jax.ShapeDtypeStruct((T, TOP_K), I32),
            jax.ShapeDtypeStruct((T, TOP_K), F32),
            jax.ShapeDtypeStruct((T, TOP_K), I32),
            jax.ShapeDtypeStruct((1, N_EXPERTS), I32),
        ],
        scratch_shapes=[pltpu.VMEM((1, N_EXPERTS), F32)],
        compiler_params=pltpu.CompilerParams(
            dimension_semantics=("arbitrary",), vmem_limit_bytes=VMEM_LIMIT_BYTES),
        name="moe_route",
    )(x1, mod, g_ffn.reshape(1, D), wr_hi, wr_lo, b_router.reshape(1, N_EXPERTS), ltri)


SC_ROWS = 128


V7X_SC_CORES = 2
V7X_SC_SUBCORES = 16


def _sc_workers():
    return V7X_SC_CORES, V7X_SC_SUBCORES


def _sc_mesh():
    return plsc.VectorSubcoreMesh(core_axis_name="c", subcore_axis_name="s",
                                  num_cores=V7X_SC_CORES, num_subcores=V7X_SC_SUBCORES)


def _sc_scatter_call(h2p, pos_km, n_rows):
    T, W = h2p.shape
    nc, ns = _sc_workers()
    n = SC_ROWS
    per_w = T // (nc * ns * n)

    def body(h2p_hbm, pos_hbm, xs_hbm, i0, i1, i2, i3, rows_v, sem):
        wid = lax.axis_index("s") * nc + lax.axis_index("c")
        idx_refs = (i0, i1, i2, i3)

        @pl.loop(0, per_w)
        def _(j):
            t0 = (wid * per_w + j) * n
            pltpu.sync_copy(h2p_hbm.at[pl.ds(t0, n)], rows_v)
            for k in range(TOP_K):
                pltpu.sync_copy(pos_hbm.at[pl.ds(k * T + t0, n)], idx_refs[k])
            copies = [pltpu.async_copy(rows_v, xs_hbm.at[idx_refs[k]], sem) for k in range(TOP_K)]
            for cp in copies:
                cp.wait()

    return pl.kernel(
        body,
        out_type=jax.ShapeDtypeStruct((n_rows, W), h2p.dtype),
        mesh=_sc_mesh(),
        scratch_types=[pltpu.VMEM((n,), I32)] * TOP_K + [pltpu.VMEM((n, W), h2p.dtype), pltpu.SemaphoreType.DMA],
        name="moe_scatter_rows",
    )(h2p, pos_km)


def _sc_gather_call(ys, pos_km):
    P = pos_km.shape[0]
    W = ys.shape[1]
    nc, ns = _sc_workers()
    n = SC_ROWS
    per_w = P // (nc * ns * n)

    def body(ys_hbm, pos_hbm, yp_hbm, idx_v, rows_v, sem):
        wid = lax.axis_index("s") * nc + lax.axis_index("c")

        @pl.loop(0, per_w)
        def _(j):
            p0 = (wid * per_w + j) * n
            pltpu.sync_copy(pos_hbm.at[pl.ds(p0, n)], idx_v)
            pltpu.async_copy(ys_hbm.at[idx_v], rows_v, sem).wait()
            pltpu.sync_copy(rows_v, yp_hbm.at[pl.ds(p0, n)])

    return pl.kernel(
        body,
        out_type=jax.ShapeDtypeStruct((P, W), ys.dtype),
        mesh=_sc_mesh(),
        scratch_types=[pltpu.VMEM((n,), I32), pltpu.VMEM((n, W), ys.dtype), pltpu.SemaphoreType.DMA],
        name="moe_gather_rows",
    )(ys, pos_km)


def _ffn_kernel(te_ref, tv_ref, nx_ref, xs_ref, wgu_hbm, bgu_ref, wd_hbm, bd_ref, y_ref,
                wgu_f32, wd_f32, wgu_bf, wd_bf, sems):
    i = pl.program_id(0)
    valid = tv_ref[i]
    tm, half = xs_ref.shape
    d_exp = wd_bf.shape[0]

    def weight_copies(e):
        return (pltpu.make_async_copy(wgu_hbm.at[e], wgu_f32, sems.at[0]),
                pltpu.make_async_copy(wd_hbm.at[e], wd_f32, sems.at[1]))

    @pl.when(i == 0)
    def _():
        for cp in weight_copies(te_ref[0]):
            cp.start()

    @pl.when(((i == 0) | (te_ref[i] != te_ref[jnp.maximum(i - 1, 0)])) & (valid > 0))
    def _():
        for cp in weight_copies(te_ref[i]):
            cp.wait()
        wgu_bf[...] = wgu_f32[...].astype(BF16)
        wd_bf[...] = wd_f32[...].astype(BF16)

        @pl.when(nx_ref[i] >= 0)
        def _():
            for cp in weight_copies(nx_ref[i]):
                cp.start()

    @pl.when(valid > 0)
    def _():
        keep = lax.broadcasted_iota(I32, (tm, half), 0) < valid
        lo, hi = _unpack_bf16_pair(xs_ref[...])
        x_lo = jnp.where(keep, lo, 0.0).astype(BF16)
        x_hi = jnp.where(keep, hi, 0.0).astype(BF16)
        d = functools.partial(jnp.dot, preferred_element_type=F32)
        nb = 256
        acc = jnp.zeros((tm, wd_bf.shape[1]), F32) + bd_ref[...]
        for c in range(0, d_exp, nb):
            gate = (d(x_lo, wgu_bf[0:half, c:c + nb]) + d(x_hi, wgu_bf[half:, c:c + nb])
                    + bgu_ref[:, c:c + nb])
            lin = (d(x_lo, wgu_bf[0:half, d_exp + c:d_exp + c + nb])
                   + d(x_hi, wgu_bf[half:, d_exp + c:d_exp + c + nb]) + bgu_ref[:, d_exp + c:d_exp + c + nb])
            gate = jnp.minimum(gate, SWIGLU_LIMIT)
            lin = jnp.clip(lin, -SWIGLU_LIMIT, SWIGLU_LIMIT)
            act = gate * jax.nn.sigmoid(SWIGLU_ALPHA * gate) * (lin + 1.0)
            acc = acc + d(act.astype(BF16), wd_bf[c:c + nb, :])
        y_ref[...] = _pack_bf16_pair(acc[:, :half], acc[:, half:])


def _ffn_call(tile_expert, tile_valid, tile_next, xs, w_gu, b_gu, w_down, b_down, tm):
    R, half = xs.shape
    E, D, two_f = w_gu.shape
    d_exp = w_down.shape[1]
    grid_spec = pltpu.PrefetchScalarGridSpec(
        num_scalar_prefetch=3,
        grid=(R // tm,),
        in_specs=[
            pl.BlockSpec((tm, half), lambda i, te, tv, nx: (i, 0)),
            pl.BlockSpec(memory_space=pl.ANY),
            pl.BlockSpec((None, 1, two_f), lambda i, te, tv, nx: (te[i], 0, 0)),
            pl.BlockSpec(memory_space=pl.ANY),
            pl.BlockSpec((None, 1, D), lambda i, te, tv, nx: (te[i], 0, 0)),
        ],
        out_specs=pl.BlockSpec((tm, half), lambda i, te, tv, nx: (i, 0)),
        scratch_shapes=[
            pltpu.VMEM((D, two_f), F32), pltpu.VMEM((d_exp, D), F32),
            pltpu.VMEM((D, two_f), BF16), pltpu.VMEM((d_exp, D), BF16),
            pltpu.SemaphoreType.DMA((2,)),
        ],
    )
    return pl.pallas_call(
        _ffn_kernel,
        grid_spec=grid_spec,
        out_shape=jax.ShapeDtypeStruct((R, half), U32),
        compiler_params=pltpu.CompilerParams(
            dimension_semantics=("arbitrary",), vmem_limit_bytes=VMEM_LIMIT_BYTES),
        name="moe_ffn",
    )(tile_expert, tile_valid, tile_next, xs, w_gu, b_gu.reshape(E, 1, two_f), w_down,
      b_down.reshape(E, 1, D))


def _final_kernel(x_ref, yp_ref, wts_ref, mod_ref, gfin_ref, *rest):
    o_ref = rest[-1]
    x = x_ref[...]
    half = x.shape[1] // 2
    w = wts_ref[...]
    lo = jnp.zeros((x.shape[0], half), F32)
    hi = jnp.zeros((x.shape[0], half), F32)
    for k in range(TOP_K):
        l, h = _unpack_bf16_pair(yp_ref[k])
        lo = lo + w[:, k:k + 1] * l
        hi = hi + w[:, k:k + 1] * h
    gate = mod_ref[5:6, :]
    x_lo = x[:, :half] + gate[:, :half] * lo
    x_hi = x[:, half:] + gate[:, half:] * hi
    ms = (jnp.sum(x_lo * x_lo, axis=-1, keepdims=True) + jnp.sum(x_hi * x_hi, axis=-1, keepdims=True)) / x.shape[1]
    inv = lax.rsqrt(ms + RMS_EPS)
    o_ref[:, :half] = x_lo * inv * gfin_ref[:, :half]
    o_ref[:, half:] = x_hi * inv * gfin_ref[:, half:]


def _final_call(x1, yp, wts, mod, g_final, tokens_per_batch, group, n_groups, prev_out):
    T_all, D = x1.shape
    T = T_all // n_groups
    tq = min(512, tokens_per_batch)
    per_b = tokens_per_batch // tq
    first = group * (T // tq)
    in_specs = [
        pl.BlockSpec((tq, D), lambda i: (first + i, 0)),
        pl.BlockSpec((TOP_K, tq, D // 2), lambda i: (0, i, 0)),
        pl.BlockSpec((tq, TOP_K), lambda i: (i, 0)),
        pl.BlockSpec((None, N_MOD, D), lambda i: ((first + i) // per_b, 0, 0)),
        pl.BlockSpec((1, D), lambda i: (0, 0)),
    ]
    args = [x1, yp, wts, mod, g_final.reshape(1, D)]
    aliases = {}
    if prev_out is not None:
        in_specs.append(pl.BlockSpec(memory_space=pl.ANY))
        args.append(prev_out)
        aliases = {len(args) - 1: 0}
    return pl.pallas_call(
        _final_kernel,
        grid=(T // tq,),
        in_specs=in_specs,
        out_specs=pl.BlockSpec((tq, D), lambda i: (first + i, 0)),
        out_shape=jax.ShapeDtypeStruct((T_all, D), F32),
        input_output_aliases=aliases,
        compiler_params=pltpu.CompilerParams(
            dimension_semantics=("arbitrary",), vmem_limit_bytes=VMEM_LIMIT_BYTES),
        name="moe_combine_final",
    )(*args)


def _group_layout(counts, n_tiles, tm):
    padded = ((counts + tm - 1) // tm) * tm
    ends = jnp.cumsum(padded)
    starts = ends - padded
    tile_row = jnp.arange(n_tiles, dtype=I32) * tm
    te = jnp.minimum(jnp.sum(tile_row[:, None] >= ends[None, :], axis=1), N_EXPERTS - 1).astype(I32)
    tv = jnp.clip(counts[te] - (tile_row - starts[te]), 0, tm).astype(I32)
    eids = jnp.arange(N_EXPERTS, dtype=I32)
    later = (eids[None, :] > eids[:, None]) & (counts[None, :] > 0)
    nxt = jnp.min(jnp.where(later, eids[None, :], N_EXPERTS), axis=1)
    nx = jnp.where(nxt < N_EXPERTS, nxt, -1).astype(I32)[te]
    return starts, te, tv, nx


def kernel(x, c, positions, w_ada, b_ada, g_mix, w_in, w_dw, b_dw, g_conv_ln, b_conv_ln, g_ret_norm,
           w_out, g_ffn, w_router, b_router, w_gu, b_gu, w_down, b_down, g_final):
    B, S, D = x.shape
    T = B * S
    assert w_ada.shape[0] == 1, "single-layer block: the final norm directly follows layer 0"
    xt = x
    for l in range(1):
        mod = _mod_call(c, w_ada[l], b_ada[l]).reshape(B, N_MOD, D)
        x1 = _mixer_call(xt, positions, mod, g_mix[l], w_in[l], w_dw[l], b_dw[l], g_conv_ln[l],
                         b_conv_ln[l], g_ret_norm[l], w_out[l]).reshape(T, D)
        n_groups = MOE_TOKEN_GROUPS if B % MOE_TOKEN_GROUPS == 0 else 1
        Tg = T // n_groups
        tm = 512 if Tg * TOP_K >= 512 * N_EXPERTS * 4 else 128
        n_tiles = (Tg * TOP_K) // tm + N_EXPERTS
        routed = [_route_call(x1, mod, g_ffn[l], w_router[l], b_router[l], S, g, n_groups)
                  for g in range(n_groups)]
        out = None
        for g, (h2p, idx, wts, rank, counts) in enumerate(routed):
            starts, te, tv, nx = _group_layout(counts[0], n_tiles, tm)
            pos_km = (starts[idx] + rank).astype(I32).T.reshape(-1)
            xs = _sc_scatter_call(h2p, pos_km, n_tiles * tm)
            ys = _ffn_call(te, tv, nx, xs, w_gu[l], b_gu[l], w_down[l], b_down[l], tm)
            yp = _sc_gather_call(ys, pos_km)
            out = _final_call(x1, yp.reshape(TOP_K, Tg, D // 2), wts, mod, g_final, S, g, n_groups, out)
        xt = out
    return xt.reshape(B, S, D)
```

```python
import functools

import numpy as np
import jax
import jax.numpy as jnp
from jax import lax
from jax.experimental import pallas as pl
from jax.experimental.pallas import tpu as pltpu
from jax.experimental.pallas import tpu_sc as plsc

F32 = jnp.float32
BF16 = jnp.bfloat16
U32 = jnp.uint32
I32 = jnp.int32

CONV_WIDTH = 31
CONV_HALO = 32
RET_HEADS = 4
RET_DIM = 128
RET_CHUNK = 128
ROPE_BASE = 10000.0
N_EXPERTS = 32
TOP_K = 4
SWIGLU_LIMIT = 7.0
SWIGLU_ALPHA = 1.702
RMS_EPS = 1e-6
LN_EPS = 1e-5
N_MOD = 6

VMEM_LIMIT_BYTES = 56 * 1024 * 1024
MOE_TOKEN_GROUPS = 2


def _split_bf16(a):
    hi = a.astype(BF16)
    lo = (a - hi.astype(F32)).astype(BF16)
    return hi, lo


def _dot3(a, b_hi, b_lo):
    a_hi, a_lo = _split_bf16(a)
    d = functools.partial(jnp.dot, preferred_element_type=F32)
    return d(a_hi, b_hi) + (d(a_hi, b_lo) + d(a_lo, b_hi))


def _pack_bf16_pair(lo, hi):
    lo_bits = lax.bitcast_convert_type(lo.astype(BF16).astype(F32), U32)
    hi_bits = lax.bitcast_convert_type(hi.astype(BF16).astype(F32), U32)
    return (lo_bits >> 16) | (hi_bits & jnp.uint32(0xFFFF0000))


def _unpack_bf16_pair(p):
    lo = lax.bitcast_convert_type(p << 16, F32)
    hi = lax.bitcast_convert_type(p & jnp.uint32(0xFFFF0000), F32)
    return lo, hi


def _mod_kernel(c_ref, whi_ref, wlo_ref, b_ref, o_ref):
    c = c_ref[...]
    c_act = c * jax.nn.sigmoid(c)
    o_ref[...] = _dot3(c_act, whi_ref[...], wlo_ref[...]) + b_ref[...]


def _mod_call(c, w_ada, b_ada):
    B, D = c.shape
    n = w_ada.shape[1]
    bn = 1024
    w_hi, w_lo = _split_bf16(w_ada)
    return pl.pallas_call(
        _mod_kernel,
        grid=(n // bn,),
        in_specs=[
            pl.BlockSpec((B, D), lambda j: (0, 0)),
            pl.BlockSpec((D, bn), lambda j: (0, j)),
            pl.BlockSpec((D, bn), lambda j: (0, j)),
            pl.BlockSpec((1, bn), lambda j: (0, j)),
        ],
        out_specs=pl.BlockSpec((B, bn), lambda j: (0, j)),
        out_shape=jax.ShapeDtypeStruct((B, n), F32),
        name="adaln_mod",
    )(c, w_hi, w_lo, b_ada.reshape(1, n))


def _retention_tables():
    h = np.arange(RET_HEADS, dtype=np.float32)
    log_gamma = np.log(1.0 - np.power(2.0, -5.0 - h)).astype(np.float32)
    idx = np.arange(RET_CHUNK, dtype=np.float32)
    diff = idx[:, None] - idx[None, :]
    causal = diff >= 0
    mask = np.where(causal[None], np.exp(log_gamma[:, None, None] * np.where(causal, diff, 0.0)[None]), 0.0)
    q_decay = np.exp(log_gamma[:, None] * (idx + 1.0))[..., None]
    k_decay = np.exp(log_gamma[:, None] * (RET_CHUNK - 1.0 - idx))[..., None]
    chunk_decay = np.exp(log_gamma * RET_CHUNK)
    return (mask.astype(np.float32), q_decay.astype(np.float32), k_decay.astype(np.float32),
            [float(v) for v in chunk_decay.astype(np.float32)])


def _mixer_kernel(chunk_decay, ts, x_ref, pos_ref, mod_ref, gmix_ref, win_ref, wdw_ref, bdw_ref,
                  gcl_ref, bcl_ref, gret_ref, wout_ref, invf_ref, dmask_ref, qdec_ref, kdec_ref,
                  o_ref, proj_ref, uext_ref, state_ref, cat_ref):
    s = pl.program_id(1)
    conv_ch = wdw_ref.shape[1]
    ret_w = RET_HEADS * RET_DIM

    @pl.when(s == 0)
    def _():
        uext_ref[0:CONV_HALO, :] = jnp.zeros((CONV_HALO, conv_ch), F32)
        state_ref[...] = jnp.zeros_like(state_ref)

    x = x_ref[...]
    sh = mod_ref[0:1, :]
    sc = mod_ref[1:2, :]
    y = x * lax.rsqrt(jnp.mean(x * x, axis=-1, keepdims=True) + RMS_EPS) * gmix_ref[...]
    h = (y * (1.0 + sc) + sh).astype(BF16)
    proj_ref[...] = jnp.dot(h, win_ref[...], preferred_element_type=F32)

    a = proj_ref[:, 0:conv_ch]
    b = proj_ref[:, conv_ch:2 * conv_ch]
    uext_ref[CONV_HALO:CONV_HALO + ts, :] = a * jax.nn.sigmoid(b)
    cb = 64
    lead = CONV_HALO - (CONV_WIDTH - 1)
    span = cb + CONV_HALO
    for r0 in range(0, ts, cb):
        for c0 in range(0, conv_ch, 128):
            xw = uext_ref[r0:r0 + span, c0:c0 + 128]
            acc = jnp.zeros((cb, 128), F32) + bdw_ref[:, c0:c0 + 128]
            for r in range(8):
                xr = xw if r == 0 else pltpu.roll(xw, span - r, 0)
                for q in range((lead + CONV_WIDTH - 1 - r) // 8 + 1):
                    j = 8 * q + r - lead
                    if 0 <= j < CONV_WIDTH:
                        acc = acc + wdw_ref[j:j + 1, c0:c0 + 128] * xr[8 * q:8 * q + cb]
            proj_ref[r0:r0 + cb, c0:c0 + 128] = acc
    rb = 128
    for r0 in range(0, ts, rb):
        acc = proj_ref[r0:r0 + rb, 0:conv_ch]
        mu = jnp.mean(acc, axis=-1, keepdims=True)
        d = acc - mu
        var = jnp.mean(d * d, axis=-1, keepdims=True)
        ln = d * lax.rsqrt(var + LN_EPS) * gcl_ref[...] + bcl_ref[...]
        cat_ref[r0:r0 + rb, 0:conv_ch] = (ln * jax.nn.sigmoid(ln)).astype(BF16)
    uext_ref[0:CONV_HALO, :] = uext_ref[ts:ts + CONV_HALO, :]

    ang = pos_ref[...].astype(F32) * invf_ref[...]
    cos2 = jnp.cos(ang)
    sin = jnp.sin(ang)
    lane = lax.broadcasted_iota(I32, ang.shape, 1)
    sin2 = jnp.where(lane < RET_DIM // 2, -sin, sin)
    q0 = 2 * conv_ch
    k0 = q0 + ret_w
    v0 = k0 + ret_w
    g0 = v0 + ret_w
    scale = RET_DIM ** -0.5
    for hd in range(RET_HEADS):
        c0 = hd * RET_DIM
        for n in range(ts // RET_CHUNK):
            r0 = n * RET_CHUNK
            rows = slice(r0, r0 + RET_CHUNK)
            cs = cos2[rows]
            sn = sin2[rows]
            q = proj_ref[rows, q0 + c0:q0 + c0 + RET_DIM]
            k = proj_ref[rows, k0 + c0:k0 + c0 + RET_DIM]
            v = proj_ref[rows, v0 + c0:v0 + c0 + RET_DIM].astype(BF16)
            g = proj_ref[rows, g0 + c0:g0 + c0 + RET_DIM]
            qr = q * cs + pltpu.roll(q, RET_DIM // 2, 1) * sn
            kr = (k * cs + pltpu.roll(k, RET_DIM // 2, 1) * sn) * scale
            st = state_ref[hd]
            scores = lax.dot_general(qr.astype(BF16), kr.astype(BF16), (((1,), (1,)), ((), ())),
                                     preferred_element_type=F32) * dmask_ref[hd]
            inner = jnp.dot(scores.astype(BF16), v, preferred_element_type=F32)
            cross = jnp.dot((qr * qdec_ref[hd]).astype(BF16), st.astype(BF16), preferred_element_type=F32)
            kv = lax.dot_general((kr * kdec_ref[hd]).astype(BF16), v, (((0,), (0,)), ((), ())),
                                 preferred_element_type=F32)
            state_ref[hd] = chunk_decay[hd] * st + kv
            r = inner + cross
            mu = jnp.mean(r, axis=-1, keepdims=True)
            d = r - mu
            var = jnp.mean(d * d, axis=-1, keepdims=True)
            rn = d * lax.rsqrt(var + LN_EPS) * gret_ref[:, c0:c0 + RET_DIM]
            cat_ref[rows, conv_ch + c0:conv_ch + c0 + RET_DIM] = (g * jax.nn.sigmoid(g) * rn).astype(BF16)

    out = jnp.dot(cat_ref[...], wout_ref[...], preferred_element_type=F32)
    o_ref[...] = x + mod_ref[2:3, :] * out


def _mixer_call(x, positions, mod, g_mix, w_in, w_dw, b_dw, g_conv_ln, b_conv_ln, g_ret_norm, w_out):
    B, S, D = x.shape
    in_cols = w_in.shape[1]
    conv_ch = w_dw.shape[1]
    ts = min(512, S)
    mask, q_decay, k_decay, chunk_decay = _retention_tables()
    half = RET_DIM // 2
    inv_freq = (ROPE_BASE ** (-np.arange(half, dtype=np.float32) / half)).astype(np.float32)
    inv_freq2 = np.concatenate([inv_freq, inv_freq])[None, :]
    w_dw_p = jnp.zeros((CONV_HALO, conv_ch), F32).at[:CONV_WIDTH].set(w_dw)
    full = lambda shape: pl.BlockSpec(shape, lambda b, s: (0,) * len(shape))
    return pl.pallas_call(
        functools.partial(_mixer_kernel, chunk_decay, ts),
        grid=(B, S // ts),
        in_specs=[
            pl.BlockSpec((None, ts, D), lambda b, s: (b, s, 0)),
            pl.BlockSpec((None, ts, 1), lambda b, s: (b, s, 0)),
            pl.BlockSpec((None, N_MOD, D), lambda b, s: (b, 0, 0)),
            full((1, D)),
            full((D, in_cols)),
            full((CONV_HALO, conv_ch)),
            full((1, conv_ch)),
            full((1, conv_ch)),
            full((1, conv_ch)),
            full((1, RET_HEADS * RET_DIM)),
            full((conv_ch + RET_HEADS * RET_DIM, D)),
            full((1, RET_DIM)),
            full((RET_HEADS, RET_CHUNK, RET_CHUNK)),
            full((RET_HEADS, RET_CHUNK, 1)),
            full((RET_HEADS, RET_CHUNK, 1)),
        ],
        out_specs=pl.BlockSpec((None, ts, D), lambda b, s: (b, s, 0)),
        out_shape=jax.ShapeDtypeStruct((B, S, D), F32),
        scratch_shapes=[
            pltpu.VMEM((ts, in_cols), F32),
            pltpu.VMEM((CONV_HALO + ts, conv_ch), F32),
            pltpu.VMEM((RET_HEADS, RET_DIM, RET_DIM), F32),
            pltpu.VMEM((ts, conv_ch + RET_HEADS * RET_DIM), BF16),
        ],
        compiler_params=pltpu.CompilerParams(
            dimension_semantics=("arbitrary", "arbitrary"), vmem_limit_bytes=VMEM_LIMIT_BYTES),
        name="hybrid_mixer",
    )(x, positions.reshape(B, S, 1), mod, g_mix.reshape(1, D), w_in.astype(BF16), w_dw_p,
      b_dw.reshape(1, -1), g_conv_ln.reshape(1, -1), b_conv_ln.reshape(1, -1), g_ret_norm.reshape(1, -1),
      w_out.astype(BF16), jnp.asarray(inv_freq2), jnp.asarray(mask), jnp.asarray(q_decay),
      jnp.asarray(k_decay))


def _route_kernel(x_ref, mod_ref, gffn_ref, wr2_ref, wrhi_ref, br_ref, utri_ref,
                  h2p_ref, idx_ref, wts_ref, rank_ref, cnt_ref, carry_ref):
    i = pl.program_id(0)

    @pl.when(i == 0)
    def _():
        carry_ref[...] = jnp.zeros_like(carry_ref)

    x = x_ref[...]
    tr = x.shape[0]
    half = x.shape[1] // 2
    y = x * lax.rsqrt(jnp.mean(x * x, axis=-1, keepdims=True) + RMS_EPS) * gffn_ref[...]
    h2 = y * (1.0 + mod_ref[4:5, :]) + mod_ref[3:4, :]
    h2p_ref[...] = _pack_bf16_pair(h2[:, :half], h2[:, half:])

    h_hi, h_lo = _split_bf16(h2)
    nt = (((1,), (1,)), ((), ()))
    r = lax.dot_general(wr2_ref[...], h_hi, nt, preferred_element_type=F32)
    r2 = lax.dot_general(wrhi_ref[...], h_lo, nt, preferred_element_type=F32)
    l = r[:N_EXPERTS] + (r[N_EXPERTS:] + r2) + br_ref[...]
    eid = lax.broadcasted_iota(I32, l.shape, 0)
    vals, sels, idxs = [], [], []
    for _ in range(TOP_K):
        m = jnp.max(l, axis=0, keepdims=True)
        ik = jnp.min(jnp.where(l == m, eid, N_EXPERTS), axis=0, keepdims=True)
        sel = eid == ik
        vals.append(m)
        sels.append(sel)
        idxs.append(ik)
        l = jnp.where(sel, -jnp.inf, l)
    exps = [jnp.exp(v - vals[0]) for v in vals]
    denom = exps[0] + exps[1] + exps[2] + exps[3]
    member = jnp.zeros(l.shape, F32)
    for sel in sels:
        member = member + sel.astype(F32)
    before = jnp.dot(member.astype(BF16), utri_ref[...], preferred_element_type=F32) + carry_ref[...]
    ranks = [jnp.sum(jnp.where(sel, before, 0.0), axis=0, keepdims=True) for sel in sels]
    carry_ref[...] = carry_ref[...] + jnp.sum(member, axis=1, keepdims=True)
    cnt_ref[...] = carry_ref[...].astype(I32)

    def rows(pieces, n_rows, dtype):
        rid = lax.broadcasted_iota(I32, (n_rows, tr), 0)
        out = jnp.zeros((n_rows, tr), dtype)
        for k, p in enumerate(pieces):
            out = jnp.where(rid == k, p.astype(dtype), out)
        return out

    idx_ref[...] = rows(idxs, idx_ref.shape[0], I32)
    rank_ref[...] = rows(ranks, rank_ref.shape[0], I32)
    wts_ref[...] = rows([e / denom for e in exps], wts_ref.shape[1], F32).T


ROUTE_ROWS = 8
ROUTE_LANES = 128


def _route_call(x1, mod, g_ffn, w_router, b_router, tokens_per_batch, group, n_groups):
    D = x1.shape[1]
    T = x1.shape[0] // n_groups
    tr = min(512, tokens_per_batch)
    per_b = tokens_per_batch // tr
    first = group * (T // tr)
    wr_hi, wr_lo = _split_bf16(w_router.T)
    utri = jnp.asarray(np.triu(np.ones((tr, tr), np.float32), 1), BF16)
    full = lambda shape: pl.BlockSpec(shape, lambda i: (0,) * len(shape))
    return pl.pallas_call(
        _route_kernel,
        grid=(T // tr,),
        in_specs=[
            pl.BlockSpec((tr, D), lambda i: (first + i, 0)),
            pl.BlockSpec((None, N_MOD, D), lambda i: ((first + i) // per_b, 0, 0)),
            full((1, D)),
            full((2 * N_EXPERTS, D)),
            full((N_EXPERTS, D)),
            full((N_EXPERTS, 1)),
            full((tr, tr)),
        ],
        out_specs=[
            pl.BlockSpec((tr, D // 2), lambda i: (i, 0)),
            pl.BlockSpec((ROUTE_ROWS, tr), lambda i: (0, i)),
            pl.BlockSpec((tr, ROUTE_LANES), lambda i: (i, 0)),
            pl.BlockSpec((ROUTE_ROWS, tr), lambda i: (0, i)),
            full((N_EXPERTS, 1)),
        ],
        out_shape=[
            jax.ShapeDtypeStruct((T, D // 2), U32),
            jax.ShapeDtypeStruct((ROUTE_ROWS, T), I32),
            jax.ShapeDtypeStruct((T, ROUTE_LANES), F32),
            jax.ShapeDtypeStruct((ROUTE_ROWS, T), I32),
            jax.ShapeDtypeStruct((N_EXPERTS, 1), I32),
        ],
        scratch_shapes=[pltpu.VMEM((N_EXPERTS, 1), F32)],
        compiler_params=pltpu.CompilerParams(
            dimension_semantics=("arbitrary",), vmem_limit_bytes=VMEM_LIMIT_BYTES),
        name="moe_route",
    )(x1, mod, g_ffn.reshape(1, D), jnp.concatenate([wr_hi, wr_lo], axis=0), wr_hi,
      b_router.reshape(N_EXPERTS, 1), utri)


SC_ROWS = 128


V7X_SC_CORES = 2
V7X_SC_SUBCORES = 16


def _sc_workers():
    return V7X_SC_CORES, V7X_SC_SUBCORES


def _sc_mesh():
    return plsc.VectorSubcoreMesh(core_axis_name="c", subcore_axis_name="s",
                                  num_cores=V7X_SC_CORES, num_subcores=V7X_SC_SUBCORES)


def _sc_scatter_call(h2p, pos_km, n_rows):
    T, W = h2p.shape
    nc, ns = _sc_workers()
    n = SC_ROWS
    per_w = T // (nc * ns * n)

    def body(h2p_hbm, pos_hbm, xs_hbm, i0, i1, i2, i3, rows_v, sem):
        wid = lax.axis_index("s") * nc + lax.axis_index("c")
        idx_refs = (i0, i1, i2, i3)

        @pl.loop(0, per_w)
        def _(j):
            t0 = (wid * per_w + j) * n
            pltpu.sync_copy(h2p_hbm.at[pl.ds(t0, n)], rows_v)
            for k in range(TOP_K):
                pltpu.sync_copy(pos_hbm.at[pl.ds(k * T + t0, n)], idx_refs[k])
            copies = [pltpu.async_copy(rows_v, xs_hbm.at[idx_refs[k]], sem) for k in range(TOP_K)]
            for cp in copies:
                cp.wait()

    return pl.kernel(
        body,
        out_type=jax.ShapeDtypeStruct((n_rows, W), h2p.dtype),
        mesh=_sc_mesh(),
        scratch_types=[pltpu.VMEM((n,), I32)] * TOP_K + [pltpu.VMEM((n, W), h2p.dtype), pltpu.SemaphoreType.DMA],
        name="moe_scatter_rows",
    )(h2p, pos_km)


def _sc_gather_call(ys, pos_km):
    P = pos_km.shape[0]
    W = ys.shape[1]
    nc, ns = _sc_workers()
    n = SC_ROWS
    per_w = P // (nc * ns * n)

    def body(ys_hbm, pos_hbm, yp_hbm, idx_v, rows_v, sem):
        wid = lax.axis_index("s") * nc + lax.axis_index("c")

        @pl.loop(0, per_w)
        def _(j):
            p0 = (wid * per_w + j) * n
            pltpu.sync_copy(pos_hbm.at[pl.ds(p0, n)], idx_v)
            pltpu.async_copy(ys_hbm.at[idx_v], rows_v, sem).wait()
            pltpu.sync_copy(rows_v, yp_hbm.at[pl.ds(p0, n)])

    return pl.kernel(
        body,
        out_type=jax.ShapeDtypeStruct((P, W), ys.dtype),
        mesh=_sc_mesh(),
        scratch_types=[pltpu.VMEM((n,), I32), pltpu.VMEM((n, W), ys.dtype), pltpu.SemaphoreType.DMA],
        name="moe_gather_rows",
    )(ys, pos_km)


def _ffn_kernel(te_ref, tv_ref, nx_ref, xs_ref, wgu_hbm, bgu_ref, wd_hbm, bd_ref, y_ref,
                wgu_f32, wd_f32, wgu_bf, wd_bf, sems):
    i = pl.program_id(0)
    valid = tv_ref[i]
    tm, half = xs_ref.shape
    d_exp = wd_bf.shape[0]

    def weight_copies(e):
        return (pltpu.make_async_copy(wgu_hbm.at[e], wgu_f32, sems.at[0]),
                pltpu.make_async_copy(wd_hbm.at[e], wd_f32, sems.at[1]))

    @pl.when(i == 0)
    def _():
        for cp in weight_copies(te_ref[0]):
            cp.start()

    @pl.when(((i == 0) | (te_ref[i] != te_ref[jnp.maximum(i - 1, 0)])) & (valid > 0))
    def _():
        for cp in weight_copies(te_ref[i]):
            cp.wait()
        wgu_bf[...] = wgu_f32[...].astype(BF16)
        wd_bf[...] = wd_f32[...].astype(BF16)

        @pl.when(nx_ref[i] >= 0)
        def _():
            for cp in weight_copies(nx_ref[i]):
                cp.start()

    @pl.when(valid > 0)
    def _():
        keep = lax.broadcasted_iota(I32, (tm, half), 0) < valid
        lo, hi = _unpack_bf16_pair(xs_ref[...])
        x_lo = jnp.where(keep, lo, 0.0).astype(BF16)
        x_hi = jnp.where(keep, hi, 0.0).astype(BF16)
        d = functools.partial(jnp.dot, preferred_element_type=F32)
        nb = 256
        acc = jnp.zeros((tm, wd_bf.shape[1]), F32) + bd_ref[...]
        for c in range(0, d_exp, nb):
            gate = (d(x_lo, wgu_bf[0:half, c:c + nb]) + d(x_hi, wgu_bf[half:, c:c + nb])
                    + bgu_ref[:, c:c + nb])
            lin = (d(x_lo, wgu_bf[0:half, d_exp + c:d_exp + c + nb])
                   + d(x_hi, wgu_bf[half:, d_exp + c:d_exp + c + nb]) + bgu_ref[:, d_exp + c:d_exp + c + nb])
            gate = jnp.minimum(gate, SWIGLU_LIMIT)
            lin = jnp.clip(lin, -SWIGLU_LIMIT, SWIGLU_LIMIT)
            act = gate * jax.nn.sigmoid(SWIGLU_ALPHA * gate) * (lin + 1.0)
            acc = acc + d(act.astype(BF16), wd_bf[c:c + nb, :])
        y_ref[...] = _pack_bf16_pair(acc[:, :half], acc[:, half:])


def _ffn_call(tile_expert, tile_valid, tile_next, xs, w_gu, b_gu, w_down, b_down, tm):
    R, half = xs.shape
    E, D, two_f = w_gu.shape
    d_exp = w_down.shape[1]
    grid_spec = pltpu.PrefetchScalarGridSpec(
        num_scalar_prefetch=3,
        grid=(R // tm,),
        in_specs=[
            pl.BlockSpec((tm, half), lambda i, te, tv, nx: (i, 0)),
            pl.BlockSpec(memory_space=pl.ANY),
            pl.BlockSpec((None, 1, two_f), lambda i, te, tv, nx: (te[i], 0, 0)),
            pl.BlockSpec(memory_space=pl.ANY),
            pl.BlockSpec((None, 1, D), lambda i, te, tv, nx: (te[i], 0, 0)),
        ],
        out_specs=pl.BlockSpec((tm, half), lambda i, te, tv, nx: (i, 0)),
        scratch_shapes=[
            pltpu.VMEM((D, two_f), F32), pltpu.VMEM((d_exp, D), F32),
            pltpu.VMEM((D, two_f), BF16), pltpu.VMEM((d_exp, D), BF16),
            pltpu.SemaphoreType.DMA((2,)),
        ],
    )
    return pl.pallas_call(
        _ffn_kernel,
        grid_spec=grid_spec,
        out_shape=jax.ShapeDtypeStruct((R, half), U32),
        compiler_params=pltpu.CompilerParams(
            dimension_semantics=("arbitrary",), vmem_limit_bytes=VMEM_LIMIT_BYTES),
        name="moe_ffn",
    )(tile_expert, tile_valid, tile_next, xs, w_gu, b_gu.reshape(E, 1, two_f), w_down,
      b_down.reshape(E, 1, D))


def _final_kernel(x_ref, yp_ref, wts_ref, mod_ref, gfin_ref, *rest):
    o_ref = rest[-1]
    x = x_ref[...]
    half = x.shape[1] // 2
    w = wts_ref[...]
    lo = jnp.zeros((x.shape[0], half), F32)
    hi = jnp.zeros((x.shape[0], half), F32)
    for k in range(TOP_K):
        l, h = _unpack_bf16_pair(yp_ref[k])
        lo = lo + w[:, k:k + 1] * l
        hi = hi + w[:, k:k + 1] * h
    gate = mod_ref[5:6, :]
    x_lo = x[:, :half] + gate[:, :half] * lo
    x_hi = x[:, half:] + gate[:, half:] * hi
    ms = (jnp.sum(x_lo * x_lo, axis=-1, keepdims=True) + jnp.sum(x_hi * x_hi, axis=-1, keepdims=True)) / x.shape[1]
    inv = lax.rsqrt(ms + RMS_EPS)
    o_ref[:, :half] = x_lo * inv * gfin_ref[:, :half]
    o_ref[:, half:] = x_hi * inv * gfin_ref[:, half:]


def _final_call(x1, yp, wts, mod, g_final, tokens_per_batch, group, n_groups, prev_out):
    T_all, D = x1.shape
    T = T_all // n_groups
    tq = min(512, tokens_per_batch)
    per_b = tokens_per_batch // tq
    first = group * (T // tq)
    in_specs = [
        pl.BlockSpec((tq, D), lambda i: (first + i, 0)),
        pl.BlockSpec((TOP_K, tq, D // 2), lambda i: (0, i, 0)),
        pl.BlockSpec((tq, ROUTE_LANES), lambda i: (i, 0)),
        pl.BlockSpec((None, N_MOD, D), lambda i: ((first + i) // per_b, 0, 0)),
        pl.BlockSpec((1, D), lambda i: (0, 0)),
    ]
    args = [x1, yp, wts, mod, g_final.reshape(1, D)]
    aliases = {}
    if prev_out is not None:
        in_specs.append(pl.BlockSpec(memory_space=pl.ANY))
        args.append(prev_out)
        aliases = {len(args) - 1: 0}
    return pl.pallas_call(
        _final_kernel,
        grid=(T // tq,),
        in_specs=in_specs,
        out_specs=pl.BlockSpec((tq, D), lambda i: (first + i, 0)),
        out_shape=jax.ShapeDtypeStruct((T_all, D), F32),
        input_output_aliases=aliases,
        compiler_params=pltpu.CompilerParams(
            dimension_semantics=("arbitrary",), vmem_limit_bytes=VMEM_LIMIT_BYTES),
        name="moe_combine_final",
    )(*args)


def _group_layout(counts, n_tiles, tm):
    padded = ((counts + tm - 1) // tm) * tm
    ends = jnp.cumsum(padded)
    starts = ends - padded
    tile_row = jnp.arange(n_tiles, dtype=I32) * tm
    te = jnp.minimum(jnp.sum(tile_row[:, None] >= ends[None, :], axis=1), N_EXPERTS - 1).astype(I32)
    tv = jnp.clip(counts[te] - (tile_row - starts[te]), 0, tm).astype(I32)
    eids = jnp.arange(N_EXPERTS, dtype=I32)
    later = (eids[None, :] > eids[:, None]) & (counts[None, :] > 0)
    nxt = jnp.min(jnp.where(later, eids[None, :], N_EXPERTS), axis=1)
    nx = jnp.where(nxt < N_EXPERTS, nxt, -1).astype(I32)[te]
    return starts, te, tv, nx


def kernel(x, c, positions, w_ada, b_ada, g_mix, w_in, w_dw, b_dw, g_conv_ln, b_conv_ln, g_ret_norm,
           w_out, g_ffn, w_router, b_router, w_gu, b_gu, w_down, b_down, g_final):
    B, S, D = x.shape
    T = B * S
    assert w_ada.shape[0] == 1, "single-layer block: the final norm directly follows layer 0"
    xt = x
    for l in range(1):
        mod = _mod_call(c, w_ada[l], b_ada[l]).reshape(B, N_MOD, D)
        x1 = _mixer_call(xt, positions, mod, g_mix[l], w_in[l], w_dw[l], b_dw[l], g_conv_ln[l],
                         b_conv_ln[l], g_ret_norm[l], w_out[l]).reshape(T, D)
        n_groups = MOE_TOKEN_GROUPS if B % MOE_TOKEN_GROUPS == 0 else 1
        Tg = T // n_groups
        tm = 512 if Tg * TOP_K >= 512 * N_EXPERTS * 4 else 128
        n_tiles = (Tg * TOP_K) // tm + N_EXPERTS
        routed = [_route_call(x1, mod, g_ffn[l], w_router[l], b_router[l], S, g, n_groups)
                  for g in range(n_groups)]
        out = None
        for g, (h2p, idx, wts, rank, counts) in enumerate(routed):
            starts, te, tv, nx = _group_layout(counts[:, 0], n_tiles, tm)
            pos_km = (starts[idx[:TOP_K]] + rank[:TOP_K]).astype(I32).reshape(-1)
            xs = _sc_scatter_call(h2p, pos_km, n_tiles * tm)
            ys = _ffn_call(te, tv, nx, xs, w_gu[l], b_gu[l], w_down[l], b_down[l], tm)
            yp = _sc_gather_call(ys, pos_km)
            out = _final_call(x1, yp.reshape(TOP_K, Tg, D // 2), wts, mod, g_final, S, g, n_groups, out)
        xt = out
    return xt.reshape(B, S, D)
```

```python
import functools

import numpy as np
import jax
import jax.numpy as jnp
from jax import lax
from jax.experimental import pallas as pl
from jax.experimental.pallas import tpu as pltpu
from jax.experimental.pallas import tpu_sc as plsc

F32 = jnp.float32
BF16 = jnp.bfloat16
U32 = jnp.uint32
I32 = jnp.int32

CONV_WIDTH = 31
CONV_HALO = 32
RET_HEADS = 4
RET_DIM = 128
RET_CHUNK = 128
ROPE_BASE = 10000.0
N_EXPERTS = 32
TOP_K = 4
SWIGLU_LIMIT = 7.0
SWIGLU_ALPHA = 1.702
RMS_EPS = 1e-6
LN_EPS = 1e-5
N_MOD = 6

VMEM_LIMIT_BYTES = 56 * 1024 * 1024
MOE_TOKEN_GROUPS = 2


def _split_bf16(a):
    hi = a.astype(BF16)
    lo = (a - hi.astype(F32)).astype(BF16)
    return hi, lo


def _dot3(a, b_hi, b_lo):
    a_hi, a_lo = _split_bf16(a)
    d = functools.partial(jnp.dot, preferred_element_type=F32)
    return d(a_hi, b_hi) + (d(a_hi, b_lo) + d(a_lo, b_hi))


def _pack_bf16_pair(lo, hi):
    lo_bits = lax.bitcast_convert_type(lo.astype(BF16).astype(F32), U32)
    hi_bits = lax.bitcast_convert_type(hi.astype(BF16).astype(F32), U32)
    return (lo_bits >> 16) | (hi_bits & jnp.uint32(0xFFFF0000))


def _unpack_bf16_pair(p):
    lo = lax.bitcast_convert_type(p << 16, F32)
    hi = lax.bitcast_convert_type(p & jnp.uint32(0xFFFF0000), F32)
    return lo, hi


def _mod_kernel(c_ref, whi_ref, wlo_ref, b_ref, o_ref):
    c = c_ref[...]
    c_act = c * jax.nn.sigmoid(c)
    o_ref[...] = _dot3(c_act, whi_ref[...], wlo_ref[...]) + b_ref[...]


def _mod_call(c, w_ada, b_ada):
    B, D = c.shape
    n = w_ada.shape[1]
    bn = 1024
    w_hi, w_lo = _split_bf16(w_ada)
    return pl.pallas_call(
        _mod_kernel,
        grid=(n // bn,),
        in_specs=[
            pl.BlockSpec((B, D), lambda j: (0, 0)),
            pl.BlockSpec((D, bn), lambda j: (0, j)),
            pl.BlockSpec((D, bn), lambda j: (0, j)),
            pl.BlockSpec((1, bn), lambda j: (0, j)),
        ],
        out_specs=pl.BlockSpec((B, bn), lambda j: (0, j)),
        out_shape=jax.ShapeDtypeStruct((B, n), F32),
        name="adaln_mod",
    )(c, w_hi, w_lo, b_ada.reshape(1, n))


def _retention_tables():
    h = np.arange(RET_HEADS, dtype=np.float32)
    log_gamma = np.log(1.0 - np.power(2.0, -5.0 - h)).astype(np.float32)
    idx = np.arange(RET_CHUNK, dtype=np.float32)
    diff = idx[:, None] - idx[None, :]
    causal = diff >= 0
    mask = np.where(causal[None], np.exp(log_gamma[:, None, None] * np.where(causal, diff, 0.0)[None]), 0.0)
    q_decay = np.exp(log_gamma[:, None] * (idx + 1.0))[..., None]
    k_decay = np.exp(log_gamma[:, None] * (RET_CHUNK - 1.0 - idx))[..., None]
    chunk_decay = np.exp(log_gamma * RET_CHUNK)
    return (mask.astype(np.float32), q_decay.astype(np.float32), k_decay.astype(np.float32),
            [float(v) for v in chunk_decay.astype(np.float32)])


def _mixer_kernel(chunk_decay, ts, x_ref, pos_ref, mod_ref, gmix_ref, win_ref, wdw_ref, bdw_ref,
                  gcl_ref, bcl_ref, gret_ref, wout_ref, rope_ref, dmask_ref, qdec_ref, kdec_ref,
                  o_ref, proj_ref, uext_ref, state_ref, cat_ref, conv_ref):
    s = pl.program_id(1)
    conv_ch = wdw_ref.shape[1]
    ret_w = RET_HEADS * RET_DIM

    @pl.when(s == 0)
    def _():
        uext_ref[0:CONV_HALO, :] = jnp.zeros((CONV_HALO, conv_ch), F32)
        state_ref[...] = jnp.zeros_like(state_ref)

    x = x_ref[...]
    sh = mod_ref[0:1, :]
    sc = mod_ref[1:2, :]
    y = x * lax.rsqrt(jnp.mean(x * x, axis=-1, keepdims=True) + RMS_EPS) * gmix_ref[...]
    h = (y * (1.0 + sc) + sh).astype(BF16)
    proj_ref[...] = jnp.dot(h, win_ref[...], preferred_element_type=F32)

    a = proj_ref[:, 0:conv_ch]
    b = proj_ref[:, conv_ch:2 * conv_ch]
    uext_ref[CONV_HALO:CONV_HALO + ts, :] = a * jax.nn.sigmoid(b)
    cb = 64
    lead = CONV_HALO - (CONV_WIDTH - 1)
    span = cb + CONV_HALO

    def conv_block(r0):
        for c0 in range(0, conv_ch, 128):
            xw = uext_ref[r0:r0 + span, c0:c0 + 128]
            acc = jnp.zeros((cb, 128), F32) + bdw_ref[:, c0:c0 + 128]
            for r in range(8):
                xr = xw if r == 0 else pltpu.roll(xw, span - r, 0)
                for q in range((lead + CONV_WIDTH - 1 - r) // 8 + 1):
                    j = 8 * q + r - lead
                    if 0 <= j < CONV_WIDTH:
                        acc = acc + wdw_ref[j:j + 1, c0:c0 + 128] * xr[8 * q:8 * q + cb]
            conv_ref[r0:r0 + cb, c0:c0 + 128] = acc
    rb = 2 * cb

    def conv_norm(r0):
        acc = conv_ref[r0:r0 + rb, 0:conv_ch]
        mu = jnp.mean(acc, axis=-1, keepdims=True)
        d = acc - mu
        var = jnp.mean(d * d, axis=-1, keepdims=True)
        ln = d * lax.rsqrt(var + LN_EPS) * gcl_ref[...] + bcl_ref[...]
        cat_ref[r0:r0 + rb, 0:conv_ch] = (ln * jax.nn.sigmoid(ln)).astype(BF16)

    ang = pos_ref[...].astype(F32) * rope_ref[...]
    cos2 = jnp.cos(ang)
    sin = jnp.sin(ang)
    lane = lax.broadcasted_iota(I32, ang.shape, 1)
    sin2 = jnp.where(lane < RET_DIM // 2, -sin, sin)
    q0 = 2 * conv_ch
    k0 = q0 + ret_w
    v0 = k0 + ret_w
    g0 = v0 + ret_w
    scale = RET_DIM ** -0.5

    def retention_unit(hd, n):
        c0 = hd * RET_DIM
        r0 = n * RET_CHUNK
        rows = slice(r0, r0 + RET_CHUNK)
        cs = cos2[rows]
        sn = sin2[rows]
        q = proj_ref[rows, q0 + c0:q0 + c0 + RET_DIM]
        k = proj_ref[rows, k0 + c0:k0 + c0 + RET_DIM]
        v = proj_ref[rows, v0 + c0:v0 + c0 + RET_DIM].astype(BF16)
        g = proj_ref[rows, g0 + c0:g0 + c0 + RET_DIM]
        qr = q * cs + pltpu.roll(q, RET_DIM // 2, 1) * sn
        kr = (k * cs + pltpu.roll(k, RET_DIM // 2, 1) * sn) * scale
        st = state_ref[hd]
        scores = lax.dot_general(qr.astype(BF16), kr.astype(BF16), (((1,), (1,)), ((), ())),
                                 preferred_element_type=F32) * dmask_ref[hd]
        inner = jnp.dot(scores.astype(BF16), v, preferred_element_type=F32)
        cross = jnp.dot((qr * qdec_ref[hd]).astype(BF16), st.astype(BF16), preferred_element_type=F32)
        kv = lax.dot_general((kr * kdec_ref[hd]).astype(BF16), v, (((0,), (0,)), ((), ())),
                             preferred_element_type=F32)
        state_ref[hd] = chunk_decay[hd] * st + kv
        r = inner + cross
        mu = jnp.mean(r, axis=-1, keepdims=True)
        d = r - mu
        var = jnp.mean(d * d, axis=-1, keepdims=True)
        rn = d * lax.rsqrt(var + LN_EPS) * gret_ref[:, c0:c0 + RET_DIM]
        cat_ref[rows, conv_ch + c0:conv_ch + c0 + RET_DIM] = (g * jax.nn.sigmoid(g) * rn).astype(BF16)

    units = [(hd, n) for n in range(ts // RET_CHUNK) for hd in range(RET_HEADS)]
    n_conv = ts // cb
    per = -(-len(units) // n_conv)
    for i in range(n_conv):
        conv_block(i * cb)
        if i % 2 == 1:
            conv_norm((i - 1) * cb)
        for hd, n in units[i * per:(i + 1) * per]:
            retention_unit(hd, n)
    uext_ref[0:CONV_HALO, :] = uext_ref[ts:ts + CONV_HALO, :]

    out = jnp.dot(cat_ref[...], wout_ref[...], preferred_element_type=F32)
    o_ref[...] = x + mod_ref[2:3, :] * out


def _mixer_call(x, positions, mod, g_mix, w_in, w_dw, b_dw, g_conv_ln, b_conv_ln, g_ret_norm, w_out):
    B, S, D = x.shape
    in_cols = w_in.shape[1]
    conv_ch = w_dw.shape[1]
    ts = min(512, S)
    mask, q_decay, k_decay, chunk_decay = _retention_tables()
    half = RET_DIM // 2
    inv_freq = (ROPE_BASE ** (-np.arange(half, dtype=np.float32) / half)).astype(np.float32)
    rope = np.concatenate([inv_freq, inv_freq])[None, :]
    w_dw_p = jnp.zeros((CONV_HALO, conv_ch), F32).at[:CONV_WIDTH].set(w_dw)
    full = lambda shape: pl.BlockSpec(shape, lambda b, s: (0,) * len(shape))
    return pl.pallas_call(
        functools.partial(_mixer_kernel, chunk_decay, ts),
        grid=(B, S // ts),
        in_specs=[
            pl.BlockSpec((None, ts, D), lambda b, s: (b, s, 0)),
            pl.BlockSpec((None, ts, 1), lambda b, s: (b, s, 0)),
            pl.BlockSpec((None, N_MOD, D), lambda b, s: (b, 0, 0)),
            full((1, D)),
            full((D, in_cols)),
            full((CONV_HALO, conv_ch)),
            full((1, conv_ch)),
            full((1, conv_ch)),
            full((1, conv_ch)),
            full((1, RET_HEADS * RET_DIM)),
            full((conv_ch + RET_HEADS * RET_DIM, D)),
            full((1, RET_DIM)),
            full((RET_HEADS, RET_CHUNK, RET_CHUNK)),
            full((RET_HEADS, RET_CHUNK, 1)),
            full((RET_HEADS, RET_CHUNK, 1)),
        ],
        out_specs=pl.BlockSpec((None, ts, D), lambda b, s: (b, s, 0)),
        out_shape=jax.ShapeDtypeStruct((B, S, D), F32),
        scratch_shapes=[
            pltpu.VMEM((ts, in_cols), F32),
            pltpu.VMEM((CONV_HALO + ts, conv_ch), F32),
            pltpu.VMEM((RET_HEADS, RET_DIM, RET_DIM), F32),
            pltpu.VMEM((ts, conv_ch + RET_HEADS * RET_DIM), BF16),
            pltpu.VMEM((ts, conv_ch), F32),
        ],
        compiler_params=pltpu.CompilerParams(
            dimension_semantics=("arbitrary", "arbitrary"), vmem_limit_bytes=VMEM_LIMIT_BYTES),
        name="hybrid_mixer",
    )(x, positions.reshape(B, S, 1), mod, g_mix.reshape(1, D), w_in.astype(BF16), w_dw_p,
      b_dw.reshape(1, -1), g_conv_ln.reshape(1, -1), b_conv_ln.reshape(1, -1), g_ret_norm.reshape(1, -1),
      w_out.astype(BF16), jnp.asarray(rope), jnp.asarray(mask), jnp.asarray(q_decay),
      jnp.asarray(k_decay))


def _route_kernel(x_ref, mod_ref, gffn_ref, wr2_ref, wrhi_ref, br_ref, utri_ref,
                  h2p_ref, idx_ref, wts_ref, rank_ref, cnt_ref, carry_ref):
    i = pl.program_id(0)

    @pl.when(i == 0)
    def _():
        carry_ref[...] = jnp.zeros_like(carry_ref)

    x = x_ref[...]
    tr = x.shape[0]
    half = x.shape[1] // 2
    y = x * lax.rsqrt(jnp.mean(x * x, axis=-1, keepdims=True) + RMS_EPS) * gffn_ref[...]
    h2 = y * (1.0 + mod_ref[4:5, :]) + mod_ref[3:4, :]
    h2p_ref[...] = _pack_bf16_pair(h2[:, :half], h2[:, half:])

    h_hi, h_lo = _split_bf16(h2)
    nt = (((1,), (1,)), ((), ()))
    r = lax.dot_general(wr2_ref[...], h_hi, nt, preferred_element_type=F32)
    r2 = lax.dot_general(wrhi_ref[...], h_lo, nt, preferred_element_type=F32)
    l = r[:N_EXPERTS] + (r[N_EXPERTS:] + r2) + br_ref[...]
    eid = lax.broadcasted_iota(I32, l.shape, 0)
    vals, sels, idxs = [], [], []
    for _ in range(TOP_K):
        m = jnp.max(l, axis=0, keepdims=True)
        ik = jnp.min(jnp.where(l == m, eid, N_EXPERTS), axis=0, keepdims=True)
        sel = eid == ik
        vals.append(m)
        sels.append(sel)
        idxs.append(ik)
        l = jnp.where(sel, -jnp.inf, l)
    exps = [jnp.exp(v - vals[0]) for v in vals]
    denom = exps[0] + exps[1] + exps[2] + exps[3]
    member = jnp.zeros(l.shape, F32)
    for sel in sels:
        member = member + sel.astype(F32)
    before = jnp.dot(member.astype(BF16), utri_ref[...], preferred_element_type=F32) + carry_ref[...]
    ranks = [jnp.sum(jnp.where(sel, before, 0.0), axis=0, keepdims=True) for sel in sels]
    carry_ref[...] = carry_ref[...] + jnp.sum(member, axis=1, keepdims=True)
    cnt_ref[...] = carry_ref[...].astype(I32)

    def rows(pieces, n_rows, dtype):
        rid = lax.broadcasted_iota(I32, (n_rows, tr), 0)
        out = jnp.zeros((n_rows, tr), dtype)
        for k, p in enumerate(pieces):
            out = jnp.where(rid == k, p.astype(dtype), out)
        return out

    idx_ref[...] = rows(idxs, idx_ref.shape[0], I32)
    rank_ref[...] = rows(ranks, rank_ref.shape[0], I32)
    wts_ref[...] = rows([e / denom for e in exps], wts_ref.shape[1], F32).T


ROUTE_ROWS = 8
ROUTE_LANES = 128


def _route_call(x1, mod, g_ffn, w_router, b_router, tokens_per_batch, group, n_groups):
    D = x1.shape[1]
    T = x1.shape[0] // n_groups
    tr = min(512, tokens_per_batch)
    per_b = tokens_per_batch // tr
    first = group * (T // tr)
    wr_hi, wr_lo = _split_bf16(w_router.T)
    utri = jnp.asarray(np.triu(np.ones((tr, tr), np.float32), 1), BF16)
    full = lambda shape: pl.BlockSpec(shape, lambda i: (0,) * len(shape))
    return pl.pallas_call(
        _route_kernel,
        grid=(T // tr,),
        in_specs=[
            pl.BlockSpec((tr, D), lambda i: (first + i, 0)),
            pl.BlockSpec((None, N_MOD, D), lambda i: ((first + i) // per_b, 0, 0)),
            full((1, D)),
            full((2 * N_EXPERTS, D)),
            full((N_EXPERTS, D)),
            full((N_EXPERTS, 1)),
            full((tr, tr)),
        ],
        out_specs=[
            pl.BlockSpec((tr, D // 2), lambda i: (i, 0)),
            pl.BlockSpec((ROUTE_ROWS, tr), lambda i: (0, i)),
            pl.BlockSpec((tr, ROUTE_LANES), lambda i: (i, 0)),
            pl.BlockSpec((ROUTE_ROWS, tr), lambda i: (0, i)),
            full((N_EXPERTS, 1)),
        ],
        out_shape=[
            jax.ShapeDtypeStruct((T, D // 2), U32),
            jax.ShapeDtypeStruct((ROUTE_ROWS, T), I32),
            jax.ShapeDtypeStruct((T, ROUTE_LANES), F32),
            jax.ShapeDtypeStruct((ROUTE_ROWS, T), I32),
            jax.ShapeDtypeStruct((N_EXPERTS, 1), I32),
        ],
        scratch_shapes=[pltpu.VMEM((N_EXPERTS, 1), F32)],
        compiler_params=pltpu.CompilerParams(
            dimension_semantics=("arbitrary",), vmem_limit_bytes=VMEM_LIMIT_BYTES),
        name="moe_route",
    )(x1, mod, g_ffn.reshape(1, D), jnp.concatenate([wr_hi, wr_lo], axis=0), wr_hi,
      b_router.reshape(N_EXPERTS, 1), utri)


SC_ROWS = 128


V7X_SC_CORES = 2
V7X_SC_SUBCORES = 16


def _sc_workers():
    return V7X_SC_CORES, V7X_SC_SUBCORES


def _sc_mesh():
    return plsc.VectorSubcoreMesh(core_axis_name="c", subcore_axis_name="s",
                                  num_cores=V7X_SC_CORES, num_subcores=V7X_SC_SUBCORES)


def _sc_scatter_call(h2p, pos_km, n_rows):
    T, W = h2p.shape
    nc, ns = _sc_workers()
    n = SC_ROWS
    per_w = T // (nc * ns * n)

    def body(h2p_hbm, pos_hbm, xs_hbm, i0, i1, i2, i3, rows_v, sem):
        wid = lax.axis_index("s") * nc + lax.axis_index("c")
        idx_refs = (i0, i1, i2, i3)

        @pl.loop(0, per_w)
        def _(j):
            t0 = (wid * per_w + j) * n
            pltpu.sync_copy(h2p_hbm.at[pl.ds(t0, n)], rows_v)
            for k in range(TOP_K):
                pltpu.sync_copy(pos_hbm.at[pl.ds(k * T + t0, n)], idx_refs[k])
            copies = [pltpu.async_copy(rows_v, xs_hbm.at[idx_refs[k]], sem) for k in range(TOP_K)]
            for cp in copies:
                cp.wait()

    return pl.kernel(
        body,
        out_type=jax.ShapeDtypeStruct((n_rows, W), h2p.dtype),
        mesh=_sc_mesh(),
        scratch_types=[pltpu.VMEM((n,), I32)] * TOP_K + [pltpu.VMEM((n, W), h2p.dtype), pltpu.SemaphoreType.DMA],
        name="moe_scatter_rows",
    )(h2p, pos_km)


def _sc_gather_call(ys, pos_km):
    P = pos_km.shape[0]
    W = ys.shape[1]
    nc, ns = _sc_workers()
    n = SC_ROWS
    per_w = P // (nc * ns * n)

    def body(ys_hbm, pos_hbm, yp_hbm, idx_v, rows_v, sem):
        wid = lax.axis_index("s") * nc + lax.axis_index("c")

        @pl.loop(0, per_w)
        def _(j):
            p0 = (wid * per_w + j) * n
            pltpu.sync_copy(pos_hbm.at[pl.ds(p0, n)], idx_v)
            pltpu.async_copy(ys_hbm.at[idx_v], rows_v, sem).wait()
            pltpu.sync_copy(rows_v, yp_hbm.at[pl.ds(p0, n)])

    return pl.kernel(
        body,
        out_type=jax.ShapeDtypeStruct((P, W), ys.dtype),
        mesh=_sc_mesh(),
        scratch_types=[pltpu.VMEM((n,), I32), pltpu.VMEM((n, W), ys.dtype), pltpu.SemaphoreType.DMA],
        name="moe_gather_rows",
    )(ys, pos_km)


def _ffn_kernel(te_ref, tv_ref, nx_ref, xs_ref, wgu_hbm, bgu_ref, wd_hbm, bd_ref, y_ref,
                wgu_f32, wd_f32, wgu_bf, wd_bf, sems):
    i = pl.program_id(0)
    valid = tv_ref[i]
    tm, half = xs_ref.shape
    d_exp = wd_bf.shape[0]

    def weight_copies(e):
        return (pltpu.make_async_copy(wgu_hbm.at[e], wgu_f32, sems.at[0]),
                pltpu.make_async_copy(wd_hbm.at[e], wd_f32, sems.at[1]))

    @pl.when(i == 0)
    def _():
        for cp in weight_copies(te_ref[0]):
            cp.start()

    @pl.when(((i == 0) | (te_ref[i] != te_ref[jnp.maximum(i - 1, 0)])) & (valid > 0))
    def _():
        for cp in weight_copies(te_ref[i]):
            cp.wait()
        wgu_bf[...] = wgu_f32[...].astype(BF16)
        wd_bf[...] = wd_f32[...].astype(BF16)

        @pl.when(nx_ref[i] >= 0)
        def _():
            for cp in weight_copies(nx_ref[i]):
                cp.start()

    @pl.when(valid > 0)
    def _():
        keep = lax.broadcasted_iota(I32, (tm, half), 0) < valid
        lo, hi = _unpack_bf16_pair(xs_ref[...])
        x_lo = jnp.where(keep, lo, 0.0).astype(BF16)
        x_hi = jnp.where(keep, hi, 0.0).astype(BF16)
        d = functools.partial(jnp.dot, preferred_element_type=F32)
        nb = 256
        acc = jnp.zeros((tm, wd_bf.shape[1]), F32) + bd_ref[...]
        for c in range(0, d_exp, nb):
            gate = (d(x_lo, wgu_bf[0:half, c:c + nb]) + d(x_hi, wgu_bf[half:, c:c + nb])
                    + bgu_ref[:, c:c + nb])
            lin = (d(x_lo, wgu_bf[0:half, d_exp + c:d_exp + c + nb])
                   + d(x_hi, wgu_bf[half:, d_exp + c:d_exp + c + nb]) + bgu_ref[:, d_exp + c:d_exp + c + nb])
            gate = jnp.minimum(gate, SWIGLU_LIMIT)
            lin = jnp.clip(lin, -SWIGLU_LIMIT, SWIGLU_LIMIT)
            act = gate * jax.nn.sigmoid(SWIGLU_ALPHA * gate) * (lin + 1.0)
            acc = acc + d(act.astype(BF16), wd_bf[c:c + nb, :])
        y_ref[...] = _pack_bf16_pair(acc[:, :half], acc[:, half:])


def _ffn_call(tile_expert, tile_valid, tile_next, xs, w_gu, b_gu, w_down, b_down, tm):
    R, half = xs.shape
    E, D, two_f = w_gu.shape
    d_exp = w_down.shape[1]
    grid_spec = pltpu.PrefetchScalarGridSpec(
        num_scalar_prefetch=3,
        grid=(R // tm,),
        in_specs=[
            pl.BlockSpec((tm, half), lambda i, te, tv, nx: (i, 0)),
            pl.BlockSpec(memory_space=pl.ANY),
            pl.BlockSpec((None, 1, two_f), lambda i, te, tv, nx: (te[i], 0, 0)),
            pl.BlockSpec(memory_space=pl.ANY),
            pl.BlockSpec((None, 1, D), lambda i, te, tv, nx: (te[i], 0, 0)),
        ],
        out_specs=pl.BlockSpec((tm, half), lambda i, te, tv, nx: (i, 0)),
        scratch_shapes=[
            pltpu.VMEM((D, two_f), F32), pltpu.VMEM((d_exp, D), F32),
            pltpu.VMEM((D, two_f), BF16), pltpu.VMEM((d_exp, D), BF16),
            pltpu.SemaphoreType.DMA((2,)),
        ],
    )
    return pl.pallas_call(
        _ffn_kernel,
        grid_spec=grid_spec,
        out_shape=jax.ShapeDtypeStruct((R, half), U32),
        compiler_params=pltpu.CompilerParams(
            dimension_semantics=("arbitrary",), vmem_limit_bytes=VMEM_LIMIT_BYTES),
        name="moe_ffn",
    )(tile_expert, tile_valid, tile_next, xs, w_gu, b_gu.reshape(E, 1, two_f), w_down,
      b_down.reshape(E, 1, D))


def _final_kernel(x_ref, yp_ref, wts_ref, mod_ref, gfin_ref, *rest):
    o_ref = rest[-1]
    x = x_ref[...]
    half = x.shape[1] // 2
    w = wts_ref[...]
    lo = jnp.zeros((x.shape[0], half), F32)
    hi = jnp.zeros((x.shape[0], half), F32)
    for k in range(TOP_K):
        l, h = _unpack_bf16_pair(yp_ref[k])
        lo = lo + w[:, k:k + 1] * l
        hi = hi + w[:, k:k + 1] * h
    gate = mod_ref[5:6, :]
    x_lo = x[:, :half] + gate[:, :half] * lo
    x_hi = x[:, half:] + gate[:, half:] * hi
    ms = (jnp.sum(x_lo * x_lo, axis=-1, keepdims=True) + jnp.sum(x_hi * x_hi, axis=-1, keepdims=True)) / x.shape[1]
    inv = lax.rsqrt(ms + RMS_EPS)
    o_ref[:, :half] = x_lo * inv * gfin_ref[:, :half]
    o_ref[:, half:] = x_hi * inv * gfin_ref[:, half:]


def _final_call(x1, yp, wts, mod, g_final, tokens_per_batch, group, n_groups, prev_out):
    T_all, D = x1.shape
    T = T_all // n_groups
    tq = min(512, tokens_per_batch)
    per_b = tokens_per_batch // tq
    first = group * (T // tq)
    in_specs = [
        pl.BlockSpec((tq, D), lambda i: (first + i, 0)),
        pl.BlockSpec((TOP_K, tq, D // 2), lambda i: (0, i, 0)),
        pl.BlockSpec((tq, ROUTE_LANES), lambda i: (i, 0)),
        pl.BlockSpec((None, N_MOD, D), lambda i: ((first + i) // per_b, 0, 0)),
        pl.BlockSpec((1, D), lambda i: (0, 0)),
    ]
    args = [x1, yp, wts, mod, g_final.reshape(1, D)]
    aliases = {}
    if prev_out is not None:
        in_specs.append(pl.BlockSpec(memory_space=pl.ANY))
        args.append(prev_out)
        aliases = {len(args) - 1: 0}
    return pl.pallas_call(
        _final_kernel,
        grid=(T // tq,),
        in_specs=in_specs,
        out_specs=pl.BlockSpec((tq, D), lambda i: (first + i, 0)),
        out_shape=jax.ShapeDtypeStruct((T_all, D), F32),
        input_output_aliases=aliases,
        compiler_params=pltpu.CompilerParams(
            dimension_semantics=("arbitrary",), vmem_limit_bytes=VMEM_LIMIT_BYTES),
        name="moe_combine_final",
    )(*args)


def _group_layout(counts, n_tiles, tm):
    padded = ((counts + tm - 1) // tm) * tm
    ends = jnp.cumsum(padded)
    starts = ends - padded
    tile_row = jnp.arange(n_tiles, dtype=I32) * tm
    te = jnp.minimum(jnp.sum(tile_row[:, None] >= ends[None, :], axis=1), N_EXPERTS - 1).astype(I32)
    eids = jnp.arange(N_EXPERTS, dtype=I32)
    mine = te[:, None] == eids[None, :]
    lookup = lambda table: jnp.sum(jnp.where(mine, table[None, :], 0), axis=1)
    tv = jnp.clip(lookup(counts) - (tile_row - lookup(starts)), 0, tm).astype(I32)
    later = (eids[None, :] > eids[:, None]) & (counts[None, :] > 0)
    nxt = jnp.min(jnp.where(later, eids[None, :], N_EXPERTS), axis=1)
    nx = lookup(jnp.where(nxt < N_EXPERTS, nxt, -1)).astype(I32)
    return starts, te, tv, nx


def kernel(x, c, positions, w_ada, b_ada, g_mix, w_in, w_dw, b_dw, g_conv_ln, b_conv_ln, g_ret_norm,
           w_out, g_ffn, w_router, b_router, w_gu, b_gu, w_down, b_down, g_final):
    B, S, D = x.shape
    T = B * S
    assert w_ada.shape[0] == 1, "single-layer block: the final norm directly follows layer 0"
    xt = x
    for l in range(1):
        mod = _mod_call(c, w_ada[l], b_ada[l]).reshape(B, N_MOD, D)
        x1 = _mixer_call(xt, positions, mod, g_mix[l], w_in[l], w_dw[l], b_dw[l], g_conv_ln[l],
                         b_conv_ln[l], g_ret_norm[l], w_out[l]).reshape(T, D)
        n_groups = MOE_TOKEN_GROUPS if B % MOE_TOKEN_GROUPS == 0 else 1
        Tg = T // n_groups
        tm = 512 if Tg * TOP_K >= 512 * N_EXPERTS * 4 else 128
        n_tiles = (Tg * TOP_K) // tm + N_EXPERTS
        routed = [_route_call(x1, mod, g_ffn[l], w_router[l], b_router[l], S, g, n_groups)
                  for g in range(n_groups)]
        out = None
        for g, (h2p, idx, wts, rank, counts) in enumerate(routed):
            starts, te, tv, nx = _group_layout(counts[:, 0], n_tiles, tm)
            pos_km = rank[:TOP_K]
            for e in range(N_EXPERTS):
                pos_km = pos_km + jnp.where(idx[:TOP_K] == e, starts[e], 0)
            pos_km = pos_km.astype(I32).reshape(-1)
            xs = _sc_scatter_call(h2p, pos_km, n_tiles * tm)
            ys = _ffn_call(te, tv, nx, xs, w_gu[l], b_gu[l], w_down[l], b_down[l], tm)
            yp = _sc_gather_call(ys, pos_km)
            out = _final_call(x1, yp.reshape(TOP_K, Tg, D // 2), wts, mod, g_final, S, g, n_groups, out)
        xt = out
    return xt.reshape(B, S, D)
```

```python
import functools

import numpy as np
import jax
import jax.numpy as jnp
from jax import lax
from jax.experimental import pallas as pl
from jax.experimental.pallas import tpu as pltpu
from jax.experimental.pallas import tpu_sc as plsc

F32 = jnp.float32
BF16 = jnp.bfloat16
U32 = jnp.uint32
I32 = jnp.int32

CONV_WIDTH = 31
CONV_HALO = 32
RET_HEADS = 4
RET_DIM = 128
RET_CHUNK = 128
ROPE_BASE = 10000.0
N_EXPERTS = 32
TOP_K = 4
SWIGLU_LIMIT = 7.0
SWIGLU_ALPHA = 1.702
RMS_EPS = 1e-6
LN_EPS = 1e-5
N_MOD = 6

VMEM_LIMIT_BYTES = 56 * 1024 * 1024
MOE_TOKEN_GROUPS = 2


def _split_bf16(a):
    hi = a.astype(BF16)
    lo = (a - hi.astype(F32)).astype(BF16)
    return hi, lo


def _dot3(a, b_hi, b_lo):
    a_hi, a_lo = _split_bf16(a)
    d = functools.partial(jnp.dot, preferred_element_type=F32)
    return d(a_hi, b_hi) + (d(a_hi, b_lo) + d(a_lo, b_hi))


def _pack_bf16_pair(lo, hi):
    lo_bits = lax.bitcast_convert_type(lo.astype(BF16).astype(F32), U32)
    hi_bits = lax.bitcast_convert_type(hi.astype(BF16).astype(F32), U32)
    return (lo_bits >> 16) | (hi_bits & jnp.uint32(0xFFFF0000))


def _unpack_bf16_pair(p):
    lo = lax.bitcast_convert_type(p << 16, F32)
    hi = lax.bitcast_convert_type(p & jnp.uint32(0xFFFF0000), F32)
    return lo, hi


def _mod_kernel(c_ref, whi_ref, wlo_ref, b_ref, o_ref):
    c = c_ref[...]
    c_act = c * jax.nn.sigmoid(c)
    o_ref[...] = _dot3(c_act, whi_ref[...], wlo_ref[...]) + b_ref[...]


def _mod_call(c, w_ada, b_ada):
    B, D = c.shape
    n = w_ada.shape[1]
    bn = 1024
    w_hi, w_lo = _split_bf16(w_ada)
    return pl.pallas_call(
        _mod_kernel,
        grid=(n // bn,),
        in_specs=[
            pl.BlockSpec((B, D), lambda j: (0, 0)),
            pl.BlockSpec((D, bn), lambda j: (0, j)),
            pl.BlockSpec((D, bn), lambda j: (0, j)),
            pl.BlockSpec((1, bn), lambda j: (0, j)),
        ],
        out_specs=pl.BlockSpec((B, bn), lambda j: (0, j)),
        out_shape=jax.ShapeDtypeStruct((B, n), F32),
        name="adaln_mod",
    )(c, w_hi, w_lo, b_ada.reshape(1, n))


def _retention_tables():
    h = np.arange(RET_HEADS, dtype=np.float32)
    log_gamma = np.log(1.0 - np.power(2.0, -5.0 - h)).astype(np.float32)
    idx = np.arange(RET_CHUNK, dtype=np.float32)
    diff = idx[:, None] - idx[None, :]
    causal = diff >= 0
    mask = np.where(causal[None], np.exp(log_gamma[:, None, None] * np.where(causal, diff, 0.0)[None]), 0.0)
    q_decay = np.exp(log_gamma[:, None] * (idx + 1.0))[..., None]
    k_decay = np.exp(log_gamma[:, None] * (RET_CHUNK - 1.0 - idx))[..., None]
    chunk_decay = np.exp(log_gamma * RET_CHUNK)
    return (mask.astype(np.float32), q_decay.astype(np.float32), k_decay.astype(np.float32),
            [float(v) for v in chunk_decay.astype(np.float32)])


def _mixer_kernel(chunk_decay, ts, x_ref, pos_ref, mod_ref, gmix_ref, win_ref, wdw_ref, bdw_ref,
                  gcl_ref, bcl_ref, gret_ref, wout_ref, rope_ref, dmask_ref, qdec_ref, kdec_ref,
                  after_ref, o_ref, proj_ref, uext_ref, state_ref, cat_ref, conv_ref):
    del after_ref
    s = pl.program_id(1)
    conv_ch = wdw_ref.shape[1]
    ret_w = RET_HEADS * RET_DIM

    @pl.when(s == 0)
    def _():
        uext_ref[0:CONV_HALO, :] = jnp.zeros((CONV_HALO, conv_ch), F32)
        state_ref[...] = jnp.zeros_like(state_ref)

    x = x_ref[...]
    sh = mod_ref[0:1, :]
    sc = mod_ref[1:2, :]
    y = x * lax.rsqrt(jnp.mean(x * x, axis=-1, keepdims=True) + RMS_EPS) * gmix_ref[...]
    h = (y * (1.0 + sc) + sh).astype(BF16)
    proj_ref[...] = jnp.dot(h, win_ref[...], preferred_element_type=F32)

    a = proj_ref[:, 0:conv_ch]
    b = proj_ref[:, conv_ch:2 * conv_ch]
    uext_ref[CONV_HALO:CONV_HALO + ts, :] = a * jax.nn.sigmoid(b)
    cb = 64
    lead = CONV_HALO - (CONV_WIDTH - 1)
    span = cb + CONV_HALO

    def conv_block(r0):
        for c0 in range(0, conv_ch, 128):
            xw = uext_ref[r0:r0 + span, c0:c0 + 128]
            acc = jnp.zeros((cb, 128), F32) + bdw_ref[:, c0:c0 + 128]
            for r in range(8):
                xr = xw if r == 0 else pltpu.roll(xw, span - r, 0)
                for q in range((lead + CONV_WIDTH - 1 - r) // 8 + 1):
                    j = 8 * q + r - lead
                    if 0 <= j < CONV_WIDTH:
                        acc = acc + wdw_ref[j:j + 1, c0:c0 + 128] * xr[8 * q:8 * q + cb]
            conv_ref[r0:r0 + cb, c0:c0 + 128] = acc
    rb = 2 * cb

    def conv_norm(r0):
        acc = conv_ref[r0:r0 + rb, 0:conv_ch]
        mu = jnp.mean(acc, axis=-1, keepdims=True)
        d = acc - mu
        var = jnp.mean(d * d, axis=-1, keepdims=True)
        ln = d * lax.rsqrt(var + LN_EPS) * gcl_ref[...] + bcl_ref[...]
        cat_ref[r0:r0 + rb, 0:conv_ch] = (ln * jax.nn.sigmoid(ln)).astype(BF16)

    ang = pos_ref[...].astype(F32) * rope_ref[...]
    cos2 = jnp.cos(ang)
    sin = jnp.sin(ang)
    lane = lax.broadcasted_iota(I32, ang.shape, 1)
    sin2 = jnp.where(lane < RET_DIM // 2, -sin, sin)
    q0 = 2 * conv_ch
    k0 = q0 + ret_w
    v0 = k0 + ret_w
    g0 = v0 + ret_w
    scale = RET_DIM ** -0.5

    def retention_unit(hd, n):
        c0 = hd * RET_DIM
        r0 = n * RET_CHUNK
        rows = slice(r0, r0 + RET_CHUNK)
        cs = cos2[rows]
        sn = sin2[rows]
        q = proj_ref[rows, q0 + c0:q0 + c0 + RET_DIM]
        k = proj_ref[rows, k0 + c0:k0 + c0 + RET_DIM]
        v = proj_ref[rows, v0 + c0:v0 + c0 + RET_DIM].astype(BF16)
        g = proj_ref[rows, g0 + c0:g0 + c0 + RET_DIM]
        qr = q * cs + pltpu.roll(q, RET_DIM // 2, 1) * sn
        kr = (k * cs + pltpu.roll(k, RET_DIM // 2, 1) * sn) * scale
        st = state_ref[hd]
        scores = lax.dot_general(qr.astype(BF16), kr.astype(BF16), (((1,), (1,)), ((), ())),
                                 preferred_element_type=F32) * dmask_ref[hd]
        inner = jnp.dot(scores.astype(BF16), v, preferred_element_type=F32)
        cross = jnp.dot((qr * qdec_ref[hd]).astype(BF16), st.astype(BF16), preferred_element_type=F32)
        kv = lax.dot_general((kr * kdec_ref[hd]).astype(BF16), v, (((0,), (0,)), ((), ())),
                             preferred_element_type=F32)
        state_ref[hd] = chunk_decay[hd] * st + kv
        r = inner + cross
        mu = jnp.mean(r, axis=-1, keepdims=True)
        d = r - mu
        var = jnp.mean(d * d, axis=-1, keepdims=True)
        rn = d * lax.rsqrt(var + LN_EPS) * gret_ref[:, c0:c0 + RET_DIM]
        cat_ref[rows, conv_ch + c0:conv_ch + c0 + RET_DIM] = (g * jax.nn.sigmoid(g) * rn).astype(BF16)

    units = [(hd, n) for n in range(ts // RET_CHUNK) for hd in range(RET_HEADS)]
    n_conv = ts // cb
    per = -(-len(units) // n_conv)
    for i in range(n_conv):
        conv_block(i * cb)
        if i % 2 == 1:
            conv_norm((i - 1) * cb)
        for hd, n in units[i * per:(i + 1) * per]:
            retention_unit(hd, n)
    uext_ref[0:CONV_HALO, :] = uext_ref[ts:ts + CONV_HALO, :]

    out = jnp.dot(cat_ref[...], wout_ref[...], preferred_element_type=F32)
    o_ref[...] = x + mod_ref[2:3, :] * out


def _mixer_call(x, positions, mod, g_mix, w_in, w_dw, b_dw, g_conv_ln, b_conv_ln, g_ret_norm, w_out,
                group, n_groups, after):
    S, D = x.shape[1:]
    B = x.shape[0] // n_groups
    b0 = group * B
    in_cols = w_in.shape[1]
    conv_ch = w_dw.shape[1]
    ts = min(512, S)
    mask, q_decay, k_decay, chunk_decay = _retention_tables()
    half = RET_DIM // 2
    inv_freq = (ROPE_BASE ** (-np.arange(half, dtype=np.float32) / half)).astype(np.float32)
    rope = np.concatenate([inv_freq, inv_freq])[None, :]
    w_dw_p = jnp.zeros((CONV_HALO, conv_ch), F32).at[:CONV_WIDTH].set(w_dw)
    full = lambda shape: pl.BlockSpec(shape, lambda b, s: (0,) * len(shape))
    return pl.pallas_call(
        functools.partial(_mixer_kernel, chunk_decay, ts),
        grid=(B, S // ts),
        in_specs=[
            pl.BlockSpec((None, ts, D), lambda b, s: (b0 + b, s, 0)),
            pl.BlockSpec((None, ts, 1), lambda b, s: (b0 + b, s, 0)),
            pl.BlockSpec((None, N_MOD, D), lambda b, s: (b0 + b, 0, 0)),
            full((1, D)),
            full((D, in_cols)),
            full((CONV_HALO, conv_ch)),
            full((1, conv_ch)),
            full((1, conv_ch)),
            full((1, conv_ch)),
            full((1, RET_HEADS * RET_DIM)),
            full((conv_ch + RET_HEADS * RET_DIM, D)),
            full((1, RET_DIM)),
            full((RET_HEADS, RET_CHUNK, RET_CHUNK)),
            full((RET_HEADS, RET_CHUNK, 1)),
            full((RET_HEADS, RET_CHUNK, 1)),
            pl.BlockSpec(memory_space=pl.ANY),
        ],
        out_specs=pl.BlockSpec((None, ts, D), lambda b, s: (b, s, 0)),
        out_shape=jax.ShapeDtypeStruct((B, S, D), F32),
        scratch_shapes=[
            pltpu.VMEM((ts, in_cols), F32),
            pltpu.VMEM((CONV_HALO + ts, conv_ch), F32),
            pltpu.VMEM((RET_HEADS, RET_DIM, RET_DIM), F32),
            pltpu.VMEM((ts, conv_ch + RET_HEADS * RET_DIM), BF16),
            pltpu.VMEM((ts, conv_ch), F32),
        ],
        compiler_params=pltpu.CompilerParams(
            dimension_semantics=("arbitrary", "arbitrary"), vmem_limit_bytes=VMEM_LIMIT_BYTES),
        name="hybrid_mixer",
    )(x, positions.reshape(-1, S, 1), mod, g_mix.reshape(1, D), w_in.astype(BF16), w_dw_p,
      b_dw.reshape(1, -1), g_conv_ln.reshape(1, -1), b_conv_ln.reshape(1, -1), g_ret_norm.reshape(1, -1),
      w_out.astype(BF16), jnp.asarray(rope), jnp.asarray(mask), jnp.asarray(q_decay),
      jnp.asarray(k_decay), after)


def _route_kernel(x_ref, mod_ref, gffn_ref, wr2_ref, wrhi_ref, br_ref, utri_ref,
                  h2p_ref, idx_ref, wts_ref, rank_ref, cnt_ref, carry_ref):
    i = pl.program_id(0)

    @pl.when(i == 0)
    def _():
        carry_ref[...] = jnp.zeros_like(carry_ref)

    x = x_ref[...]
    tr = x.shape[0]
    half = x.shape[1] // 2
    y = x * lax.rsqrt(jnp.mean(x * x, axis=-1, keepdims=True) + RMS_EPS) * gffn_ref[...]
    h2 = y * (1.0 + mod_ref[4:5, :]) + mod_ref[3:4, :]
    h2p_ref[...] = _pack_bf16_pair(h2[:, :half], h2[:, half:])

    h_hi, h_lo = _split_bf16(h2)
    nt = (((1,), (1,)), ((), ()))
    r = lax.dot_general(wr2_ref[...], h_hi, nt, preferred_element_type=F32)
    r2 = lax.dot_general(wrhi_ref[...], h_lo, nt, preferred_element_type=F32)
    l = r[:N_EXPERTS] + (r[N_EXPERTS:] + r2) + br_ref[...]
    eid = lax.broadcasted_iota(I32, l.shape, 0)
    vals, sels, idxs = [], [], []
    for _ in range(TOP_K):
        m = jnp.max(l, axis=0, keepdims=True)
        ik = jnp.min(jnp.where(l == m, eid, N_EXPERTS), axis=0, keepdims=True)
        sel = eid == ik
        vals.append(m)
        sels.append(sel)
        idxs.append(ik)
        l = jnp.where(sel, -jnp.inf, l)
    exps = [jnp.exp(v - vals[0]) for v in vals]
    denom = exps[0] + exps[1] + exps[2] + exps[3]
    member = jnp.zeros(l.shape, F32)
    for sel in sels:
        member = member + sel.astype(F32)
    before = jnp.dot(member.astype(BF16), utri_ref[...], preferred_element_type=F32) + carry_ref[...]
    ranks = [jnp.sum(jnp.where(sel, before, 0.0), axis=0, keepdims=True) for sel in sels]
    carry_ref[...] = carry_ref[...] + jnp.sum(member, axis=1, keepdims=True)
    cnt_ref[...] = carry_ref[...].astype(I32)

    def rows(pieces, n_rows, dtype):
        rid = lax.broadcasted_iota(I32, (n_rows, tr), 0)
        out = jnp.zeros((n_rows, tr), dtype)
        for k, p in enumerate(pieces):
            out = jnp.where(rid == k, p.astype(dtype), out)
        return out

    idx_ref[...] = rows(idxs, idx_ref.shape[0], I32)
    rank_ref[...] = rows(ranks, rank_ref.shape[0], I32)
    wts_ref[...] = rows([e / denom for e in exps], wts_ref.shape[1], F32).T


ROUTE_ROWS = 8
ROUTE_LANES = 128


def _route_call(x1, mod, g_ffn, w_router, b_router, tokens_per_batch, batch0):
    T, D = x1.shape
    tr = min(512, tokens_per_batch)
    per_b = tokens_per_batch // tr
    wr_hi, wr_lo = _split_bf16(w_router.T)
    utri = jnp.asarray(np.triu(np.ones((tr, tr), np.float32), 1), BF16)
    full = lambda shape: pl.BlockSpec(shape, lambda i: (0,) * len(shape))
    return pl.pallas_call(
        _route_kernel,
        grid=(T // tr,),
        in_specs=[
            pl.BlockSpec((tr, D), lambda i: (i, 0)),
            pl.BlockSpec((None, N_MOD, D), lambda i: (batch0 + i // per_b, 0, 0)),
            full((1, D)),
            full((2 * N_EXPERTS, D)),
            full((N_EXPERTS, D)),
            full((N_EXPERTS, 1)),
            full((tr, tr)),
        ],
        out_specs=[
            pl.BlockSpec((tr, D // 2), lambda i: (i, 0)),
            pl.BlockSpec((ROUTE_ROWS, tr), lambda i: (0, i)),
            pl.BlockSpec((tr, ROUTE_LANES), lambda i: (i, 0)),
            pl.BlockSpec((ROUTE_ROWS, tr), lambda i: (0, i)),
            full((N_EXPERTS, 1)),
        ],
        out_shape=[
            jax.ShapeDtypeStruct((T, D // 2), U32),
            jax.ShapeDtypeStruct((ROUTE_ROWS, T), I32),
            jax.ShapeDtypeStruct((T, ROUTE_LANES), F32),
            jax.ShapeDtypeStruct((ROUTE_ROWS, T), I32),
            jax.ShapeDtypeStruct((N_EXPERTS, 1), I32),
        ],
        scratch_shapes=[pltpu.VMEM((N_EXPERTS, 1), F32)],
        compiler_params=pltpu.CompilerParams(
            dimension_semantics=("arbitrary",), vmem_limit_bytes=VMEM_LIMIT_BYTES),
        name="moe_route",
    )(x1, mod, g_ffn.reshape(1, D), jnp.concatenate([wr_hi, wr_lo], axis=0), wr_hi,
      b_router.reshape(N_EXPERTS, 1), utri)


SC_ROWS = 128


V7X_SC_CORES = 2
V7X_SC_SUBCORES = 16


def _sc_workers():
    return V7X_SC_CORES, V7X_SC_SUBCORES


def _sc_mesh():
    return plsc.VectorSubcoreMesh(core_axis_name="c", subcore_axis_name="s",
                                  num_cores=V7X_SC_CORES, num_subcores=V7X_SC_SUBCORES)


def _sc_scatter_call(h2p, pos_km, n_rows):
    T, W = h2p.shape
    nc, ns = _sc_workers()
    n = SC_ROWS
    per_w = T // (nc * ns * n)

    def body(h2p_hbm, pos_hbm, xs_hbm, i0, i1, i2, i3, rows_v, sem):
        wid = lax.axis_index("s") * nc + lax.axis_index("c")
        idx_refs = (i0, i1, i2, i3)

        @pl.loop(0, per_w)
        def _(j):
            t0 = (wid * per_w + j) * n
            pltpu.sync_copy(h2p_hbm.at[pl.ds(t0, n)], rows_v)
            for k in range(TOP_K):
                pltpu.sync_copy(pos_hbm.at[pl.ds(k * T + t0, n)], idx_refs[k])
            copies = [pltpu.async_copy(rows_v, xs_hbm.at[idx_refs[k]], sem) for k in range(TOP_K)]
            for cp in copies:
                cp.wait()

    return pl.kernel(
        body,
        out_type=jax.ShapeDtypeStruct((n_rows, W), h2p.dtype),
        mesh=_sc_mesh(),
        scratch_types=[pltpu.VMEM((n,), I32)] * TOP_K + [pltpu.VMEM((n, W), h2p.dtype), pltpu.SemaphoreType.DMA],
        name="moe_scatter_rows",
    )(h2p, pos_km)


def _sc_gather_call(ys, pos_km):
    P = pos_km.shape[0]
    W = ys.shape[1]
    nc, ns = _sc_workers()
    n = SC_ROWS
    per_w = P // (nc * ns * n)

    def body(ys_hbm, pos_hbm, yp_hbm, idx_v, rows_v, sem):
        wid = lax.axis_index("s") * nc + lax.axis_index("c")

        @pl.loop(0, per_w)
        def _(j):
            p0 = (wid * per_w + j) * n
            pltpu.sync_copy(pos_hbm.at[pl.ds(p0, n)], idx_v)
            pltpu.async_copy(ys_hbm.at[idx_v], rows_v, sem).wait()
            pltpu.sync_copy(rows_v, yp_hbm.at[pl.ds(p0, n)])

    return pl.kernel(
        body,
        out_type=jax.ShapeDtypeStruct((P, W), ys.dtype),
        mesh=_sc_mesh(),
        scratch_types=[pltpu.VMEM((n,), I32), pltpu.VMEM((n, W), ys.dtype), pltpu.SemaphoreType.DMA],
        name="moe_gather_rows",
    )(ys, pos_km)


def _ffn_kernel(te_ref, tv_ref, nx_ref, xs_ref, wgu_hbm, bgu_ref, wd_hbm, bd_ref, y_ref,
                wgu_f32, wd_f32, wgu_bf, wd_bf, sems):
    i = pl.program_id(0)
    valid = tv_ref[i]
    tm, half = xs_ref.shape
    d_exp = wd_bf.shape[0]

    def weight_copies(e):
        return (pltpu.make_async_copy(wgu_hbm.at[e], wgu_f32, sems.at[0]),
                pltpu.make_async_copy(wd_hbm.at[e], wd_f32, sems.at[1]))

    @pl.when(i == 0)
    def _():
        for cp in weight_copies(te_ref[0]):
            cp.start()

    @pl.when(((i == 0) | (te_ref[i] != te_ref[jnp.maximum(i - 1, 0)])) & (valid > 0))
    def _():
        for cp in weight_copies(te_ref[i]):
            cp.wait()
        wgu_bf[...] = wgu_f32[...].astype(BF16)
        wd_bf[...] = wd_f32[...].astype(BF16)

        @pl.when(nx_ref[i] >= 0)
        def _():
            for cp in weight_copies(nx_ref[i]):
                cp.start()

    @pl.when(valid > 0)
    def _():
        keep = lax.broadcasted_iota(I32, (tm, half), 0) < valid
        lo, hi = _unpack_bf16_pair(xs_ref[...])
        x_lo = jnp.where(keep, lo, 0.0).astype(BF16)
        x_hi = jnp.where(keep, hi, 0.0).astype(BF16)
        d = functools.partial(jnp.dot, preferred_element_type=F32)
        nb = 256
        def up(c):
            gate = (d(x_lo, wgu_bf[0:half, c:c + nb]) + d(x_hi, wgu_bf[half:, c:c + nb])
                    + bgu_ref[:, c:c + nb])
            lin = (d(x_lo, wgu_bf[0:half, d_exp + c:d_exp + c + nb])
                   + d(x_hi, wgu_bf[half:, d_exp + c:d_exp + c + nb]) + bgu_ref[:, d_exp + c:d_exp + c + nb])
            return gate, lin

        def activate(gate, lin):
            gate = jnp.minimum(gate, SWIGLU_LIMIT)
            lin = jnp.clip(lin, -SWIGLU_LIMIT, SWIGLU_LIMIT)
            return (gate * jax.nn.sigmoid(SWIGLU_ALPHA * gate) * (lin + 1.0)).astype(BF16)

        chunks = list(range(0, d_exp, nb))
        acc = jnp.zeros((tm, wd_bf.shape[1]), F32) + bd_ref[...]
        pending = up(chunks[0])
        for n, c in enumerate(chunks):
            nxt = up(chunks[n + 1]) if n + 1 < len(chunks) else None
            acc = acc + d(activate(*pending), wd_bf[c:c + nb, :])
            pending = nxt
        y_ref[...] = _pack_bf16_pair(acc[:, :half], acc[:, half:])


def _ffn_call(tile_expert, tile_valid, tile_next, xs, w_gu, b_gu, w_down, b_down, tm):
    R, half = xs.shape
    E, D, two_f = w_gu.shape
    d_exp = w_down.shape[1]
    grid_spec = pltpu.PrefetchScalarGridSpec(
        num_scalar_prefetch=3,
        grid=(R // tm,),
        in_specs=[
            pl.BlockSpec((tm, half), lambda i, te, tv, nx: (i, 0)),
            pl.BlockSpec(memory_space=pl.ANY),
            pl.BlockSpec((None, 1, two_f), lambda i, te, tv, nx: (te[i], 0, 0)),
            pl.BlockSpec(memory_space=pl.ANY),
            pl.BlockSpec((None, 1, D), lambda i, te, tv, nx: (te[i], 0, 0)),
        ],
        out_specs=pl.BlockSpec((tm, half), lambda i, te, tv, nx: (i, 0)),
        scratch_shapes=[
            pltpu.VMEM((D, two_f), F32), pltpu.VMEM((d_exp, D), F32),
            pltpu.VMEM((D, two_f), BF16), pltpu.VMEM((d_exp, D), BF16),
            pltpu.SemaphoreType.DMA((2,)),
        ],
    )
    return pl.pallas_call(
        _ffn_kernel,
        grid_spec=grid_spec,
        out_shape=jax.ShapeDtypeStruct((R, half), U32),
        compiler_params=pltpu.CompilerParams(
            dimension_semantics=("arbitrary",), vmem_limit_bytes=VMEM_LIMIT_BYTES),
        name="moe_ffn",
    )(tile_expert, tile_valid, tile_next, xs, w_gu, b_gu.reshape(E, 1, two_f), w_down,
      b_down.reshape(E, 1, D))


def _final_kernel(x_ref, yp_ref, wts_ref, mod_ref, gfin_ref, *rest):
    o_ref = rest[-1]
    x = x_ref[...]
    half = x.shape[1] // 2
    w = wts_ref[...]
    lo = jnp.zeros((x.shape[0], half), F32)
    hi = jnp.zeros((x.shape[0], half), F32)
    for k in range(TOP_K):
        l, h = _unpack_bf16_pair(yp_ref[k])
        lo = lo + w[:, k:k + 1] * l
        hi = hi + w[:, k:k + 1] * h
    gate = mod_ref[5:6, :]
    x_lo = x[:, :half] + gate[:, :half] * lo
    x_hi = x[:, half:] + gate[:, half:] * hi
    ms = (jnp.sum(x_lo * x_lo, axis=-1, keepdims=True) + jnp.sum(x_hi * x_hi, axis=-1, keepdims=True)) / x.shape[1]
    inv = lax.rsqrt(ms + RMS_EPS)
    o_ref[:, :half] = x_lo * inv * gfin_ref[:, :half]
    o_ref[:, half:] = x_hi * inv * gfin_ref[:, half:]


def _final_call(x1, yp, wts, mod, g_final, tokens_per_batch, group, n_groups, prev_out):
    T, D = x1.shape
    T_all = T * n_groups
    tq = min(512, tokens_per_batch)
    per_b = tokens_per_batch // tq
    first = group * (T // tq)
    in_specs = [
        pl.BlockSpec((tq, D), lambda i: (i, 0)),
        pl.BlockSpec((TOP_K, tq, D // 2), lambda i: (0, i, 0)),
        pl.BlockSpec((tq, ROUTE_LANES), lambda i: (i, 0)),
        pl.BlockSpec((None, N_MOD, D), lambda i: ((first + i) // per_b, 0, 0)),
        pl.BlockSpec((1, D), lambda i: (0, 0)),
    ]
    args = [x1, yp, wts, mod, g_final.reshape(1, D)]
    aliases = {}
    if prev_out is not None:
        in_specs.append(pl.BlockSpec(memory_space=pl.ANY))
        args.append(prev_out)
        aliases = {len(args) - 1: 0}
    return pl.pallas_call(
        _final_kernel,
        grid=(T // tq,),
        in_specs=in_specs,
        out_specs=pl.BlockSpec((tq, D), lambda i: (first + i, 0)),
        out_shape=jax.ShapeDtypeStruct((T_all, D), F32),
        input_output_aliases=aliases,
        compiler_params=pltpu.CompilerParams(
            dimension_semantics=("arbitrary",), vmem_limit_bytes=VMEM_LIMIT_BYTES),
        name="moe_combine_final",
    )(*args)


def _group_layout(counts, n_tiles, tm):
    padded = ((counts + tm - 1) // tm) * tm
    ends = jnp.cumsum(padded)
    starts = ends - padded
    tile_row = jnp.arange(n_tiles, dtype=I32) * tm
    te = jnp.minimum(jnp.sum(tile_row[:, None] >= ends[None, :], axis=1), N_EXPERTS - 1).astype(I32)
    eids = jnp.arange(N_EXPERTS, dtype=I32)
    mine = te[:, None] == eids[None, :]
    lookup = lambda table: jnp.sum(jnp.where(mine, table[None, :], 0), axis=1)
    tv = jnp.clip(lookup(counts) - (tile_row - lookup(starts)), 0, tm).astype(I32)
    later = (eids[None, :] > eids[:, None]) & (counts[None, :] > 0)
    nxt = jnp.min(jnp.where(later, eids[None, :], N_EXPERTS), axis=1)
    nx = lookup(jnp.where(nxt < N_EXPERTS, nxt, -1)).astype(I32)
    return starts, te, tv, nx


def kernel(x, c, positions, w_ada, b_ada, g_mix, w_in, w_dw, b_dw, g_conv_ln, b_conv_ln, g_ret_norm,
           w_out, g_ffn, w_router, b_router, w_gu, b_gu, w_down, b_down, g_final):
    B, S, D = x.shape
    T = B * S
    assert w_ada.shape[0] == 1, "single-layer block: the final norm directly follows layer 0"
    xt = x
    for l in range(1):
        mod = _mod_call(c, w_ada[l], b_ada[l]).reshape(B, N_MOD, D)
        n_groups = MOE_TOKEN_GROUPS if B % MOE_TOKEN_GROUPS == 0 else 1
        Bg = B // n_groups
        Tg = Bg * S
        tm = 512 if Tg * TOP_K >= 512 * N_EXPERTS * 4 else 128
        n_tiles = (Tg * TOP_K) // tm + N_EXPERTS
        x1s, scattered = [], []
        pos_km = jnp.zeros((TOP_K * Tg,), I32)
        for g in range(n_groups):
            x1 = _mixer_call(xt, positions, mod, g_mix[l], w_in[l], w_dw[l], b_dw[l], g_conv_ln[l],
                             b_conv_ln[l], g_ret_norm[l], w_out[l], g, n_groups, pos_km).reshape(Tg, D)
            h2p, idx, wts, rank, counts = _route_call(x1, mod, g_ffn[l], w_router[l], b_router[l], S, g * Bg)
            starts, te, tv, nx = _group_layout(counts[:, 0], n_tiles, tm)
            pos_km = rank[:TOP_K]
            for e in range(N_EXPERTS):
                pos_km = pos_km + jnp.where(idx[:TOP_K] == e, starts[e], 0)
            pos_km = pos_km.astype(I32).reshape(-1)
            x1s.append(x1)
            scattered.append((_sc_scatter_call(h2p, pos_km, n_tiles * tm), pos_km, wts, te, tv, nx))
        out = None
        for g, (xs, pos_km, wts, te, tv, nx) in enumerate(scattered):
            ys = _ffn_call(te, tv, nx, xs, w_gu[l], b_gu[l], w_down[l], b_down[l], tm)
            yp = _sc_gather_call(ys, pos_km)
            out = _final_call(x1s[g], yp.reshape(TOP_K, Tg, D // 2), wts, mod, g_final, S, g, n_groups, out)
        xt = out
    return xt.reshape(B, S, D)
```

```python
import functools

import numpy as np
import jax
import jax.numpy as jnp
from jax import lax
from jax.experimental import pallas as pl
from jax.experimental.pallas import tpu as pltpu
from jax.experimental.pallas import tpu_sc as plsc

F32 = jnp.float32
BF16 = jnp.bfloat16
U32 = jnp.uint32
I32 = jnp.int32

CONV_WIDTH = 31
CONV_HALO = 32
RET_HEADS = 4
RET_DIM = 128
RET_CHUNK = 128
ROPE_BASE = 10000.0
N_EXPERTS = 32
TOP_K = 4
SWIGLU_LIMIT = 7.0
SWIGLU_ALPHA = 1.702
RMS_EPS = 1e-6
LN_EPS = 1e-5
N_MOD = 6

VMEM_LIMIT_BYTES = 56 * 1024 * 1024
MOE_TOKEN_GROUPS = 2


def _split_bf16(a):
    hi = a.astype(BF16)
    lo = (a - hi.astype(F32)).astype(BF16)
    return hi, lo


def _dot3(a, b_hi, b_lo):
    a_hi, a_lo = _split_bf16(a)
    d = functools.partial(jnp.dot, preferred_element_type=F32)
    return d(a_hi, b_hi) + (d(a_hi, b_lo) + d(a_lo, b_hi))


def _pack_bf16_pair(lo, hi):
    lo_bits = lax.bitcast_convert_type(lo.astype(BF16).astype(F32), U32)
    hi_bits = lax.bitcast_convert_type(hi.astype(BF16).astype(F32), U32)
    return (lo_bits >> 16) | (hi_bits & jnp.uint32(0xFFFF0000))


def _unpack_bf16_pair(p):
    lo = lax.bitcast_convert_type(p << 16, F32)
    hi = lax.bitcast_convert_type(p & jnp.uint32(0xFFFF0000), F32)
    return lo, hi


def _mod_kernel(c_ref, whi_ref, wlo_ref, b_ref, o_ref):
    c = c_ref[...]
    c_act = c * jax.nn.sigmoid(c)
    o_ref[...] = _dot3(c_act, whi_ref[...], wlo_ref[...]) + b_ref[...]


def _mod_call(c, w_ada, b_ada):
    B, D = c.shape
    n = w_ada.shape[1]
    bn = 1024
    w_hi, w_lo = _split_bf16(w_ada)
    return pl.pallas_call(
        _mod_kernel,
        grid=(n // bn,),
        in_specs=[
            pl.BlockSpec((B, D), lambda j: (0, 0)),
            pl.BlockSpec((D, bn), lambda j: (0, j)),
            pl.BlockSpec((D, bn), lambda j: (0, j)),
            pl.BlockSpec((1, bn), lambda j: (0, j)),
        ],
        out_specs=pl.BlockSpec((B, bn), lambda j: (0, j)),
        out_shape=jax.ShapeDtypeStruct((B, n), F32),
        name="adaln_mod",
    )(c, w_hi, w_lo, b_ada.reshape(1, n))


def _retention_tables():
    h = np.arange(RET_HEADS, dtype=np.float32)
    log_gamma = np.log(1.0 - np.power(2.0, -5.0 - h)).astype(np.float32)
    idx = np.arange(RET_CHUNK, dtype=np.float32)
    diff = idx[:, None] - idx[None, :]
    causal = diff >= 0
    mask = np.where(causal[None], np.exp(log_gamma[:, None, None] * np.where(causal, diff, 0.0)[None]), 0.0)
    q_decay = np.exp(log_gamma[:, None] * (idx + 1.0))[..., None]
    k_decay = np.exp(log_gamma[:, None] * (RET_CHUNK - 1.0 - idx))[..., None]
    chunk_decay = np.exp(log_gamma * RET_CHUNK)
    return (mask.astype(np.float32), q_decay.astype(np.float32), k_decay.astype(np.float32),
            [float(v) for v in chunk_decay.astype(np.float32)])


def _mixer_kernel(chunk_decay, ts, x_ref, pos_ref, mod_ref, gmix_ref, win_ref, wdw_ref, bdw_ref,
                  gcl_ref, bcl_ref, gret_ref, wout_ref, rope_ref, dmask_ref, qdec_ref, kdec_ref,
                  after_ref, o_ref, proj_ref, uext_ref, state_ref, cat_ref, conv_ref):
    del after_ref
    s = pl.program_id(1)
    conv_ch = wdw_ref.shape[1]
    ret_w = RET_HEADS * RET_DIM

    @pl.when(s == 0)
    def _():
        uext_ref[0:CONV_HALO, :] = jnp.zeros((CONV_HALO, conv_ch), F32)
        state_ref[...] = jnp.zeros_like(state_ref)

    x = x_ref[...]
    sh = mod_ref[0:1, :]
    sc = mod_ref[1:2, :]
    y = x * lax.rsqrt(jnp.mean(x * x, axis=-1, keepdims=True) + RMS_EPS) * gmix_ref[...]
    h = (y * (1.0 + sc) + sh).astype(BF16)
    proj_ref[...] = jnp.dot(h, win_ref[...], preferred_element_type=F32)

    a = proj_ref[:, 0:conv_ch]
    b = proj_ref[:, conv_ch:2 * conv_ch]
    uext_ref[CONV_HALO:CONV_HALO + ts, :] = a * jax.nn.sigmoid(b)
    cb = 64
    lead = CONV_HALO - (CONV_WIDTH - 1)
    span = cb + CONV_HALO

    def conv_block(r0):
        for c0 in range(0, conv_ch, 128):
            xw = uext_ref[r0:r0 + span, c0:c0 + 128]
            acc = jnp.zeros((cb, 128), F32) + bdw_ref[:, c0:c0 + 128]
            for r in range(8):
                xr = xw if r == 0 else pltpu.roll(xw, span - r, 0)
                for q in range((lead + CONV_WIDTH - 1 - r) // 8 + 1):
                    j = 8 * q + r - lead
                    if 0 <= j < CONV_WIDTH:
                        acc = acc + wdw_ref[j:j + 1, c0:c0 + 128] * xr[8 * q:8 * q + cb]
            conv_ref[r0:r0 + cb, c0:c0 + 128] = acc
    rb = 2 * cb

    def conv_norm(r0):
        acc = conv_ref[r0:r0 + rb, 0:conv_ch]
        mu = jnp.mean(acc, axis=-1, keepdims=True)
        d = acc - mu
        var = jnp.mean(d * d, axis=-1, keepdims=True)
        ln = d * lax.rsqrt(var + LN_EPS) * gcl_ref[...] + bcl_ref[...]
        cat_ref[r0:r0 + rb, 0:conv_ch] = (ln * jax.nn.sigmoid(ln)).astype(BF16)

    ang = pos_ref[...].astype(F32) * rope_ref[...]
    cos2 = jnp.cos(ang)
    sin = jnp.sin(ang)
    lane = lax.broadcasted_iota(I32, ang.shape, 1)
    sin2 = jnp.where(lane < RET_DIM // 2, -sin, sin)
    q0 = 2 * conv_ch
    k0 = q0 + ret_w
    v0 = k0 + ret_w
    g0 = v0 + ret_w
    scale = RET_DIM ** -0.5

    def retention_unit(hd, n):
        c0 = hd * RET_DIM
        r0 = n * RET_CHUNK
        rows = slice(r0, r0 + RET_CHUNK)
        cs = cos2[rows]
        sn = sin2[rows]
        q = proj_ref[rows, q0 + c0:q0 + c0 + RET_DIM]
        k = proj_ref[rows, k0 + c0:k0 + c0 + RET_DIM]
        v = proj_ref[rows, v0 + c0:v0 + c0 + RET_DIM].astype(BF16)
        g = proj_ref[rows, g0 + c0:g0 + c0 + RET_DIM]
        qr = q * cs + pltpu.roll(q, RET_DIM // 2, 1) * sn
        kr = (k * cs + pltpu.roll(k, RET_DIM // 2, 1) * sn) * scale
        st = state_ref[hd]
        scores = lax.dot_general(qr.astype(BF16), kr.astype(BF16), (((1,), (1,)), ((), ())),
                                 preferred_element_type=F32) * dmask_ref[hd]
        inner = jnp.dot(scores.astype(BF16), v, preferred_element_type=F32)
        cross = jnp.dot((qr * qdec_ref[hd]).astype(BF16), st.astype(BF16), preferred_element_type=F32)
        kv = lax.dot_general((kr * kdec_ref[hd]).astype(BF16), v, (((0,), (0,)), ((), ())),
                             preferred_element_type=F32)
        state_ref[hd] = chunk_decay[hd] * st + kv
        r = inner + cross
        mu = jnp.mean(r, axis=-1, keepdims=True)
        d = r - mu
        var = jnp.mean(d * d, axis=-1, keepdims=True)
        rn = d * lax.rsqrt(var + LN_EPS) * gret_ref[:, c0:c0 + RET_DIM]
        cat_ref[rows, conv_ch + c0:conv_ch + c0 + RET_DIM] = (g * jax.nn.sigmoid(g) * rn).astype(BF16)

    units = [(hd, n) for n in range(ts // RET_CHUNK) for hd in range(RET_HEADS)]
    n_conv = ts // cb
    per = -(-len(units) // n_conv)
    for i in range(n_conv):
        conv_block(i * cb)
        if i % 2 == 1:
            conv_norm((i - 1) * cb)
        for hd, n in units[i * per:(i + 1) * per]:
            retention_unit(hd, n)
    uext_ref[0:CONV_HALO, :] = uext_ref[ts:ts + CONV_HALO, :]

    out = jnp.dot(cat_ref[...], wout_ref[...], preferred_element_type=F32)
    o_ref[...] = x + mod_ref[2:3, :] * out


def _mixer_call(x, positions, mod, g_mix, w_in, w_dw, b_dw, g_conv_ln, b_conv_ln, g_ret_norm, w_out,
                group, n_groups, after):
    S, D = x.shape[1:]
    B = x.shape[0] // n_groups
    b0 = group * B
    in_cols = w_in.shape[1]
    conv_ch = w_dw.shape[1]
    ts = min(512, S)
    mask, q_decay, k_decay, chunk_decay = _retention_tables()
    half = RET_DIM // 2
    inv_freq = (ROPE_BASE ** (-np.arange(half, dtype=np.float32) / half)).astype(np.float32)
    rope = np.concatenate([inv_freq, inv_freq])[None, :]
    w_dw_p = jnp.zeros((CONV_HALO, conv_ch), F32).at[:CONV_WIDTH].set(w_dw)
    full = lambda shape: pl.BlockSpec(shape, lambda b, s: (0,) * len(shape))
    return pl.pallas_call(
        functools.partial(_mixer_kernel, chunk_decay, ts),
        grid=(B, S // ts),
        in_specs=[
            pl.BlockSpec((None, ts, D), lambda b, s: (b0 + b, s, 0)),
            pl.BlockSpec((None, ts, 1), lambda b, s: (b0 + b, s, 0)),
            pl.BlockSpec((None, N_MOD, D), lambda b, s: (b0 + b, 0, 0)),
            full((1, D)),
            full((D, in_cols)),
            full((CONV_HALO, conv_ch)),
            full((1, conv_ch)),
            full((1, conv_ch)),
            full((1, conv_ch)),
            full((1, RET_HEADS * RET_DIM)),
            full((conv_ch + RET_HEADS * RET_DIM, D)),
            full((1, RET_DIM)),
            full((RET_HEADS, RET_CHUNK, RET_CHUNK)),
            full((RET_HEADS, RET_CHUNK, 1)),
            full((RET_HEADS, RET_CHUNK, 1)),
            pl.BlockSpec(memory_space=pl.ANY),
        ],
        out_specs=pl.BlockSpec((None, ts, D), lambda b, s: (b, s, 0)),
        out_shape=jax.ShapeDtypeStruct((B, S, D), F32),
        scratch_shapes=[
            pltpu.VMEM((ts, in_cols), F32),
            pltpu.VMEM((CONV_HALO + ts, conv_ch), F32),
            pltpu.VMEM((RET_HEADS, RET_DIM, RET_DIM), F32),
            pltpu.VMEM((ts, conv_ch + RET_HEADS * RET_DIM), BF16),
            pltpu.VMEM((ts, conv_ch), F32),
        ],
        compiler_params=pltpu.CompilerParams(
            dimension_semantics=("arbitrary", "arbitrary"), vmem_limit_bytes=VMEM_LIMIT_BYTES),
        name="hybrid_mixer",
    )(x, positions.reshape(-1, S, 1), mod, g_mix.reshape(1, D), w_in.astype(BF16), w_dw_p,
      b_dw.reshape(1, -1), g_conv_ln.reshape(1, -1), b_conv_ln.reshape(1, -1), g_ret_norm.reshape(1, -1),
      w_out.astype(BF16), jnp.asarray(rope), jnp.asarray(mask), jnp.asarray(q_decay),
      jnp.asarray(k_decay), after)


def _route_kernel(x_ref, mod_ref, gffn_ref, wr2_ref, wrhi_ref, br_ref, utri_ref,
                  h2p_ref, idx_ref, wts_ref, rank_ref, cnt_ref, carry_ref):
    i = pl.program_id(0)

    @pl.when(i == 0)
    def _():
        carry_ref[...] = jnp.zeros_like(carry_ref)

    x = x_ref[...]
    tr = x.shape[0]
    half = x.shape[1] // 2
    y = x * lax.rsqrt(jnp.mean(x * x, axis=-1, keepdims=True) + RMS_EPS) * gffn_ref[...]
    h2 = y * (1.0 + mod_ref[4:5, :]) + mod_ref[3:4, :]
    h2p_ref[...] = _pack_bf16_pair(h2[:, :half], h2[:, half:])

    h_hi, h_lo = _split_bf16(h2)
    nt = (((1,), (1,)), ((), ()))
    r = lax.dot_general(wr2_ref[...], h_hi, nt, preferred_element_type=F32)
    r2 = lax.dot_general(wrhi_ref[...], h_lo, nt, preferred_element_type=F32)
    l = r[:N_EXPERTS] + (r[N_EXPERTS:] + r2) + br_ref[...]
    eid = lax.broadcasted_iota(I32, l.shape, 0)
    vals, sels, idxs = [], [], []
    for _ in range(TOP_K):
        m = jnp.max(l, axis=0, keepdims=True)
        ik = jnp.min(jnp.where(l == m, eid, N_EXPERTS), axis=0, keepdims=True)
        sel = eid == ik
        vals.append(m)
        sels.append(sel)
        idxs.append(ik)
        l = jnp.where(sel, -jnp.inf, l)
    exps = [jnp.exp(v - vals[0]) for v in vals]
    denom = exps[0] + exps[1] + exps[2] + exps[3]
    member = jnp.zeros(l.shape, F32)
    for sel in sels:
        member = member + sel.astype(F32)
    before = jnp.dot(member.astype(BF16), utri_ref[...], preferred_element_type=F32) + carry_ref[...]
    ranks = [jnp.sum(jnp.where(sel, before, 0.0), axis=0, keepdims=True) for sel in sels]
    carry_ref[...] = carry_ref[...] + jnp.sum(member, axis=1, keepdims=True)
    cnt_ref[...] = carry_ref[...].astype(I32)

    def rows(pieces, n_rows, dtype):
        rid = lax.broadcasted_iota(I32, (n_rows, tr), 0)
        out = jnp.zeros((n_rows, tr), dtype)
        for k, p in enumerate(pieces):
            out = jnp.where(rid == k, p.astype(dtype), out)
        return out

    idx_ref[...] = rows(idxs, idx_ref.shape[0], I32)
    rank_ref[...] = rows(ranks, rank_ref.shape[0], I32)
    wts_ref[...] = rows([e / denom for e in exps], wts_ref.shape[1], F32).T


ROUTE_ROWS = 8
ROUTE_LANES = 128


def _route_call(x1, mod, g_ffn, w_router, b_router, tokens_per_batch, batch0):
    T, D = x1.shape
    tr = min(512, tokens_per_batch)
    per_b = tokens_per_batch // tr
    wr_hi, wr_lo = _split_bf16(w_router.T)
    utri = jnp.asarray(np.triu(np.ones((tr, tr), np.float32), 1), BF16)
    full = lambda shape: pl.BlockSpec(shape, lambda i: (0,) * len(shape))
    return pl.pallas_call(
        _route_kernel,
        grid=(T // tr,),
        in_specs=[
            pl.BlockSpec((tr, D), lambda i: (i, 0)),
            pl.BlockSpec((None, N_MOD, D), lambda i: (batch0 + i // per_b, 0, 0)),
            full((1, D)),
            full((2 * N_EXPERTS, D)),
            full((N_EXPERTS, D)),
            full((N_EXPERTS, 1)),
            full((tr, tr)),
        ],
        out_specs=[
            pl.BlockSpec((tr, D // 2), lambda i: (i, 0)),
            pl.BlockSpec((ROUTE_ROWS, tr), lambda i: (0, i)),
            pl.BlockSpec((tr, ROUTE_LANES), lambda i: (i, 0)),
            pl.BlockSpec((ROUTE_ROWS, tr), lambda i: (0, i)),
            full((N_EXPERTS, 1)),
        ],
        out_shape=[
            jax.ShapeDtypeStruct((T, D // 2), U32),
            jax.ShapeDtypeStruct((ROUTE_ROWS, T), I32),
            jax.ShapeDtypeStruct((T, ROUTE_LANES), F32),
            jax.ShapeDtypeStruct((ROUTE_ROWS, T), I32),
            jax.ShapeDtypeStruct((N_EXPERTS, 1), I32),
        ],
        scratch_shapes=[pltpu.VMEM((N_EXPERTS, 1), F32)],
        compiler_params=pltpu.CompilerParams(
            dimension_semantics=("arbitrary",), vmem_limit_bytes=VMEM_LIMIT_BYTES),
        name="moe_route",
    )(x1, mod, g_ffn.reshape(1, D), jnp.concatenate([wr_hi, wr_lo], axis=0), wr_hi,
      b_router.reshape(N_EXPERTS, 1), utri)


SC_ROWS = 128


V7X_SC_CORES = 2
V7X_SC_SUBCORES = 16


def _sc_workers():
    return V7X_SC_CORES, V7X_SC_SUBCORES


def _sc_mesh():
    return plsc.VectorSubcoreMesh(core_axis_name="c", subcore_axis_name="s",
                                  num_cores=V7X_SC_CORES, num_subcores=V7X_SC_SUBCORES)


def _sc_scatter_call(h2p, pos_km, n_rows):
    T, W = h2p.shape
    nc, ns = _sc_workers()
    n = SC_ROWS
    per_w = T // (nc * ns * n)

    def body(h2p_hbm, pos_hbm, xs_hbm, i0, i1, i2, i3, rows_v, sem):
        wid = lax.axis_index("s") * nc + lax.axis_index("c")
        idx_refs = (i0, i1, i2, i3)

        @pl.loop(0, per_w)
        def _(j):
            t0 = (wid * per_w + j) * n
            pltpu.sync_copy(h2p_hbm.at[pl.ds(t0, n)], rows_v)
            for k in range(TOP_K):
                pltpu.sync_copy(pos_hbm.at[pl.ds(k * T + t0, n)], idx_refs[k])
            copies = [pltpu.async_copy(rows_v, xs_hbm.at[idx_refs[k]], sem) for k in range(TOP_K)]
            for cp in copies:
                cp.wait()

    return pl.kernel(
        body,
        out_type=jax.ShapeDtypeStruct((n_rows, W), h2p.dtype),
        mesh=_sc_mesh(),
        scratch_types=[pltpu.VMEM((n,), I32)] * TOP_K + [pltpu.VMEM((n, W), h2p.dtype), pltpu.SemaphoreType.DMA],
        name="moe_scatter_rows",
    )(h2p, pos_km)


def _sc_gather_call(ys, pos_km):
    P = pos_km.shape[0]
    W = ys.shape[1]
    nc, ns = _sc_workers()
    n = SC_ROWS
    per_w = P // (nc * ns * n)

    def body(ys_hbm, pos_hbm, yp_hbm, idx_v, rows_v, sem):
        wid = lax.axis_index("s") * nc + lax.axis_index("c")

        @pl.loop(0, per_w)
        def _(j):
            p0 = (wid * per_w + j) * n
            pltpu.sync_copy(pos_hbm.at[pl.ds(p0, n)], idx_v)
            pltpu.async_copy(ys_hbm.at[idx_v], rows_v, sem).wait()
            pltpu.sync_copy(rows_v, yp_hbm.at[pl.ds(p0, n)])

    return pl.kernel(
        body,
        out_type=jax.ShapeDtypeStruct((P, W), ys.dtype),
        mesh=_sc_mesh(),
        scratch_types=[pltpu.VMEM((n,), I32), pltpu.VMEM((n, W), ys.dtype), pltpu.SemaphoreType.DMA],
        name="moe_gather_rows",
    )(ys, pos_km)


def _ffn_kernel(te_ref, tv_ref, nx_ref, xs_ref, wgu_hbm, bgu_ref, wd_hbm, bd_ref, y_ref,
                wgu_f32, wd_f32, wgu_bf, wd_bf, sems):
    i = pl.program_id(0)
    valid = tv_ref[i]
    tm, half = xs_ref.shape
    d_exp = wd_bf.shape[0]

    def weight_copies(e):
        return (pltpu.make_async_copy(wgu_hbm.at[e], wgu_f32, sems.at[0]),
                pltpu.make_async_copy(wd_hbm.at[e], wd_f32, sems.at[1]))

    @pl.when(i == 0)
    def _():
        for cp in weight_copies(te_ref[0]):
            cp.start(priority=1)

    @pl.when(((i == 0) | (te_ref[i] != te_ref[jnp.maximum(i - 1, 0)])) & (valid > 0))
    def _():
        for cp in weight_copies(te_ref[i]):
            cp.wait()
        wgu_bf[...] = wgu_f32[...].astype(BF16)
        wd_bf[...] = wd_f32[...].astype(BF16)

        @pl.when(nx_ref[i] >= 0)
        def _():
            for cp in weight_copies(nx_ref[i]):
                cp.start(priority=1)

    @pl.when(valid > 0)
    def _():
        keep = lax.broadcasted_iota(I32, (tm, half), 0) < valid
        lo, hi = _unpack_bf16_pair(xs_ref[...])
        x_lo = jnp.where(keep, lo, 0.0).astype(BF16)
        x_hi = jnp.where(keep, hi, 0.0).astype(BF16)
        d = functools.partial(jnp.dot, preferred_element_type=F32)
        nb = 256
        def up(c):
            gate = (d(x_lo, wgu_bf[0:half, c:c + nb]) + d(x_hi, wgu_bf[half:, c:c + nb])
                    + bgu_ref[:, c:c + nb])
            lin = (d(x_lo, wgu_bf[0:half, d_exp + c:d_exp + c + nb])
                   + d(x_hi, wgu_bf[half:, d_exp + c:d_exp + c + nb]) + bgu_ref[:, d_exp + c:d_exp + c + nb])
            return gate, lin

        def activate(gate, lin):
            gate = jnp.minimum(gate, SWIGLU_LIMIT)
            lin = jnp.clip(lin, -SWIGLU_LIMIT, SWIGLU_LIMIT)
            return (gate * jax.nn.sigmoid(SWIGLU_ALPHA * gate) * (lin + 1.0)).astype(BF16)

        chunks = list(range(0, d_exp, nb))
        acc = jnp.zeros((tm, wd_bf.shape[1]), F32) + bd_ref[...]
        pending = up(chunks[0])
        for n, c in enumerate(chunks):
            nxt = up(chunks[n + 1]) if n + 1 < len(chunks) else None
            acc = acc + d(activate(*pending), wd_bf[c:c + nb, :])
            pending = nxt
        y_ref[...] = _pack_bf16_pair(acc[:, :half], acc[:, half:])


def _ffn_call(tile_expert, tile_valid, tile_next, xs, w_gu, b_gu, w_down, b_down, tm):
    R, half = xs.shape
    E, D, two_f = w_gu.shape
    d_exp = w_down.shape[1]
    grid_spec = pltpu.PrefetchScalarGridSpec(
        num_scalar_prefetch=3,
        grid=(R // tm,),
        in_specs=[
            pl.BlockSpec((tm, half), lambda i, te, tv, nx: (i, 0)),
            pl.BlockSpec(memory_space=pl.ANY),
            pl.BlockSpec((None, 1, two_f), lambda i, te, tv, nx: (te[i], 0, 0)),
            pl.BlockSpec(memory_space=pl.ANY),
            pl.BlockSpec((None, 1, D), lambda i, te, tv, nx: (te[i], 0, 0)),
        ],
        out_specs=pl.BlockSpec((tm, half), lambda i, te, tv, nx: (i, 0)),
        scratch_shapes=[
            pltpu.VMEM((D, two_f), F32), pltpu.VMEM((d_exp, D), F32),
            pltpu.VMEM((D, two_f), BF16), pltpu.VMEM((d_exp, D), BF16),
            pltpu.SemaphoreType.DMA((2,)),
        ],
    )
    return pl.pallas_call(
        _ffn_kernel,
        grid_spec=grid_spec,
        out_shape=jax.ShapeDtypeStruct((R, half), U32),
        compiler_params=pltpu.CompilerParams(
            dimension_semantics=("arbitrary",), vmem_limit_bytes=VMEM_LIMIT_BYTES),
        name="moe_ffn",
    )(tile_expert, tile_valid, tile_next, xs, w_gu, b_gu.reshape(E, 1, two_f), w_down,
      b_down.reshape(E, 1, D))


def _final_kernel(x_ref, yp_ref, wts_ref, mod_ref, gfin_ref, *rest):
    o_ref = rest[-1]
    x = x_ref[...]
    half = x.shape[1] // 2
    w = wts_ref[...]
    lo = jnp.zeros((x.shape[0], half), F32)
    hi = jnp.zeros((x.shape[0], half), F32)
    for k in range(TOP_K):
        l, h = _unpack_bf16_pair(yp_ref[k])
        lo = lo + w[:, k:k + 1] * l
        hi = hi + w[:, k:k + 1] * h
    gate = mod_ref[5:6, :]
    x_lo = x[:, :half] + gate[:, :half] * lo
    x_hi = x[:, half:] + gate[:, half:] * hi
    ms = (jnp.sum(x_lo * x_lo, axis=-1, keepdims=True) + jnp.sum(x_hi * x_hi, axis=-1, keepdims=True)) / x.shape[1]
    inv = lax.rsqrt(ms + RMS_EPS)
    o_ref[:, :half] = x_lo * inv * gfin_ref[:, :half]
    o_ref[:, half:] = x_hi * inv * gfin_ref[:, half:]


def _final_call(x1, yp, wts, mod, g_final, tokens_per_batch, group, n_groups, prev_out):
    T, D = x1.shape
    T_all = T * n_groups
    tq = min(512, tokens_per_batch)
    per_b = tokens_per_batch // tq
    first = group * (T // tq)
    in_specs = [
        pl.BlockSpec((tq, D), lambda i: (i, 0)),
        pl.BlockSpec((TOP_K, tq, D // 2), lambda i: (0, i, 0)),
        pl.BlockSpec((tq, ROUTE_LANES), lambda i: (i, 0)),
        pl.BlockSpec((None, N_MOD, D), lambda i: ((first + i) // per_b, 0, 0)),
        pl.BlockSpec((1, D), lambda i: (0, 0)),
    ]
    args = [x1, yp, wts, mod, g_final.reshape(1, D)]
    aliases = {}
    if prev_out is not None:
        in_specs.append(pl.BlockSpec(memory_space=pl.ANY))
        args.append(prev_out)
        aliases = {len(args) - 1: 0}
    return pl.pallas_call(
        _final_kernel,
        grid=(T // tq,),
        in_specs=in_specs,
        out_specs=pl.BlockSpec((tq, D), lambda i: (first + i, 0)),
        out_shape=jax.ShapeDtypeStruct((T_all, D), F32),
        input_output_aliases=aliases,
        compiler_params=pltpu.CompilerParams(
            dimension_semantics=("arbitrary",), vmem_limit_bytes=VMEM_LIMIT_BYTES),
        name="moe_combine_final",
    )(*args)


def _group_layout(counts, n_tiles, tm):
    padded = ((counts + tm - 1) // tm) * tm
    ends = jnp.cumsum(padded)
    starts = ends - padded
    tile_row = jnp.arange(n_tiles, dtype=I32) * tm
    te = jnp.minimum(jnp.sum(tile_row[:, None] >= ends[None, :], axis=1), N_EXPERTS - 1).astype(I32)
    eids = jnp.arange(N_EXPERTS, dtype=I32)
    mine = te[:, None] == eids[None, :]
    lookup = lambda table: jnp.sum(jnp.where(mine, table[None, :], 0), axis=1)
    tv = jnp.clip(lookup(counts) - (tile_row - lookup(starts)), 0, tm).astype(I32)
    later = (eids[None, :] > eids[:, None]) & (counts[None, :] > 0)
    nxt = jnp.min(jnp.where(later, eids[None, :], N_EXPERTS), axis=1)
    nx = lookup(jnp.where(nxt < N_EXPERTS, nxt, -1)).astype(I32)
    return starts, te, tv, nx


def kernel(x, c, positions, w_ada, b_ada, g_mix, w_in, w_dw, b_dw, g_conv_ln, b_conv_ln, g_ret_norm,
           w_out, g_ffn, w_router, b_router, w_gu, b_gu, w_down, b_down, g_final):
    B, S, D = x.shape
    T = B * S
    assert w_ada.shape[0] == 1, "single-layer block: the final norm directly follows layer 0"
    xt = x
    for l in range(1):
        mod = _mod_call(c, w_ada[l], b_ada[l]).reshape(B, N_MOD, D)
        n_groups = MOE_TOKEN_GROUPS if B % MOE_TOKEN_GROUPS == 0 else 1
        Bg = B // n_groups
        Tg = Bg * S
        tm = 512 if Tg * TOP_K >= 512 * N_EXPERTS * 4 else 128
        n_tiles = (Tg * TOP_K) // tm + N_EXPERTS
        x1s, scattered = [], []
        pos_km = jnp.zeros((TOP_K * Tg,), I32)
        for g in range(n_groups):
            x1 = _mixer_call(xt, positions, mod, g_mix[l], w_in[l], w_dw[l], b_dw[l], g_conv_ln[l],
                             b_conv_ln[l], g_ret_norm[l], w_out[l], g, n_groups, pos_km).reshape(Tg, D)
            h2p, idx, wts, rank, counts = _route_call(x1, mod, g_ffn[l], w_router[l], b_router[l], S, g * Bg)
            starts, te, tv, nx = _group_layout(counts[:, 0], n_tiles, tm)
            pos_km = rank[:TOP_K]
            for e in range(N_EXPERTS):
                pos_km = pos_km + jnp.where(idx[:TOP_K] == e, starts[e], 0)
            pos_km = pos_km.astype(I32).reshape(-1)
            x1s.append(x1)
            scattered.append((_sc_scatter_call(h2p, pos_km, n_tiles * tm), pos_km, wts, te, tv, nx))
        out = None
        for g, (xs, pos_km, wts, te, tv, nx) in enumerate(scattered):
            ys = _ffn_call(te, tv, nx, xs, w_gu[l], b_gu[l], w_down[l], b_down[l], tm)
            yp = _sc_gather_call(ys, pos_km)
            out = _final_call(x1s[g], yp.reshape(TOP_K, Tg, D // 2), wts, mod, g_final, S, g, n_groups, out)
        xt = out
    return xt.reshape(B, S, D)
```

```python
import functools

import numpy as np
import jax
import jax.numpy as jnp
from jax import lax
from jax.experimental import pallas as pl
from jax.experimental.pallas import tpu as pltpu
from jax.experimental.pallas import tpu_sc as plsc

F32 = jnp.float32
BF16 = jnp.bfloat16
U32 = jnp.uint32
I32 = jnp.int32

CONV_WIDTH = 31
CONV_HALO = 32
RET_HEADS = 4
RET_DIM = 128
RET_CHUNK = 128
ROPE_BASE = 10000.0
N_EXPERTS = 32
TOP_K = 4
SWIGLU_LIMIT = 7.0
SWIGLU_ALPHA = 1.702
RMS_EPS = 1e-6
LN_EPS = 1e-5
N_MOD = 6

VMEM_LIMIT_BYTES = 56 * 1024 * 1024
MOE_TOKEN_GROUPS = 2
FFN_TILE_ROWS = 1024
FFN_SUB_ROWS = 512


def _split_bf16(a):
    hi = a.astype(BF16)
    lo = (a - hi.astype(F32)).astype(BF16)
    return hi, lo


def _dot3(a, b_hi, b_lo):
    a_hi, a_lo = _split_bf16(a)
    d = functools.partial(jnp.dot, preferred_element_type=F32)
    return d(a_hi, b_hi) + (d(a_hi, b_lo) + d(a_lo, b_hi))


def _pack_bf16_pair(lo, hi):
    lo_bits = lax.bitcast_convert_type(lo.astype(BF16).astype(F32), U32)
    hi_bits = lax.bitcast_convert_type(hi.astype(BF16).astype(F32), U32)
    return (lo_bits >> 16) | (hi_bits & jnp.uint32(0xFFFF0000))


def _unpack_bf16_pair(p):
    lo = lax.bitcast_convert_type(p << 16, F32)
    hi = lax.bitcast_convert_type(p & jnp.uint32(0xFFFF0000), F32)
    return lo, hi


def _mod_kernel(c_ref, whi_ref, wlo_ref, b_ref, o_ref):
    c = c_ref[...]
    c_act = c * jax.nn.sigmoid(c)
    o_ref[...] = _dot3(c_act, whi_ref[...], wlo_ref[...]) + b_ref[...]


def _mod_call(c, w_ada, b_ada):
    B, D = c.shape
    n = w_ada.shape[1]
    bn = 1024
    w_hi, w_lo = _split_bf16(w_ada)
    return pl.pallas_call(
        _mod_kernel,
        grid=(n // bn,),
        in_specs=[
            pl.BlockSpec((B, D), lambda j: (0, 0)),
            pl.BlockSpec((D, bn), lambda j: (0, j)),
            pl.BlockSpec((D, bn), lambda j: (0, j)),
            pl.BlockSpec((1, bn), lambda j: (0, j)),
        ],
        out_specs=pl.BlockSpec((B, bn), lambda j: (0, j)),
        out_shape=jax.ShapeDtypeStruct((B, n), F32),
        name="adaln_mod",
    )(c, w_hi, w_lo, b_ada.reshape(1, n))


def _retention_tables():
    h = np.arange(RET_HEADS, dtype=np.float32)
    log_gamma = np.log(1.0 - np.power(2.0, -5.0 - h)).astype(np.float32)
    idx = np.arange(RET_CHUNK, dtype=np.float32)
    diff = idx[:, None] - idx[None, :]
    causal = diff >= 0
    mask = np.where(causal[None], np.exp(log_gamma[:, None, None] * np.where(causal, diff, 0.0)[None]), 0.0)
    q_decay = np.exp(log_gamma[:, None] * (idx + 1.0))[..., None]
    k_decay = np.exp(log_gamma[:, None] * (RET_CHUNK - 1.0 - idx))[..., None]
    chunk_decay = np.exp(log_gamma * RET_CHUNK)
    return (mask.astype(np.float32), q_decay.astype(np.float32), k_decay.astype(np.float32),
            [float(v) for v in chunk_decay.astype(np.float32)])


def _mixer_kernel(chunk_decay, ts, x_ref, pos_ref, mod_ref, gmix_ref, win_ref, wdw_ref, bdw_ref,
                  gcl_ref, bcl_ref, gret_ref, wout_ref, rope_ref, dmask_ref, qdec_ref, kdec_ref,
                  after_ref, o_ref, proj_ref, uext_ref, state_ref, cat_ref, conv_ref):
    del after_ref
    s = pl.program_id(1)
    conv_ch = wdw_ref.shape[1]
    ret_w = RET_HEADS * RET_DIM

    @pl.when(s == 0)
    def _():
        uext_ref[0:CONV_HALO, :] = jnp.zeros((CONV_HALO, conv_ch), F32)
        state_ref[...] = jnp.zeros_like(state_ref)

    x = x_ref[...]
    sh = mod_ref[0:1, :]
    sc = mod_ref[1:2, :]
    y = x * lax.rsqrt(jnp.mean(x * x, axis=-1, keepdims=True) + RMS_EPS) * gmix_ref[...]
    h = (y * (1.0 + sc) + sh).astype(BF16)
    proj_ref[...] = jnp.dot(h, win_ref[...], preferred_element_type=F32)

    a = proj_ref[:, 0:conv_ch]
    b = proj_ref[:, conv_ch:2 * conv_ch]
    uext_ref[CONV_HALO:CONV_HALO + ts, :] = a * jax.nn.sigmoid(b)
    cb = 64
    lead = CONV_HALO - (CONV_WIDTH - 1)
    span = cb + CONV_HALO

    def conv_block(r0):
        for c0 in range(0, conv_ch, 128):
            xw = uext_ref[r0:r0 + span, c0:c0 + 128]
            acc = jnp.zeros((cb, 128), F32) + bdw_ref[:, c0:c0 + 128]
            for r in range(8):
                xr = xw if r == 0 else pltpu.roll(xw, span - r, 0)
                for q in range((lead + CONV_WIDTH - 1 - r) // 8 + 1):
                    j = 8 * q + r - lead
                    if 0 <= j < CONV_WIDTH:
                        acc = acc + wdw_ref[j:j + 1, c0:c0 + 128] * xr[8 * q:8 * q + cb]
            conv_ref[r0:r0 + cb, c0:c0 + 128] = acc
    rb = 2 * cb

    def conv_norm(r0):
        acc = conv_ref[r0:r0 + rb, 0:conv_ch]
        mu = jnp.mean(acc, axis=-1, keepdims=True)
        d = acc - mu
        var = jnp.mean(d * d, axis=-1, keepdims=True)
        ln = d * lax.rsqrt(var + LN_EPS) * gcl_ref[...] + bcl_ref[...]
        cat_ref[r0:r0 + rb, 0:conv_ch] = (ln * jax.nn.sigmoid(ln)).astype(BF16)

    ang = pos_ref[...].astype(F32) * rope_ref[...]
    cos2 = jnp.cos(ang)
    sin = jnp.sin(ang)
    lane = lax.broadcasted_iota(I32, ang.shape, 1)
    sin2 = jnp.where(lane < RET_DIM // 2, -sin, sin)
    q0 = 2 * conv_ch
    k0 = q0 + ret_w
    v0 = k0 + ret_w
    g0 = v0 + ret_w
    scale = RET_DIM ** -0.5

    def retention_unit(hd, n):
        c0 = hd * RET_DIM
        r0 = n * RET_CHUNK
        rows = slice(r0, r0 + RET_CHUNK)
        cs = cos2[rows]
        sn = sin2[rows]
        q = proj_ref[rows, q0 + c0:q0 + c0 + RET_DIM]
        k = proj_ref[rows, k0 + c0:k0 + c0 + RET_DIM]
        v = proj_ref[rows, v0 + c0:v0 + c0 + RET_DIM].astype(BF16)
        g = proj_ref[rows, g0 + c0:g0 + c0 + RET_DIM]
        qr = q * cs + pltpu.roll(q, RET_DIM // 2, 1) * sn
        kr = (k * cs + pltpu.roll(k, RET_DIM // 2, 1) * sn) * scale
        st = state_ref[hd]
        scores = lax.dot_general(qr.astype(BF16), kr.astype(BF16), (((1,), (1,)), ((), ())),
                                 preferred_element_type=F32) * dmask_ref[hd]
        inner = jnp.dot(scores.astype(BF16), v, preferred_element_type=F32)
        cross = jnp.dot((qr * qdec_ref[hd]).astype(BF16), st.astype(BF16), preferred_element_type=F32)
        kv = lax.dot_general((kr * kdec_ref[hd]).astype(BF16), v, (((0,), (0,)), ((), ())),
                             preferred_element_type=F32)
        state_ref[hd] = chunk_decay[hd] * st + kv
        r = inner + cross
        mu = jnp.mean(r, axis=-1, keepdims=True)
        d = r - mu
        var = jnp.mean(d * d, axis=-1, keepdims=True)
        rn = d * lax.rsqrt(var + LN_EPS) * gret_ref[:, c0:c0 + RET_DIM]
        cat_ref[rows, conv_ch + c0:conv_ch + c0 + RET_DIM] = (g * jax.nn.sigmoid(g) * rn).astype(BF16)

    units = [(hd, n) for n in range(ts // RET_CHUNK) for hd in range(RET_HEADS)]
    n_conv = ts // cb
    per = -(-len(units) // n_conv)
    for i in range(n_conv):
        conv_block(i * cb)
        if i % 2 == 1:
            conv_norm((i - 1) * cb)
        for hd, n in units[i * per:(i + 1) * per]:
            retention_unit(hd, n)
    uext_ref[0:CONV_HALO, :] = uext_ref[ts:ts + CONV_HALO, :]

    out = jnp.dot(cat_ref[...], wout_ref[...], preferred_element_type=F32)
    o_ref[...] = x + mod_ref[2:3, :] * out


def _mixer_call(x, positions, mod, g_mix, w_in, w_dw, b_dw, g_conv_ln, b_conv_ln, g_ret_norm, w_out,
                group, n_groups, after):
    S, D = x.shape[1:]
    B = x.shape[0] // n_groups
    b0 = group * B
    in_cols = w_in.shape[1]
    conv_ch = w_dw.shape[1]
    ts = min(512, S)
    mask, q_decay, k_decay, chunk_decay = _retention_tables()
    half = RET_DIM // 2
    inv_freq = (ROPE_BASE ** (-np.arange(half, dtype=np.float32) / half)).astype(np.float32)
    rope = np.concatenate([inv_freq, inv_freq])[None, :]
    w_dw_p = jnp.zeros((CONV_HALO, conv_ch), F32).at[:CONV_WIDTH].set(w_dw)
    full = lambda shape: pl.BlockSpec(shape, lambda b, s: (0,) * len(shape))
    return pl.pallas_call(
        functools.partial(_mixer_kernel, chunk_decay, ts),
        grid=(B, S // ts),
        in_specs=[
            pl.BlockSpec((None, ts, D), lambda b, s: (b0 + b, s, 0)),
            pl.BlockSpec((None, ts, 1), lambda b, s: (b0 + b, s, 0)),
            pl.BlockSpec((None, N_MOD, D), lambda b, s: (b0 + b, 0, 0)),
            full((1, D)),
            full((D, in_cols)),
            full((CONV_HALO, conv_ch)),
            full((1, conv_ch)),
            full((1, conv_ch)),
            full((1, conv_ch)),
            full((1, RET_HEADS * RET_DIM)),
            full((conv_ch + RET_HEADS * RET_DIM, D)),
            full((1, RET_DIM)),
            full((RET_HEADS, RET_CHUNK, RET_CHUNK)),
            full((RET_HEADS, RET_CHUNK, 1)),
            full((RET_HEADS, RET_CHUNK, 1)),
            pl.BlockSpec(memory_space=pl.ANY),
        ],
        out_specs=pl.BlockSpec((None, ts, D), lambda b, s: (b, s, 0)),
        out_shape=jax.ShapeDtypeStruct((B, S, D), F32),
        scratch_shapes=[
            pltpu.VMEM((ts, in_cols), F32),
            pltpu.VMEM((CONV_HALO + ts, conv_ch), F32),
            pltpu.VMEM((RET_HEADS, RET_DIM, RET_DIM), F32),
            pltpu.VMEM((ts, conv_ch + RET_HEADS * RET_DIM), BF16),
            pltpu.VMEM((ts, conv_ch), F32),
        ],
        compiler_params=pltpu.CompilerParams(
            dimension_semantics=("arbitrary", "arbitrary"), vmem_limit_bytes=VMEM_LIMIT_BYTES),
        name="hybrid_mixer",
    )(x, positions.reshape(-1, S, 1), mod, g_mix.reshape(1, D), w_in.astype(BF16), w_dw_p,
      b_dw.reshape(1, -1), g_conv_ln.reshape(1, -1), b_conv_ln.reshape(1, -1), g_ret_norm.reshape(1, -1),
      w_out.astype(BF16), jnp.asarray(rope), jnp.asarray(mask), jnp.asarray(q_decay),
      jnp.asarray(k_decay), after)


def _route_kernel(x_ref, mod_ref, gffn_ref, wr2_ref, wrhi_ref, br_ref, utri_ref,
                  h2p_ref, idx_ref, wts_ref, rank_ref, cnt_ref, carry_ref):
    i = pl.program_id(0)

    @pl.when(i == 0)
    def _():
        carry_ref[...] = jnp.zeros_like(carry_ref)

    x = x_ref[...]
    tr = x.shape[0]
    half = x.shape[1] // 2
    y = x * lax.rsqrt(jnp.mean(x * x, axis=-1, keepdims=True) + RMS_EPS) * gffn_ref[...]
    h2 = y * (1.0 + mod_ref[4:5, :]) + mod_ref[3:4, :]
    h2p_ref[...] = _pack_bf16_pair(h2[:, :half], h2[:, half:])

    h_hi, h_lo = _split_bf16(h2)
    nt = (((1,), (1,)), ((), ()))
    r = lax.dot_general(wr2_ref[...], h_hi, nt, preferred_element_type=F32)
    r2 = lax.dot_general(wrhi_ref[...], h_lo, nt, preferred_element_type=F32)
    l = r[:N_EXPERTS] + (r[N_EXPERTS:] + r2) + br_ref[...]
    eid = lax.broadcasted_iota(I32, l.shape, 0)
    vals, sels, idxs = [], [], []
    for _ in range(TOP_K):
        m = jnp.max(l, axis=0, keepdims=True)
        ik = jnp.min(jnp.where(l == m, eid, N_EXPERTS), axis=0, keepdims=True)
        sel = eid == ik
        vals.append(m)
        sels.append(sel)
        idxs.append(ik)
        l = jnp.where(sel, -jnp.inf, l)
    exps = [jnp.exp(v - vals[0]) for v in vals]
    denom = exps[0] + exps[1] + exps[2] + exps[3]
    member = jnp.zeros(l.shape, F32)
    for sel in sels:
        member = member + sel.astype(F32)
    before = jnp.dot(member.astype(BF16), utri_ref[...], preferred_element_type=F32) + carry_ref[...]
    ranks = [jnp.sum(jnp.where(sel, before, 0.0), axis=0, keepdims=True) for sel in sels]
    carry_ref[...] = carry_ref[...] + jnp.sum(member, axis=1, keepdims=True)
    cnt_ref[...] = carry_ref[...].astype(I32)

    def rows(pieces, n_rows, dtype):
        rid = lax.broadcasted_iota(I32, (n_rows, tr), 0)
        out = jnp.zeros((n_rows, tr), dtype)
        for k, p in enumerate(pieces):
            out = jnp.where(rid == k, p.astype(dtype), out)
        return out

    idx_ref[...] = rows(idxs, idx_ref.shape[0], I32)
    rank_ref[...] = rows(ranks, rank_ref.shape[0], I32)
    wts_ref[...] = rows([e / denom for e in exps], wts_ref.shape[1], F32).T


ROUTE_ROWS = 8
ROUTE_LANES = 128


def _route_call(x1, mod, g_ffn, w_router, b_router, tokens_per_batch, batch0):
    T, D = x1.shape
    tr = min(512, tokens_per_batch)
    per_b = tokens_per_batch // tr
    wr_hi, wr_lo = _split_bf16(w_router.T)
    utri = jnp.asarray(np.triu(np.ones((tr, tr), np.float32), 1), BF16)
    full = lambda shape: pl.BlockSpec(shape, lambda i: (0,) * len(shape))
    return pl.pallas_call(
        _route_kernel,
        grid=(T // tr,),
        in_specs=[
            pl.BlockSpec((tr, D), lambda i: (i, 0)),
            pl.BlockSpec((None, N_MOD, D), lambda i: (batch0 + i // per_b, 0, 0)),
            full((1, D)),
            full((2 * N_EXPERTS, D)),
            full((N_EXPERTS, D)),
            full((N_EXPERTS, 1)),
            full((tr, tr)),
        ],
        out_specs=[
            pl.BlockSpec((tr, D // 2), lambda i: (i, 0)),
            pl.BlockSpec((ROUTE_ROWS, tr), lambda i: (0, i)),
            pl.BlockSpec((tr, ROUTE_LANES), lambda i: (i, 0)),
            pl.BlockSpec((ROUTE_ROWS, tr), lambda i: (0, i)),
            full((N_EXPERTS, 1)),
        ],
        out_shape=[
            jax.ShapeDtypeStruct((T, D // 2), U32),
            jax.ShapeDtypeStruct((ROUTE_ROWS, T), I32),
            jax.ShapeDtypeStruct((T, ROUTE_LANES), F32),
            jax.ShapeDtypeStruct((ROUTE_ROWS, T), I32),
            jax.ShapeDtypeStruct((N_EXPERTS, 1), I32),
        ],
        scratch_shapes=[pltpu.VMEM((N_EXPERTS, 1), F32)],
        compiler_params=pltpu.CompilerParams(
            dimension_semantics=("arbitrary",), vmem_limit_bytes=VMEM_LIMIT_BYTES),
        name="moe_route",
    )(x1, mod, g_ffn.reshape(1, D), jnp.concatenate([wr_hi, wr_lo], axis=0), wr_hi,
      b_router.reshape(N_EXPERTS, 1), utri)


SC_ROWS = 128


V7X_SC_CORES = 2
V7X_SC_SUBCORES = 16


def _sc_workers():
    return V7X_SC_CORES, V7X_SC_SUBCORES


def _sc_mesh():
    return plsc.VectorSubcoreMesh(core_axis_name="c", subcore_axis_name="s",
                                  num_cores=V7X_SC_CORES, num_subcores=V7X_SC_SUBCORES)


def _sc_scatter_call(h2p, pos_km, n_rows):
    T, W = h2p.shape
    nc, ns = _sc_workers()
    n = SC_ROWS
    per_w = T // (nc * ns * n)

    def body(h2p_hbm, pos_hbm, xs_hbm, i0, i1, i2, i3, rows_v, sem):
        wid = lax.axis_index("s") * nc + lax.axis_index("c")
        idx_refs = (i0, i1, i2, i3)

        @pl.loop(0, per_w)
        def _(j):
            t0 = (wid * per_w + j) * n
            pltpu.sync_copy(h2p_hbm.at[pl.ds(t0, n)], rows_v)
            for k in range(TOP_K):
                pltpu.sync_copy(pos_hbm.at[pl.ds(k * T + t0, n)], idx_refs[k])
            copies = [pltpu.async_copy(rows_v, xs_hbm.at[idx_refs[k]], sem) for k in range(TOP_K)]
            for cp in copies:
                cp.wait()

    return pl.kernel(
        body,
        out_type=jax.ShapeDtypeStruct((n_rows, W), h2p.dtype),
        mesh=_sc_mesh(),
        scratch_types=[pltpu.VMEM((n,), I32)] * TOP_K + [pltpu.VMEM((n, W), h2p.dtype), pltpu.SemaphoreType.DMA],
        name="moe_scatter_rows",
    )(h2p, pos_km)


def _sc_gather_call(ys, pos_km):
    P = pos_km.shape[0]
    W = ys.shape[1]
    nc, ns = _sc_workers()
    n = SC_ROWS
    per_w = P // (nc * ns * n)

    def body(ys_hbm, pos_hbm, yp_hbm, idx_v, rows_v, sem):
        wid = lax.axis_index("s") * nc + lax.axis_index("c")

        @pl.loop(0, per_w)
        def _(j):
            p0 = (wid * per_w + j) * n
            pltpu.sync_copy(pos_hbm.at[pl.ds(p0, n)], idx_v)
            pltpu.async_copy(ys_hbm.at[idx_v], rows_v, sem).wait()
            pltpu.sync_copy(rows_v, yp_hbm.at[pl.ds(p0, n)])

    return pl.kernel(
        body,
        out_type=jax.ShapeDtypeStruct((P, W), ys.dtype),
        mesh=_sc_mesh(),
        scratch_types=[pltpu.VMEM((n,), I32), pltpu.VMEM((n, W), ys.dtype), pltpu.SemaphoreType.DMA],
        name="moe_gather_rows",
    )(ys, pos_km)


def _ffn_kernel(te_ref, tv_ref, nx_ref, xs_ref, wgu_hbm, bgu_ref, wd_hbm, bd_ref, y_ref,
                wgu_f32, wd_f32, wgu_bf, wd_bf, sems):
    i = pl.program_id(0)
    valid = tv_ref[i]
    tm, half = xs_ref.shape
    d_exp = wd_bf.shape[0]

    def weight_copies(e):
        return (pltpu.make_async_copy(wgu_hbm.at[e], wgu_f32, sems.at[0]),
                pltpu.make_async_copy(wd_hbm.at[e], wd_f32, sems.at[1]))

    @pl.when(i == 0)
    def _():
        for cp in weight_copies(te_ref[0]):
            cp.start(priority=1)

    @pl.when(((i == 0) | (te_ref[i] != te_ref[jnp.maximum(i - 1, 0)])) & (valid > 0))
    def _():
        for cp in weight_copies(te_ref[i]):
            cp.wait()
        wgu_bf[...] = wgu_f32[...].astype(BF16)
        wd_bf[...] = wd_f32[...].astype(BF16)

        @pl.when(nx_ref[i] >= 0)
        def _():
            for cp in weight_copies(nx_ref[i]):
                cp.start(priority=1)

    def sub_tile(r0):
        rows = slice(r0, r0 + FFN_SUB_ROWS)
        keep = lax.broadcasted_iota(I32, (FFN_SUB_ROWS, half), 0) < valid - r0
        lo, hi = _unpack_bf16_pair(xs_ref[rows, :])
        x_lo = jnp.where(keep, lo, 0.0).astype(BF16)
        x_hi = jnp.where(keep, hi, 0.0).astype(BF16)
        d = functools.partial(jnp.dot, preferred_element_type=F32)
        nb = 256

        def up(c):
            gate = (d(x_lo, wgu_bf[0:half, c:c + nb]) + d(x_hi, wgu_bf[half:, c:c + nb])
                    + bgu_ref[:, c:c + nb])
            lin = (d(x_lo, wgu_bf[0:half, d_exp + c:d_exp + c + nb])
                   + d(x_hi, wgu_bf[half:, d_exp + c:d_exp + c + nb]) + bgu_ref[:, d_exp + c:d_exp + c + nb])
            return gate, lin

        def activate(gate, lin):
            gate = jnp.minimum(gate, SWIGLU_LIMIT)
            lin = jnp.clip(lin, -SWIGLU_LIMIT, SWIGLU_LIMIT)
            return (gate * jax.nn.sigmoid(SWIGLU_ALPHA * gate) * (lin + 1.0)).astype(BF16)

        chunks = list(range(0, d_exp, nb))
        acc = jnp.zeros((FFN_SUB_ROWS, wd_bf.shape[1]), F32) + bd_ref[...]
        pending = up(chunks[0])
        for n, c in enumerate(chunks):
            nxt = up(chunks[n + 1]) if n + 1 < len(chunks) else None
            acc = acc + d(activate(*pending), wd_bf[c:c + nb, :])
            pending = nxt
        y_ref[rows, :] = _pack_bf16_pair(acc[:, :half], acc[:, half:])

    for r0 in range(0, tm, FFN_SUB_ROWS):
        pl.when(valid > r0)(functools.partial(sub_tile, r0))


def _ffn_call(tile_expert, tile_valid, tile_next, xs, w_gu, b_gu, w_down, b_down, tm):
    R, half = xs.shape
    E, D, two_f = w_gu.shape
    d_exp = w_down.shape[1]
    grid_spec = pltpu.PrefetchScalarGridSpec(
        num_scalar_prefetch=3,
        grid=(R // tm,),
        in_specs=[
            pl.BlockSpec((tm, half), lambda i, te, tv, nx: (i, 0)),
            pl.BlockSpec(memory_space=pl.ANY),
            pl.BlockSpec((None, 1, two_f), lambda i, te, tv, nx: (te[i], 0, 0)),
            pl.BlockSpec(memory_space=pl.ANY),
            pl.BlockSpec((None, 1, D), lambda i, te, tv, nx: (te[i], 0, 0)),
        ],
        out_specs=pl.BlockSpec((tm, half), lambda i, te, tv, nx: (i, 0)),
        scratch_shapes=[
            pltpu.VMEM((D, two_f), F32), pltpu.VMEM((d_exp, D), F32),
            pltpu.VMEM((D, two_f), BF16), pltpu.VMEM((d_exp, D), BF16),
            pltpu.SemaphoreType.DMA((2,)),
        ],
    )
    return pl.pallas_call(
        _ffn_kernel,
        grid_spec=grid_spec,
        out_shape=jax.ShapeDtypeStruct((R, half), U32),
        compiler_params=pltpu.CompilerParams(
            dimension_semantics=("arbitrary",), vmem_limit_bytes=VMEM_LIMIT_BYTES),
        name="moe_ffn",
    )(tile_expert, tile_valid, tile_next, xs, w_gu, b_gu.reshape(E, 1, two_f), w_down,
      b_down.reshape(E, 1, D))


def _final_kernel(x_ref, yp_ref, wts_ref, mod_ref, gfin_ref, *rest):
    o_ref = rest[-1]
    x = x_ref[...]
    half = x.shape[1] // 2
    w = wts_ref[...]
    lo = jnp.zeros((x.shape[0], half), F32)
    hi = jnp.zeros((x.shape[0], half), F32)
    for k in range(TOP_K):
        l, h = _unpack_bf16_pair(yp_ref[k])
        lo = lo + w[:, k:k + 1] * l
        hi = hi + w[:, k:k + 1] * h
    gate = mod_ref[5:6, :]
    x_lo = x[:, :half] + gate[:, :half] * lo
    x_hi = x[:, half:] + gate[:, half:] * hi
    ms = (jnp.sum(x_lo * x_lo, axis=-1, keepdims=True) + jnp.sum(x_hi * x_hi, axis=-1, keepdims=True)) / x.shape[1]
    inv = lax.rsqrt(ms + RMS_EPS)
    o_ref[:, :half] = x_lo * inv * gfin_ref[:, :half]
    o_ref[:, half:] = x_hi * inv * gfin_ref[:, half:]


def _final_call(x1, yp, wts, mod, g_final, tokens_per_batch, group, n_groups, prev_out):
    T, D = x1.shape
    T_all = T * n_groups
    tq = min(512, tokens_per_batch)
    per_b = tokens_per_batch // tq
    first = group * (T // tq)
    in_specs = [
        pl.BlockSpec((tq, D), lambda i: (i, 0)),
        pl.BlockSpec((TOP_K, tq, D // 2), lambda i: (0, i, 0)),
        pl.BlockSpec((tq, ROUTE_LANES), lambda i: (i, 0)),
        pl.BlockSpec((None, N_MOD, D), lambda i: ((first + i) // per_b, 0, 0)),
        pl.BlockSpec((1, D), lambda i: (0, 0)),
    ]
    args = [x1, yp, wts, mod, g_final.reshape(1, D)]
    aliases = {}
    if prev_out is not None:
        in_specs.append(pl.BlockSpec(memory_space=pl.ANY))
        args.append(prev_out)
        aliases = {len(args) - 1: 0}
    return pl.pallas_call(
        _final_kernel,
        grid=(T // tq,),
        in_specs=in_specs,
        out_specs=pl.BlockSpec((tq, D), lambda i: (first + i, 0)),
        out_shape=jax.ShapeDtypeStruct((T_all, D), F32),
        input_output_aliases=aliases,
        compiler_params=pltpu.CompilerParams(
            dimension_semantics=("arbitrary",), vmem_limit_bytes=VMEM_LIMIT_BYTES),
        name="moe_combine_final",
    )(*args)


def _group_layout(counts, n_tiles, tm):
    padded = ((counts + tm - 1) // tm) * tm
    ends = jnp.cumsum(padded)
    starts = ends - padded
    tile_row = jnp.arange(n_tiles, dtype=I32) * tm
    te = jnp.minimum(jnp.sum(tile_row[:, None] >= ends[None, :], axis=1), N_EXPERTS - 1).astype(I32)
    eids = jnp.arange(N_EXPERTS, dtype=I32)
    mine = te[:, None] == eids[None, :]
    lookup = lambda table: jnp.sum(jnp.where(mine, table[None, :], 0), axis=1)
    tv = jnp.clip(lookup(counts) - (tile_row - lookup(starts)), 0, tm).astype(I32)
    later = (eids[None, :] > eids[:, None]) & (counts[None, :] > 0)
    nxt = jnp.min(jnp.where(later, eids[None, :], N_EXPERTS), axis=1)
    nx = lookup(jnp.where(nxt < N_EXPERTS, nxt, -1)).astype(I32)
    return starts, te, tv, nx


def kernel(x, c, positions, w_ada, b_ada, g_mix, w_in, w_dw, b_dw, g_conv_ln, b_conv_ln, g_ret_norm,
           w_out, g_ffn, w_router, b_router, w_gu, b_gu, w_down, b_down, g_final):
    B, S, D = x.shape
    T = B * S
    assert w_ada.shape[0] == 1, "single-layer block: the final norm directly follows layer 0"
    xt = x
    for l in range(1):
        mod = _mod_call(c, w_ada[l], b_ada[l]).reshape(B, N_MOD, D)
        n_groups = MOE_TOKEN_GROUPS if B % MOE_TOKEN_GROUPS == 0 else 1
        Bg = B // n_groups
        Tg = Bg * S
        tm = FFN_TILE_ROWS if Tg * TOP_K >= FFN_TILE_ROWS * N_EXPERTS * 4 else FFN_SUB_ROWS
        n_tiles = (Tg * TOP_K) // tm + N_EXPERTS
        x1s, scattered = [], []
        pos_km = jnp.zeros((TOP_K * Tg,), I32)
        for g in range(n_groups):
            x1 = _mixer_call(xt, positions, mod, g_mix[l], w_in[l], w_dw[l], b_dw[l], g_conv_ln[l],
                             b_conv_ln[l], g_ret_norm[l], w_out[l], g, n_groups, pos_km).reshape(Tg, D)
            h2p, idx, wts, rank, counts = _route_call(x1, mod, g_ffn[l], w_router[l], b_router[l], S, g * Bg)
            starts, te, tv, nx = _group_layout(counts[:, 0], n_tiles, tm)
            pos_km = rank[:TOP_K]
            for e in range(N_EXPERTS):
                pos_km = pos_km + jnp.where(idx[:TOP_K] == e, starts[e], 0)
            pos_km = pos_km.astype(I32).reshape(-1)
            x1s.append(x1)
            scattered.append((_sc_scatter_call(h2p, pos_km, n_tiles * tm), pos_km, wts, te, tv, nx))
        out = None
        for g, (xs, pos_km, wts, te, tv, nx) in enumerate(scattered):
            ys = _ffn_call(te, tv, nx, xs, w_gu[l], b_gu[l], w_down[l], b_down[l], tm)
            yp = _sc_gather_call(ys, pos_km)
            out = _final_call(x1s[g], yp.reshape(TOP_K, Tg, D // 2), wts, mod, g_final, S, g, n_groups, out)
        xt = out
    return xt.reshape(B, S, D)
```

```python
import functools

import numpy as np
import jax
import jax.numpy as jnp
from jax import lax
from jax.experimental import pallas as pl
from jax.experimental.pallas import tpu as pltpu
from jax.experimental.pallas import tpu_sc as plsc

F32 = jnp.float32
BF16 = jnp.bfloat16
U32 = jnp.uint32
I32 = jnp.int32

CONV_WIDTH = 31
CONV_HALO = 32
RET_HEADS = 4
RET_DIM = 128
RET_CHUNK = 128
ROPE_BASE = 10000.0
N_EXPERTS = 32
TOP_K = 4
SWIGLU_LIMIT = 7.0
SWIGLU_ALPHA = 1.702
RMS_EPS = 1e-6
LN_EPS = 1e-5
N_MOD = 6

VMEM_LIMIT_BYTES = 56 * 1024 * 1024
MOE_TOKEN_GROUPS = 2
FFN_TILE_ROWS = 1024
FFN_SUB_ROWS = 512


def _split_bf16(a):
    hi = a.astype(BF16)
    lo = (a - hi.astype(F32)).astype(BF16)
    return hi, lo


def _dot3(a, b_hi, b_lo):
    a_hi, a_lo = _split_bf16(a)
    d = functools.partial(jnp.dot, preferred_element_type=F32)
    return d(a_hi, b_hi) + (d(a_hi, b_lo) + d(a_lo, b_hi))


def _pack_bf16_pair(lo, hi):
    lo_bits = lax.bitcast_convert_type(lo.astype(BF16).astype(F32), U32)
    hi_bits = lax.bitcast_convert_type(hi.astype(BF16).astype(F32), U32)
    return (lo_bits >> 16) | (hi_bits & jnp.uint32(0xFFFF0000))


def _unpack_bf16_pair(p):
    lo = lax.bitcast_convert_type(p << 16, F32)
    hi = lax.bitcast_convert_type(p & jnp.uint32(0xFFFF0000), F32)
    return lo, hi


def _mod_kernel(c_ref, whi_ref, wlo_ref, b_ref, o_ref):
    c = c_ref[...]
    c_act = c * jax.nn.sigmoid(c)
    o_ref[...] = _dot3(c_act, whi_ref[...], wlo_ref[...]) + b_ref[...]


def _mod_call(c, w_ada, b_ada):
    B, D = c.shape
    n = w_ada.shape[1]
    bn = 1024
    w_hi, w_lo = _split_bf16(w_ada)
    return pl.pallas_call(
        _mod_kernel,
        grid=(n // bn,),
        in_specs=[
            pl.BlockSpec((B, D), lambda j: (0, 0)),
            pl.BlockSpec((D, bn), lambda j: (0, j)),
            pl.BlockSpec((D, bn), lambda j: (0, j)),
            pl.BlockSpec((1, bn), lambda j: (0, j)),
        ],
        out_specs=pl.BlockSpec((B, bn), lambda j: (0, j)),
        out_shape=jax.ShapeDtypeStruct((B, n), F32),
        name="adaln_mod",
    )(c, w_hi, w_lo, b_ada.reshape(1, n))


def _retention_tables():
    h = np.arange(RET_HEADS, dtype=np.float32)
    log_gamma = np.log(1.0 - np.power(2.0, -5.0 - h)).astype(np.float32)
    idx = np.arange(RET_CHUNK, dtype=np.float32)
    diff = idx[:, None] - idx[None, :]
    causal = diff >= 0
    mask = np.where(causal[None], np.exp(log_gamma[:, None, None] * np.where(causal, diff, 0.0)[None]), 0.0)
    q_decay = np.exp(log_gamma[:, None] * (idx + 1.0))[..., None]
    k_decay = np.exp(log_gamma[:, None] * (RET_CHUNK - 1.0 - idx))[..., None]
    chunk_decay = np.exp(log_gamma * RET_CHUNK)
    return (mask.astype(np.float32), q_decay.astype(np.float32), k_decay.astype(np.float32),
            [float(v) for v in chunk_decay.astype(np.float32)])


def _mixer_kernel(chunk_decay, ts, x_ref, pos_ref, mod_ref, gmix_ref, win_ref, wdw_ref, bdw_ref,
                  gcl_ref, bcl_ref, gret_ref, wout_ref, rope_ref, dmask_ref, qdec_ref, kdec_ref,
                  after_ref, o_ref, proj_ref, uext_ref, state_ref, cat_ref, conv_ref):
    del after_ref
    s = pl.program_id(1)
    conv_ch = wdw_ref.shape[1]
    ret_w = RET_HEADS * RET_DIM

    @pl.when(s == 0)
    def _():
        uext_ref[0:CONV_HALO, :] = jnp.zeros((CONV_HALO, conv_ch), F32)
        state_ref[...] = jnp.zeros_like(state_ref)

    x = x_ref[...]
    sh = mod_ref[0:1, :]
    sc = mod_ref[1:2, :]
    y = x * lax.rsqrt(jnp.mean(x * x, axis=-1, keepdims=True) + RMS_EPS) * gmix_ref[...]
    h = (y * (1.0 + sc) + sh).astype(BF16)
    proj_ref[...] = jnp.dot(h, win_ref[...], preferred_element_type=F32)

    a = proj_ref[:, 0:conv_ch]
    b = proj_ref[:, conv_ch:2 * conv_ch]
    uext_ref[CONV_HALO:CONV_HALO + ts, :] = a * jax.nn.sigmoid(b)
    cb = 64
    lead = CONV_HALO - (CONV_WIDTH - 1)
    span = cb + CONV_HALO

    def conv_block(r0):
        for c0 in range(0, conv_ch, 128):
            xw = uext_ref[r0:r0 + span, c0:c0 + 128]
            acc = jnp.zeros((cb, 128), F32) + bdw_ref[:, c0:c0 + 128]
            for r in range(8):
                xr = xw if r == 0 else pltpu.roll(xw, span - r, 0)
                for q in range((lead + CONV_WIDTH - 1 - r) // 8 + 1):
                    j = 8 * q + r - lead
                    if 0 <= j < CONV_WIDTH:
                        acc = acc + wdw_ref[j:j + 1, c0:c0 + 128] * xr[8 * q:8 * q + cb]
            conv_ref[r0:r0 + cb, c0:c0 + 128] = acc
    rb = 2 * cb

    def conv_norm(r0):
        acc = conv_ref[r0:r0 + rb, 0:conv_ch]
        mu = jnp.mean(acc, axis=-1, keepdims=True)
        d = acc - mu
        var = jnp.mean(d * d, axis=-1, keepdims=True)
        ln = d * lax.rsqrt(var + LN_EPS) * gcl_ref[...] + bcl_ref[...]
        cat_ref[r0:r0 + rb, 0:conv_ch] = (ln * jax.nn.sigmoid(ln)).astype(BF16)

    ang = pos_ref[...].astype(F32) * rope_ref[...]
    cos2 = jnp.cos(ang)
    sin = jnp.sin(ang)
    lane = lax.broadcasted_iota(I32, ang.shape, 1)
    sin2 = jnp.where(lane < RET_DIM // 2, -sin, sin)
    q0 = 2 * conv_ch
    k0 = q0 + ret_w
    v0 = k0 + ret_w
    g0 = v0 + ret_w
    scale = RET_DIM ** -0.5

    def retention_unit(hd, n):
        c0 = hd * RET_DIM
        r0 = n * RET_CHUNK
        rows = slice(r0, r0 + RET_CHUNK)
        cs = cos2[rows]
        sn = sin2[rows]
        q = proj_ref[rows, q0 + c0:q0 + c0 + RET_DIM]
        k = proj_ref[rows, k0 + c0:k0 + c0 + RET_DIM]
        v = proj_ref[rows, v0 + c0:v0 + c0 + RET_DIM].astype(BF16)
        g = proj_ref[rows, g0 + c0:g0 + c0 + RET_DIM]
        qr = q * cs + pltpu.roll(q, RET_DIM // 2, 1) * sn
        kr = (k * cs + pltpu.roll(k, RET_DIM // 2, 1) * sn) * scale
        st = state_ref[hd]
        scores = lax.dot_general(qr.astype(BF16), kr.astype(BF16), (((1,), (1,)), ((), ())),
                                 preferred_element_type=F32) * dmask_ref[hd]
        inner = jnp.dot(scores.astype(BF16), v, preferred_element_type=F32)
        cross = jnp.dot((qr * qdec_ref[hd]).astype(BF16), st.astype(BF16), preferred_element_type=F32)
        kv = lax.dot_general((kr * kdec_ref[hd]).astype(BF16), v, (((0,), (0,)), ((), ())),
                             preferred_element_type=F32)
        state_ref[hd] = chunk_decay[hd] * st + kv
        r = inner + cross
        mu = jnp.mean(r, axis=-1, keepdims=True)
        d = r - mu
        var = jnp.mean(d * d, axis=-1, keepdims=True)
        rn = d * lax.rsqrt(var + LN_EPS) * gret_ref[:, c0:c0 + RET_DIM]
        cat_ref[rows, conv_ch + c0:conv_ch + c0 + RET_DIM] = (g * jax.nn.sigmoid(g) * rn).astype(BF16)

    units = [(hd, n) for n in range(ts // RET_CHUNK) for hd in range(RET_HEADS)]
    n_conv = ts // cb
    per = -(-len(units) // n_conv)
    for i in range(n_conv):
        conv_block(i * cb)
        if i % 2 == 1:
            conv_norm((i - 1) * cb)
        for hd, n in units[i * per:(i + 1) * per]:
            retention_unit(hd, n)
    uext_ref[0:CONV_HALO, :] = uext_ref[ts:ts + CONV_HALO, :]

    out = jnp.dot(cat_ref[...], wout_ref[...], preferred_element_type=F32)
    o_ref[...] = x + mod_ref[2:3, :] * out


def _mixer_call(x, positions, mod, g_mix, w_in, w_dw, b_dw, g_conv_ln, b_conv_ln, g_ret_norm, w_out,
                group, n_groups, after):
    S, D = x.shape[1:]
    B = x.shape[0] // n_groups
    b0 = group * B
    in_cols = w_in.shape[1]
    conv_ch = w_dw.shape[1]
    ts = min(512, S)
    mask, q_decay, k_decay, chunk_decay = _retention_tables()
    half = RET_DIM // 2
    inv_freq = (ROPE_BASE ** (-np.arange(half, dtype=np.float32) / half)).astype(np.float32)
    rope = np.concatenate([inv_freq, inv_freq])[None, :]
    w_dw_p = jnp.zeros((CONV_HALO, conv_ch), F32).at[:CONV_WIDTH].set(w_dw)
    full = lambda shape: pl.BlockSpec(shape, lambda b, s: (0,) * len(shape))
    return pl.pallas_call(
        functools.partial(_mixer_kernel, chunk_decay, ts),
        grid=(B, S // ts),
        in_specs=[
            pl.BlockSpec((None, ts, D), lambda b, s: (b0 + b, s, 0)),
            pl.BlockSpec((None, ts, 1), lambda b, s: (b0 + b, s, 0)),
            pl.BlockSpec((None, N_MOD, D), lambda b, s: (b0 + b, 0, 0)),
            full((1, D)),
            full((D, in_cols)),
            full((CONV_HALO, conv_ch)),
            full((1, conv_ch)),
            full((1, conv_ch)),
            full((1, conv_ch)),
            full((1, RET_HEADS * RET_DIM)),
            full((conv_ch + RET_HEADS * RET_DIM, D)),
            full((1, RET_DIM)),
            full((RET_HEADS, RET_CHUNK, RET_CHUNK)),
            full((RET_HEADS, RET_CHUNK, 1)),
            full((RET_HEADS, RET_CHUNK, 1)),
            pl.BlockSpec(memory_space=pl.ANY),
        ],
        out_specs=pl.BlockSpec((None, ts, D), lambda b, s: (b, s, 0)),
        out_shape=jax.ShapeDtypeStruct((B, S, D), F32),
        scratch_shapes=[
            pltpu.VMEM((ts, in_cols), F32),
            pltpu.VMEM((CONV_HALO + ts, conv_ch), F32),
            pltpu.VMEM((RET_HEADS, RET_DIM, RET_DIM), F32),
            pltpu.VMEM((ts, conv_ch + RET_HEADS * RET_DIM), BF16),
            pltpu.VMEM((ts, conv_ch), F32),
        ],
        compiler_params=pltpu.CompilerParams(
            dimension_semantics=("arbitrary", "arbitrary"), vmem_limit_bytes=VMEM_LIMIT_BYTES),
        name="hybrid_mixer",
    )(x, positions.reshape(-1, S, 1), mod, g_mix.reshape(1, D), w_in.astype(BF16), w_dw_p,
      b_dw.reshape(1, -1), g_conv_ln.reshape(1, -1), b_conv_ln.reshape(1, -1), g_ret_norm.reshape(1, -1),
      w_out.astype(BF16), jnp.asarray(rope), jnp.asarray(mask), jnp.asarray(q_decay),
      jnp.asarray(k_decay), after)


def _route_kernel(x_ref, mod_ref, gffn_ref, wr2_ref, wrhi_ref, br_ref, utri_ref,
                  h2p_ref, idx_ref, wts_ref, rank_ref, cnt_ref, carry_ref):
    i = pl.program_id(0)

    @pl.when(i == 0)
    def _():
        carry_ref[...] = jnp.zeros_like(carry_ref)

    x = x_ref[...]
    tr = x.shape[0]
    half = x.shape[1] // 2
    y = x * lax.rsqrt(jnp.mean(x * x, axis=-1, keepdims=True) + RMS_EPS) * gffn_ref[...]
    h2 = y * (1.0 + mod_ref[4:5, :]) + mod_ref[3:4, :]
    h2p_ref[...] = _pack_bf16_pair(h2[:, :half], h2[:, half:])

    h_hi, h_lo = _split_bf16(h2)
    nt = (((1,), (1,)), ((), ()))
    r = lax.dot_general(wr2_ref[...], h_hi, nt, preferred_element_type=F32)
    r2 = lax.dot_general(wrhi_ref[...], h_lo, nt, preferred_element_type=F32)
    l = r[:N_EXPERTS] + (r[N_EXPERTS:] + r2) + br_ref[...]
    eid = lax.broadcasted_iota(I32, l.shape, 0)
    vals, sels, idxs = [], [], []
    for _ in range(TOP_K):
        m = jnp.max(l, axis=0, keepdims=True)
        ik = jnp.min(jnp.where(l == m, eid, N_EXPERTS), axis=0, keepdims=True)
        sel = eid == ik
        vals.append(m)
        sels.append(sel)
        idxs.append(ik)
        l = jnp.where(sel, -jnp.inf, l)
    exps = [jnp.exp(v - vals[0]) for v in vals]
    denom = exps[0] + exps[1] + exps[2] + exps[3]
    member = jnp.zeros(l.shape, F32)
    for sel in sels:
        member = member + sel.astype(F32)
    before = jnp.dot(member.astype(BF16), utri_ref[...], preferred_element_type=F32) + carry_ref[...]
    ranks = [jnp.sum(jnp.where(sel, before, 0.0), axis=0, keepdims=True) for sel in sels]
    carry_ref[...] = carry_ref[...] + jnp.sum(member, axis=1, keepdims=True)
    cnt_ref[...] = carry_ref[...].astype(I32)

    def rows(pieces, n_rows, dtype):
        rid = lax.broadcasted_iota(I32, (n_rows, tr), 0)
        out = jnp.zeros((n_rows, tr), dtype)
        for k, p in enumerate(pieces):
            out = jnp.where(rid == k, p.astype(dtype), out)
        return out

    idx_ref[...] = rows(idxs, idx_ref.shape[0], I32)
    rank_ref[...] = rows(ranks, rank_ref.shape[0], I32)
    wts_ref[...] = rows([e / denom for e in exps], wts_ref.shape[1], F32).T


ROUTE_ROWS = 8
ROUTE_LANES = 128


def _route_call(x1, mod, g_ffn, w_router, b_router, tokens_per_batch, batch0):
    T, D = x1.shape
    tr = min(512, tokens_per_batch)
    per_b = tokens_per_batch // tr
    wr_hi, wr_lo = _split_bf16(w_router.T)
    utri = jnp.asarray(np.triu(np.ones((tr, tr), np.float32), 1), BF16)
    full = lambda shape: pl.BlockSpec(shape, lambda i: (0,) * len(shape))
    return pl.pallas_call(
        _route_kernel,
        grid=(T // tr,),
        in_specs=[
            pl.BlockSpec((tr, D), lambda i: (i, 0)),
            pl.BlockSpec((None, N_MOD, D), lambda i: (batch0 + i // per_b, 0, 0)),
            full((1, D)),
            full((2 * N_EXPERTS, D)),
            full((N_EXPERTS, D)),
            full((N_EXPERTS, 1)),
            full((tr, tr)),
        ],
        out_specs=[
            pl.BlockSpec((tr, D // 2), lambda i: (i, 0)),
            pl.BlockSpec((ROUTE_ROWS, tr), lambda i: (0, i)),
            pl.BlockSpec((tr, ROUTE_LANES), lambda i: (i, 0)),
            pl.BlockSpec((ROUTE_ROWS, tr), lambda i: (0, i)),
            full((N_EXPERTS, 1)),
        ],
        out_shape=[
            jax.ShapeDtypeStruct((T, D // 2), U32),
            jax.ShapeDtypeStruct((ROUTE_ROWS, T), I32),
            jax.ShapeDtypeStruct((T, ROUTE_LANES), F32),
            jax.ShapeDtypeStruct((ROUTE_ROWS, T), I32),
            jax.ShapeDtypeStruct((N_EXPERTS, 1), I32),
        ],
        scratch_shapes=[pltpu.VMEM((N_EXPERTS, 1), F32)],
        compiler_params=pltpu.CompilerParams(
            dimension_semantics=("arbitrary",), vmem_limit_bytes=VMEM_LIMIT_BYTES),
        name="moe_route",
    )(x1, mod, g_ffn.reshape(1, D), jnp.concatenate([wr_hi, wr_lo], axis=0), wr_hi,
      b_router.reshape(N_EXPERTS, 1), utri)


SC_ROWS = 128


V7X_SC_CORES = 2
V7X_SC_SUBCORES = 16


def _sc_workers():
    return V7X_SC_CORES, V7X_SC_SUBCORES


def _sc_mesh():
    return plsc.VectorSubcoreMesh(core_axis_name="c", subcore_axis_name="s",
                                  num_cores=V7X_SC_CORES, num_subcores=V7X_SC_SUBCORES)


def _sc_scatter_call(h2p, pos_km, n_rows):
    T, W = h2p.shape
    nc, ns = _sc_workers()
    n = SC_ROWS
    per_w = T // (nc * ns * n)

    def body(h2p_hbm, pos_hbm, xs_hbm, i0, i1, i2, i3, rows_v, sem):
        wid = lax.axis_index("s") * nc + lax.axis_index("c")
        idx_refs = (i0, i1, i2, i3)

        @pl.loop(0, per_w)
        def _(j):
            t0 = (wid * per_w + j) * n
            pltpu.sync_copy(h2p_hbm.at[pl.ds(t0, n)], rows_v)
            for k in range(TOP_K):
                pltpu.sync_copy(pos_hbm.at[pl.ds(k * T + t0, n)], idx_refs[k])
            copies = [pltpu.async_copy(rows_v, xs_hbm.at[idx_refs[k]], sem) for k in range(TOP_K)]
            for cp in copies:
                cp.wait()

    return pl.kernel(
        body,
        out_type=jax.ShapeDtypeStruct((n_rows, W), h2p.dtype),
        mesh=_sc_mesh(),
        scratch_types=[pltpu.VMEM((n,), I32)] * TOP_K + [pltpu.VMEM((n, W), h2p.dtype), pltpu.SemaphoreType.DMA],
        name="moe_scatter_rows",
    )(h2p, pos_km)


def _sc_gather_call(ys, pos_km):
    P = pos_km.shape[0]
    W = ys.shape[1]
    nc, ns = _sc_workers()
    n = SC_ROWS
    per_w = P // (nc * ns * n)

    def body(ys_hbm, pos_hbm, yp_hbm, idx_v, rows_v, sem):
        wid = lax.axis_index("s") * nc + lax.axis_index("c")

        @pl.loop(0, per_w)
        def _(j):
            p0 = (wid * per_w + j) * n
            pltpu.sync_copy(pos_hbm.at[pl.ds(p0, n)], idx_v)
            pltpu.async_copy(ys_hbm.at[idx_v], rows_v, sem).wait()
            pltpu.sync_copy(rows_v, yp_hbm.at[pl.ds(p0, n)])

    return pl.kernel(
        body,
        out_type=jax.ShapeDtypeStruct((P, W), ys.dtype),
        mesh=_sc_mesh(),
        scratch_types=[pltpu.VMEM((n,), I32), pltpu.VMEM((n, W), ys.dtype), pltpu.SemaphoreType.DMA],
        name="moe_gather_rows",
    )(ys, pos_km)


def _ffn_kernel(te_ref, tv_ref, nx_ref, xs_ref, wgu_hbm, bgu_ref, wd_hbm, bd_ref, y_ref,
                wgu_f32, wd_f32, wgu_bf, wd_bf, sems):
    i = pl.program_id(0)
    valid = tv_ref[i]
    tm, half = xs_ref.shape
    d_exp = wd_bf.shape[0]

    def weight_copies(e):
        return (pltpu.make_async_copy(wgu_hbm.at[e], wgu_f32, sems.at[0]),
                pltpu.make_async_copy(wd_hbm.at[e], wd_f32, sems.at[1]))

    @pl.when(i == 0)
    def _():
        for cp in weight_copies(te_ref[0]):
            cp.start(priority=1)

    @pl.when(((i == 0) | (te_ref[i] != te_ref[jnp.maximum(i - 1, 0)])) & (valid > 0))
    def _():
        for cp in weight_copies(te_ref[i]):
            cp.wait()
        wgu_bf[...] = wgu_f32[...].astype(BF16)
        wd_bf[...] = wd_f32[...].astype(BF16)

        @pl.when(nx_ref[i] >= 0)
        def _():
            for cp in weight_copies(nx_ref[i]):
                cp.start(priority=1)

    def sub_tile(r0):
        rows = slice(r0, r0 + FFN_SUB_ROWS)
        keep = lax.broadcasted_iota(I32, (FFN_SUB_ROWS, half), 0) < valid - r0
        lo, hi = _unpack_bf16_pair(xs_ref[rows, :])
        xt = jnp.concatenate([jnp.where(keep, lo, 0.0).astype(BF16),
                              jnp.where(keep, hi, 0.0).astype(BF16)], axis=1)
        d = functools.partial(jnp.dot, preferred_element_type=F32)
        nb = 256

        def up(c):
            gate = d(xt, wgu_bf[:, c:c + nb]) + bgu_ref[:, c:c + nb]
            lin = d(xt, wgu_bf[:, d_exp + c:d_exp + c + nb]) + bgu_ref[:, d_exp + c:d_exp + c + nb]
            return gate, lin

        def activate(gate, lin):
            gate = jnp.minimum(gate, SWIGLU_LIMIT)
            lin = jnp.clip(lin, -SWIGLU_LIMIT, SWIGLU_LIMIT)
            return (gate * jax.nn.sigmoid(SWIGLU_ALPHA * gate) * (lin + 1.0)).astype(BF16)

        chunks = list(range(0, d_exp, nb))
        acts = []
        pending = up(chunks[0])
        for n in range(len(chunks)):
            nxt = up(chunks[n + 1]) if n + 1 < len(chunks) else None
            acts.append(activate(*pending))
            pending = nxt
        out = d(jnp.concatenate(acts, axis=1), wd_bf[...]) + bd_ref[...]
        y_ref[rows, :] = _pack_bf16_pair(out[:, :half], out[:, half:])

    for r0 in range(0, tm, FFN_SUB_ROWS):
        pl.when(valid > r0)(functools.partial(sub_tile, r0))


def _ffn_call(tile_expert, tile_valid, tile_next, xs, w_gu, b_gu, w_down, b_down, tm):
    R, half = xs.shape
    E, D, two_f = w_gu.shape
    d_exp = w_down.shape[1]
    grid_spec = pltpu.PrefetchScalarGridSpec(
        num_scalar_prefetch=3,
        grid=(R // tm,),
        in_specs=[
            pl.BlockSpec((tm, half), lambda i, te, tv, nx: (i, 0)),
            pl.BlockSpec(memory_space=pl.ANY),
            pl.BlockSpec((None, 1, two_f), lambda i, te, tv, nx: (te[i], 0, 0)),
            pl.BlockSpec(memory_space=pl.ANY),
            pl.BlockSpec((None, 1, D), lambda i, te, tv, nx: (te[i], 0, 0)),
        ],
        out_specs=pl.BlockSpec((tm, half), lambda i, te, tv, nx: (i, 0)),
        scratch_shapes=[
            pltpu.VMEM((D, two_f), F32), pltpu.VMEM((d_exp, D), F32),
            pltpu.VMEM((D, two_f), BF16), pltpu.VMEM((d_exp, D), BF16),
            pltpu.SemaphoreType.DMA((2,)),
        ],
    )
    return pl.pallas_call(
        _ffn_kernel,
        grid_spec=grid_spec,
        out_shape=jax.ShapeDtypeStruct((R, half), U32),
        compiler_params=pltpu.CompilerParams(
            dimension_semantics=("arbitrary",), vmem_limit_bytes=VMEM_LIMIT_BYTES),
        name="moe_ffn",
    )(tile_expert, tile_valid, tile_next, xs, w_gu, b_gu.reshape(E, 1, two_f), w_down,
      b_down.reshape(E, 1, D))


def _final_kernel(x_ref, yp_ref, wts_ref, mod_ref, gfin_ref, *rest):
    o_ref = rest[-1]
    x = x_ref[...]
    half = x.shape[1] // 2
    w = wts_ref[...]
    lo = jnp.zeros((x.shape[0], half), F32)
    hi = jnp.zeros((x.shape[0], half), F32)
    for k in range(TOP_K):
        l, h = _unpack_bf16_pair(yp_ref[k])
        lo = lo + w[:, k:k + 1] * l
        hi = hi + w[:, k:k + 1] * h
    gate = mod_ref[5:6, :]
    x_lo = x[:, :half] + gate[:, :half] * lo
    x_hi = x[:, half:] + gate[:, half:] * hi
    ms = (jnp.sum(x_lo * x_lo, axis=-1, keepdims=True) + jnp.sum(x_hi * x_hi, axis=-1, keepdims=True)) / x.shape[1]
    inv = lax.rsqrt(ms + RMS_EPS)
    o_ref[:, :half] = x_lo * inv * gfin_ref[:, :half]
    o_ref[:, half:] = x_hi * inv * gfin_ref[:, half:]


def _final_call(x1, yp, wts, mod, g_final, tokens_per_batch, group, n_groups, prev_out):
    T, D = x1.shape
    T_all = T * n_groups
    tq = min(512, tokens_per_batch)
    per_b = tokens_per_batch // tq
    first = group * (T // tq)
    in_specs = [
        pl.BlockSpec((tq, D), lambda i: (i, 0)),
        pl.BlockSpec((TOP_K, tq, D // 2), lambda i: (0, i, 0)),
        pl.BlockSpec((tq, ROUTE_LANES), lambda i: (i, 0)),
        pl.BlockSpec((None, N_MOD, D), lambda i: ((first + i) // per_b, 0, 0)),
        pl.BlockSpec((1, D), lambda i: (0, 0)),
    ]
    args = [x1, yp, wts, mod, g_final.reshape(1, D)]
    aliases = {}
    if prev_out is not None:
        in_specs.append(pl.BlockSpec(memory_space=pl.ANY))
        args.append(prev_out)
        aliases = {len(args) - 1: 0}
    return pl.pallas_call(
        _final_kernel,
        grid=(T // tq,),
        in_specs=in_specs,
        out_specs=pl.BlockSpec((tq, D), lambda i: (first + i, 0)),
        out_shape=jax.ShapeDtypeStruct((T_all, D), F32),
        input_output_aliases=aliases,
        compiler_params=pltpu.CompilerParams(
            dimension_semantics=("arbitrary",), vmem_limit_bytes=VMEM_LIMIT_BYTES),
        name="moe_combine_final",
    )(*args)


def _group_layout(counts, n_tiles, tm):
    padded = ((counts + tm - 1) // tm) * tm
    ends = jnp.cumsum(padded)
    starts = ends - padded
    tile_row = jnp.arange(n_tiles, dtype=I32) * tm
    te = jnp.minimum(jnp.sum(tile_row[:, None] >= ends[None, :], axis=1), N_EXPERTS - 1).astype(I32)
    eids = jnp.arange(N_EXPERTS, dtype=I32)
    mine = te[:, None] == eids[None, :]
    lookup = lambda table: jnp.sum(jnp.where(mine, table[None, :], 0), axis=1)
    tv = jnp.clip(lookup(counts) - (tile_row - lookup(starts)), 0, tm).astype(I32)
    later = (eids[None, :] > eids[:, None]) & (counts[None, :] > 0)
    nxt = jnp.min(jnp.where(later, eids[None, :], N_EXPERTS), axis=1)
    nx = lookup(jnp.where(nxt < N_EXPERTS, nxt, -1)).astype(I32)
    return starts, te, tv, nx


def kernel(x, c, positions, w_ada, b_ada, g_mix, w_in, w_dw, b_dw, g_conv_ln, b_conv_ln, g_ret_norm,
           w_out, g_ffn, w_router, b_router, w_gu, b_gu, w_down, b_down, g_final):
    B, S, D = x.shape
    T = B * S
    assert w_ada.shape[0] == 1, "single-layer block: the final norm directly follows layer 0"
    xt = x
    for l in range(1):
        mod = _mod_call(c, w_ada[l], b_ada[l]).reshape(B, N_MOD, D)
        n_groups = MOE_TOKEN_GROUPS if B % MOE_TOKEN_GROUPS == 0 else 1
        Bg = B // n_groups
        Tg = Bg * S
        tm = FFN_TILE_ROWS if Tg * TOP_K >= FFN_TILE_ROWS * N_EXPERTS * 4 else FFN_SUB_ROWS
        n_tiles = (Tg * TOP_K) // tm + N_EXPERTS
        x1s, scattered = [], []
        pos_km = jnp.zeros((TOP_K * Tg,), I32)
        for g in range(n_groups):
            x1 = _mixer_call(xt, positions, mod, g_mix[l], w_in[l], w_dw[l], b_dw[l], g_conv_ln[l],
                             b_conv_ln[l], g_ret_norm[l], w_out[l], g, n_groups, pos_km).reshape(Tg, D)
            h2p, idx, wts, rank, counts = _route_call(x1, mod, g_ffn[l], w_router[l], b_router[l], S, g * Bg)
            starts, te, tv, nx = _group_layout(counts[:, 0], n_tiles, tm)
            pos_km = rank[:TOP_K]
            for e in range(N_EXPERTS):
                pos_km = pos_km + jnp.where(idx[:TOP_K] == e, starts[e], 0)
            pos_km = pos_km.astype(I32).reshape(-1)
            x1s.append(x1)
            scattered.append((_sc_scatter_call(h2p, pos_km, n_tiles * tm), pos_km, wts, te, tv, nx))
        out = None
        for g, (xs, pos_km, wts, te, tv, nx) in enumerate(scattered):
            ys = _ffn_call(te, tv, nx, xs, w_gu[l], b_gu[l], w_down[l], b_down[l], tm)
            yp = _sc_gather_call(ys, pos_km)
            out = _final_call(x1s[g], yp.reshape(TOP_K, Tg, D // 2), wts, mod, g_final, S, g, n_groups, out)
        xt = out
    return xt.reshape(B, S, D)
```

```python
import functools

import numpy as np
import jax
import jax.numpy as jnp
from jax import lax
from jax.experimental import pallas as pl
from jax.experimental.pallas import tpu as pltpu
from jax.experimental.pallas import tpu_sc as plsc

F32 = jnp.float32
BF16 = jnp.bfloat16
U32 = jnp.uint32
I32 = jnp.int32

CONV_WIDTH = 31
CONV_HALO = 32
RET_HEADS = 4
RET_DIM = 128
RET_CHUNK = 128
ROPE_BASE = 10000.0
N_EXPERTS = 32
TOP_K = 4
SWIGLU_LIMIT = 7.0
SWIGLU_ALPHA = 1.702
RMS_EPS = 1e-6
LN_EPS = 1e-5
N_MOD = 6

VMEM_LIMIT_BYTES = 56 * 1024 * 1024
MOE_TOKEN_GROUPS = 2
FFN_TILE_ROWS = 1024
FFN_SUB_ROWS = 512


def _split_bf16(a):
    hi = a.astype(BF16)
    lo = (a - hi.astype(F32)).astype(BF16)
    return hi, lo


def _dot3(a, b_hi, b_lo):
    a_hi, a_lo = _split_bf16(a)
    d = functools.partial(jnp.dot, preferred_element_type=F32)
    return d(a_hi, b_hi) + (d(a_hi, b_lo) + d(a_lo, b_hi))


def _pack_bf16_pair(lo, hi):
    lo_bits = lax.bitcast_convert_type(lo.astype(BF16).astype(F32), U32)
    hi_bits = lax.bitcast_convert_type(hi.astype(BF16).astype(F32), U32)
    return (lo_bits >> 16) | (hi_bits & jnp.uint32(0xFFFF0000))


def _unpack_bf16_pair(p):
    lo = lax.bitcast_convert_type(p << 16, F32)
    hi = lax.bitcast_convert_type(p & jnp.uint32(0xFFFF0000), F32)
    return lo, hi


def _mod_kernel(c_ref, whi_ref, wlo_ref, b_ref, o_ref):
    c = c_ref[...]
    c_act = c * jax.nn.sigmoid(c)
    o_ref[...] = _dot3(c_act, whi_ref[...], wlo_ref[...]) + b_ref[...]


def _mod_call(c, w_ada, b_ada):
    B, D = c.shape
    n = w_ada.shape[1]
    bn = 1024
    w_hi, w_lo = _split_bf16(w_ada)
    return pl.pallas_call(
        _mod_kernel,
        grid=(n // bn,),
        in_specs=[
            pl.BlockSpec((B, D), lambda j: (0, 0)),
            pl.BlockSpec((D, bn), lambda j: (0, j)),
            pl.BlockSpec((D, bn), lambda j: (0, j)),
            pl.BlockSpec((1, bn), lambda j: (0, j)),
        ],
        out_specs=pl.BlockSpec((B, bn), lambda j: (0, j)),
        out_shape=jax.ShapeDtypeStruct((B, n), F32),
        name="adaln_mod",
    )(c, w_hi, w_lo, b_ada.reshape(1, n))


def _retention_tables():
    h = np.arange(RET_HEADS, dtype=np.float32)
    log_gamma = np.log(1.0 - np.power(2.0, -5.0 - h)).astype(np.float32)
    idx = np.arange(RET_CHUNK, dtype=np.float32)
    diff = idx[:, None] - idx[None, :]
    causal = diff >= 0
    mask = np.where(causal[None], np.exp(log_gamma[:, None, None] * np.where(causal, diff, 0.0)[None]), 0.0)
    q_decay = np.exp(log_gamma[:, None] * (idx + 1.0))[..., None]
    k_decay = np.exp(log_gamma[:, None] * (RET_CHUNK - 1.0 - idx))[..., None]
    chunk_decay = np.exp(log_gamma * RET_CHUNK)
    return (mask.astype(np.float32), q_decay.astype(np.float32), k_decay.astype(np.float32),
            [float(v) for v in chunk_decay.astype(np.float32)])


def _mixer_kernel(chunk_decay, ts, x_ref, pos_ref, mod_ref, gmix_ref, win_ref, wdw_ref, bdw_ref,
                  gcl_ref, bcl_ref, gret_ref, wout_ref, rope_ref, dmask_ref, qdec_ref, kdec_ref,
                  after_ref, o_ref, proj_ref, uext_ref, state_ref, cat_ref, conv_ref):
    del after_ref
    s = pl.program_id(1)
    conv_ch = wdw_ref.shape[1]
    ret_w = RET_HEADS * RET_DIM

    @pl.when(s == 0)
    def _():
        uext_ref[0:CONV_HALO, :] = jnp.zeros((CONV_HALO, conv_ch), F32)
        state_ref[...] = jnp.zeros_like(state_ref)

    x = x_ref[...]
    row_gain = gmix_ref[...] * (1.0 + mod_ref[1:2, :])
    h = (x * lax.rsqrt(jnp.mean(x * x, axis=-1, keepdims=True) + RMS_EPS) * row_gain + mod_ref[0:1, :]).astype(BF16)
    proj_ref[...] = jnp.dot(h, win_ref[...], preferred_element_type=F32)

    a = proj_ref[:, 0:conv_ch]
    b = proj_ref[:, conv_ch:2 * conv_ch]
    uext_ref[CONV_HALO:CONV_HALO + ts, :] = a * jax.nn.sigmoid(b)
    cb = 64
    lead = CONV_HALO - (CONV_WIDTH - 1)
    span = cb + CONV_HALO

    def conv_block(r0):
        for c0 in range(0, conv_ch, 128):
            xw = uext_ref[r0:r0 + span, c0:c0 + 128]
            acc = jnp.zeros((cb, 128), F32) + bdw_ref[:, c0:c0 + 128]
            for r in range(8):
                xr = xw if r == 0 else pltpu.roll(xw, span - r, 0)
                for q in range((lead + CONV_WIDTH - 1 - r) // 8 + 1):
                    j = 8 * q + r - lead
                    if 0 <= j < CONV_WIDTH:
                        acc = acc + wdw_ref[j:j + 1, c0:c0 + 128] * xr[8 * q:8 * q + cb]
            conv_ref[r0:r0 + cb, c0:c0 + 128] = acc
    rb = 2 * cb

    def conv_norm(r0):
        acc = conv_ref[r0:r0 + rb, 0:conv_ch]
        mu = jnp.mean(acc, axis=-1, keepdims=True)
        d = acc - mu
        var = jnp.mean(d * d, axis=-1, keepdims=True)
        ln = d * lax.rsqrt(var + LN_EPS) * gcl_ref[...] + bcl_ref[...]
        cat_ref[r0:r0 + rb, 0:conv_ch] = (ln * jax.nn.sigmoid(ln)).astype(BF16)

    hts = ts // 2
    low = lax.broadcasted_iota(I32, (hts, RET_DIM), 1) < RET_DIM // 2
    posf = jnp.where(low, pos_ref[0:hts, :].astype(F32), pos_ref[hts:ts, :].astype(F32))
    ang = posf * rope_ref[...]
    cos_p = jnp.cos(ang)
    sin_p = jnp.sin(ang)
    cos_s = pltpu.roll(cos_p, RET_DIM // 2, 1)
    sin_s = pltpu.roll(sin_p, RET_DIM // 2, 1)
    cos2 = jnp.concatenate([jnp.where(low, cos_p, cos_s), jnp.where(low, cos_s, cos_p)], axis=0)
    sin2 = jnp.concatenate([jnp.where(low, -sin_p, sin_s), jnp.where(low, -sin_s, sin_p)], axis=0)
    q0 = 2 * conv_ch
    k0 = q0 + ret_w
    v0 = k0 + ret_w
    g0 = v0 + ret_w

    def retention_unit(hd, n):
        c0 = hd * RET_DIM
        r0 = n * RET_CHUNK
        rows = slice(r0, r0 + RET_CHUNK)
        cs = cos2[rows]
        sn = sin2[rows]
        q = proj_ref[rows, q0 + c0:q0 + c0 + RET_DIM]
        k = proj_ref[rows, k0 + c0:k0 + c0 + RET_DIM]
        v = proj_ref[rows, v0 + c0:v0 + c0 + RET_DIM].astype(BF16)
        g = proj_ref[rows, g0 + c0:g0 + c0 + RET_DIM]
        qr = q * cs + pltpu.roll(q, RET_DIM // 2, 1) * sn
        kr = k * cs + pltpu.roll(k, RET_DIM // 2, 1) * sn
        st = state_ref[hd]
        scores = lax.dot_general(qr.astype(BF16), kr.astype(BF16), (((1,), (1,)), ((), ())),
                                 preferred_element_type=F32) * dmask_ref[hd]
        inner = jnp.dot(scores.astype(BF16), v, preferred_element_type=F32)
        cross = jnp.dot((qr * qdec_ref[hd]).astype(BF16), st.astype(BF16), preferred_element_type=F32)
        kv = lax.dot_general((kr * kdec_ref[hd]).astype(BF16), v, (((0,), (0,)), ((), ())),
                             preferred_element_type=F32)
        state_ref[hd] = chunk_decay[hd] * st + kv
        r = inner + cross
        mu = jnp.mean(r, axis=-1, keepdims=True)
        d = r - mu
        var = jnp.mean(d * d, axis=-1, keepdims=True)
        rn = d * lax.rsqrt(var + LN_EPS) * gret_ref[:, c0:c0 + RET_DIM]
        cat_ref[rows, conv_ch + c0:conv_ch + c0 + RET_DIM] = (g * jax.nn.sigmoid(g) * rn).astype(BF16)

    units = [(hd, n) for n in range(ts // RET_CHUNK) for hd in range(RET_HEADS)]
    n_conv = ts // cb
    per = -(-len(units) // n_conv)
    for i in range(n_conv):
        conv_block(i * cb)
        if i % 2 == 1:
            conv_norm((i - 1) * cb)
        for hd, n in units[i * per:(i + 1) * per]:
            retention_unit(hd, n)
    uext_ref[0:CONV_HALO, :] = uext_ref[ts:ts + CONV_HALO, :]

    out = jnp.dot(cat_ref[...], wout_ref[...], preferred_element_type=F32)
    o_ref[...] = x + mod_ref[2:3, :] * out


def _mixer_call(x, positions, mod, g_mix, w_in, w_dw, b_dw, g_conv_ln, b_conv_ln, g_ret_norm, w_out,
                group, n_groups, after):
    S, D = x.shape[1:]
    B = x.shape[0] // n_groups
    b0 = group * B
    in_cols = w_in.shape[1]
    conv_ch = w_dw.shape[1]
    ts = min(512, S)
    mask, q_decay, k_decay, chunk_decay = _retention_tables()
    key_scale = np.float32(RET_DIM ** -0.5)
    half = RET_DIM // 2
    inv_freq = (ROPE_BASE ** (-np.arange(half, dtype=np.float32) / half)).astype(np.float32)
    rope = np.concatenate([inv_freq, inv_freq])[None, :]
    w_dw_p = jnp.zeros((CONV_HALO, conv_ch), F32).at[:CONV_WIDTH].set(w_dw)
    full = lambda shape: pl.BlockSpec(shape, lambda b, s: (0,) * len(shape))
    return pl.pallas_call(
        functools.partial(_mixer_kernel, chunk_decay, ts),
        grid=(B, S // ts),
        in_specs=[
            pl.BlockSpec((None, ts, D), lambda b, s: (b0 + b, s, 0)),
            pl.BlockSpec((None, ts, 1), lambda b, s: (b0 + b, s, 0)),
            pl.BlockSpec((None, N_MOD, D), lambda b, s: (b0 + b, 0, 0)),
            full((1, D)),
            full((D, in_cols)),
            full((CONV_HALO, conv_ch)),
            full((1, conv_ch)),
            full((1, conv_ch)),
            full((1, conv_ch)),
            full((1, RET_HEADS * RET_DIM)),
            full((conv_ch + RET_HEADS * RET_DIM, D)),
            full((1, RET_DIM)),
            full((RET_HEADS, RET_CHUNK, RET_CHUNK)),
            full((RET_HEADS, RET_CHUNK, 1)),
            full((RET_HEADS, RET_CHUNK, 1)),
            pl.BlockSpec(memory_space=pl.ANY),
        ],
        out_specs=pl.BlockSpec((None, ts, D), lambda b, s: (b, s, 0)),
        out_shape=jax.ShapeDtypeStruct((B, S, D), F32),
        scratch_shapes=[
            pltpu.VMEM((ts, in_cols), F32),
            pltpu.VMEM((CONV_HALO + ts, conv_ch), F32),
            pltpu.VMEM((RET_HEADS, RET_DIM, RET_DIM), F32),
            pltpu.VMEM((ts, conv_ch + RET_HEADS * RET_DIM), BF16),
            pltpu.VMEM((ts, conv_ch), F32),
        ],
        compiler_params=pltpu.CompilerParams(
            dimension_semantics=("arbitrary", "arbitrary"), vmem_limit_bytes=VMEM_LIMIT_BYTES),
        name="hybrid_mixer",
    )(x, positions.reshape(-1, S, 1), mod, g_mix.reshape(1, D), w_in.astype(BF16), w_dw_p,
      b_dw.reshape(1, -1), g_conv_ln.reshape(1, -1), b_conv_ln.reshape(1, -1), g_ret_norm.reshape(1, -1),
      w_out.astype(BF16), jnp.asarray(rope), jnp.asarray(mask * key_scale), jnp.asarray(q_decay),
      jnp.asarray(k_decay * key_scale), after)


def _route_kernel(x_ref, mod_ref, gffn_ref, wr2_ref, wrhi_ref, br_ref, utri_ref,
                  h2p_ref, idx_ref, wts_ref, rank_ref, cnt_ref, carry_ref):
    i = pl.program_id(0)

    @pl.when(i == 0)
    def _():
        carry_ref[...] = jnp.zeros_like(carry_ref)

    x = x_ref[...]
    tr = x.shape[0]
    half = x.shape[1] // 2
    y = x * lax.rsqrt(jnp.mean(x * x, axis=-1, keepdims=True) + RMS_EPS) * gffn_ref[...]
    h2 = y * (1.0 + mod_ref[4:5, :]) + mod_ref[3:4, :]
    h2p_ref[...] = _pack_bf16_pair(h2[:, :half], h2[:, half:])

    h_hi, h_lo = _split_bf16(h2)
    nt = (((1,), (1,)), ((), ()))
    r = lax.dot_general(wr2_ref[...], h_hi, nt, preferred_element_type=F32)
    r2 = lax.dot_general(wrhi_ref[...], h_lo, nt, preferred_element_type=F32)
    l = r[:N_EXPERTS] + (r[N_EXPERTS:] + r2) + br_ref[...]
    eid = lax.broadcasted_iota(I32, l.shape, 0)
    vals, sels, idxs = [], [], []
    for _ in range(TOP_K):
        m = jnp.max(l, axis=0, keepdims=True)
        ik = jnp.min(jnp.where(l == m, eid, N_EXPERTS), axis=0, keepdims=True)
        sel = eid == ik
        vals.append(m)
        sels.append(sel)
        idxs.append(ik)
        l = jnp.where(sel, -jnp.inf, l)
    exps = [jnp.exp(v - vals[0]) for v in vals]
    denom = exps[0] + exps[1] + exps[2] + exps[3]
    member = jnp.zeros(l.shape, F32)
    for sel in sels:
        member = member + sel.astype(F32)
    before = jnp.dot(member.astype(BF16), utri_ref[...], preferred_element_type=F32) + carry_ref[...]
    ranks = [jnp.sum(jnp.where(sel, before, 0.0), axis=0, keepdims=True) for sel in sels]
    carry_ref[...] = carry_ref[...] + jnp.sum(member, axis=1, keepdims=True)
    cnt_ref[...] = carry_ref[...].astype(I32)

    def rows(pieces, n_rows, dtype):
        rid = lax.broadcasted_iota(I32, (n_rows, tr), 0)
        out = jnp.zeros((n_rows, tr), dtype)
        for k, p in enumerate(pieces):
            out = jnp.where(rid == k, p.astype(dtype), out)
        return out

    idx_ref[...] = rows(idxs, idx_ref.shape[0], I32)
    rank_ref[...] = rows(ranks, rank_ref.shape[0], I32)
    wts_ref[...] = rows([e / denom for e in exps], wts_ref.shape[1], F32).T


ROUTE_ROWS = 8
ROUTE_LANES = 128


def _route_call(x1, mod, g_ffn, w_router, b_router, tokens_per_batch, batch0):
    T, D = x1.shape
    tr = min(512, tokens_per_batch)
    per_b = tokens_per_batch // tr
    wr_hi, wr_lo = _split_bf16(w_router.T)
    utri = jnp.asarray(np.triu(np.ones((tr, tr), np.float32), 1), BF16)
    full = lambda shape: pl.BlockSpec(shape, lambda i: (0,) * len(shape))
    return pl.pallas_call(
        _route_kernel,
        grid=(T // tr,),
        in_specs=[
            pl.BlockSpec((tr, D), lambda i: (i, 0)),
            pl.BlockSpec((None, N_MOD, D), lambda i: (batch0 + i // per_b, 0, 0)),
            full((1, D)),
            full((2 * N_EXPERTS, D)),
            full((N_EXPERTS, D)),
            full((N_EXPERTS, 1)),
            full((tr, tr)),
        ],
        out_specs=[
            pl.BlockSpec((tr, D // 2), lambda i: (i, 0)),
            pl.BlockSpec((ROUTE_ROWS, tr), lambda i: (0, i)),
            pl.BlockSpec((tr, ROUTE_LANES), lambda i: (i, 0)),
            pl.BlockSpec((ROUTE_ROWS, tr), lambda i: (0, i)),
            full((N_EXPERTS, 1)),
        ],
        out_shape=[
            jax.ShapeDtypeStruct((T, D // 2), U32),
            jax.ShapeDtypeStruct((ROUTE_ROWS, T), I32),
            jax.ShapeDtypeStruct((T, ROUTE_LANES), F32),
            jax.ShapeDtypeStruct((ROUTE_ROWS, T), I32),
            jax.ShapeDtypeStruct((N_EXPERTS, 1), I32),
        ],
        scratch_shapes=[pltpu.VMEM((N_EXPERTS, 1), F32)],
        compiler_params=pltpu.CompilerParams(
            dimension_semantics=("arbitrary",), vmem_limit_bytes=VMEM_LIMIT_BYTES),
        name="moe_route",
    )(x1, mod, g_ffn.reshape(1, D), jnp.concatenate([wr_hi, wr_lo], axis=0), wr_hi,
      b_router.reshape(N_EXPERTS, 1), utri)


SC_ROWS = 128


V7X_SC_CORES = 2
V7X_SC_SUBCORES = 16


def _sc_workers():
    return V7X_SC_CORES, V7X_SC_SUBCORES


def _sc_mesh():
    return plsc.VectorSubcoreMesh(core_axis_name="c", subcore_axis_name="s",
                                  num_cores=V7X_SC_CORES, num_subcores=V7X_SC_SUBCORES)


def _sc_scatter_call(h2p, pos_km, n_rows):
    T, W = h2p.shape
    nc, ns = _sc_workers()
    n = SC_ROWS
    per_w = T // (nc * ns * n)

    def body(h2p_hbm, pos_hbm, xs_hbm, i0, i1, i2, i3, rows_v, sem):
        wid = lax.axis_index("s") * nc + lax.axis_index("c")
        idx_refs = (i0, i1, i2, i3)

        @pl.loop(0, per_w)
        def _(j):
            t0 = (wid * per_w + j) * n
            pltpu.sync_copy(h2p_hbm.at[pl.ds(t0, n)], rows_v)
            for k in range(TOP_K):
                pltpu.sync_copy(pos_hbm.at[pl.ds(k * T + t0, n)], idx_refs[k])
            copies = [pltpu.async_copy(rows_v, xs_hbm.at[idx_refs[k]], sem) for k in range(TOP_K)]
            for cp in copies:
                cp.wait()

    return pl.kernel(
        body,
        out_type=jax.ShapeDtypeStruct((n_rows, W), h2p.dtype),
        mesh=_sc_mesh(),
        scratch_types=[pltpu.VMEM((n,), I32)] * TOP_K + [pltpu.VMEM((n, W), h2p.dtype), pltpu.SemaphoreType.DMA],
        name="moe_scatter_rows",
    )(h2p, pos_km)


def _sc_gather_call(ys, pos_km):
    P = pos_km.shape[0]
    W = ys.shape[1]
    nc, ns = _sc_workers()
    n = SC_ROWS
    per_w = P // (nc * ns * n)

    def body(ys_hbm, pos_hbm, yp_hbm, idx_v, rows_v, sem):
        wid = lax.axis_index("s") * nc + lax.axis_index("c")

        @pl.loop(0, per_w)
        def _(j):
            p0 = (wid * per_w + j) * n
            pltpu.sync_copy(pos_hbm.at[pl.ds(p0, n)], idx_v)
            pltpu.async_copy(ys_hbm.at[idx_v], rows_v, sem).wait()
            pltpu.sync_copy(rows_v, yp_hbm.at[pl.ds(p0, n)])

    return pl.kernel(
        body,
        out_type=jax.ShapeDtypeStruct((P, W), ys.dtype),
        mesh=_sc_mesh(),
        scratch_types=[pltpu.VMEM((n,), I32), pltpu.VMEM((n, W), ys.dtype), pltpu.SemaphoreType.DMA],
        name="moe_gather_rows",
    )(ys, pos_km)


def _ffn_kernel(te_ref, tv_ref, nx_ref, xs_ref, wgu_hbm, bgu_ref, wd_hbm, bd_ref, y_ref,
                wgu_f32, wd_f32, wgu_bf, wd_bf, sems):
    i = pl.program_id(0)
    valid = tv_ref[i]
    tm, half = xs_ref.shape
    d_exp = wd_bf.shape[0]

    def weight_copies(e):
        return (pltpu.make_async_copy(wgu_hbm.at[e], wgu_f32, sems.at[0]),
                pltpu.make_async_copy(wd_hbm.at[e], wd_f32, sems.at[1]))

    @pl.when(i == 0)
    def _():
        for cp in weight_copies(te_ref[0]):
            cp.start(priority=1)

    @pl.when(((i == 0) | (te_ref[i] != te_ref[jnp.maximum(i - 1, 0)])) & (valid > 0))
    def _():
        for cp in weight_copies(te_ref[i]):
            cp.wait()
        wgu_bf[...] = wgu_f32[...].astype(BF16)
        wd_bf[...] = wd_f32[...].astype(BF16)

        @pl.when(nx_ref[i] >= 0)
        def _():
            for cp in weight_copies(nx_ref[i]):
                cp.start(priority=1)

    def sub_tile(r0):
        rows = slice(r0, r0 + FFN_SUB_ROWS)
        keep = lax.broadcasted_iota(I32, (FFN_SUB_ROWS, half), 0) < valid - r0
        lo, hi = _unpack_bf16_pair(xs_ref[rows, :])
        xt = jnp.concatenate([jnp.where(keep, lo, 0.0).astype(BF16),
                              jnp.where(keep, hi, 0.0).astype(BF16)], axis=1)
        d = functools.partial(jnp.dot, preferred_element_type=F32)
        nb = 256

        def up(c):
            gate = d(xt, wgu_bf[:, c:c + nb]) + bgu_ref[:, c:c + nb]
            lin = d(xt, wgu_bf[:, d_exp + c:d_exp + c + nb]) + bgu_ref[:, d_exp + c:d_exp + c + nb]
            return gate, lin

        def activate(gate, lin):
            gate = jnp.minimum(gate, SWIGLU_LIMIT)
            lin = jnp.clip(lin, -SWIGLU_LIMIT, SWIGLU_LIMIT)
            return (gate * jax.nn.sigmoid(SWIGLU_ALPHA * gate) * (lin + 1.0)).astype(BF16)

        chunks = list(range(0, d_exp, nb))
        acts = []
        pending = up(chunks[0])
        for n in range(len(chunks)):
            nxt = up(chunks[n + 1]) if n + 1 < len(chunks) else None
            acts.append(activate(*pending))
            pending = nxt
        out = d(jnp.concatenate(acts, axis=1), wd_bf[...]) + bd_ref[...]
        y_ref[rows, :] = _pack_bf16_pair(out[:, :half], out[:, half:])

    for r0 in range(0, tm, FFN_SUB_ROWS):
        pl.when(valid > r0)(functools.partial(sub_tile, r0))


def _ffn_call(tile_expert, tile_valid, tile_next, xs, w_gu, b_gu, w_down, b_down, tm):
    R, half = xs.shape
    E, D, two_f = w_gu.shape
    d_exp = w_down.shape[1]
    grid_spec = pltpu.PrefetchScalarGridSpec(
        num_scalar_prefetch=3,
        grid=(R // tm,),
        in_specs=[
            pl.BlockSpec((tm, half), lambda i, te, tv, nx: (i, 0)),
            pl.BlockSpec(memory_space=pl.ANY),
            pl.BlockSpec((None, 1, two_f), lambda i, te, tv, nx: (te[i], 0, 0)),
            pl.BlockSpec(memory_space=pl.ANY),
            pl.BlockSpec((None, 1, D), lambda i, te, tv, nx: (te[i], 0, 0)),
        ],
        out_specs=pl.BlockSpec((tm, half), lambda i, te, tv, nx: (i, 0)),
        scratch_shapes=[
            pltpu.VMEM((D, two_f), F32), pltpu.VMEM((d_exp, D), F32),
            pltpu.VMEM((D, two_f), BF16), pltpu.VMEM((d_exp, D), BF16),
            pltpu.SemaphoreType.DMA((2,)),
        ],
    )
    return pl.pallas_call(
        _ffn_kernel,
        grid_spec=grid_spec,
        out_shape=jax.ShapeDtypeStruct((R, half), U32),
        compiler_params=pltpu.CompilerParams(
            dimension_semantics=("arbitrary",), vmem_limit_bytes=VMEM_LIMIT_BYTES),
        name="moe_ffn",
    )(tile_expert, tile_valid, tile_next, xs, w_gu, b_gu.reshape(E, 1, two_f), w_down,
      b_down.reshape(E, 1, D))


def _final_kernel(x_ref, yp_ref, wts_ref, mod_ref, gfin_ref, *rest):
    o_ref = rest[-1]
    x = x_ref[...]
    half = x.shape[1] // 2
    w = wts_ref[...]
    lo = jnp.zeros((x.shape[0], half), F32)
    hi = jnp.zeros((x.shape[0], half), F32)
    for k in range(TOP_K):
        l, h = _unpack_bf16_pair(yp_ref[k])
        lo = lo + w[:, k:k + 1] * l
        hi = hi + w[:, k:k + 1] * h
    gate = mod_ref[5:6, :]
    x_lo = x[:, :half] + gate[:, :half] * lo
    x_hi = x[:, half:] + gate[:, half:] * hi
    ms = (jnp.sum(x_lo * x_lo, axis=-1, keepdims=True) + jnp.sum(x_hi * x_hi, axis=-1, keepdims=True)) / x.shape[1]
    inv = lax.rsqrt(ms + RMS_EPS)
    o_ref[:, :half] = x_lo * inv * gfin_ref[:, :half]
    o_ref[:, half:] = x_hi * inv * gfin_ref[:, half:]


def _final_call(x1, yp, wts, mod, g_final, tokens_per_batch, group, n_groups, prev_out):
    T, D = x1.shape
    T_all = T * n_groups
    tq = min(512, tokens_per_batch)
    per_b = tokens_per_batch // tq
    first = group * (T // tq)
    in_specs = [
        pl.BlockSpec((tq, D), lambda i: (i, 0)),
        pl.BlockSpec((TOP_K, tq, D // 2), lambda i: (0, i, 0)),
        pl.BlockSpec((tq, ROUTE_LANES), lambda i: (i, 0)),
        pl.BlockSpec((None, N_MOD, D), lambda i: ((first + i) // per_b, 0, 0)),
        pl.BlockSpec((1, D), lambda i: (0, 0)),
    ]
    args = [x1, yp, wts, mod, g_final.reshape(1, D)]
    aliases = {}
    if prev_out is not None:
        in_specs.append(pl.BlockSpec(memory_space=pl.ANY))
        args.append(prev_out)
        aliases = {len(args) - 1: 0}
    return pl.pallas_call(
        _final_kernel,
        grid=(T // tq,),
        in_specs=in_specs,
        out_specs=pl.BlockSpec((tq, D), lambda i: (first + i, 0)),
        out_shape=jax.ShapeDtypeStruct((T_all, D), F32),
        input_output_aliases=aliases,
        compiler_params=pltpu.CompilerParams(
            dimension_semantics=("arbitrary",), vmem_limit_bytes=VMEM_LIMIT_BYTES),
        name="moe_combine_final",
    )(*args)


def _group_layout(counts, n_tiles, tm):
    padded = ((counts + tm - 1) // tm) * tm
    ends = jnp.cumsum(padded)
    starts = ends - padded
    tile_row = jnp.arange(n_tiles, dtype=I32) * tm
    te = jnp.minimum(jnp.sum(tile_row[:, None] >= ends[None, :], axis=1), N_EXPERTS - 1).astype(I32)
    eids = jnp.arange(N_EXPERTS, dtype=I32)
    mine = te[:, None] == eids[None, :]
    lookup = lambda table: jnp.sum(jnp.where(mine, table[None, :], 0), axis=1)
    tv = jnp.clip(lookup(counts) - (tile_row - lookup(starts)), 0, tm).astype(I32)
    later = (eids[None, :] > eids[:, None]) & (counts[None, :] > 0)
    nxt = jnp.min(jnp.where(later, eids[None, :], N_EXPERTS), axis=1)
    nx = lookup(jnp.where(nxt < N_EXPERTS, nxt, -1)).astype(I32)
    return starts, te, tv, nx


def kernel(x, c, positions, w_ada, b_ada, g_mix, w_in, w_dw, b_dw, g_conv_ln, b_conv_ln, g_ret_norm,
           w_out, g_ffn, w_router, b_router, w_gu, b_gu, w_down, b_down, g_final):
    B, S, D = x.shape
    T = B * S
    assert w_ada.shape[0] == 1, "single-layer block: the final norm directly follows layer 0"
    xt = x
    for l in range(1):
        mod = _mod_call(c, w_ada[l], b_ada[l]).reshape(B, N_MOD, D)
        n_groups = MOE_TOKEN_GROUPS if B % MOE_TOKEN_GROUPS == 0 else 1
        Bg = B // n_groups
        Tg = Bg * S
        tm = FFN_TILE_ROWS if Tg * TOP_K >= FFN_TILE_ROWS * N_EXPERTS * 4 else FFN_SUB_ROWS
        n_tiles = (Tg * TOP_K) // tm + N_EXPERTS
        x1s, scattered = [], []
        pos_km = jnp.zeros((TOP_K * Tg,), I32)
        for g in range(n_groups):
            x1 = _mixer_call(xt, positions, mod, g_mix[l], w_in[l], w_dw[l], b_dw[l], g_conv_ln[l],
                             b_conv_ln[l], g_ret_norm[l], w_out[l], g, n_groups, pos_km).reshape(Tg, D)
            h2p, idx, wts, rank, counts = _route_call(x1, mod, g_ffn[l], w_router[l], b_router[l], S, g * Bg)
            starts, te, tv, nx = _group_layout(counts[:, 0], n_tiles, tm)
            pos_km = rank[:TOP_K]
            for e in range(N_EXPERTS):
                pos_km = pos_km + jnp.where(idx[:TOP_K] == e, starts[e], 0)
            pos_km = pos_km.astype(I32).reshape(-1)
            x1s.append(x1)
            scattered.append((_sc_scatter_call(h2p, pos_km, n_tiles * tm), pos_km, wts, te, tv, nx))
        out = None
        for g, (xs, pos_km, wts, te, tv, nx) in enumerate(scattered):
            ys = _ffn_call(te, tv, nx, xs, w_gu[l], b_gu[l], w_down[l], b_down[l], tm)
            yp = _sc_gather_call(ys, pos_km)
            out = _final_call(x1s[g], yp.reshape(TOP_K, Tg, D // 2), wts, mod, g_final, S, g, n_groups, out)
        xt = out
    return xt.reshape(B, S, D)
```

```python
import functools

import numpy as np
import jax
import jax.numpy as jnp
from jax import lax
from jax.experimental import pallas as pl
from jax.experimental.pallas import tpu as pltpu
from jax.experimental.pallas import tpu_sc as plsc

F32 = jnp.float32
BF16 = jnp.bfloat16
U32 = jnp.uint32
I32 = jnp.int32

CONV_WIDTH = 31
CONV_HALO = 32
RET_HEADS = 4
RET_DIM = 128
RET_CHUNK = 128
ROPE_BASE = 10000.0
N_EXPERTS = 32
TOP_K = 4
SWIGLU_LIMIT = 7.0
SWIGLU_ALPHA = 1.702
RMS_EPS = 1e-6
LN_EPS = 1e-5
N_MOD = 6

VMEM_LIMIT_BYTES = 56 * 1024 * 1024
MOE_TOKEN_GROUPS = 2
FFN_TILE_ROWS = 1024
FFN_SUB_ROWS = 512


def _split_bf16(a):
    hi = a.astype(BF16)
    lo = (a - hi.astype(F32)).astype(BF16)
    return hi, lo


def _dot3(a, b_hi, b_lo):
    a_hi, a_lo = _split_bf16(a)
    d = functools.partial(jnp.dot, preferred_element_type=F32)
    return d(a_hi, b_hi) + (d(a_hi, b_lo) + d(a_lo, b_hi))


def _pack_bf16_pair(lo, hi):
    lo_bits = lax.bitcast_convert_type(lo.astype(BF16).astype(F32), U32)
    hi_bits = lax.bitcast_convert_type(hi.astype(BF16).astype(F32), U32)
    return (lo_bits >> 16) | (hi_bits & jnp.uint32(0xFFFF0000))


def _unpack_bf16_pair(p):
    lo = lax.bitcast_convert_type(p << 16, F32)
    hi = lax.bitcast_convert_type(p & jnp.uint32(0xFFFF0000), F32)
    return lo, hi


def _mod_kernel(c_ref, whi_ref, wlo_ref, b_ref, o_ref):
    c = c_ref[...]
    c_act = c * jax.nn.sigmoid(c)
    o_ref[...] = _dot3(c_act, whi_ref[...], wlo_ref[...]) + b_ref[...]


def _mod_call(c, w_ada, b_ada):
    B, D = c.shape
    n = w_ada.shape[1]
    bn = 1024
    w_hi, w_lo = _split_bf16(w_ada)
    return pl.pallas_call(
        _mod_kernel,
        grid=(n // bn,),
        in_specs=[
            pl.BlockSpec((B, D), lambda j: (0, 0)),
            pl.BlockSpec((D, bn), lambda j: (0, j)),
            pl.BlockSpec((D, bn), lambda j: (0, j)),
            pl.BlockSpec((1, bn), lambda j: (0, j)),
        ],
        out_specs=pl.BlockSpec((B, bn), lambda j: (0, j)),
        out_shape=jax.ShapeDtypeStruct((B, n), F32),
        name="adaln_mod",
    )(c, w_hi, w_lo, b_ada.reshape(1, n))


def _retention_tables():
    h = np.arange(RET_HEADS, dtype=np.float32)
    log_gamma = np.log(1.0 - np.power(2.0, -5.0 - h)).astype(np.float32)
    idx = np.arange(RET_CHUNK, dtype=np.float32)
    diff = idx[:, None] - idx[None, :]
    causal = diff >= 0
    mask = np.where(causal[None], np.exp(log_gamma[:, None, None] * np.where(causal, diff, 0.0)[None]), 0.0)
    q_decay = np.exp(log_gamma[:, None] * (idx + 1.0))[..., None]
    k_decay = np.exp(log_gamma[:, None] * (RET_CHUNK - 1.0 - idx))[..., None]
    chunk_decay = np.exp(log_gamma * RET_CHUNK)
    return (mask.astype(np.float32), q_decay.astype(np.float32), k_decay.astype(np.float32),
            [float(v) for v in chunk_decay.astype(np.float32)])


def _mixer_kernel(chunk_decay, ts, x_ref, pos_ref, mod_ref, gmix_ref, win_ref, wdw_ref, bdw_ref,
                  gcl_ref, bcl_ref, gret_ref, wout_ref, rope_ref, dmask_ref, qdec_ref, kdec_ref,
                  after_ref, o_ref, proj_ref, uext_ref, state_ref, cat_ref, conv_ref):
    del after_ref
    s = pl.program_id(1)
    conv_ch = wdw_ref.shape[1]
    ret_w = RET_HEADS * RET_DIM

    @pl.when(s == 0)
    def _():
        uext_ref[0:CONV_HALO, :] = jnp.zeros((CONV_HALO, conv_ch), F32)
        state_ref[...] = jnp.zeros_like(state_ref)

    x = x_ref[...]
    row_gain = gmix_ref[...] * (1.0 + mod_ref[1:2, :])
    h = (x * lax.rsqrt(jnp.mean(x * x, axis=-1, keepdims=True) + RMS_EPS) * row_gain + mod_ref[0:1, :]).astype(BF16)
    proj_ref[...] = jnp.dot(h, win_ref[...], preferred_element_type=F32)

    a = proj_ref[:, 0:conv_ch]
    b = proj_ref[:, conv_ch:2 * conv_ch]
    uext_ref[CONV_HALO:CONV_HALO + ts, :] = a * jax.nn.sigmoid(b)
    cb = 64
    lead = CONV_HALO - (CONV_WIDTH - 1)
    span = cb + CONV_HALO

    def conv_block(r0):
        for c0 in range(0, conv_ch, 128):
            xw = uext_ref[r0:r0 + span, c0:c0 + 128]
            acc = jnp.zeros((cb, 128), F32) + bdw_ref[:, c0:c0 + 128]
            for r in range(8):
                xr = xw if r == 0 else pltpu.roll(xw, span - r, 0)
                for q in range((lead + CONV_WIDTH - 1 - r) // 8 + 1):
                    j = 8 * q + r - lead
                    if 0 <= j < CONV_WIDTH:
                        acc = acc + wdw_ref[j:j + 1, c0:c0 + 128] * xr[8 * q:8 * q + cb]
            conv_ref[r0:r0 + cb, c0:c0 + 128] = acc
    rb = 2 * cb

    def conv_norm(r0):
        acc = conv_ref[r0:r0 + rb, 0:conv_ch]
        mu = jnp.mean(acc, axis=-1, keepdims=True)
        d = acc - mu
        var = jnp.mean(d * d, axis=-1, keepdims=True)
        ln = d * lax.rsqrt(var + LN_EPS) * gcl_ref[...] + bcl_ref[...]
        cat_ref[r0:r0 + rb, 0:conv_ch] = (ln * jax.nn.sigmoid(ln)).astype(BF16)

    hts = ts // 2
    low = lax.broadcasted_iota(I32, (hts, RET_DIM), 1) < RET_DIM // 2
    posf = jnp.where(low, pos_ref[0:hts, :].astype(F32), pos_ref[hts:ts, :].astype(F32))
    ang = posf * rope_ref[...]
    cos_p = jnp.cos(ang)
    sin_p = jnp.sin(ang)
    cos_s = pltpu.roll(cos_p, RET_DIM // 2, 1)
    sin_s = pltpu.roll(sin_p, RET_DIM // 2, 1)
    cos2 = jnp.concatenate([jnp.where(low, cos_p, cos_s), jnp.where(low, cos_s, cos_p)], axis=0)
    sin2 = jnp.concatenate([jnp.where(low, -sin_p, sin_s), jnp.where(low, -sin_s, sin_p)], axis=0)
    q0 = 2 * conv_ch
    k0 = q0 + ret_w
    v0 = k0 + ret_w
    g0 = v0 + ret_w

    def retention_unit(hd, n):
        c0 = hd * RET_DIM
        r0 = n * RET_CHUNK
        rows = slice(r0, r0 + RET_CHUNK)
        cs = cos2[rows]
        sn = sin2[rows]
        q = proj_ref[rows, q0 + c0:q0 + c0 + RET_DIM]
        k = proj_ref[rows, k0 + c0:k0 + c0 + RET_DIM]
        v = proj_ref[rows, v0 + c0:v0 + c0 + RET_DIM].astype(BF16)
        g = proj_ref[rows, g0 + c0:g0 + c0 + RET_DIM]
        qr = q * cs + pltpu.roll(q, RET_DIM // 2, 1) * sn
        kr = k * cs + pltpu.roll(k, RET_DIM // 2, 1) * sn
        st = state_ref[hd]
        scores = lax.dot_general(qr.astype(BF16), kr.astype(BF16), (((1,), (1,)), ((), ())),
                                 preferred_element_type=F32) * dmask_ref[hd]
        inner = jnp.dot(scores.astype(BF16), v, preferred_element_type=F32)
        cross = jnp.dot((qr * qdec_ref[hd]).astype(BF16), st.astype(BF16), preferred_element_type=F32)
        kv = lax.dot_general((kr * kdec_ref[hd]).astype(BF16), v, (((0,), (0,)), ((), ())),
                             preferred_element_type=F32)
        state_ref[hd] = chunk_decay[hd] * st + kv
        r = inner + cross
        mu = jnp.mean(r, axis=-1, keepdims=True)
        d = r - mu
        var = jnp.mean(d * d, axis=-1, keepdims=True)
        rn = d * lax.rsqrt(var + LN_EPS) * gret_ref[:, c0:c0 + RET_DIM]
        cat_ref[rows, conv_ch + c0:conv_ch + c0 + RET_DIM] = (g * jax.nn.sigmoid(g) * rn).astype(BF16)

    units = [(hd, n) for n in range(ts // RET_CHUNK) for hd in range(RET_HEADS)]
    n_conv = ts // cb
    per = -(-len(units) // n_conv)
    for i in range(n_conv):
        conv_block(i * cb)
        if i % 2 == 1:
            conv_norm((i - 1) * cb)
        for hd, n in units[i * per:(i + 1) * per]:
            retention_unit(hd, n)
    uext_ref[0:CONV_HALO, :] = uext_ref[ts:ts + CONV_HALO, :]

    out = jnp.dot(cat_ref[...], wout_ref[...], preferred_element_type=F32)
    o_ref[...] = x + mod_ref[2:3, :] * out


def _mixer_call(x, positions, mod, g_mix, w_in, w_dw, b_dw, g_conv_ln, b_conv_ln, g_ret_norm, w_out,
                group, n_groups, after):
    S, D = x.shape[1:]
    B = x.shape[0] // n_groups
    b0 = group * B
    in_cols = w_in.shape[1]
    conv_ch = w_dw.shape[1]
    ts = min(512, S)
    mask, q_decay, k_decay, chunk_decay = _retention_tables()
    key_scale = np.float32(RET_DIM ** -0.5)
    half = RET_DIM // 2
    inv_freq = (ROPE_BASE ** (-np.arange(half, dtype=np.float32) / half)).astype(np.float32)
    rope = np.concatenate([inv_freq, inv_freq])[None, :]
    w_dw_p = jnp.zeros((CONV_HALO, conv_ch), F32).at[:CONV_WIDTH].set(w_dw)
    full = lambda shape: pl.BlockSpec(shape, lambda b, s: (0,) * len(shape))
    return pl.pallas_call(
        functools.partial(_mixer_kernel, chunk_decay, ts),
        grid=(B, S // ts),
        in_specs=[
            pl.BlockSpec((None, ts, D), lambda b, s: (b0 + b, s, 0)),
            pl.BlockSpec((None, ts, 1), lambda b, s: (b0 + b, s, 0)),
            pl.BlockSpec((None, N_MOD, D), lambda b, s: (b0 + b, 0, 0)),
            full((1, D)),
            full((D, in_cols)),
            full((CONV_HALO, conv_ch)),
            full((1, conv_ch)),
            full((1, conv_ch)),
            full((1, conv_ch)),
            full((1, RET_HEADS * RET_DIM)),
            full((conv_ch + RET_HEADS * RET_DIM, D)),
            full((1, RET_DIM)),
            full((RET_HEADS, RET_CHUNK, RET_CHUNK)),
            full((RET_HEADS, RET_CHUNK, 1)),
            full((RET_HEADS, RET_CHUNK, 1)),
            pl.BlockSpec(memory_space=pl.ANY),
        ],
        out_specs=pl.BlockSpec((None, ts, D), lambda b, s: (b, s, 0)),
        out_shape=jax.ShapeDtypeStruct((B, S, D), F32),
        scratch_shapes=[
            pltpu.VMEM((ts, in_cols), F32),
            pltpu.VMEM((CONV_HALO + ts, conv_ch), F32),
            pltpu.VMEM((RET_HEADS, RET_DIM, RET_DIM), F32),
            pltpu.VMEM((ts, conv_ch + RET_HEADS * RET_DIM), BF16),
            pltpu.VMEM((ts, conv_ch), F32),
        ],
        compiler_params=pltpu.CompilerParams(
            dimension_semantics=("arbitrary", "arbitrary"), vmem_limit_bytes=VMEM_LIMIT_BYTES),
        name="hybrid_mixer",
    )(x, positions.reshape(-1, S, 1), mod, g_mix.reshape(1, D), w_in.astype(BF16), w_dw_p,
      b_dw.reshape(1, -1), g_conv_ln.reshape(1, -1), b_conv_ln.reshape(1, -1), g_ret_norm.reshape(1, -1),
      w_out.astype(BF16), jnp.asarray(rope), jnp.asarray(mask * key_scale), jnp.asarray(q_decay),
      jnp.asarray(k_decay * key_scale), after)


def _route_kernel(x_ref, mod_ref, gffn_ref, wr2_ref, wrhi_ref, br_ref, utri_ref,
                  h2p_ref, idx_ref, wts_ref, rank_ref, cnt_ref, carry_ref):
    i = pl.program_id(0)

    @pl.when(i == 0)
    def _():
        carry_ref[...] = jnp.zeros_like(carry_ref)

    x = x_ref[...]
    tr = x.shape[0]
    half = x.shape[1] // 2
    y = x * lax.rsqrt(jnp.mean(x * x, axis=-1, keepdims=True) + RMS_EPS) * gffn_ref[...]
    h2 = y * (1.0 + mod_ref[4:5, :]) + mod_ref[3:4, :]
    h2p_ref[...] = _pack_bf16_pair(h2[:, :half], h2[:, half:])

    h_hi, h_lo = _split_bf16(h2)
    nt = (((1,), (1,)), ((), ()))
    r = lax.dot_general(wr2_ref[...], h_hi, nt, preferred_element_type=F32)
    r2 = lax.dot_general(wrhi_ref[...], h_lo, nt, preferred_element_type=F32)
    l = r[:N_EXPERTS] + (r[N_EXPERTS:] + r2) + br_ref[...]
    eid = lax.broadcasted_iota(I32, l.shape, 0)
    vals, sels, idxs = [], [], []
    for _ in range(TOP_K):
        m = jnp.max(l, axis=0, keepdims=True)
        ik = jnp.min(jnp.where(l == m, eid, N_EXPERTS), axis=0, keepdims=True)
        sel = eid == ik
        vals.append(m)
        sels.append(sel)
        idxs.append(ik)
        l = jnp.where(sel, -jnp.inf, l)
    exps = [jnp.exp(v - vals[0]) for v in vals]
    denom = exps[0] + exps[1] + exps[2] + exps[3]
    member = jnp.zeros(l.shape, F32)
    for sel in sels:
        member = member + sel.astype(F32)
    before = jnp.dot(member.astype(BF16), utri_ref[...], preferred_element_type=F32) + carry_ref[...]
    ranks = [jnp.sum(jnp.where(sel, before, 0.0), axis=0, keepdims=True) for sel in sels]
    carry_ref[...] = carry_ref[...] + jnp.sum(member, axis=1, keepdims=True)
    cnt_ref[...] = carry_ref[...].astype(I32)

    def rows(pieces, n_rows, dtype):
        rid = lax.broadcasted_iota(I32, (n_rows, tr), 0)
        out = jnp.zeros((n_rows, tr), dtype)
        for k, p in enumerate(pieces):
            out = jnp.where(rid == k, p.astype(dtype), out)
        return out

    idx_ref[...] = rows(idxs, idx_ref.shape[0], I32)
    rank_ref[...] = rows(ranks, rank_ref.shape[0], I32)
    wts_ref[...] = rows([e / denom for e in exps], wts_ref.shape[1], F32).T


ROUTE_ROWS = 8
ROUTE_LANES = 128


def _route_call(x1, mod, g_ffn, w_router, b_router, tokens_per_batch, batch0):
    T, D = x1.shape
    tr = min(512, tokens_per_batch)
    per_b = tokens_per_batch // tr
    wr_hi, wr_lo = _split_bf16(w_router.T)
    utri = jnp.asarray(np.triu(np.ones((tr, tr), np.float32), 1), BF16)
    full = lambda shape: pl.BlockSpec(shape, lambda i: (0,) * len(shape))
    return pl.pallas_call(
        _route_kernel,
        grid=(T // tr,),
        in_specs=[
            pl.BlockSpec((tr, D), lambda i: (i, 0)),
            pl.BlockSpec((None, N_MOD, D), lambda i: (batch0 + i // per_b, 0, 0)),
            full((1, D)),
            full((2 * N_EXPERTS, D)),
            full((N_EXPERTS, D)),
            full((N_EXPERTS, 1)),
            full((tr, tr)),
        ],
        out_specs=[
            pl.BlockSpec((tr, D // 2), lambda i: (i, 0)),
            pl.BlockSpec((ROUTE_ROWS, tr), lambda i: (0, i)),
            pl.BlockSpec((tr, ROUTE_LANES), lambda i: (i, 0)),
            pl.BlockSpec((ROUTE_ROWS, tr), lambda i: (0, i)),
            full((N_EXPERTS, 1)),
        ],
        out_shape=[
            jax.ShapeDtypeStruct((T, D // 2), U32),
            jax.ShapeDtypeStruct((ROUTE_ROWS, T), I32),
            jax.ShapeDtypeStruct((T, ROUTE_LANES), F32),
            jax.ShapeDtypeStruct((ROUTE_ROWS, T), I32),
            jax.ShapeDtypeStruct((N_EXPERTS, 1), I32),
        ],
        scratch_shapes=[pltpu.VMEM((N_EXPERTS, 1), F32)],
        compiler_params=pltpu.CompilerParams(
            dimension_semantics=("arbitrary",), vmem_limit_bytes=VMEM_LIMIT_BYTES),
        name="moe_route",
    )(x1, mod, g_ffn.reshape(1, D), jnp.concatenate([wr_hi, wr_lo], axis=0), wr_hi,
      b_router.reshape(N_EXPERTS, 1), utri)


SC_ROWS = 128


V7X_SC_CORES = 2
V7X_SC_SUBCORES = 16


def _sc_workers():
    return V7X_SC_CORES, V7X_SC_SUBCORES


def _sc_mesh():
    return plsc.VectorSubcoreMesh(core_axis_name="c", subcore_axis_name="s",
                                  num_cores=V7X_SC_CORES, num_subcores=V7X_SC_SUBCORES)


def _sc_scatter_call(h2p, pos_km, n_rows):
    T, W = h2p.shape
    nc, ns = _sc_workers()
    n = SC_ROWS
    per_w = T // (nc * ns * n)

    def body(h2p_hbm, pos_hbm, xs_hbm, i0, i1, i2, i3, rows_v, sem):
        wid = lax.axis_index("s") * nc + lax.axis_index("c")
        idx_refs = (i0, i1, i2, i3)

        @pl.loop(0, per_w)
        def _(j):
            t0 = (wid * per_w + j) * n
            pltpu.sync_copy(h2p_hbm.at[pl.ds(t0, n)], rows_v)
            for k in range(TOP_K):
                pltpu.sync_copy(pos_hbm.at[pl.ds(k * T + t0, n)], idx_refs[k])
            copies = [pltpu.async_copy(rows_v, xs_hbm.at[idx_refs[k]], sem) for k in range(TOP_K)]
            for cp in copies:
                cp.wait()

    return pl.kernel(
        body,
        out_type=jax.ShapeDtypeStruct((n_rows, W), h2p.dtype),
        mesh=_sc_mesh(),
        scratch_types=[pltpu.VMEM((n,), I32)] * TOP_K + [pltpu.VMEM((n, W), h2p.dtype), pltpu.SemaphoreType.DMA],
        name="moe_scatter_rows",
    )(h2p, pos_km)


def _sc_gather_call(ys, pos_km):
    P = pos_km.shape[0]
    W = ys.shape[1]
    nc, ns = _sc_workers()
    n = SC_ROWS
    per_w = P // (nc * ns * n)

    def body(ys_hbm, pos_hbm, yp_hbm, idx_v, rows_v, sem):
        wid = lax.axis_index("s") * nc + lax.axis_index("c")

        @pl.loop(0, per_w)
        def _(j):
            p0 = (wid * per_w + j) * n
            pltpu.sync_copy(pos_hbm.at[pl.ds(p0, n)], idx_v)
            pltpu.async_copy(ys_hbm.at[idx_v], rows_v, sem).wait()
            pltpu.sync_copy(rows_v, yp_hbm.at[pl.ds(p0, n)])

    return pl.kernel(
        body,
        out_type=jax.ShapeDtypeStruct((P, W), ys.dtype),
        mesh=_sc_mesh(),
        scratch_types=[pltpu.VMEM((n,), I32), pltpu.VMEM((n, W), ys.dtype), pltpu.SemaphoreType.DMA],
        name="moe_gather_rows",
    )(ys, pos_km)


def _ffn_kernel(te_ref, tv_ref, nx_ref, xs_ref, wgu_hbm, bgu_ref, wd_hbm, bd_ref, y_ref,
                wgu_f32, wd_f32, wgu_bf, wd_bf, sems):
    i = pl.program_id(0)
    valid = tv_ref[i]
    tm, half = xs_ref.shape
    d_exp = wd_bf.shape[0]

    def weight_copies(e):
        return (pltpu.make_async_copy(wgu_hbm.at[e], wgu_f32, sems.at[0]),
                pltpu.make_async_copy(wd_hbm.at[e], wd_f32, sems.at[1]))

    @pl.when(i == 0)
    def _():
        for cp in weight_copies(te_ref[0]):
            cp.start(priority=1)

    @pl.when(((i == 0) | (te_ref[i] != te_ref[jnp.maximum(i - 1, 0)])) & (valid > 0))
    def _():
        for cp in weight_copies(te_ref[i]):
            cp.wait()
        wgu_bf[...] = wgu_f32[...].astype(BF16)
        wd_bf[...] = wd_f32[...].astype(BF16)

        @pl.when(nx_ref[i] >= 0)
        def _():
            for cp in weight_copies(nx_ref[i]):
                cp.start(priority=1)

    def sub_tile(r0, n_rows, masked):
        rows = slice(r0, r0 + n_rows)
        lo, hi = _unpack_bf16_pair(xs_ref[rows, :])
        if masked:
            keep = lax.broadcasted_iota(I32, (n_rows, half), 0) < valid - r0
            lo = jnp.where(keep, lo, 0.0)
            hi = jnp.where(keep, hi, 0.0)
        xt = jnp.concatenate([lo.astype(BF16), hi.astype(BF16)], axis=1)
        d = functools.partial(jnp.dot, preferred_element_type=F32)
        nb = 256

        def up(c):
            gate = d(xt, wgu_bf[:, c:c + nb]) + bgu_ref[:, c:c + nb]
            lin = d(xt, wgu_bf[:, d_exp + c:d_exp + c + nb]) + bgu_ref[:, d_exp + c:d_exp + c + nb]
            return gate, lin

        def activate(gate, lin):
            gate = jnp.minimum(gate, SWIGLU_LIMIT)
            lin = jnp.clip(lin, -SWIGLU_LIMIT, SWIGLU_LIMIT)
            return (gate * jax.nn.sigmoid(SWIGLU_ALPHA * gate) * (lin + 1.0)).astype(BF16)

        chunks = list(range(0, d_exp, nb))
        acts = []
        pending = up(chunks[0])
        for n in range(len(chunks)):
            nxt = up(chunks[n + 1]) if n + 1 < len(chunks) else None
            acts.append(activate(*pending))
            pending = nxt
        out = d(jnp.concatenate(acts, axis=1), wd_bf[...]) + bd_ref[...]
        y_ref[rows, :] = _pack_bf16_pair(out[:, :half], out[:, half:])

    pl.when(valid == tm)(functools.partial(sub_tile, 0, tm, False))
    for r0 in range(0, tm, FFN_SUB_ROWS):
        pl.when((valid > r0) & (valid < tm))(functools.partial(sub_tile, r0, FFN_SUB_ROWS, True))


def _ffn_call(tile_expert, tile_valid, tile_next, xs, w_gu, b_gu, w_down, b_down, tm):
    R, half = xs.shape
    E, D, two_f = w_gu.shape
    d_exp = w_down.shape[1]
    grid_spec = pltpu.PrefetchScalarGridSpec(
        num_scalar_prefetch=3,
        grid=(R // tm,),
        in_specs=[
            pl.BlockSpec((tm, half), lambda i, te, tv, nx: (i, 0)),
            pl.BlockSpec(memory_space=pl.ANY),
            pl.BlockSpec((None, 1, two_f), lambda i, te, tv, nx: (te[i], 0, 0)),
            pl.BlockSpec(memory_space=pl.ANY),
            pl.BlockSpec((None, 1, D), lambda i, te, tv, nx: (te[i], 0, 0)),
        ],
        out_specs=pl.BlockSpec((tm, half), lambda i, te, tv, nx: (i, 0)),
        scratch_shapes=[
            pltpu.VMEM((D, two_f), F32), pltpu.VMEM((d_exp, D), F32),
            pltpu.VMEM((D, two_f), BF16), pltpu.VMEM((d_exp, D), BF16),
            pltpu.SemaphoreType.DMA((2,)),
        ],
    )
    return pl.pallas_call(
        _ffn_kernel,
        grid_spec=grid_spec,
        out_shape=jax.ShapeDtypeStruct((R, half), U32),
        compiler_params=pltpu.CompilerParams(
            dimension_semantics=("arbitrary",), vmem_limit_bytes=VMEM_LIMIT_BYTES),
        name="moe_ffn",
    )(tile_expert, tile_valid, tile_next, xs, w_gu, b_gu.reshape(E, 1, two_f), w_down,
      b_down.reshape(E, 1, D))


def _final_kernel(x_ref, yp_ref, wts_ref, mod_ref, gfin_ref, *rest):
    o_ref = rest[-1]
    x = x_ref[...]
    half = x.shape[1] // 2
    w = wts_ref[...]
    lo = jnp.zeros((x.shape[0], half), F32)
    hi = jnp.zeros((x.shape[0], half), F32)
    for k in range(TOP_K):
        l, h = _unpack_bf16_pair(yp_ref[k])
        lo = lo + w[:, k:k + 1] * l
        hi = hi + w[:, k:k + 1] * h
    gate = mod_ref[5:6, :]
    x_lo = x[:, :half] + gate[:, :half] * lo
    x_hi = x[:, half:] + gate[:, half:] * hi
    ms = (jnp.sum(x_lo * x_lo, axis=-1, keepdims=True) + jnp.sum(x_hi * x_hi, axis=-1, keepdims=True)) / x.shape[1]
    inv = lax.rsqrt(ms + RMS_EPS)
    o_ref[:, :half] = x_lo * inv * gfin_ref[:, :half]
    o_ref[:, half:] = x_hi * inv * gfin_ref[:, half:]


def _final_call(x1, yp, wts, mod, g_final, tokens_per_batch, group, n_groups, prev_out):
    T, D = x1.shape
    T_all = T * n_groups
    tq = min(512, tokens_per_batch)
    per_b = tokens_per_batch // tq
    first = group * (T // tq)
    in_specs = [
        pl.BlockSpec((tq, D), lambda i: (i, 0)),
        pl.BlockSpec((TOP_K, tq, D // 2), lambda i: (0, i, 0)),
        pl.BlockSpec((tq, ROUTE_LANES), lambda i: (i, 0)),
        pl.BlockSpec((None, N_MOD, D), lambda i: ((first + i) // per_b, 0, 0)),
        pl.BlockSpec((1, D), lambda i: (0, 0)),
    ]
    args = [x1, yp, wts, mod, g_final.reshape(1, D)]
    aliases = {}
    if prev_out is not None:
        in_specs.append(pl.BlockSpec(memory_space=pl.ANY))
        args.append(prev_out)
        aliases = {len(args) - 1: 0}
    return pl.pallas_call(
        _final_kernel,
        grid=(T // tq,),
        in_specs=in_specs,
        out_specs=pl.BlockSpec((tq, D), lambda i: (first + i, 0)),
        out_shape=jax.ShapeDtypeStruct((T_all, D), F32),
        input_output_aliases=aliases,
        compiler_params=pltpu.CompilerParams(
            dimension_semantics=("arbitrary",), vmem_limit_bytes=VMEM_LIMIT_BYTES),
        name="moe_combine_final",
    )(*args)


def _group_layout(counts, n_tiles, tm):
    padded = ((counts + tm - 1) // tm) * tm
    ends = jnp.cumsum(padded)
    starts = ends - padded
    tile_row = jnp.arange(n_tiles, dtype=I32) * tm
    te = jnp.minimum(jnp.sum(tile_row[:, None] >= ends[None, :], axis=1), N_EXPERTS - 1).astype(I32)
    eids = jnp.arange(N_EXPERTS, dtype=I32)
    mine = te[:, None] == eids[None, :]
    lookup = lambda table: jnp.sum(jnp.where(mine, table[None, :], 0), axis=1)
    tv = jnp.clip(lookup(counts) - (tile_row - lookup(starts)), 0, tm).astype(I32)
    later = (eids[None, :] > eids[:, None]) & (counts[None, :] > 0)
    nxt = jnp.min(jnp.where(later, eids[None, :], N_EXPERTS), axis=1)
    nx = lookup(jnp.where(nxt < N_EXPERTS, nxt, -1)).astype(I32)
    return starts, te, tv, nx


def kernel(x, c, positions, w_ada, b_ada, g_mix, w_in, w_dw, b_dw, g_conv_ln, b_conv_ln, g_ret_norm,
           w_out, g_ffn, w_router, b_router, w_gu, b_gu, w_down, b_down, g_final):
    B, S, D = x.shape
    T = B * S
    assert w_ada.shape[0] == 1, "single-layer block: the final norm directly follows layer 0"
    xt = x
    for l in range(1):
        mod = _mod_call(c, w_ada[l], b_ada[l]).reshape(B, N_MOD, D)
        n_groups = MOE_TOKEN_GROUPS if B % MOE_TOKEN_GROUPS == 0 else 1
        Bg = B // n_groups
        Tg = Bg * S
        tm = FFN_TILE_ROWS if Tg * TOP_K >= FFN_TILE_ROWS * N_EXPERTS * 4 else FFN_SUB_ROWS
        n_tiles = (Tg * TOP_K) // tm + N_EXPERTS
        x1s, scattered = [], []
        pos_km = jnp.zeros((TOP_K * Tg,), I32)
        for g in range(n_groups):
            x1 = _mixer_call(xt, positions, mod, g_mix[l], w_in[l], w_dw[l], b_dw[l], g_conv_ln[l],
                             b_conv_ln[l], g_ret_norm[l], w_out[l], g, n_groups, pos_km).reshape(Tg, D)
            h2p, idx, wts, rank, counts = _route_call(x1, mod, g_ffn[l], w_router[l], b_router[l], S, g * Bg)
            starts, te, tv, nx = _group_layout(counts[:, 0], n_tiles, tm)
            pos_km = rank[:TOP_K]
            for e in range(N_EXPERTS):
                pos_km = pos_km + jnp.where(idx[:TOP_K] == e, starts[e], 0)
            pos_km = pos_km.astype(I32).reshape(-1)
            x1s.append(x1)
            scattered.append((_sc_scatter_call(h2p, pos_km, n_tiles * tm), pos_km, wts, te, tv, nx))
        out = None
        for g, (xs, pos_km, wts, te, tv, nx) in enumerate(scattered):
            ys = _ffn_call(te, tv, nx, xs, w_gu[l], b_gu[l], w_down[l], b_down[l], tm)
            yp = _sc_gather_call(ys, pos_km)
            out = _final_call(x1s[g], yp.reshape(TOP_K, Tg, D // 2), wts, mod, g_final, S, g, n_groups, out)
        xt = out
    return xt.reshape(B, S, D)
```

```python
import functools

import numpy as np
import jax
import jax.numpy as jnp
from jax import lax
from jax.experimental import pallas as pl
from jax.experimental.pallas import tpu as pltpu
from jax.experimental.pallas import tpu_sc as plsc

F32 = jnp.float32
BF16 = jnp.bfloat16
U32 = jnp.uint32
I32 = jnp.int32

CONV_WIDTH = 31
CONV_HALO = 32
RET_HEADS = 4
RET_DIM = 128
RET_CHUNK = 128
ROPE_BASE = 10000.0
N_EXPERTS = 32
TOP_K = 4
SWIGLU_LIMIT = 7.0
SWIGLU_ALPHA = 1.702
RMS_EPS = 1e-6
LN_EPS = 1e-5
N_MOD = 6

VMEM_LIMIT_BYTES = 56 * 1024 * 1024
MOE_TOKEN_GROUPS = 2
FFN_TILE_ROWS = 1024
FFN_SUB_ROWS = 512


def _split_bf16(a):
    hi = a.astype(BF16)
    lo = (a - hi.astype(F32)).astype(BF16)
    return hi, lo


def _dot3(a, b_hi, b_lo):
    a_hi, a_lo = _split_bf16(a)
    d = functools.partial(jnp.dot, preferred_element_type=F32)
    return d(a_hi, b_hi) + (d(a_hi, b_lo) + d(a_lo, b_hi))


def _pack_bf16_pair(lo, hi):
    lo_bits = lax.bitcast_convert_type(lo.astype(BF16).astype(F32), U32)
    hi_bits = lax.bitcast_convert_type(hi.astype(BF16).astype(F32), U32)
    return (lo_bits >> 16) | (hi_bits & jnp.uint32(0xFFFF0000))


def _unpack_bf16_pair(p):
    lo = lax.bitcast_convert_type(p << 16, F32)
    hi = lax.bitcast_convert_type(p & jnp.uint32(0xFFFF0000), F32)
    return lo, hi


def _mod_kernel(c_ref, whi_ref, wlo_ref, b_ref, o_ref):
    c = c_ref[...]
    c_act = c * jax.nn.sigmoid(c)
    o_ref[...] = _dot3(c_act, whi_ref[...], wlo_ref[...]) + b_ref[...]


def _mod_call(c, w_ada, b_ada):
    B, D = c.shape
    n = w_ada.shape[1]
    bn = 1024
    w_hi, w_lo = _split_bf16(w_ada)
    return pl.pallas_call(
        _mod_kernel,
        grid=(n // bn,),
        in_specs=[
            pl.BlockSpec((B, D), lambda j: (0, 0)),
            pl.BlockSpec((D, bn), lambda j: (0, j)),
            pl.BlockSpec((D, bn), lambda j: (0, j)),
            pl.BlockSpec((1, bn), lambda j: (0, j)),
        ],
        out_specs=pl.BlockSpec((B, bn), lambda j: (0, j)),
        out_shape=jax.ShapeDtypeStruct((B, n), F32),
        name="adaln_mod",
    )(c, w_hi, w_lo, b_ada.reshape(1, n))


def _retention_tables():
    h = np.arange(RET_HEADS, dtype=np.float32)
    log_gamma = np.log(1.0 - np.power(2.0, -5.0 - h)).astype(np.float32)
    idx = np.arange(RET_CHUNK, dtype=np.float32)
    diff = idx[:, None] - idx[None, :]
    causal = diff >= 0
    mask = np.where(causal[None], np.exp(log_gamma[:, None, None] * np.where(causal, diff, 0.0)[None]), 0.0)
    q_decay = np.exp(log_gamma[:, None] * (idx + 1.0))[..., None]
    k_decay = np.exp(log_gamma[:, None] * (RET_CHUNK - 1.0 - idx))[..., None]
    chunk_decay = np.exp(log_gamma * RET_CHUNK)
    return (mask.astype(np.float32), q_decay.astype(np.float32), k_decay.astype(np.float32),
            [float(v) for v in chunk_decay.astype(np.float32)])


def _mixer_kernel(chunk_decay, ts, x_ref, pos_ref, mod_ref, gmix_ref, win_ref, wdw_ref, bdw_ref,
                  gcl_ref, bcl_ref, gret_ref, wout_ref, rope_ref, dmask_ref, qdec_ref, kdec_ref,
                  gffn_ref, wr2_ref, wrhi_ref, br_ref, utri_ref, after_ref,
                  o_ref, h2p_ref, idx_ref, wts_ref, rank_ref, cnt_ref,
                  proj_ref, uext_ref, state_ref, cat_ref, conv_ref, carry_ref):
    del after_ref
    s = pl.program_id(1)
    conv_ch = wdw_ref.shape[1]
    ret_w = RET_HEADS * RET_DIM

    @pl.when(s == 0)
    def _():
        uext_ref[0:CONV_HALO, :] = jnp.zeros((CONV_HALO, conv_ch), F32)
        state_ref[...] = jnp.zeros_like(state_ref)

    x = x_ref[...]
    row_gain = gmix_ref[...] * (1.0 + mod_ref[1:2, :])
    h = (x * lax.rsqrt(jnp.mean(x * x, axis=-1, keepdims=True) + RMS_EPS) * row_gain + mod_ref[0:1, :]).astype(BF16)
    proj_ref[...] = jnp.dot(h, win_ref[...], preferred_element_type=F32)

    a = proj_ref[:, 0:conv_ch]
    b = proj_ref[:, conv_ch:2 * conv_ch]
    uext_ref[CONV_HALO:CONV_HALO + ts, :] = a * jax.nn.sigmoid(b)
    cb = 64
    lead = CONV_HALO - (CONV_WIDTH - 1)
    span = cb + CONV_HALO

    def conv_block(r0):
        for c0 in range(0, conv_ch, 128):
            xw = uext_ref[r0:r0 + span, c0:c0 + 128]
            acc = jnp.zeros((cb, 128), F32) + bdw_ref[:, c0:c0 + 128]
            for r in range(8):
                xr = xw if r == 0 else pltpu.roll(xw, span - r, 0)
                for q in range((lead + CONV_WIDTH - 1 - r) // 8 + 1):
                    j = 8 * q + r - lead
                    if 0 <= j < CONV_WIDTH:
                        acc = acc + wdw_ref[j:j + 1, c0:c0 + 128] * xr[8 * q:8 * q + cb]
            conv_ref[r0:r0 + cb, c0:c0 + 128] = acc
    rb = 2 * cb

    def conv_norm(r0):
        acc = conv_ref[r0:r0 + rb, 0:conv_ch]
        mu = jnp.mean(acc, axis=-1, keepdims=True)
        d = acc - mu
        var = jnp.mean(d * d, axis=-1, keepdims=True)
        ln = d * lax.rsqrt(var + LN_EPS) * gcl_ref[...] + bcl_ref[...]
        cat_ref[r0:r0 + rb, 0:conv_ch] = (ln * jax.nn.sigmoid(ln)).astype(BF16)

    hts = ts // 2
    low = lax.broadcasted_iota(I32, (hts, RET_DIM), 1) < RET_DIM // 2
    posf = jnp.where(low, pos_ref[0:hts, :].astype(F32), pos_ref[hts:ts, :].astype(F32))
    ang = posf * rope_ref[...]
    cos_p = jnp.cos(ang)
    sin_p = jnp.sin(ang)
    cos_s = pltpu.roll(cos_p, RET_DIM // 2, 1)
    sin_s = pltpu.roll(sin_p, RET_DIM // 2, 1)
    cos2 = jnp.concatenate([jnp.where(low, cos_p, cos_s), jnp.where(low, cos_s, cos_p)], axis=0)
    sin2 = jnp.concatenate([jnp.where(low, -sin_p, sin_s), jnp.where(low, -sin_s, sin_p)], axis=0)
    q0 = 2 * conv_ch
    k0 = q0 + ret_w
    v0 = k0 + ret_w
    g0 = v0 + ret_w

    def retention_unit(hd, n):
        c0 = hd * RET_DIM
        r0 = n * RET_CHUNK
        rows = slice(r0, r0 + RET_CHUNK)
        cs = cos2[rows]
        sn = sin2[rows]
        q = proj_ref[rows, q0 + c0:q0 + c0 + RET_DIM]
        k = proj_ref[rows, k0 + c0:k0 + c0 + RET_DIM]
        v = proj_ref[rows, v0 + c0:v0 + c0 + RET_DIM].astype(BF16)
        g = proj_ref[rows, g0 + c0:g0 + c0 + RET_DIM]
        qr = q * cs + pltpu.roll(q, RET_DIM // 2, 1) * sn
        kr = k * cs + pltpu.roll(k, RET_DIM // 2, 1) * sn
        st = state_ref[hd]
        scores = lax.dot_general(qr.astype(BF16), kr.astype(BF16), (((1,), (1,)), ((), ())),
                                 preferred_element_type=F32) * dmask_ref[hd]
        inner = jnp.dot(scores.astype(BF16), v, preferred_element_type=F32)
        cross = jnp.dot((qr * qdec_ref[hd]).astype(BF16), st.astype(BF16), preferred_element_type=F32)
        kv = lax.dot_general((kr * kdec_ref[hd]).astype(BF16), v, (((0,), (0,)), ((), ())),
                             preferred_element_type=F32)
        state_ref[hd] = chunk_decay[hd] * st + kv
        r = inner + cross
        mu = jnp.mean(r, axis=-1, keepdims=True)
        d = r - mu
        var = jnp.mean(d * d, axis=-1, keepdims=True)
        rn = d * lax.rsqrt(var + LN_EPS) * gret_ref[:, c0:c0 + RET_DIM]
        cat_ref[rows, conv_ch + c0:conv_ch + c0 + RET_DIM] = (g * jax.nn.sigmoid(g) * rn).astype(BF16)

    units = [(hd, n) for n in range(ts // RET_CHUNK) for hd in range(RET_HEADS)]
    n_conv = ts // cb
    per = -(-len(units) // n_conv)
    for i in range(n_conv):
        conv_block(i * cb)
        if i % 2 == 1:
            conv_norm((i - 1) * cb)
        for hd, n in units[i * per:(i + 1) * per]:
            retention_unit(hd, n)
    uext_ref[0:CONV_HALO, :] = uext_ref[ts:ts + CONV_HALO, :]

    out = jnp.dot(cat_ref[...], wout_ref[...], preferred_element_type=F32)
    x1 = x + mod_ref[2:3, :] * out
    o_ref[...] = x1

    @pl.when((pl.program_id(0) == 0) & (s == 0))
    def _():
        carry_ref[...] = jnp.zeros_like(carry_ref)

    _route_tile(x1, mod_ref, gffn_ref, wr2_ref, wrhi_ref, br_ref, utri_ref,
                h2p_ref, idx_ref, wts_ref, rank_ref, cnt_ref, carry_ref)


def _mixer_call(x, positions, mod, g_mix, w_in, w_dw, b_dw, g_conv_ln, b_conv_ln, g_ret_norm, w_out,
                g_ffn, w_router, b_router, group, n_groups, after):
    S, D = x.shape[1:]
    B = x.shape[0] // n_groups
    b0 = group * B
    in_cols = w_in.shape[1]
    conv_ch = w_dw.shape[1]
    ts = min(512, S)
    mask, q_decay, k_decay, chunk_decay = _retention_tables()
    key_scale = np.float32(RET_DIM ** -0.5)
    half = RET_DIM // 2
    inv_freq = (ROPE_BASE ** (-np.arange(half, dtype=np.float32) / half)).astype(np.float32)
    rope = np.concatenate([inv_freq, inv_freq])[None, :]
    w_dw_p = jnp.zeros((CONV_HALO, conv_ch), F32).at[:CONV_WIDTH].set(w_dw)
    wr_hi, wr_lo = _split_bf16(w_router.T)
    utri = jnp.asarray(np.triu(np.ones((ts, ts), np.float32), 1), BF16)
    n_s = S // ts
    T = B * S
    full = lambda shape: pl.BlockSpec(shape, lambda b, s: (0,) * len(shape))
    return pl.pallas_call(
        functools.partial(_mixer_kernel, chunk_decay, ts),
        grid=(B, S // ts),
        in_specs=[
            pl.BlockSpec((None, ts, D), lambda b, s: (b0 + b, s, 0)),
            pl.BlockSpec((None, ts, 1), lambda b, s: (b0 + b, s, 0)),
            pl.BlockSpec((None, N_MOD, D), lambda b, s: (b0 + b, 0, 0)),
            full((1, D)),
            full((D, in_cols)),
            full((CONV_HALO, conv_ch)),
            full((1, conv_ch)),
            full((1, conv_ch)),
            full((1, conv_ch)),
            full((1, RET_HEADS * RET_DIM)),
            full((conv_ch + RET_HEADS * RET_DIM, D)),
            full((1, RET_DIM)),
            full((RET_HEADS, RET_CHUNK, RET_CHUNK)),
            full((RET_HEADS, RET_CHUNK, 1)),
            full((RET_HEADS, RET_CHUNK, 1)),
            full((1, D)),
            full((2 * N_EXPERTS, D)),
            full((N_EXPERTS, D)),
            full((N_EXPERTS, 1)),
            full((ts, ts)),
            pl.BlockSpec(memory_space=pl.ANY),
        ],
        out_specs=[
            pl.BlockSpec((None, ts, D), lambda b, s: (b, s, 0)),
            pl.BlockSpec((ts, D // 2), lambda b, s: (b * n_s + s, 0)),
            pl.BlockSpec((ROUTE_ROWS, ts), lambda b, s: (0, b * n_s + s)),
            pl.BlockSpec((ts, ROUTE_LANES), lambda b, s: (b * n_s + s, 0)),
            pl.BlockSpec((ROUTE_ROWS, ts), lambda b, s: (0, b * n_s + s)),
            full((N_EXPERTS, 1)),
        ],
        out_shape=[
            jax.ShapeDtypeStruct((B, S, D), F32),
            jax.ShapeDtypeStruct((T, D // 2), U32),
            jax.ShapeDtypeStruct((ROUTE_ROWS, T), I32),
            jax.ShapeDtypeStruct((T, ROUTE_LANES), F32),
            jax.ShapeDtypeStruct((ROUTE_ROWS, T), I32),
            jax.ShapeDtypeStruct((N_EXPERTS, 1), I32),
        ],
        scratch_shapes=[
            pltpu.VMEM((ts, in_cols), F32),
            pltpu.VMEM((CONV_HALO + ts, conv_ch), F32),
            pltpu.VMEM((RET_HEADS, RET_DIM, RET_DIM), F32),
            pltpu.VMEM((ts, conv_ch + RET_HEADS * RET_DIM), BF16),
            pltpu.VMEM((ts, conv_ch), F32),
            pltpu.VMEM((N_EXPERTS, 1), F32),
        ],
        compiler_params=pltpu.CompilerParams(
            dimension_semantics=("arbitrary", "arbitrary"), vmem_limit_bytes=VMEM_LIMIT_BYTES),
        name="hybrid_mixer",
    )(x, positions.reshape(-1, S, 1), mod, g_mix.reshape(1, D), w_in.astype(BF16), w_dw_p,
      b_dw.reshape(1, -1), g_conv_ln.reshape(1, -1), b_conv_ln.reshape(1, -1), g_ret_norm.reshape(1, -1),
      w_out.astype(BF16), jnp.asarray(rope), jnp.asarray(mask * key_scale), jnp.asarray(q_decay),
      jnp.asarray(k_decay * key_scale), g_ffn.reshape(1, D), jnp.concatenate([wr_hi, wr_lo], axis=0), wr_hi,
      b_router.reshape(N_EXPERTS, 1), utri, after)


def _route_tile(x, mod_ref, gffn_ref, wr2_ref, wrhi_ref, br_ref, utri_ref,
                h2p_ref, idx_ref, wts_ref, rank_ref, cnt_ref, carry_ref):
    tr = x.shape[0]
    half = x.shape[1] // 2
    y = x * lax.rsqrt(jnp.mean(x * x, axis=-1, keepdims=True) + RMS_EPS) * gffn_ref[...]
    h2 = y * (1.0 + mod_ref[4:5, :]) + mod_ref[3:4, :]
    h2p_ref[...] = _pack_bf16_pair(h2[:, :half], h2[:, half:])

    h_hi, h_lo = _split_bf16(h2)
    nt = (((1,), (1,)), ((), ()))
    r = lax.dot_general(wr2_ref[...], h_hi, nt, preferred_element_type=F32)
    r2 = lax.dot_general(wrhi_ref[...], h_lo, nt, preferred_element_type=F32)
    l = r[:N_EXPERTS] + (r[N_EXPERTS:] + r2) + br_ref[...]
    eid = lax.broadcasted_iota(I32, l.shape, 0)
    vals, sels, idxs = [], [], []
    for _ in range(TOP_K):
        m = jnp.max(l, axis=0, keepdims=True)
        ik = jnp.min(jnp.where(l == m, eid, N_EXPERTS), axis=0, keepdims=True)
        sel = eid == ik
        vals.append(m)
        sels.append(sel)
        idxs.append(ik)
        l = jnp.where(sel, -jnp.inf, l)
    exps = [jnp.exp(v - vals[0]) for v in vals]
    denom = exps[0] + exps[1] + exps[2] + exps[3]
    member = jnp.zeros(l.shape, F32)
    for sel in sels:
        member = member + sel.astype(F32)
    before = jnp.dot(member.astype(BF16), utri_ref[...], preferred_element_type=F32) + carry_ref[...]
    ranks = [jnp.sum(jnp.where(sel, before, 0.0), axis=0, keepdims=True) for sel in sels]
    carry_ref[...] = carry_ref[...] + jnp.sum(member, axis=1, keepdims=True)
    cnt_ref[...] = carry_ref[...].astype(I32)

    def rows(pieces, n_rows, dtype):
        rid = lax.broadcasted_iota(I32, (n_rows, tr), 0)
        out = jnp.zeros((n_rows, tr), dtype)
        for k, p in enumerate(pieces):
            out = jnp.where(rid == k, p.astype(dtype), out)
        return out

    idx_ref[...] = rows(idxs, idx_ref.shape[0], I32)
    rank_ref[...] = rows(ranks, rank_ref.shape[0], I32)
    wts_ref[...] = rows([e / denom for e in exps], wts_ref.shape[1], F32).T


ROUTE_ROWS = 8
ROUTE_LANES = 128


SC_ROWS = 128


V7X_SC_CORES = 2
V7X_SC_SUBCORES = 16


def _sc_workers():
    return V7X_SC_CORES, V7X_SC_SUBCORES


def _sc_mesh():
    return plsc.VectorSubcoreMesh(core_axis_name="c", subcore_axis_name="s",
                                  num_cores=V7X_SC_CORES, num_subcores=V7X_SC_SUBCORES)


def _sc_scatter_call(h2p, pos_km, n_rows):
    T, W = h2p.shape
    nc, ns = _sc_workers()
    n = SC_ROWS
    per_w = T // (nc * ns * n)

    def body(h2p_hbm, pos_hbm, xs_hbm, i0, i1, i2, i3, rows_v, sem):
        wid = lax.axis_index("s") * nc + lax.axis_index("c")
        idx_refs = (i0, i1, i2, i3)

        @pl.loop(0, per_w)
        def _(j):
            t0 = (wid * per_w + j) * n
            pltpu.sync_copy(h2p_hbm.at[pl.ds(t0, n)], rows_v)
            for k in range(TOP_K):
                pltpu.sync_copy(pos_hbm.at[pl.ds(k * T + t0, n)], idx_refs[k])
            copies = [pltpu.async_copy(rows_v, xs_hbm.at[idx_refs[k]], sem) for k in range(TOP_K)]
            for cp in copies:
                cp.wait()

    return pl.kernel(
        body,
        out_type=jax.ShapeDtypeStruct((n_rows, W), h2p.dtype),
        mesh=_sc_mesh(),
        scratch_types=[pltpu.VMEM((n,), I32)] * TOP_K + [pltpu.VMEM((n, W), h2p.dtype), pltpu.SemaphoreType.DMA],
        name="moe_scatter_rows",
    )(h2p, pos_km)


def _sc_gather_call(ys, pos_km):
    P = pos_km.shape[0]
    W = ys.shape[1]
    nc, ns = _sc_workers()
    n = SC_ROWS
    per_w = P // (nc * ns * n)

    def body(ys_hbm, pos_hbm, yp_hbm, idx_v, rows_v, sem):
        wid = lax.axis_index("s") * nc + lax.axis_index("c")

        @pl.loop(0, per_w)
        def _(j):
            p0 = (wid * per_w + j) * n
            pltpu.sync_copy(pos_hbm.at[pl.ds(p0, n)], idx_v)
            pltpu.async_copy(ys_hbm.at[idx_v], rows_v, sem).wait()
            pltpu.sync_copy(rows_v, yp_hbm.at[pl.ds(p0, n)])

    return pl.kernel(
        body,
        out_type=jax.ShapeDtypeStruct((P, W), ys.dtype),
        mesh=_sc_mesh(),
        scratch_types=[pltpu.VMEM((n,), I32), pltpu.VMEM((n, W), ys.dtype), pltpu.SemaphoreType.DMA],
        name="moe_gather_rows",
    )(ys, pos_km)


def _ffn_kernel(te_ref, tv_ref, nx_ref, xs_ref, wgu_hbm, bgu_ref, wd_hbm, bd_ref, y_ref,
                wgu_f32, wd_f32, wgu_bf, wd_bf, sems):
    i = pl.program_id(0)
    valid = tv_ref[i]
    tm, half = xs_ref.shape
    d_exp = wd_bf.shape[0]

    def weight_copies(e):
        return (pltpu.make_async_copy(wgu_hbm.at[e], wgu_f32, sems.at[0]),
                pltpu.make_async_copy(wd_hbm.at[e], wd_f32, sems.at[1]))

    @pl.when(i == 0)
    def _():
        for cp in weight_copies(te_ref[0]):
            cp.start(priority=1)

    @pl.when(((i == 0) | (te_ref[i] != te_ref[jnp.maximum(i - 1, 0)])) & (valid > 0))
    def _():
        for cp in weight_copies(te_ref[i]):
            cp.wait()
        wgu_bf[...] = wgu_f32[...].astype(BF16)
        wd_bf[...] = wd_f32[...].astype(BF16)

        @pl.when(nx_ref[i] >= 0)
        def _():
            for cp in weight_copies(nx_ref[i]):
                cp.start(priority=1)

    def sub_tile(r0, n_rows, masked):
        rows = slice(r0, r0 + n_rows)
        lo, hi = _unpack_bf16_pair(xs_ref[rows, :])
        if masked:
            keep = lax.broadcasted_iota(I32, (n_rows, half), 0) < valid - r0
            lo = jnp.where(keep, lo, 0.0)
            hi = jnp.where(keep, hi, 0.0)
        xt = jnp.concatenate([lo.astype(BF16), hi.astype(BF16)], axis=1)
        d = functools.partial(jnp.dot, preferred_element_type=F32)
        nb = 256

        def up(c):
            gate = d(xt, wgu_bf[:, c:c + nb]) + bgu_ref[:, c:c + nb]
            lin = d(xt, wgu_bf[:, d_exp + c:d_exp + c + nb]) + bgu_ref[:, d_exp + c:d_exp + c + nb]
            return gate, lin

        def activate(gate, lin):
            gate = jnp.minimum(gate, SWIGLU_LIMIT)
            lin = jnp.clip(lin, -SWIGLU_LIMIT, SWIGLU_LIMIT)
            return (gate * jax.nn.sigmoid(SWIGLU_ALPHA * gate) * (lin + 1.0)).astype(BF16)

        chunks = list(range(0, d_exp, nb))
        acts = []
        pending = up(chunks[0])
        for n in range(len(chunks)):
            nxt = up(chunks[n + 1]) if n + 1 < len(chunks) else None
            acts.append(activate(*pending))
            pending = nxt
        out = d(jnp.concatenate(acts, axis=1), wd_bf[...]) + bd_ref[...]
        y_ref[rows, :] = _pack_bf16_pair(out[:, :half], out[:, half:])

    pl.when(valid == tm)(functools.partial(sub_tile, 0, tm, False))
    for r0 in range(0, tm, FFN_SUB_ROWS):
        pl.when((valid > r0) & (valid < tm))(functools.partial(sub_tile, r0, FFN_SUB_ROWS, True))


def _ffn_call(tile_expert, tile_valid, tile_next, xs, w_gu, b_gu, w_down, b_down, tm):
    R, half = xs.shape
    E, D, two_f = w_gu.shape
    d_exp = w_down.shape[1]
    grid_spec = pltpu.PrefetchScalarGridSpec(
        num_scalar_prefetch=3,
        grid=(R // tm,),
        in_specs=[
            pl.BlockSpec((tm, half), lambda i, te, tv, nx: (i, 0)),
            pl.BlockSpec(memory_space=pl.ANY),
            pl.BlockSpec((None, 1, two_f), lambda i, te, tv, nx: (te[i], 0, 0)),
            pl.BlockSpec(memory_space=pl.ANY),
            pl.BlockSpec((None, 1, D), lambda i, te, tv, nx: (te[i], 0, 0)),
        ],
        out_specs=pl.BlockSpec((tm, half), lambda i, te, tv, nx: (i, 0)),
        scratch_shapes=[
            pltpu.VMEM((D, two_f), F32), pltpu.VMEM((d_exp, D), F32),
            pltpu.VMEM((D, two_f), BF16), pltpu.VMEM((d_exp, D), BF16),
            pltpu.SemaphoreType.DMA((2,)),
        ],
    )
    return pl.pallas_call(
        _ffn_kernel,
        grid_spec=grid_spec,
        out_shape=jax.ShapeDtypeStruct((R, half), U32),
        compiler_params=pltpu.CompilerParams(
            dimension_semantics=("arbitrary",), vmem_limit_bytes=VMEM_LIMIT_BYTES),
        name="moe_ffn",
    )(tile_expert, tile_valid, tile_next, xs, w_gu, b_gu.reshape(E, 1, two_f), w_down,
      b_down.reshape(E, 1, D))


def _final_kernel(x_ref, yp_ref, wts_ref, mod_ref, gfin_ref, *rest):
    o_ref = rest[-1]
    x = x_ref[...]
    half = x.shape[1] // 2
    w = wts_ref[...]
    lo = jnp.zeros((x.shape[0], half), F32)
    hi = jnp.zeros((x.shape[0], half), F32)
    for k in range(TOP_K):
        l, h = _unpack_bf16_pair(yp_ref[k])
        lo = lo + w[:, k:k + 1] * l
        hi = hi + w[:, k:k + 1] * h
    gate = mod_ref[5:6, :]
    x_lo = x[:, :half] + gate[:, :half] * lo
    x_hi = x[:, half:] + gate[:, half:] * hi
    ms = (jnp.sum(x_lo * x_lo, axis=-1, keepdims=True) + jnp.sum(x_hi * x_hi, axis=-1, keepdims=True)) / x.shape[1]
    inv = lax.rsqrt(ms + RMS_EPS)
    o_ref[:, :half] = x_lo * inv * gfin_ref[:, :half]
    o_ref[:, half:] = x_hi * inv * gfin_ref[:, half:]


def _final_call(x1, yp, wts, mod, g_final, tokens_per_batch, group, n_groups, prev_out):
    T, D = x1.shape
    T_all = T * n_groups
    tq = min(512, tokens_per_batch)
    per_b = tokens_per_batch // tq
    first = group * (T // tq)
    in_specs = [
        pl.BlockSpec((tq, D), lambda i: (i, 0)),
        pl.BlockSpec((TOP_K, tq, D // 2), lambda i: (0, i, 0)),
        pl.BlockSpec((tq, ROUTE_LANES), lambda i: (i, 0)),
        pl.BlockSpec((None, N_MOD, D), lambda i: ((first + i) // per_b, 0, 0)),
        pl.BlockSpec((1, D), lambda i: (0, 0)),
    ]
    args = [x1, yp, wts, mod, g_final.reshape(1, D)]
    aliases = {}
    if prev_out is not None:
        in_specs.append(pl.BlockSpec(memory_space=pl.ANY))
        args.append(prev_out)
        aliases = {len(args) - 1: 0}
    return pl.pallas_call(
        _final_kernel,
        grid=(T // tq,),
        in_specs=in_specs,
        out_specs=pl.BlockSpec((tq, D), lambda i: (first + i, 0)),
        out_shape=jax.ShapeDtypeStruct((T_all, D), F32),
        input_output_aliases=aliases,
        compiler_params=pltpu.CompilerParams(
            dimension_semantics=("arbitrary",), vmem_limit_bytes=VMEM_LIMIT_BYTES),
        name="moe_combine_final",
    )(*args)


def _group_layout(counts, n_tiles, tm):
    padded = ((counts + tm - 1) // tm) * tm
    ends = jnp.cumsum(padded)
    starts = ends - padded
    tile_row = jnp.arange(n_tiles, dtype=I32) * tm
    te = jnp.minimum(jnp.sum(tile_row[:, None] >= ends[None, :], axis=1), N_EXPERTS - 1).astype(I32)
    eids = jnp.arange(N_EXPERTS, dtype=I32)
    mine = te[:, None] == eids[None, :]
    lookup = lambda table: jnp.sum(jnp.where(mine, table[None, :], 0), axis=1)
    tv = jnp.clip(lookup(counts) - (tile_row - lookup(starts)), 0, tm).astype(I32)
    later = (eids[None, :] > eids[:, None]) & (counts[None, :] > 0)
    nxt = jnp.min(jnp.where(later, eids[None, :], N_EXPERTS), axis=1)
    nx = lookup(jnp.where(nxt < N_EXPERTS, nxt, -1)).astype(I32)
    return starts, te, tv, nx


def kernel(x, c, positions, w_ada, b_ada, g_mix, w_in, w_dw, b_dw, g_conv_ln, b_conv_ln, g_ret_norm,
           w_out, g_ffn, w_router, b_router, w_gu, b_gu, w_down, b_down, g_final):
    B, S, D = x.shape
    T = B * S
    assert w_ada.shape[0] == 1, "single-layer block: the final norm directly follows layer 0"
    xt = x
    for l in range(1):
        mod = _mod_call(c, w_ada[l], b_ada[l]).reshape(B, N_MOD, D)
        n_groups = MOE_TOKEN_GROUPS if B % MOE_TOKEN_GROUPS == 0 else 1
        Bg = B // n_groups
        Tg = Bg * S
        tm = FFN_TILE_ROWS if Tg * TOP_K >= FFN_TILE_ROWS * N_EXPERTS * 4 else FFN_SUB_ROWS
        n_tiles = (Tg * TOP_K) // tm + N_EXPERTS
        x1s, scattered = [], []
        pos_km = jnp.zeros((TOP_K * Tg,), I32)
        for g in range(n_groups):
            x1, h2p, idx, wts, rank, counts = _mixer_call(
                xt, positions, mod, g_mix[l], w_in[l], w_dw[l], b_dw[l], g_conv_ln[l], b_conv_ln[l],
                g_ret_norm[l], w_out[l], g_ffn[l], w_router[l], b_router[l], g, n_groups, pos_km)
            x1 = x1.reshape(Tg, D)
            starts, te, tv, nx = _group_layout(counts[:, 0], n_tiles, tm)
            pos_km = rank[:TOP_K]
            for e in range(N_EXPERTS):
                pos_km = pos_km + jnp.where(idx[:TOP_K] == e, starts[e], 0)
            pos_km = pos_km.astype(I32).reshape(-1)
            x1s.append(x1)
            scattered.append((_sc_scatter_call(h2p, pos_km, n_tiles * tm), pos_km, wts, te, tv, nx))
        out = None
        for g, (xs, pos_km, wts, te, tv, nx) in enumerate(scattered):
            ys = _ffn_call(te, tv, nx, xs, w_gu[l], b_gu[l], w_down[l], b_down[l], tm)
            yp = _sc_gather_call(ys, pos_km)
            out = _final_call(x1s[g], yp.reshape(TOP_K, Tg, D // 2), wts, mod, g_final, S, g, n_groups, out)
        xt = out
    return xt.reshape(B, S, D)
```

```python
import functools

import numpy as np
import jax
import jax.numpy as jnp
from jax import lax
from jax.experimental import pallas as pl
from jax.experimental.pallas import tpu as pltpu
from jax.experimental.pallas import tpu_sc as plsc

F32 = jnp.float32
BF16 = jnp.bfloat16
U32 = jnp.uint32
I32 = jnp.int32

CONV_WIDTH = 31
CONV_HALO = 32
RET_HEADS = 4
RET_DIM = 128
RET_CHUNK = 128
ROPE_BASE = 10000.0
N_EXPERTS = 32
TOP_K = 4
SWIGLU_LIMIT = 7.0
SWIGLU_ALPHA = 1.702
RMS_EPS = 1e-6
LN_EPS = 1e-5
N_MOD = 6

VMEM_LIMIT_BYTES = 56 * 1024 * 1024
MOE_TOKEN_GROUPS = 2
FFN_TILE_ROWS = 1024
FFN_SUB_ROWS = 512


def _split_bf16(a):
    hi = a.astype(BF16)
    lo = (a - hi.astype(F32)).astype(BF16)
    return hi, lo


def _dot3(a, b_hi, b_lo):
    a_hi, a_lo = _split_bf16(a)
    d = functools.partial(jnp.dot, preferred_element_type=F32)
    return d(a_hi, b_hi) + (d(a_hi, b_lo) + d(a_lo, b_hi))


def _pack_bf16_pair(lo, hi):
    lo_bits = lax.bitcast_convert_type(lo.astype(BF16).astype(F32), U32)
    hi_bits = lax.bitcast_convert_type(hi.astype(BF16).astype(F32), U32)
    return (lo_bits >> 16) | (hi_bits & jnp.uint32(0xFFFF0000))


def _unpack_bf16_pair(p):
    lo = lax.bitcast_convert_type(p << 16, F32)
    hi = lax.bitcast_convert_type(p & jnp.uint32(0xFFFF0000), F32)
    return lo, hi


def _mod_kernel(c_ref, whi_ref, wlo_ref, b_ref, o_ref):
    c = c_ref[...]
    c_act = c * jax.nn.sigmoid(c)
    o_ref[...] = _dot3(c_act, whi_ref[...], wlo_ref[...]) + b_ref[...]


def _mod_call(c, w_ada, b_ada):
    B, D = c.shape
    n = w_ada.shape[1]
    bn = 1024
    w_hi, w_lo = _split_bf16(w_ada)
    return pl.pallas_call(
        _mod_kernel,
        grid=(n // bn,),
        in_specs=[
            pl.BlockSpec((B, D), lambda j: (0, 0)),
            pl.BlockSpec((D, bn), lambda j: (0, j)),
            pl.BlockSpec((D, bn), lambda j: (0, j)),
            pl.BlockSpec((1, bn), lambda j: (0, j)),
        ],
        out_specs=pl.BlockSpec((B, bn), lambda j: (0, j)),
        out_shape=jax.ShapeDtypeStruct((B, n), F32),
        name="adaln_mod",
    )(c, w_hi, w_lo, b_ada.reshape(1, n))


def _retention_tables():
    h = np.arange(RET_HEADS, dtype=np.float32)
    log_gamma = np.log(1.0 - np.power(2.0, -5.0 - h)).astype(np.float32)
    idx = np.arange(RET_CHUNK, dtype=np.float32)
    diff = idx[:, None] - idx[None, :]
    causal = diff >= 0
    mask = np.where(causal[None], np.exp(log_gamma[:, None, None] * np.where(causal, diff, 0.0)[None]), 0.0)
    q_decay = np.exp(log_gamma[:, None] * (idx + 1.0))[..., None]
    k_decay = np.exp(log_gamma[:, None] * (RET_CHUNK - 1.0 - idx))[..., None]
    chunk_decay = np.exp(log_gamma * RET_CHUNK)
    return (mask.astype(np.float32), q_decay.astype(np.float32), k_decay.astype(np.float32),
            [float(v) for v in chunk_decay.astype(np.float32)])


def _mixer_kernel(chunk_decay, ts, x_ref, pos_ref, mod_ref, gmix_ref, win_ref, wdw_ref, bdw_ref,
                  gcl_ref, bcl_ref, gret_ref, wout_ref, rope_ref, dmask_ref, qdec_ref, kdec_ref,
                  gffn_ref, wr2_ref, wrhi_ref, br_ref, utri_ref, after_ref,
                  o_ref, h2p_ref, idx_ref, wts_ref, rank_ref, cnt_ref,
                  proj_ref, uext_ref, state_ref, cat_ref, conv_ref, carry_ref):
    del after_ref
    s = pl.program_id(1)
    conv_ch = wdw_ref.shape[1]
    ret_w = RET_HEADS * RET_DIM

    @pl.when(s == 0)
    def _():
        uext_ref[0:CONV_HALO, :] = jnp.zeros((CONV_HALO, conv_ch), F32)
        state_ref[...] = jnp.zeros_like(state_ref)

    x = x_ref[...]
    row_gain = gmix_ref[...] * (1.0 + mod_ref[1:2, :])
    h = (x * lax.rsqrt(jnp.mean(x * x, axis=-1, keepdims=True) + RMS_EPS) * row_gain + mod_ref[0:1, :]).astype(BF16)
    proj_ref[...] = jnp.dot(h, win_ref[...], preferred_element_type=F32)

    a = proj_ref[:, 0:conv_ch]
    b = proj_ref[:, conv_ch:2 * conv_ch]
    uext_ref[CONV_HALO:CONV_HALO + ts, :] = a * jax.nn.sigmoid(b)
    cb = 64
    lead = CONV_HALO - (CONV_WIDTH - 1)
    span = cb + CONV_HALO

    def conv_block(r0):
        for c0 in range(0, conv_ch, 128):
            xw = uext_ref[r0:r0 + span, c0:c0 + 128]
            acc = jnp.zeros((cb, 128), F32) + bdw_ref[:, c0:c0 + 128]
            for r in range(8):
                xr = xw if r == 0 else pltpu.roll(xw, span - r, 0)
                for q in range((lead + CONV_WIDTH - 1 - r) // 8 + 1):
                    j = 8 * q + r - lead
                    if 0 <= j < CONV_WIDTH:
                        acc = acc + wdw_ref[j:j + 1, c0:c0 + 128] * xr[8 * q:8 * q + cb]
            conv_ref[r0:r0 + cb, c0:c0 + 128] = acc
    rb = 2 * cb

    def conv_norm(r0):
        acc = conv_ref[r0:r0 + rb, 0:conv_ch]
        mu = jnp.mean(acc, axis=-1, keepdims=True)
        d = acc - mu
        var = jnp.mean(d * d, axis=-1, keepdims=True)
        ln = d * lax.rsqrt(var + LN_EPS) * gcl_ref[...] + bcl_ref[...]
        cat_ref[r0:r0 + rb, 0:conv_ch] = (ln * jax.nn.sigmoid(ln)).astype(BF16)

    hts = ts // 2
    low = lax.broadcasted_iota(I32, (hts, RET_DIM), 1) < RET_DIM // 2
    posf = jnp.where(low, pos_ref[0:hts, :].astype(F32), pos_ref[hts:ts, :].astype(F32))
    ang = posf * rope_ref[...]
    cos_p = jnp.cos(ang)
    sin_p = jnp.sin(ang)
    cos_s = pltpu.roll(cos_p, RET_DIM // 2, 1)
    sin_s = pltpu.roll(sin_p, RET_DIM // 2, 1)
    cos2 = jnp.concatenate([jnp.where(low, cos_p, cos_s), jnp.where(low, cos_s, cos_p)], axis=0)
    sin2 = jnp.concatenate([jnp.where(low, -sin_p, sin_s), jnp.where(low, -sin_s, sin_p)], axis=0)
    q0 = 2 * conv_ch
    k0 = q0 + ret_w
    v0 = k0 + ret_w
    g0 = v0 + ret_w

    def retention_unit(hd, n):
        c0 = hd * RET_DIM
        r0 = n * RET_CHUNK
        rows = slice(r0, r0 + RET_CHUNK)
        cs = cos2[rows]
        sn = sin2[rows]
        q = proj_ref[rows, q0 + c0:q0 + c0 + RET_DIM]
        k = proj_ref[rows, k0 + c0:k0 + c0 + RET_DIM]
        v = proj_ref[rows, v0 + c0:v0 + c0 + RET_DIM].astype(BF16)
        g = proj_ref[rows, g0 + c0:g0 + c0 + RET_DIM]
        qr = q * cs + pltpu.roll(q, RET_DIM // 2, 1) * sn
        kr = k * cs + pltpu.roll(k, RET_DIM // 2, 1) * sn
        st = state_ref[hd]
        scores = lax.dot_general(qr.astype(BF16), kr.astype(BF16), (((1,), (1,)), ((), ())),
                                 preferred_element_type=F32) * dmask_ref[hd]
        inner = jnp.dot(scores.astype(BF16), v, preferred_element_type=F32)
        cross = jnp.dot((qr * qdec_ref[hd]).astype(BF16), st.astype(BF16), preferred_element_type=F32)
        kv = lax.dot_general((kr * kdec_ref[hd]).astype(BF16), v, (((0,), (0,)), ((), ())),
                             preferred_element_type=F32)
        state_ref[hd] = chunk_decay[hd] * st + kv
        r = inner + cross
        mu = jnp.mean(r, axis=-1, keepdims=True)
        d = r - mu
        var = jnp.mean(d * d, axis=-1, keepdims=True)
        rn = d * lax.rsqrt(var + LN_EPS) * gret_ref[:, c0:c0 + RET_DIM]
        cat_ref[rows, conv_ch + c0:conv_ch + c0 + RET_DIM] = (g * jax.nn.sigmoid(g) * rn).astype(BF16)

    units = [(hd, n) for n in range(ts // RET_CHUNK) for hd in range(RET_HEADS)]
    n_conv = ts // cb
    per = -(-len(units) // n_conv)
    for i in range(n_conv):
        conv_block(i * cb)
        if i % 2 == 1:
            conv_norm((i - 1) * cb)
        for hd, n in units[i * per:(i + 1) * per]:
            retention_unit(hd, n)
    uext_ref[0:CONV_HALO, :] = uext_ref[ts:ts + CONV_HALO, :]

    out = jnp.dot(cat_ref[...], wout_ref[...], preferred_element_type=F32)
    x1 = x + mod_ref[2:3, :] * out
    o_ref[...] = x1

    @pl.when((pl.program_id(0) == 0) & (s == 0))
    def _():
        carry_ref[...] = jnp.zeros_like(carry_ref)

    _route_tile(x1, mod_ref, gffn_ref, wr2_ref, wrhi_ref, br_ref, utri_ref,
                h2p_ref, idx_ref, wts_ref, rank_ref, cnt_ref, carry_ref)


def _mixer_call(x, positions, mod, g_mix, w_in, w_dw, b_dw, g_conv_ln, b_conv_ln, g_ret_norm, w_out,
                g_ffn, w_router, b_router, group, n_groups, after):
    S, D = x.shape[1:]
    B = x.shape[0] // n_groups
    b0 = group * B
    in_cols = w_in.shape[1]
    conv_ch = w_dw.shape[1]
    ts = min(512, S)
    mask, q_decay, k_decay, chunk_decay = _retention_tables()
    key_scale = np.float32(RET_DIM ** -0.5)
    half = RET_DIM // 2
    inv_freq = (ROPE_BASE ** (-np.arange(half, dtype=np.float32) / half)).astype(np.float32)
    rope = np.concatenate([inv_freq, inv_freq])[None, :]
    w_dw_p = jnp.zeros((CONV_HALO, conv_ch), F32).at[:CONV_WIDTH].set(w_dw)
    wr_hi, wr_lo = _split_bf16(w_router.T)
    utri = jnp.asarray(np.triu(np.ones((ts, ts), np.float32), 1), BF16)
    n_s = S // ts
    T = B * S
    full = lambda shape: pl.BlockSpec(shape, lambda b, s: (0,) * len(shape))
    return pl.pallas_call(
        functools.partial(_mixer_kernel, chunk_decay, ts),
        grid=(B, S // ts),
        in_specs=[
            pl.BlockSpec((None, ts, D), lambda b, s: (b0 + b, s, 0)),
            pl.BlockSpec((None, ts, 1), lambda b, s: (b0 + b, s, 0)),
            pl.BlockSpec((None, N_MOD, D), lambda b, s: (b0 + b, 0, 0)),
            full((1, D)),
            full((D, in_cols)),
            full((CONV_HALO, conv_ch)),
            full((1, conv_ch)),
            full((1, conv_ch)),
            full((1, conv_ch)),
            full((1, RET_HEADS * RET_DIM)),
            full((conv_ch + RET_HEADS * RET_DIM, D)),
            full((1, RET_DIM)),
            full((RET_HEADS, RET_CHUNK, RET_CHUNK)),
            full((RET_HEADS, RET_CHUNK, 1)),
            full((RET_HEADS, RET_CHUNK, 1)),
            full((1, D)),
            full((2 * N_EXPERTS, D)),
            full((N_EXPERTS, D)),
            full((N_EXPERTS, 1)),
            full((ts, ts)),
            pl.BlockSpec(memory_space=pl.ANY),
        ],
        out_specs=[
            pl.BlockSpec((None, ts, D), lambda b, s: (b, s, 0)),
            pl.BlockSpec((ts, D // 2), lambda b, s: (b * n_s + s, 0)),
            pl.BlockSpec((ROUTE_ROWS, ts), lambda b, s: (0, b * n_s + s)),
            pl.BlockSpec((ts, ROUTE_LANES), lambda b, s: (b * n_s + s, 0)),
            pl.BlockSpec((ROUTE_ROWS, ts), lambda b, s: (0, b * n_s + s)),
            full((N_EXPERTS, 1)),
        ],
        out_shape=[
            jax.ShapeDtypeStruct((B, S, D), F32),
            jax.ShapeDtypeStruct((T, D // 2), U32),
            jax.ShapeDtypeStruct((ROUTE_ROWS, T), I32),
            jax.ShapeDtypeStruct((T, ROUTE_LANES), F32),
            jax.ShapeDtypeStruct((ROUTE_ROWS, T), I32),
            jax.ShapeDtypeStruct((N_EXPERTS, 1), I32),
        ],
        scratch_shapes=[
            pltpu.VMEM((ts, in_cols), F32),
            pltpu.VMEM((CONV_HALO + ts, conv_ch), F32),
            pltpu.VMEM((RET_HEADS, RET_DIM, RET_DIM), F32),
            pltpu.VMEM((ts, conv_ch + RET_HEADS * RET_DIM), BF16),
            pltpu.VMEM((ts, conv_ch), F32),
            pltpu.VMEM((N_EXPERTS, 1), F32),
        ],
        compiler_params=pltpu.CompilerParams(
            dimension_semantics=("arbitrary", "arbitrary"), vmem_limit_bytes=VMEM_LIMIT_BYTES),
        name="hybrid_mixer",
    )(x, positions.reshape(-1, S, 1), mod, g_mix.reshape(1, D), w_in.astype(BF16), w_dw_p,
      b_dw.reshape(1, -1), g_conv_ln.reshape(1, -1), b_conv_ln.reshape(1, -1), g_ret_norm.reshape(1, -1),
      w_out.astype(BF16), jnp.asarray(rope), jnp.asarray(mask * key_scale), jnp.asarray(q_decay),
      jnp.asarray(k_decay * key_scale), g_ffn.reshape(1, D), jnp.concatenate([wr_hi, wr_lo], axis=0), wr_hi,
      b_router.reshape(N_EXPERTS, 1), utri, after)


def _route_tile(x, mod_ref, gffn_ref, wr2_ref, wrhi_ref, br_ref, utri_ref,
                h2p_ref, idx_ref, wts_ref, rank_ref, cnt_ref, carry_ref):
    tr = x.shape[0]
    half = x.shape[1] // 2
    y = x * lax.rsqrt(jnp.mean(x * x, axis=-1, keepdims=True) + RMS_EPS) * gffn_ref[...]
    h2 = y * (1.0 + mod_ref[4:5, :]) + mod_ref[3:4, :]
    h2p_ref[...] = _pack_bf16_pair(h2[:, :half], h2[:, half:])

    h_hi, h_lo = _split_bf16(h2)
    nt = (((1,), (1,)), ((), ()))
    r = lax.dot_general(wr2_ref[...], h_hi, nt, preferred_element_type=F32)
    r2 = lax.dot_general(wrhi_ref[...], h_lo, nt, preferred_element_type=F32)
    l = r[:N_EXPERTS] + (r[N_EXPERTS:] + r2) + br_ref[...]
    eid = lax.broadcasted_iota(I32, l.shape, 0)
    vals, sels, idxs = [], [], []
    for _ in range(TOP_K):
        m = jnp.max(l, axis=0, keepdims=True)
        ik = jnp.min(jnp.where(l == m, eid, N_EXPERTS), axis=0, keepdims=True)
        sel = eid == ik
        vals.append(m)
        sels.append(sel)
        idxs.append(ik)
        l = jnp.where(sel, -jnp.inf, l)
    exps = [jnp.exp(v - vals[0]) for v in vals]
    denom = exps[0] + exps[1] + exps[2] + exps[3]
    member = jnp.zeros(l.shape, F32)
    for sel in sels:
        member = member + sel.astype(F32)
    before = jnp.dot(member.astype(BF16), utri_ref[...], preferred_element_type=F32) + carry_ref[...]
    ranks = [jnp.sum(jnp.where(sel, before, 0.0), axis=0, keepdims=True) for sel in sels]
    carry_ref[...] = carry_ref[...] + jnp.sum(member, axis=1, keepdims=True)
    cnt_ref[...] = carry_ref[...].astype(I32)

    def rows(pieces, n_rows, dtype):
        rid = lax.broadcasted_iota(I32, (n_rows, tr), 0)
        out = jnp.zeros((n_rows, tr), dtype)
        for k, p in enumerate(pieces):
            out = jnp.where(rid == k, p.astype(dtype), out)
        return out

    idx_ref[...] = rows(idxs, idx_ref.shape[0], I32)
    rank_ref[...] = rows(ranks, rank_ref.shape[0], I32)
    wts_ref[...] = rows([e / denom for e in exps], wts_ref.shape[1], F32).T


ROUTE_ROWS = 8
ROUTE_LANES = 128


SC_ROWS = 128


V7X_SC_CORES = 2
V7X_SC_SUBCORES = 16


def _sc_workers():
    return V7X_SC_CORES, V7X_SC_SUBCORES


def _sc_mesh():
    return plsc.VectorSubcoreMesh(core_axis_name="c", subcore_axis_name="s",
                                  num_cores=V7X_SC_CORES, num_subcores=V7X_SC_SUBCORES)


def _sc_scatter_call(h2p, pos_km, n_rows):
    T, W = h2p.shape
    nc, ns = _sc_workers()
    n = SC_ROWS
    per_w = T // (nc * ns * n)

    def body(h2p_hbm, pos_hbm, xs_hbm, i0, i1, i2, i3, rows_v, sem):
        wid = lax.axis_index("s") * nc + lax.axis_index("c")
        idx_refs = (i0, i1, i2, i3)

        @pl.loop(0, per_w)
        def _(j):
            t0 = (wid * per_w + j) * n
            pltpu.sync_copy(h2p_hbm.at[pl.ds(t0, n)], rows_v)
            for k in range(TOP_K):
                pltpu.sync_copy(pos_hbm.at[pl.ds(k * T + t0, n)], idx_refs[k])
            copies = [pltpu.async_copy(rows_v, xs_hbm.at[idx_refs[k]], sem) for k in range(TOP_K)]
            for cp in copies:
                cp.wait()

    return pl.kernel(
        body,
        out_type=jax.ShapeDtypeStruct((n_rows, W), h2p.dtype),
        mesh=_sc_mesh(),
        scratch_types=[pltpu.VMEM((n,), I32)] * TOP_K + [pltpu.VMEM((n, W), h2p.dtype), pltpu.SemaphoreType.DMA],
        name="moe_scatter_rows",
    )(h2p, pos_km)


def _sc_gather_call(ys, pos_km):
    P = pos_km.shape[0]
    W = ys.shape[1]
    nc, ns = _sc_workers()
    n = SC_ROWS
    per_w = P // (nc * ns * n)

    def body(ys_hbm, pos_hbm, yp_hbm, idx_v, rows_v, sem):
        wid = lax.axis_index("s") * nc + lax.axis_index("c")

        @pl.loop(0, per_w)
        def _(j):
            p0 = (wid * per_w + j) * n
            pltpu.sync_copy(pos_hbm.at[pl.ds(p0, n)], idx_v)
            pltpu.async_copy(ys_hbm.at[idx_v], rows_v, sem).wait()
            pltpu.sync_copy(rows_v, yp_hbm.at[pl.ds(p0, n)])

    return pl.kernel(
        body,
        out_type=jax.ShapeDtypeStruct((P, W), ys.dtype),
        mesh=_sc_mesh(),
        scratch_types=[pltpu.VMEM((n,), I32), pltpu.VMEM((n, W), ys.dtype), pltpu.SemaphoreType.DMA],
        name="moe_gather_rows",
    )(ys, pos_km)


def _ffn_kernel(te_ref, tv_ref, nx_ref, xs_ref, wgu_hbm, bgu_ref, wd_hbm, bd_ref, y_ref,
                wgu_f32, wd_f32, wgu_bf, wd_bf, sems):
    i = pl.program_id(0)
    valid = tv_ref[i]
    tm, half = xs_ref.shape
    d_exp = wd_bf.shape[0]

    def weight_copies(e):
        return (pltpu.make_async_copy(wgu_hbm.at[e], wgu_f32, sems.at[0]),
                pltpu.make_async_copy(wd_hbm.at[e], wd_f32, sems.at[1]))

    @pl.when(i == 0)
    def _():
        for cp in weight_copies(te_ref[0]):
            cp.start(priority=1)

    @pl.when(((i == 0) | (te_ref[i] != te_ref[jnp.maximum(i - 1, 0)])) & (valid > 0))
    def _():
        for cp in weight_copies(te_ref[i]):
            cp.wait()
        wgu_bf[...] = wgu_f32[...].astype(BF16)
        wd_bf[...] = wd_f32[...].astype(BF16)

        @pl.when(nx_ref[i] >= 0)
        def _():
            for cp in weight_copies(nx_ref[i]):
                cp.start(priority=1)

    def sub_tile(r0, n_rows, masked):
        rows = slice(r0, r0 + n_rows)
        lo, hi = _unpack_bf16_pair(xs_ref[rows, :])
        if masked:
            keep = lax.broadcasted_iota(I32, (n_rows, half), 0) < valid - r0
            lo = jnp.where(keep, lo, 0.0)
            hi = jnp.where(keep, hi, 0.0)
        xt = jnp.concatenate([lo.astype(BF16), hi.astype(BF16)], axis=1)
        d = functools.partial(jnp.dot, preferred_element_type=F32)
        nb = 256

        def up(c):
            gate = d(xt, wgu_bf[:, c:c + nb]) + bgu_ref[:, c:c + nb]
            lin = d(xt, wgu_bf[:, d_exp + c:d_exp + c + nb]) + bgu_ref[:, d_exp + c:d_exp + c + nb]
            return gate, lin

        def activate(gate, lin):
            gate = jnp.minimum(gate, SWIGLU_LIMIT)
            lin = jnp.clip(lin, -SWIGLU_LIMIT, SWIGLU_LIMIT)
            return (gate * jax.nn.sigmoid(SWIGLU_ALPHA * gate) * (lin + 1.0)).astype(BF16)

        chunks = list(range(0, d_exp, nb))
        acts = []
        pending = up(chunks[0])
        for n in range(len(chunks)):
            nxt = up(chunks[n + 1]) if n + 1 < len(chunks) else None
            acts.append(activate(*pending))
            pending = nxt
        out = d(jnp.concatenate(acts, axis=1), wd_bf[...]) + bd_ref[...]
        y_ref[rows, :] = _pack_bf16_pair(out[:, :half], out[:, half:])

    pl.when(valid == tm)(functools.partial(sub_tile, 0, tm, False))
    for n_sub in range(1, tm // FFN_SUB_ROWS + 1):
        n_rows = n_sub * FFN_SUB_ROWS
        pl.when((valid > n_rows - FFN_SUB_ROWS) & (valid <= n_rows) & (valid < tm))(
            functools.partial(sub_tile, 0, n_rows, True))


def _ffn_call(tile_expert, tile_valid, tile_next, xs, w_gu, b_gu, w_down, b_down, tm):
    R, half = xs.shape
    E, D, two_f = w_gu.shape
    d_exp = w_down.shape[1]
    grid_spec = pltpu.PrefetchScalarGridSpec(
        num_scalar_prefetch=3,
        grid=(R // tm,),
        in_specs=[
            pl.BlockSpec((tm, half), lambda i, te, tv, nx: (i, 0)),
            pl.BlockSpec(memory_space=pl.ANY),
            pl.BlockSpec((None, 1, two_f), lambda i, te, tv, nx: (te[i], 0, 0)),
            pl.BlockSpec(memory_space=pl.ANY),
            pl.BlockSpec((None, 1, D), lambda i, te, tv, nx: (te[i], 0, 0)),
        ],
        out_specs=pl.BlockSpec((tm, half), lambda i, te, tv, nx: (i, 0)),
        scratch_shapes=[
            pltpu.VMEM((D, two_f), F32), pltpu.VMEM((d_exp, D), F32),
            pltpu.VMEM((D, two_f), BF16), pltpu.VMEM((d_exp, D), BF16),
            pltpu.SemaphoreType.DMA((2,)),
        ],
    )
    return pl.pallas_call(
        _ffn_kernel,
        grid_spec=grid_spec,
        out_shape=jax.ShapeDtypeStruct((R, half), U32),
        compiler_params=pltpu.CompilerParams(
            dimension_semantics=("arbitrary",), vmem_limit_bytes=VMEM_LIMIT_BYTES),
        name="moe_ffn",
    )(tile_expert, tile_valid, tile_next, xs, w_gu, b_gu.reshape(E, 1, two_f), w_down,
      b_down.reshape(E, 1, D))


def _final_kernel(x_ref, yp_ref, wts_ref, mod_ref, gfin_ref, *rest):
    o_ref = rest[-1]
    x = x_ref[...]
    half = x.shape[1] // 2
    w = wts_ref[...]
    lo = jnp.zeros((x.shape[0], half), F32)
    hi = jnp.zeros((x.shape[0], half), F32)
    for k in range(TOP_K):
        l, h = _unpack_bf16_pair(yp_ref[k])
        lo = lo + w[:, k:k + 1] * l
        hi = hi + w[:, k:k + 1] * h
    gate = mod_ref[5:6, :]
    x_lo = x[:, :half] + gate[:, :half] * lo
    x_hi = x[:, half:] + gate[:, half:] * hi
    ms = (jnp.sum(x_lo * x_lo, axis=-1, keepdims=True) + jnp.sum(x_hi * x_hi, axis=-1, keepdims=True)) / x.shape[1]
    inv = lax.rsqrt(ms + RMS_EPS)
    o_ref[:, :half] = x_lo * inv * gfin_ref[:, :half]
    o_ref[:, half:] = x_hi * inv * gfin_ref[:, half:]


def _final_call(x1, yp, wts, mod, g_final, tokens_per_batch, group, n_groups, prev_out):
    T, D = x1.shape
    T_all = T * n_groups
    tq = min(512, tokens_per_batch)
    per_b = tokens_per_batch // tq
    first = group * (T // tq)
    in_specs = [
        pl.BlockSpec((tq, D), lambda i: (i, 0)),
        pl.BlockSpec((TOP_K, tq, D // 2), lambda i: (0, i, 0)),
        pl.BlockSpec((tq, ROUTE_LANES), lambda i: (i, 0)),
        pl.BlockSpec((None, N_MOD, D), lambda i: ((first + i) // per_b, 0, 0)),
        pl.BlockSpec((1, D), lambda i: (0, 0)),
    ]
    args = [x1, yp, wts, mod, g_final.reshape(1, D)]
    aliases = {}
    if prev_out is not None:
        in_specs.append(pl.BlockSpec(memory_space=pl.ANY))
        args.append(prev_out)
        aliases = {len(args) - 1: 0}
    return pl.pallas_call(
        _final_kernel,
        grid=(T // tq,),
        in_specs=in_specs,
        out_specs=pl.BlockSpec((tq, D), lambda i: (first + i, 0)),
        out_shape=jax.ShapeDtypeStruct((T_all, D), F32),
        input_output_aliases=aliases,
        compiler_params=pltpu.CompilerParams(
            dimension_semantics=("arbitrary",), vmem_limit_bytes=VMEM_LIMIT_BYTES),
        name="moe_combine_final",
    )(*args)


def _group_layout(counts, n_tiles, tm):
    padded = ((counts + tm - 1) // tm) * tm
    ends = jnp.cumsum(padded)
    starts = ends - padded
    tile_row = jnp.arange(n_tiles, dtype=I32) * tm
    te = jnp.minimum(jnp.sum(tile_row[:, None] >= ends[None, :], axis=1), N_EXPERTS - 1).astype(I32)
    eids = jnp.arange(N_EXPERTS, dtype=I32)
    mine = te[:, None] == eids[None, :]
    lookup = lambda table: jnp.sum(jnp.where(mine, table[None, :], 0), axis=1)
    tv = jnp.clip(lookup(counts) - (tile_row - lookup(starts)), 0, tm).astype(I32)
    later = (eids[None, :] > eids[:, None]) & (counts[None, :] > 0)
    nxt = jnp.min(jnp.where(later, eids[None, :], N_EXPERTS), axis=1)
    nx = lookup(jnp.where(nxt < N_EXPERTS, nxt, -1)).astype(I32)
    return starts, te, tv, nx


def kernel(x, c, positions, w_ada, b_ada, g_mix, w_in, w_dw, b_dw, g_conv_ln, b_conv_ln, g_ret_norm,
           w_out, g_ffn, w_router, b_router, w_gu, b_gu, w_down, b_down, g_final):
    B, S, D = x.shape
    T = B * S
    assert w_ada.shape[0] == 1, "single-layer block: the final norm directly follows layer 0"
    xt = x
    for l in range(1):
        mod = _mod_call(c, w_ada[l], b_ada[l]).reshape(B, N_MOD, D)
        n_groups = MOE_TOKEN_GROUPS if B % MOE_TOKEN_GROUPS == 0 else 1
        Bg = B // n_groups
        Tg = Bg * S
        tm = FFN_TILE_ROWS if Tg * TOP_K >= FFN_TILE_ROWS * N_EXPERTS * 4 else FFN_SUB_ROWS
        n_tiles = (Tg * TOP_K) // tm + N_EXPERTS
        x1s, scattered = [], []
        pos_km = jnp.zeros((TOP_K * Tg,), I32)
        for g in range(n_groups):
            x1, h2p, idx, wts, rank, counts = _mixer_call(
                xt, positions, mod, g_mix[l], w_in[l], w_dw[l], b_dw[l], g_conv_ln[l], b_conv_ln[l],
                g_ret_norm[l], w_out[l], g_ffn[l], w_router[l], b_router[l], g, n_groups, pos_km)
            x1 = x1.reshape(Tg, D)
            starts, te, tv, nx = _group_layout(counts[:, 0], n_tiles, tm)
            pos_km = rank
            for e in range(N_EXPERTS):
                pos_km = pos_km + jnp.where(idx == e, starts[e], 0)
            pos_km = pos_km[:TOP_K].astype(I32).reshape(-1)
            x1s.append(x1)
            scattered.append((_sc_scatter_call(h2p, pos_km, n_tiles * tm), pos_km, wts, te, tv, nx))
        out = None
        for g, (xs, pos_km, wts, te, tv, nx) in enumerate(scattered):
            ys = _ffn_call(te, tv, nx, xs, w_gu[l], b_gu[l], w_down[l], b_down[l], tm)
            yp = _sc_gather_call(ys, pos_km)
            out = _final_call(x1s[g], yp.reshape(TOP_K, Tg, D // 2), wts, mod, g_final, S, g, n_groups, out)
        xt = out
    return xt.reshape(B, S, D)
```

```python
import functools

import numpy as np
import jax
import jax.numpy as jnp
from jax import lax
from jax.experimental import pallas as pl
from jax.experimental.pallas import tpu as pltpu
from jax.experimental.pallas import tpu_sc as plsc

F32 = jnp.float32
BF16 = jnp.bfloat16
U32 = jnp.uint32
I32 = jnp.int32

CONV_WIDTH = 31
CONV_HALO = 32
RET_HEADS = 4
RET_DIM = 128
RET_CHUNK = 128
ROPE_BASE = 10000.0
N_EXPERTS = 32
TOP_K = 4
SWIGLU_LIMIT = 7.0
SWIGLU_ALPHA = 1.702
RMS_EPS = 1e-6
LN_EPS = 1e-5
N_MOD = 6

VMEM_LIMIT_BYTES = 56 * 1024 * 1024
MOE_TOKEN_GROUPS = 2
FFN_TILE_ROWS = 1024
FFN_SUB_ROWS = 256


def _split_bf16(a):
    hi = a.astype(BF16)
    lo = (a - hi.astype(F32)).astype(BF16)
    return hi, lo


def _dot3(a, b_hi, b_lo):
    a_hi, a_lo = _split_bf16(a)
    d = functools.partial(jnp.dot, preferred_element_type=F32)
    return d(a_hi, b_hi) + (d(a_hi, b_lo) + d(a_lo, b_hi))


def _pack_bf16_pair(lo, hi):
    lo_bits = lax.bitcast_convert_type(lo.astype(BF16).astype(F32), U32)
    hi_bits = lax.bitcast_convert_type(hi.astype(BF16).astype(F32), U32)
    return (lo_bits >> 16) | (hi_bits & jnp.uint32(0xFFFF0000))


def _unpack_bf16_pair(p):
    lo = lax.bitcast_convert_type(p << 16, F32)
    hi = lax.bitcast_convert_type(p & jnp.uint32(0xFFFF0000), F32)
    return lo, hi


def _mod_kernel(c_ref, whi_ref, wlo_ref, b_ref, o_ref):
    c = c_ref[...]
    c_act = c * jax.nn.sigmoid(c)
    o_ref[...] = _dot3(c_act, whi_ref[...], wlo_ref[...]) + b_ref[...]


def _mod_call(c, w_ada, b_ada):
    B, D = c.shape
    n = w_ada.shape[1]
    bn = 1024
    w_hi, w_lo = _split_bf16(w_ada)
    return pl.pallas_call(
        _mod_kernel,
        grid=(n // bn,),
        in_specs=[
            pl.BlockSpec((B, D), lambda j: (0, 0)),
            pl.BlockSpec((D, bn), lambda j: (0, j)),
            pl.BlockSpec((D, bn), lambda j: (0, j)),
            pl.BlockSpec((1, bn), lambda j: (0, j)),
        ],
        out_specs=pl.BlockSpec((B, bn), lambda j: (0, j)),
        out_shape=jax.ShapeDtypeStruct((B, n), F32),
        name="adaln_mod",
    )(c, w_hi, w_lo, b_ada.reshape(1, n))


def _retention_tables():
    h = np.arange(RET_HEADS, dtype=np.float32)
    log_gamma = np.log(1.0 - np.power(2.0, -5.0 - h)).astype(np.float32)
    idx = np.arange(RET_CHUNK, dtype=np.float32)
    diff = idx[:, None] - idx[None, :]
    causal = diff >= 0
    mask = np.where(causal[None], np.exp(log_gamma[:, None, None] * np.where(causal, diff, 0.0)[None]), 0.0)
    q_decay = np.exp(log_gamma[:, None] * (idx + 1.0))[..., None]
    k_decay = np.exp(log_gamma[:, None] * (RET_CHUNK - 1.0 - idx))[..., None]
    chunk_decay = np.exp(log_gamma * RET_CHUNK)
    return (mask.astype(np.float32), q_decay.astype(np.float32), k_decay.astype(np.float32),
            [float(v) for v in chunk_decay.astype(np.float32)])


def _mixer_kernel(chunk_decay, ts, x_ref, pos_ref, mod_ref, gmix_ref, win_ref, wdw_ref, bdw_ref,
                  gcl_ref, bcl_ref, gret_ref, wout_ref, rope_ref, dmask_ref, qdec_ref, kdec_ref,
                  gffn_ref, wr2_ref, wrhi_ref, br_ref, utri_ref, after_ref,
                  o_ref, h2p_ref, idx_ref, wts_ref, rank_ref, cnt_ref,
                  proj_ref, uext_ref, state_ref, cat_ref, conv_ref, carry_ref):
    del after_ref
    s = pl.program_id(1)
    conv_ch = wdw_ref.shape[1]
    ret_w = RET_HEADS * RET_DIM

    @pl.when(s == 0)
    def _():
        uext_ref[0:CONV_HALO, :] = jnp.zeros((CONV_HALO, conv_ch), F32)
        state_ref[...] = jnp.zeros_like(state_ref)

    x = x_ref[...]
    row_gain = gmix_ref[...] * (1.0 + mod_ref[1:2, :])
    h = (x * lax.rsqrt(jnp.mean(x * x, axis=-1, keepdims=True) + RMS_EPS) * row_gain + mod_ref[0:1, :]).astype(BF16)
    proj_ref[...] = jnp.dot(h, win_ref[...], preferred_element_type=F32)

    a = proj_ref[:, 0:conv_ch]
    b = proj_ref[:, conv_ch:2 * conv_ch]
    uext_ref[CONV_HALO:CONV_HALO + ts, :] = a * jax.nn.sigmoid(b)
    cb = 64
    lead = CONV_HALO - (CONV_WIDTH - 1)
    span = cb + CONV_HALO

    def conv_block(r0):
        for c0 in range(0, conv_ch, 128):
            xw = uext_ref[r0:r0 + span, c0:c0 + 128]
            acc = jnp.zeros((cb, 128), F32) + bdw_ref[:, c0:c0 + 128]
            for r in range(8):
                xr = xw if r == 0 else pltpu.roll(xw, span - r, 0)
                for q in range((lead + CONV_WIDTH - 1 - r) // 8 + 1):
                    j = 8 * q + r - lead
                    if 0 <= j < CONV_WIDTH:
                        acc = acc + wdw_ref[j:j + 1, c0:c0 + 128] * xr[8 * q:8 * q + cb]
            conv_ref[r0:r0 + cb, c0:c0 + 128] = acc
    rb = 2 * cb

    def conv_norm(r0):
        acc = conv_ref[r0:r0 + rb, 0:conv_ch]
        mu = jnp.mean(acc, axis=-1, keepdims=True)
        d = acc - mu
        var = jnp.mean(d * d, axis=-1, keepdims=True)
        ln = d * lax.rsqrt(var + LN_EPS) * gcl_ref[...] + bcl_ref[...]
        cat_ref[r0:r0 + rb, 0:conv_ch] = (ln * jax.nn.sigmoid(ln)).astype(BF16)

    hts = ts // 2
    low = lax.broadcasted_iota(I32, (hts, RET_DIM), 1) < RET_DIM // 2
    posf = jnp.where(low, pos_ref[0:hts, :].astype(F32), pos_ref[hts:ts, :].astype(F32))
    ang = posf * rope_ref[...]
    cos_p = jnp.cos(ang)
    sin_p = jnp.sin(ang)
    cos_s = pltpu.roll(cos_p, RET_DIM // 2, 1)
    sin_s = pltpu.roll(sin_p, RET_DIM // 2, 1)
    cos2 = jnp.concatenate([jnp.where(low, cos_p, cos_s), jnp.where(low, cos_s, cos_p)], axis=0)
    sin2 = jnp.concatenate([jnp.where(low, -sin_p, sin_s), jnp.where(low, -sin_s, sin_p)], axis=0)
    q0 = 2 * conv_ch
    k0 = q0 + ret_w
    v0 = k0 + ret_w
    g0 = v0 + ret_w

    def retention_unit(hd, n):
        c0 = hd * RET_DIM
        r0 = n * RET_CHUNK
        rows = slice(r0, r0 + RET_CHUNK)
        cs = cos2[rows]
        sn = sin2[rows]
        q = proj_ref[rows, q0 + c0:q0 + c0 + RET_DIM]
        k = proj_ref[rows, k0 + c0:k0 + c0 + RET_DIM]
        v = proj_ref[rows, v0 + c0:v0 + c0 + RET_DIM].astype(BF16)
        g = proj_ref[rows, g0 + c0:g0 + c0 + RET_DIM]
        qr = q * cs + pltpu.roll(q, RET_DIM // 2, 1) * sn
        kr = k * cs + pltpu.roll(k, RET_DIM // 2, 1) * sn
        st = state_ref[hd]
        scores = lax.dot_general(qr.astype(BF16), kr.astype(BF16), (((1,), (1,)), ((), ())),
                                 preferred_element_type=F32) * dmask_ref[hd]
        inner = jnp.dot(scores.astype(BF16), v, preferred_element_type=F32)
        cross = jnp.dot((qr * qdec_ref[hd]).astype(BF16), st.astype(BF16), preferred_element_type=F32)
        kv = lax.dot_general((kr * kdec_ref[hd]).astype(BF16), v, (((0,), (0,)), ((), ())),
                             preferred_element_type=F32)
        state_ref[hd] = chunk_decay[hd] * st + kv
        r = inner + cross
        mu = jnp.mean(r, axis=-1, keepdims=True)
        d = r - mu
        var = jnp.mean(d * d, axis=-1, keepdims=True)
        rn = d * lax.rsqrt(var + LN_EPS) * gret_ref[:, c0:c0 + RET_DIM]
        cat_ref[rows, conv_ch + c0:conv_ch + c0 + RET_DIM] = (g * jax.nn.sigmoid(g) * rn).astype(BF16)

    units = [(hd, n) for n in range(ts // RET_CHUNK) for hd in range(RET_HEADS)]
    n_conv = ts // cb
    per = -(-len(units) // n_conv)
    for i in range(n_conv):
        conv_block(i * cb)
        if i % 2 == 1:
            conv_norm((i - 1) * cb)
        for hd, n in units[i * per:(i + 1) * per]:
            retention_unit(hd, n)
    uext_ref[0:CONV_HALO, :] = uext_ref[ts:ts + CONV_HALO, :]

    out = jnp.dot(cat_ref[...], wout_ref[...], preferred_element_type=F32)
    x1 = x + mod_ref[2:3, :] * out
    o_ref[...] = x1

    @pl.when((pl.program_id(0) == 0) & (s == 0))
    def _():
        carry_ref[...] = jnp.zeros_like(carry_ref)

    _route_tile(x1, mod_ref, gffn_ref, wr2_ref, wrhi_ref, br_ref, utri_ref,
                h2p_ref, idx_ref, wts_ref, rank_ref, cnt_ref, carry_ref)


def _mixer_call(x, positions, mod, g_mix, w_in, w_dw, b_dw, g_conv_ln, b_conv_ln, g_ret_norm, w_out,
                g_ffn, w_router, b_router, group, n_groups, after):
    S, D = x.shape[1:]
    B = x.shape[0] // n_groups
    b0 = group * B
    in_cols = w_in.shape[1]
    conv_ch = w_dw.shape[1]
    ts = min(512, S)
    mask, q_decay, k_decay, chunk_decay = _retention_tables()
    key_scale = np.float32(RET_DIM ** -0.5)
    half = RET_DIM // 2
    inv_freq = (ROPE_BASE ** (-np.arange(half, dtype=np.float32) / half)).astype(np.float32)
    rope = np.concatenate([inv_freq, inv_freq])[None, :]
    w_dw_p = jnp.zeros((CONV_HALO, conv_ch), F32).at[:CONV_WIDTH].set(w_dw)
    wr_hi, wr_lo = _split_bf16(w_router.T)
    utri = jnp.asarray(np.triu(np.ones((ts, ts), np.float32), 1), BF16)
    n_s = S // ts
    T = B * S
    full = lambda shape: pl.BlockSpec(shape, lambda b, s: (0,) * len(shape))
    return pl.pallas_call(
        functools.partial(_mixer_kernel, chunk_decay, ts),
        grid=(B, S // ts),
        in_specs=[
            pl.BlockSpec((None, ts, D), lambda b, s: (b0 + b, s, 0)),
            pl.BlockSpec((None, ts, 1), lambda b, s: (b0 + b, s, 0)),
            pl.BlockSpec((None, N_MOD, D), lambda b, s: (b0 + b, 0, 0)),
            full((1, D)),
            full((D, in_cols)),
            full((CONV_HALO, conv_ch)),
            full((1, conv_ch)),
            full((1, conv_ch)),
            full((1, conv_ch)),
            full((1, RET_HEADS * RET_DIM)),
            full((conv_ch + RET_HEADS * RET_DIM, D)),
            full((1, RET_DIM)),
            full((RET_HEADS, RET_CHUNK, RET_CHUNK)),
            full((RET_HEADS, RET_CHUNK, 1)),
            full((RET_HEADS, RET_CHUNK, 1)),
            full((1, D)),
            full((2 * N_EXPERTS, D)),
            full((N_EXPERTS, D)),
            full((N_EXPERTS, 1)),
            full((ts, ts)),
            pl.BlockSpec(memory_space=pl.ANY),
        ],
        out_specs=[
            pl.BlockSpec((None, ts, D), lambda b, s: (b, s, 0)),
            pl.BlockSpec((ts, D // 2), lambda b, s: (b * n_s + s, 0)),
            pl.BlockSpec((ROUTE_ROWS, ts), lambda b, s: (0, b * n_s + s)),
            pl.BlockSpec((ts, ROUTE_LANES), lambda b, s: (b * n_s + s, 0)),
            pl.BlockSpec((ROUTE_ROWS, ts), lambda b, s: (0, b * n_s + s)),
            full((N_EXPERTS, 1)),
        ],
        out_shape=[
            jax.ShapeDtypeStruct((B, S, D), F32),
            jax.ShapeDtypeStruct((T, D // 2), U32),
            jax.ShapeDtypeStruct((ROUTE_ROWS, T), I32),
            jax.ShapeDtypeStruct((T, ROUTE_LANES), F32),
            jax.ShapeDtypeStruct((ROUTE_ROWS, T), I32),
            jax.ShapeDtypeStruct((N_EXPERTS, 1), I32),
        ],
        scratch_shapes=[
            pltpu.VMEM((ts, in_cols), F32),
            pltpu.VMEM((CONV_HALO + ts, conv_ch), F32),
            pltpu.VMEM((RET_HEADS, RET_DIM, RET_DIM), F32),
            pltpu.VMEM((ts, conv_ch + RET_HEADS * RET_DIM), BF16),
            pltpu.VMEM((ts, conv_ch), F32),
            pltpu.VMEM((N_EXPERTS, 1), F32),
        ],
        compiler_params=pltpu.CompilerParams(
            dimension_semantics=("arbitrary", "arbitrary"), vmem_limit_bytes=VMEM_LIMIT_BYTES),
        name="hybrid_mixer",
    )(x, positions.reshape(-1, S, 1), mod, g_mix.reshape(1, D), w_in.astype(BF16), w_dw_p,
      b_dw.reshape(1, -1), g_conv_ln.reshape(1, -1), b_conv_ln.reshape(1, -1), g_ret_norm.reshape(1, -1),
      w_out.astype(BF16), jnp.asarray(rope), jnp.asarray(mask * key_scale), jnp.asarray(q_decay),
      jnp.asarray(k_decay * key_scale), g_ffn.reshape(1, D), jnp.concatenate([wr_hi, wr_lo], axis=0), wr_hi,
      b_router.reshape(N_EXPERTS, 1), utri, after)


def _route_tile(x, mod_ref, gffn_ref, wr2_ref, wrhi_ref, br_ref, utri_ref,
                h2p_ref, idx_ref, wts_ref, rank_ref, cnt_ref, carry_ref):
    tr = x.shape[0]
    half = x.shape[1] // 2
    y = x * lax.rsqrt(jnp.mean(x * x, axis=-1, keepdims=True) + RMS_EPS) * gffn_ref[...]
    h2 = y * (1.0 + mod_ref[4:5, :]) + mod_ref[3:4, :]
    h2p_ref[...] = _pack_bf16_pair(h2[:, :half], h2[:, half:])

    h_hi, h_lo = _split_bf16(h2)
    nt = (((1,), (1,)), ((), ()))
    r = lax.dot_general(wr2_ref[...], h_hi, nt, preferred_element_type=F32)
    r2 = lax.dot_general(wrhi_ref[...], h_lo, nt, preferred_element_type=F32)
    l = r[:N_EXPERTS] + (r[N_EXPERTS:] + r2) + br_ref[...]
    eid = lax.broadcasted_iota(I32, l.shape, 0)
    vals, sels, idxs = [], [], []
    for _ in range(TOP_K):
        m = jnp.max(l, axis=0, keepdims=True)
        ik = jnp.min(jnp.where(l == m, eid, N_EXPERTS), axis=0, keepdims=True)
        sel = eid == ik
        vals.append(m)
        sels.append(sel)
        idxs.append(ik)
        l = jnp.where(sel, -jnp.inf, l)
    exps = [jnp.exp(v - vals[0]) for v in vals]
    denom = exps[0] + exps[1] + exps[2] + exps[3]
    member = jnp.zeros(l.shape, F32)
    for sel in sels:
        member = member + sel.astype(F32)
    before = jnp.dot(member.astype(BF16), utri_ref[...], preferred_element_type=F32) + carry_ref[...]
    ranks = [jnp.sum(jnp.where(sel, before, 0.0), axis=0, keepdims=True) for sel in sels]
    carry_ref[...] = carry_ref[...] + jnp.sum(member, axis=1, keepdims=True)
    cnt_ref[...] = carry_ref[...].astype(I32)

    def rows(pieces, n_rows, dtype):
        rid = lax.broadcasted_iota(I32, (n_rows, tr), 0)
        out = jnp.zeros((n_rows, tr), dtype)
        for k, p in enumerate(pieces):
            out = jnp.where(rid == k, p.astype(dtype), out)
        return out

    idx_ref[...] = rows(idxs, idx_ref.shape[0], I32)
    rank_ref[...] = rows(ranks, rank_ref.shape[0], I32)
    wts_ref[...] = rows([e / denom for e in exps], wts_ref.shape[1], F32).T


ROUTE_ROWS = 8
ROUTE_LANES = 128


SC_ROWS = 128


V7X_SC_CORES = 2
V7X_SC_SUBCORES = 16


def _sc_workers():
    return V7X_SC_CORES, V7X_SC_SUBCORES


def _sc_mesh():
    return plsc.VectorSubcoreMesh(core_axis_name="c", subcore_axis_name="s",
                                  num_cores=V7X_SC_CORES, num_subcores=V7X_SC_SUBCORES)


def _sc_scatter_call(h2p, pos_km, n_rows):
    T, W = h2p.shape
    nc, ns = _sc_workers()
    n = SC_ROWS
    per_w = T // (nc * ns * n)

    def body(h2p_hbm, pos_hbm, xs_hbm, i0, i1, i2, i3, rows_v, sem):
        wid = lax.axis_index("s") * nc + lax.axis_index("c")
        idx_refs = (i0, i1, i2, i3)

        @pl.loop(0, per_w)
        def _(j):
            t0 = (wid * per_w + j) * n
            pltpu.sync_copy(h2p_hbm.at[pl.ds(t0, n)], rows_v)
            for k in range(TOP_K):
                pltpu.sync_copy(pos_hbm.at[pl.ds(k * T + t0, n)], idx_refs[k])
            copies = [pltpu.async_copy(rows_v, xs_hbm.at[idx_refs[k]], sem) for k in range(TOP_K)]
            for cp in copies:
                cp.wait()

    return pl.kernel(
        body,
        out_type=jax.ShapeDtypeStruct((n_rows, W), h2p.dtype),
        mesh=_sc_mesh(),
        scratch_types=[pltpu.VMEM((n,), I32)] * TOP_K + [pltpu.VMEM((n, W), h2p.dtype), pltpu.SemaphoreType.DMA],
        name="moe_scatter_rows",
    )(h2p, pos_km)


def _sc_gather_call(ys, pos_km):
    P = pos_km.shape[0]
    W = ys.shape[1]
    nc, ns = _sc_workers()
    n = SC_ROWS
    per_w = P // (nc * ns * n)

    def body(ys_hbm, pos_hbm, yp_hbm, idx_v, rows_v, sem):
        wid = lax.axis_index("s") * nc + lax.axis_index("c")

        @pl.loop(0, per_w)
        def _(j):
            p0 = (wid * per_w + j) * n
            pltpu.sync_copy(pos_hbm.at[pl.ds(p0, n)], idx_v)
            pltpu.async_copy(ys_hbm.at[idx_v], rows_v, sem).wait()
            pltpu.sync_copy(rows_v, yp_hbm.at[pl.ds(p0, n)])

    return pl.kernel(
        body,
        out_type=jax.ShapeDtypeStruct((P, W), ys.dtype),
        mesh=_sc_mesh(),
        scratch_types=[pltpu.VMEM((n,), I32), pltpu.VMEM((n, W), ys.dtype), pltpu.SemaphoreType.DMA],
        name="moe_gather_rows",
    )(ys, pos_km)


def _ffn_kernel(te_ref, tv_ref, nx_ref, xs_ref, wgu_hbm, bgu_ref, wd_hbm, bd_ref, y_ref,
                wgu_f32, wd_f32, wgu_bf, wd_bf, sems):
    i = pl.program_id(0)
    valid = tv_ref[i]
    tm, half = xs_ref.shape
    d_exp = wd_bf.shape[0]

    def weight_copies(e):
        return (pltpu.make_async_copy(wgu_hbm.at[e], wgu_f32, sems.at[0]),
                pltpu.make_async_copy(wd_hbm.at[e], wd_f32, sems.at[1]))

    @pl.when(i == 0)
    def _():
        for cp in weight_copies(te_ref[0]):
            cp.start(priority=1)

    @pl.when(((i == 0) | (te_ref[i] != te_ref[jnp.maximum(i - 1, 0)])) & (valid > 0))
    def _():
        for cp in weight_copies(te_ref[i]):
            cp.wait()
        wgu_bf[...] = wgu_f32[...].astype(BF16)
        wd_bf[...] = wd_f32[...].astype(BF16)

        @pl.when(nx_ref[i] >= 0)
        def _():
            for cp in weight_copies(nx_ref[i]):
                cp.start(priority=1)

    def sub_tile(r0, n_rows, masked):
        rows = slice(r0, r0 + n_rows)
        lo, hi = _unpack_bf16_pair(xs_ref[rows, :])
        if masked:
            keep = lax.broadcasted_iota(I32, (n_rows, half), 0) < valid - r0
            lo = jnp.where(keep, lo, 0.0)
            hi = jnp.where(keep, hi, 0.0)
        xt = jnp.concatenate([lo.astype(BF16), hi.astype(BF16)], axis=1)
        d = functools.partial(jnp.dot, preferred_element_type=F32)
        nb = 256

        def up(c):
            gate = d(xt, wgu_bf[:, c:c + nb]) + bgu_ref[:, c:c + nb]
            lin = d(xt, wgu_bf[:, d_exp + c:d_exp + c + nb]) + bgu_ref[:, d_exp + c:d_exp + c + nb]
            return gate, lin

        def activate(gate, lin):
            gate = jnp.minimum(gate, SWIGLU_LIMIT)
            lin = jnp.clip(lin, -SWIGLU_LIMIT, SWIGLU_LIMIT)
            return (gate * jax.nn.sigmoid(SWIGLU_ALPHA * gate) * (lin + 1.0)).astype(BF16)

        chunks = list(range(0, d_exp, nb))
        acts = []
        pending = up(chunks[0])
        for n in range(len(chunks)):
            nxt = up(chunks[n + 1]) if n + 1 < len(chunks) else None
            acts.append(activate(*pending))
            pending = nxt
        out = d(jnp.concatenate(acts, axis=1), wd_bf[...]) + bd_ref[...]
        y_ref[rows, :] = _pack_bf16_pair(out[:, :half], out[:, half:])

    pl.when(valid == tm)(functools.partial(sub_tile, 0, tm, False))
    for n_sub in range(1, tm // FFN_SUB_ROWS + 1):
        n_rows = n_sub * FFN_SUB_ROWS
        pl.when((valid > n_rows - FFN_SUB_ROWS) & (valid <= n_rows) & (valid < tm))(
            functools.partial(sub_tile, 0, n_rows, True))


def _ffn_call(tile_expert, tile_valid, tile_next, xs, w_gu, b_gu, w_down, b_down, tm):
    R, half = xs.shape
    E, D, two_f = w_gu.shape
    d_exp = w_down.shape[1]
    grid_spec = pltpu.PrefetchScalarGridSpec(
        num_scalar_prefetch=3,
        grid=(R // tm,),
        in_specs=[
            pl.BlockSpec((tm, half), lambda i, te, tv, nx: (i, 0)),
            pl.BlockSpec(memory_space=pl.ANY),
            pl.BlockSpec((None, 1, two_f), lambda i, te, tv, nx: (te[i], 0, 0)),
            pl.BlockSpec(memory_space=pl.ANY),
            pl.BlockSpec((None, 1, D), lambda i, te, tv, nx: (te[i], 0, 0)),
        ],
        out_specs=pl.BlockSpec((tm, half), lambda i, te, tv, nx: (i, 0)),
        scratch_shapes=[
            pltpu.VMEM((D, two_f), F32), pltpu.VMEM((d_exp, D), F32),
            pltpu.VMEM((D, two_f), BF16), pltpu.VMEM((d_exp, D), BF16),
            pltpu.SemaphoreType.DMA((2,)),
        ],
    )
    return pl.pallas_call(
        _ffn_kernel,
        grid_spec=grid_spec,
        out_shape=jax.ShapeDtypeStruct((R, half), U32),
        compiler_params=pltpu.CompilerParams(
            dimension_semantics=("arbitrary",), vmem_limit_bytes=VMEM_LIMIT_BYTES),
        name="moe_ffn",
    )(tile_expert, tile_valid, tile_next, xs, w_gu, b_gu.reshape(E, 1, two_f), w_down,
      b_down.reshape(E, 1, D))


def _final_kernel(x_ref, yp_ref, wts_ref, mod_ref, gfin_ref, *rest):
    o_ref = rest[-1]
    x = x_ref[...]
    half = x.shape[1] // 2
    w = wts_ref[...]
    lo = jnp.zeros((x.shape[0], half), F32)
    hi = jnp.zeros((x.shape[0], half), F32)
    for k in range(TOP_K):
        l, h = _unpack_bf16_pair(yp_ref[k])
        lo = lo + w[:, k:k + 1] * l
        hi = hi + w[:, k:k + 1] * h
    gate = mod_ref[5:6, :]
    x_lo = x[:, :half] + gate[:, :half] * lo
    x_hi = x[:, half:] + gate[:, half:] * hi
    ms = (jnp.sum(x_lo * x_lo, axis=-1, keepdims=True) + jnp.sum(x_hi * x_hi, axis=-1, keepdims=True)) / x.shape[1]
    inv = lax.rsqrt(ms + RMS_EPS)
    o_ref[:, :half] = x_lo * inv * gfin_ref[:, :half]
    o_ref[:, half:] = x_hi * inv * gfin_ref[:, half:]


def _final_call(x1, yp, wts, mod, g_final, tokens_per_batch, group, n_groups, prev_out):
    T, D = x1.shape
    T_all = T * n_groups
    tq = min(512, tokens_per_batch)
    per_b = tokens_per_batch // tq
    first = group * (T // tq)
    in_specs = [
        pl.BlockSpec((tq, D), lambda i: (i, 0)),
        pl.BlockSpec((TOP_K, tq, D // 2), lambda i: (0, i, 0)),
        pl.BlockSpec((tq, ROUTE_LANES), lambda i: (i, 0)),
        pl.BlockSpec((None, N_MOD, D), lambda i: ((first + i) // per_b, 0, 0)),
        pl.BlockSpec((1, D), lambda i: (0, 0)),
    ]
    args = [x1, yp, wts, mod, g_final.reshape(1, D)]
    aliases = {}
    if prev_out is not None:
        in_specs.append(pl.BlockSpec(memory_space=pl.ANY))
        args.append(prev_out)
        aliases = {len(args) - 1: 0}
    return pl.pallas_call(
        _final_kernel,
        grid=(T // tq,),
        in_specs=in_specs,
        out_specs=pl.BlockSpec((tq, D), lambda i: (first + i, 0)),
        out_shape=jax.ShapeDtypeStruct((T_all, D), F32),
        input_output_aliases=aliases,
        compiler_params=pltpu.CompilerParams(
            dimension_semantics=("arbitrary",), vmem_limit_bytes=VMEM_LIMIT_BYTES),
        name="moe_combine_final",
    )(*args)


def _group_layout(counts, n_tiles, tm):
    padded = ((counts + tm - 1) // tm) * tm
    ends = jnp.cumsum(padded)
    starts = ends - padded
    tile_row = jnp.arange(n_tiles, dtype=I32) * tm
    te = jnp.minimum(jnp.sum(tile_row[:, None] >= ends[None, :], axis=1), N_EXPERTS - 1).astype(I32)
    eids = jnp.arange(N_EXPERTS, dtype=I32)
    mine = te[:, None] == eids[None, :]
    lookup = lambda table: jnp.sum(jnp.where(mine, table[None, :], 0), axis=1)
    tv = jnp.clip(lookup(counts) - (tile_row - lookup(starts)), 0, tm).astype(I32)
    later = (eids[None, :] > eids[:, None]) & (counts[None, :] > 0)
    nxt = jnp.min(jnp.where(later, eids[None, :], N_EXPERTS), axis=1)
    nx = lookup(jnp.where(nxt < N_EXPERTS, nxt, -1)).astype(I32)
    return starts, te, tv, nx


def kernel(x, c, positions, w_ada, b_ada, g_mix, w_in, w_dw, b_dw, g_conv_ln, b_conv_ln, g_ret_norm,
           w_out, g_ffn, w_router, b_router, w_gu, b_gu, w_down, b_down, g_final):
    B, S, D = x.shape
    T = B * S
    assert w_ada.shape[0] == 1, "single-layer block: the final norm directly follows layer 0"
    xt = x
    for l in range(1):
        mod = _mod_call(c, w_ada[l], b_ada[l]).reshape(B, N_MOD, D)
        n_groups = MOE_TOKEN_GROUPS if B % MOE_TOKEN_GROUPS == 0 else 1
        Bg = B // n_groups
        Tg = Bg * S
        tm = FFN_TILE_ROWS if Tg * TOP_K >= FFN_TILE_ROWS * N_EXPERTS * 4 else FFN_SUB_ROWS
        n_tiles = (Tg * TOP_K) // tm + N_EXPERTS
        x1s, scattered = [], []
        pos_km = jnp.zeros((TOP_K * Tg,), I32)
        for g in range(n_groups):
            x1, h2p, idx, wts, rank, counts = _mixer_call(
                xt, positions, mod, g_mix[l], w_in[l], w_dw[l], b_dw[l], g_conv_ln[l], b_conv_ln[l],
                g_ret_norm[l], w_out[l], g_ffn[l], w_router[l], b_router[l], g, n_groups, pos_km)
            x1 = x1.reshape(Tg, D)
            starts, te, tv, nx = _group_layout(counts[:, 0], n_tiles, tm)
            pos_km = rank
            for e in range(N_EXPERTS):
                pos_km = pos_km + jnp.where(idx == e, starts[e], 0)
            pos_km = pos_km[:TOP_K].astype(I32).reshape(-1)
            x1s.append(x1)
            scattered.append((_sc_scatter_call(h2p, pos_km, n_tiles * tm), pos_km, wts, te, tv, nx))
        out = None
        for g, (xs, pos_km, wts, te, tv, nx) in enumerate(scattered):
            ys = _ffn_call(te, tv, nx, xs, w_gu[l], b_gu[l], w_down[l], b_down[l], tm)
            yp = _sc_gather_call(ys, pos_km)
            out = _final_call(x1s[g], yp.reshape(TOP_K, Tg, D // 2), wts, mod, g_final, S, g, n_groups, out)
        xt = out
    return xt.reshape(B, S, D)
```

```python
import functools

import numpy as np
import jax
import jax.numpy as jnp
from jax import lax
from jax.experimental import pallas as pl
from jax.experimental.pallas import tpu as pltpu
from jax.experimental.pallas import tpu_sc as plsc

F32 = jnp.float32
BF16 = jnp.bfloat16
U32 = jnp.uint32
I32 = jnp.int32

CONV_WIDTH = 31
CONV_HALO = 32
RET_HEADS = 4
RET_DIM = 128
RET_CHUNK = 128
ROPE_BASE = 10000.0
N_EXPERTS = 32
TOP_K = 4
SWIGLU_LIMIT = 7.0
SWIGLU_ALPHA = 1.702
RMS_EPS = 1e-6
LN_EPS = 1e-5
N_MOD = 6

VMEM_LIMIT_BYTES = 56 * 1024 * 1024
MOE_TOKEN_GROUPS = 2
FFN_TILE_ROWS = 1024
FFN_SUB_ROWS = 256


def _split_bf16(a):
    hi = a.astype(BF16)
    lo = (a - hi.astype(F32)).astype(BF16)
    return hi, lo


def _dot3(a, b_hi, b_lo):
    a_hi, a_lo = _split_bf16(a)
    d = functools.partial(jnp.dot, preferred_element_type=F32)
    return d(a_hi, b_hi) + (d(a_hi, b_lo) + d(a_lo, b_hi))


def _pack_bf16_pair(lo, hi):
    lo_bits = lax.bitcast_convert_type(lo.astype(BF16).astype(F32), U32)
    hi_bits = lax.bitcast_convert_type(hi.astype(BF16).astype(F32), U32)
    return (lo_bits >> 16) | (hi_bits & jnp.uint32(0xFFFF0000))


def _unpack_bf16_pair(p):
    lo = lax.bitcast_convert_type(p << 16, F32)
    hi = lax.bitcast_convert_type(p & jnp.uint32(0xFFFF0000), F32)
    return lo, hi


def _mod_kernel(c_ref, whi_ref, wlo_ref, b_ref, o_ref):
    c = c_ref[...]
    c_act = c * jax.nn.sigmoid(c)
    o_ref[...] = _dot3(c_act, whi_ref[...], wlo_ref[...]) + b_ref[...]


def _mod_call(c, w_ada, b_ada):
    B, D = c.shape
    n = w_ada.shape[1]
    bn = 1024
    w_hi, w_lo = _split_bf16(w_ada)
    return pl.pallas_call(
        _mod_kernel,
        grid=(n // bn,),
        in_specs=[
            pl.BlockSpec((B, D), lambda j: (0, 0)),
            pl.BlockSpec((D, bn), lambda j: (0, j)),
            pl.BlockSpec((D, bn), lambda j: (0, j)),
            pl.BlockSpec((1, bn), lambda j: (0, j)),
        ],
        out_specs=pl.BlockSpec((B, bn), lambda j: (0, j)),
        out_shape=jax.ShapeDtypeStruct((B, n), F32),
        name="adaln_mod",
    )(c, w_hi, w_lo, b_ada.reshape(1, n))


def _retention_tables():
    h = np.arange(RET_HEADS, dtype=np.float32)
    log_gamma = np.log(1.0 - np.power(2.0, -5.0 - h)).astype(np.float32)
    idx = np.arange(RET_CHUNK, dtype=np.float32)
    diff = idx[:, None] - idx[None, :]
    causal = diff >= 0
    mask = np.where(causal[None], np.exp(log_gamma[:, None, None] * np.where(causal, diff, 0.0)[None]), 0.0)
    q_decay = np.exp(log_gamma[:, None] * (idx + 1.0))[..., None]
    k_decay = np.exp(log_gamma[:, None] * (RET_CHUNK - 1.0 - idx))[..., None]
    chunk_decay = np.exp(log_gamma * RET_CHUNK)
    return (mask.astype(np.float32), q_decay.astype(np.float32), k_decay.astype(np.float32),
            [float(v) for v in chunk_decay.astype(np.float32)])


def _mixer_kernel(chunk_decay, ts, x_ref, pos_ref, mod_ref, gmix_ref, win_ref, wdw_ref, bdw_ref,
                  gcl_ref, bcl_ref, gret_ref, wout_ref, rope_ref, dmask_ref, qdec_ref, kdec_ref,
                  gffn_ref, wr2_ref, wrhi_ref, br_ref, utri_ref, after_ref,
                  o_ref, h2p_ref, idx_ref, wts_ref, rank_ref, cnt_ref,
                  proj_ref, uext_ref, state_ref, cat_ref, conv_ref, carry_ref):
    del after_ref
    s = pl.program_id(1)
    conv_ch = wdw_ref.shape[1]
    ret_w = RET_HEADS * RET_DIM

    @pl.when(s == 0)
    def _():
        uext_ref[0:CONV_HALO, :] = jnp.zeros((CONV_HALO, conv_ch), F32)
        state_ref[...] = jnp.zeros_like(state_ref)

    x = x_ref[...]
    row_gain = gmix_ref[...] * (1.0 + mod_ref[1:2, :])
    h = (x * lax.rsqrt(jnp.mean(x * x, axis=-1, keepdims=True) + RMS_EPS) * row_gain + mod_ref[0:1, :]).astype(BF16)
    proj_ref[...] = jnp.dot(h, win_ref[...], preferred_element_type=F32)

    a = proj_ref[:, 0:conv_ch]
    b = proj_ref[:, conv_ch:2 * conv_ch]
    uext_ref[CONV_HALO:CONV_HALO + ts, :] = a * jax.nn.sigmoid(b)
    cb = 64
    lead = CONV_HALO - (CONV_WIDTH - 1)
    span = cb + CONV_HALO

    def conv_block(r0):
        for c0 in range(0, conv_ch, 128):
            xw = uext_ref[r0:r0 + span, c0:c0 + 128]
            acc = jnp.zeros((cb, 128), F32) + bdw_ref[:, c0:c0 + 128]
            for r in range(8):
                xr = xw if r == 0 else pltpu.roll(xw, span - r, 0)
                for q in range((lead + CONV_WIDTH - 1 - r) // 8 + 1):
                    j = 8 * q + r - lead
                    if 0 <= j < CONV_WIDTH:
                        acc = acc + wdw_ref[j:j + 1, c0:c0 + 128] * xr[8 * q:8 * q + cb]
            conv_ref[r0:r0 + cb, c0:c0 + 128] = acc
    rb = 2 * cb

    def conv_norm(r0):
        acc = conv_ref[r0:r0 + rb, 0:conv_ch]
        mu = jnp.mean(acc, axis=-1, keepdims=True)
        d = acc - mu
        var = jnp.mean(d * d, axis=-1, keepdims=True)
        ln = d * lax.rsqrt(var + LN_EPS) * gcl_ref[...] + bcl_ref[...]
        cat_ref[r0:r0 + rb, 0:conv_ch] = (ln * jax.nn.sigmoid(ln)).astype(BF16)

    hts = ts // 2
    low = lax.broadcasted_iota(I32, (hts, RET_DIM), 1) < RET_DIM // 2
    posf = jnp.where(low, pos_ref[0:hts, :].astype(F32), pos_ref[hts:ts, :].astype(F32))
    ang = posf * rope_ref[...]
    cos_p = jnp.cos(ang)
    sin_p = jnp.sin(ang)
    cos_s = pltpu.roll(cos_p, RET_DIM // 2, 1)
    sin_s = pltpu.roll(sin_p, RET_DIM // 2, 1)
    cos2 = jnp.concatenate([jnp.where(low, cos_p, cos_s), jnp.where(low, cos_s, cos_p)], axis=0)
    sin2 = jnp.concatenate([jnp.where(low, -sin_p, sin_s), jnp.where(low, -sin_s, sin_p)], axis=0)
    q0 = 2 * conv_ch
    k0 = q0 + ret_w
    v0 = k0 + ret_w
    g0 = v0 + ret_w

    def retention_unit(hd, n):
        c0 = hd * RET_DIM
        r0 = n * RET_CHUNK
        rows = slice(r0, r0 + RET_CHUNK)
        cs = cos2[rows]
        sn = sin2[rows]
        q = proj_ref[rows, q0 + c0:q0 + c0 + RET_DIM]
        k = proj_ref[rows, k0 + c0:k0 + c0 + RET_DIM]
        v = proj_ref[rows, v0 + c0:v0 + c0 + RET_DIM].astype(BF16)
        g = proj_ref[rows, g0 + c0:g0 + c0 + RET_DIM]
        qr = q * cs + pltpu.roll(q, RET_DIM // 2, 1) * sn
        kr = k * cs + pltpu.roll(k, RET_DIM // 2, 1) * sn
        st = state_ref[hd]
        scores = lax.dot_general(qr.astype(BF16), kr.astype(BF16), (((1,), (1,)), ((), ())),
                                 preferred_element_type=F32) * dmask_ref[hd]
        inner = jnp.dot(scores.astype(BF16), v, preferred_element_type=F32)
        cross = jnp.dot((qr * qdec_ref[hd]).astype(BF16), st.astype(BF16), preferred_element_type=F32)
        kv = lax.dot_general((kr * kdec_ref[hd]).astype(BF16), v, (((0,), (0,)), ((), ())),
                             preferred_element_type=F32)
        state_ref[hd] = chunk_decay[hd] * st + kv
        r = inner + cross
        mu = jnp.mean(r, axis=-1, keepdims=True)
        d = r - mu
        var = jnp.mean(d * d, axis=-1, keepdims=True)
        rn = d * lax.rsqrt(var + LN_EPS) * gret_ref[:, c0:c0 + RET_DIM]
        cat_ref[rows, conv_ch + c0:conv_ch + c0 + RET_DIM] = (g * jax.nn.sigmoid(g) * rn).astype(BF16)

    units = [(hd, n) for n in range(ts // RET_CHUNK) for hd in range(RET_HEADS)]
    n_conv = ts // cb
    per = -(-len(units) // n_conv)
    for i in range(n_conv):
        conv_block(i * cb)
        if i % 2 == 1:
            conv_norm((i - 1) * cb)
        for hd, n in units[i * per:(i + 1) * per]:
            retention_unit(hd, n)
    uext_ref[0:CONV_HALO, :] = uext_ref[ts:ts + CONV_HALO, :]

    out = jnp.dot(cat_ref[...], wout_ref[...], preferred_element_type=F32)
    x1 = x + mod_ref[2:3, :] * out
    o_ref[...] = x1

    @pl.when((pl.program_id(0) == 0) & (s == 0))
    def _():
        carry_ref[...] = jnp.zeros_like(carry_ref)

    _route_tile(x1, mod_ref, gffn_ref, wr2_ref, wrhi_ref, br_ref, utri_ref,
                h2p_ref, idx_ref, wts_ref, rank_ref, cnt_ref, carry_ref)


def _mixer_call(x, positions, mod, g_mix, w_in, w_dw, b_dw, g_conv_ln, b_conv_ln, g_ret_norm, w_out,
                g_ffn, w_router, b_router, group, n_groups, after):
    S, D = x.shape[1:]
    B = x.shape[0] // n_groups
    b0 = group * B
    in_cols = w_in.shape[1]
    conv_ch = w_dw.shape[1]
    ts = min(512, S)
    mask, q_decay, k_decay, chunk_decay = _retention_tables()
    key_scale = np.float32(RET_DIM ** -0.5)
    half = RET_DIM // 2
    inv_freq = (ROPE_BASE ** (-np.arange(half, dtype=np.float32) / half)).astype(np.float32)
    rope = np.concatenate([inv_freq, inv_freq])[None, :]
    w_dw_p = jnp.zeros((CONV_HALO, conv_ch), F32).at[:CONV_WIDTH].set(w_dw)
    wr_hi, wr_lo = _split_bf16(w_router.T)
    utri = jnp.asarray(np.triu(np.ones((ts, ts), np.float32), 1), BF16)
    n_s = S // ts
    T = B * S
    full = lambda shape: pl.BlockSpec(shape, lambda b, s: (0,) * len(shape))
    return pl.pallas_call(
        functools.partial(_mixer_kernel, chunk_decay, ts),
        grid=(B, S // ts),
        in_specs=[
            pl.BlockSpec((None, ts, D), lambda b, s: (b0 + b, s, 0)),
            pl.BlockSpec((None, ts, 1), lambda b, s: (b0 + b, s, 0)),
            pl.BlockSpec((None, N_MOD, D), lambda b, s: (b0 + b, 0, 0)),
            full((1, D)),
            full((D, in_cols)),
            full((CONV_HALO, conv_ch)),
            full((1, conv_ch)),
            full((1, conv_ch)),
            full((1, conv_ch)),
            full((1, RET_HEADS * RET_DIM)),
            full((conv_ch + RET_HEADS * RET_DIM, D)),
            full((1, RET_DIM)),
            full((RET_HEADS, RET_CHUNK, RET_CHUNK)),
            full((RET_HEADS, RET_CHUNK, 1)),
            full((RET_HEADS, RET_CHUNK, 1)),
            full((1, D)),
            full((2 * N_EXPERTS, D)),
            full((N_EXPERTS, D)),
            full((N_EXPERTS, 1)),
            full((ts, ts)),
            pl.BlockSpec(memory_space=pl.ANY),
        ],
        out_specs=[
            pl.BlockSpec((None, ts, D), lambda b, s: (b, s, 0)),
            pl.BlockSpec((ts, D // 2), lambda b, s: (b * n_s + s, 0)),
            pl.BlockSpec((ROUTE_ROWS, ts), lambda b, s: (0, b * n_s + s)),
            pl.BlockSpec((ts, ROUTE_LANES), lambda b, s: (b * n_s + s, 0)),
            pl.BlockSpec((ROUTE_ROWS, ts), lambda b, s: (0, b * n_s + s)),
            full((N_EXPERTS, 1)),
        ],
        out_shape=[
            jax.ShapeDtypeStruct((B, S, D), F32),
            jax.ShapeDtypeStruct((T, D // 2), U32),
            jax.ShapeDtypeStruct((ROUTE_ROWS, T), I32),
            jax.ShapeDtypeStruct((T, ROUTE_LANES), F32),
            jax.ShapeDtypeStruct((ROUTE_ROWS, T), I32),
            jax.ShapeDtypeStruct((N_EXPERTS, 1), I32),
        ],
        scratch_shapes=[
            pltpu.VMEM((ts, in_cols), F32),
            pltpu.VMEM((CONV_HALO + ts, conv_ch), F32),
            pltpu.VMEM((RET_HEADS, RET_DIM, RET_DIM), F32),
            pltpu.VMEM((ts, conv_ch + RET_HEADS * RET_DIM), BF16),
            pltpu.VMEM((ts, conv_ch), F32),
            pltpu.VMEM((N_EXPERTS, 1), F32),
        ],
        compiler_params=pltpu.CompilerParams(
            dimension_semantics=("arbitrary", "arbitrary"), vmem_limit_bytes=VMEM_LIMIT_BYTES),
        name="hybrid_mixer",
    )(x, positions.reshape(-1, S, 1), mod, g_mix.reshape(1, D), w_in.astype(BF16), w_dw_p,
      b_dw.reshape(1, -1), g_conv_ln.reshape(1, -1), b_conv_ln.reshape(1, -1), g_ret_norm.reshape(1, -1),
      w_out.astype(BF16), jnp.asarray(rope), jnp.asarray(mask * key_scale), jnp.asarray(q_decay),
      jnp.asarray(k_decay * key_scale), g_ffn.reshape(1, D), jnp.concatenate([wr_hi, wr_lo], axis=0), wr_hi,
      b_router.reshape(N_EXPERTS, 1), utri, after)


def _route_tile(x, mod_ref, gffn_ref, wr2_ref, wrhi_ref, br_ref, utri_ref,
                h2p_ref, idx_ref, wts_ref, rank_ref, cnt_ref, carry_ref):
    tr = x.shape[0]
    half = x.shape[1] // 2
    y = x * lax.rsqrt(jnp.mean(x * x, axis=-1, keepdims=True) + RMS_EPS) * gffn_ref[...]
    h2 = y * (1.0 + mod_ref[4:5, :]) + mod_ref[3:4, :]
    h2p_ref[...] = _pack_bf16_pair(h2[:, :half], h2[:, half:])

    h_hi, h_lo = _split_bf16(h2)
    nt = (((1,), (1,)), ((), ()))
    r = lax.dot_general(wr2_ref[...], h_hi, nt, preferred_element_type=F32)
    r2 = lax.dot_general(wrhi_ref[...], h_lo, nt, preferred_element_type=F32)
    l = r[:N_EXPERTS] + (r[N_EXPERTS:] + r2) + br_ref[...]
    eid = lax.broadcasted_iota(I32, l.shape, 0)
    vals, sels, idxs = [], [], []
    for _ in range(TOP_K):
        m = jnp.max(l, axis=0, keepdims=True)
        ik = jnp.min(jnp.where(l == m, eid, N_EXPERTS), axis=0, keepdims=True)
        sel = eid == ik
        vals.append(m)
        sels.append(sel)
        idxs.append(ik)
        l = jnp.where(sel, -jnp.inf, l)
    exps = [jnp.exp(v - vals[0]) for v in vals]
    denom = exps[0] + exps[1] + exps[2] + exps[3]
    member = jnp.zeros(l.shape, F32)
    for sel in sels:
        member = member + sel.astype(F32)
    before = jnp.dot(member.astype(BF16), utri_ref[...], preferred_element_type=F32) + carry_ref[...]
    ranks = [jnp.sum(jnp.where(sel, before, 0.0), axis=0, keepdims=True) for sel in sels]
    carry_ref[...] = carry_ref[...] + jnp.sum(member, axis=1, keepdims=True)
    cnt_ref[...] = carry_ref[...].astype(I32)

    def rows(pieces, n_rows, dtype):
        rid = lax.broadcasted_iota(I32, (n_rows, tr), 0)
        out = jnp.zeros((n_rows, tr), dtype)
        for k, p in enumerate(pieces):
            out = jnp.where(rid == k, p.astype(dtype), out)
        return out

    idx_ref[...] = rows(idxs, idx_ref.shape[0], I32)
    rank_ref[...] = rows(ranks, rank_ref.shape[0], I32)
    wts_ref[...] = rows([e / denom for e in exps], wts_ref.shape[1], F32).T


ROUTE_ROWS = 8
ROUTE_LANES = 128


SC_ROWS = 128


V7X_SC_CORES = 2
V7X_SC_SUBCORES = 16


def _sc_workers():
    return V7X_SC_CORES, V7X_SC_SUBCORES


def _sc_mesh():
    return plsc.VectorSubcoreMesh(core_axis_name="c", subcore_axis_name="s",
                                  num_cores=V7X_SC_CORES, num_subcores=V7X_SC_SUBCORES)


def _sc_scatter_call(h2p, pos_km, n_rows):
    T, W = h2p.shape
    nc, ns = _sc_workers()
    n = SC_ROWS
    per_w = T // (nc * ns * n)

    def body(h2p_hbm, pos_hbm, xs_hbm, i0, i1, i2, i3, rows_v, sem):
        wid = lax.axis_index("s") * nc + lax.axis_index("c")
        idx_refs = (i0, i1, i2, i3)

        @pl.loop(0, per_w)
        def _(j):
            t0 = (wid * per_w + j) * n
            pltpu.sync_copy(h2p_hbm.at[pl.ds(t0, n)], rows_v)
            for k in range(TOP_K):
                pltpu.sync_copy(pos_hbm.at[pl.ds(k * T + t0, n)], idx_refs[k])
            copies = [pltpu.async_copy(rows_v, xs_hbm.at[idx_refs[k]], sem) for k in range(TOP_K)]
            for cp in copies:
                cp.wait()

    return pl.kernel(
        body,
        out_type=jax.ShapeDtypeStruct((n_rows, W), h2p.dtype),
        mesh=_sc_mesh(),
        scratch_types=[pltpu.VMEM((n,), I32)] * TOP_K + [pltpu.VMEM((n, W), h2p.dtype), pltpu.SemaphoreType.DMA],
        name="moe_scatter_rows",
    )(h2p, pos_km)


def _sc_gather_call(ys, pos_km):
    P = pos_km.shape[0]
    W = ys.shape[1]
    nc, ns = _sc_workers()
    n = SC_ROWS
    per_w = P // (nc * ns * n)

    def body(ys_hbm, pos_hbm, yp_hbm, idx_v, rows_v, sem):
        wid = lax.axis_index("s") * nc + lax.axis_index("c")

        @pl.loop(0, per_w)
        def _(j):
            p0 = (wid * per_w + j) * n
            pltpu.sync_copy(pos_hbm.at[pl.ds(p0, n)], idx_v)
            pltpu.async_copy(ys_hbm.at[idx_v], rows_v, sem).wait()
            pltpu.sync_copy(rows_v, yp_hbm.at[pl.ds(p0, n)])

    return pl.kernel(
        body,
        out_type=jax.ShapeDtypeStruct((P, W), ys.dtype),
        mesh=_sc_mesh(),
        scratch_types=[pltpu.VMEM((n,), I32), pltpu.VMEM((n, W), ys.dtype), pltpu.SemaphoreType.DMA],
        name="moe_gather_rows",
    )(ys, pos_km)


def _ffn_kernel(te_ref, tv_ref, nx_ref, nu_ref, xs_ref, wgu_hbm, bgu_ref, wd_hbm, bd_ref, y_ref,
                wgu_f32, wd_f32, wgu_bf, wd_bf, sems):
    del nu_ref
    i = pl.program_id(0)
    valid = tv_ref[i]
    tm, half = xs_ref.shape
    d_exp = wd_bf.shape[0]

    def weight_copies(e):
        return (pltpu.make_async_copy(wgu_hbm.at[e], wgu_f32, sems.at[0]),
                pltpu.make_async_copy(wd_hbm.at[e], wd_f32, sems.at[1]))

    @pl.when(i == 0)
    def _():
        for cp in weight_copies(te_ref[0]):
            cp.start(priority=1)

    @pl.when(((i == 0) | (te_ref[i] != te_ref[jnp.maximum(i - 1, 0)])) & (valid > 0))
    def _():
        for cp in weight_copies(te_ref[i]):
            cp.wait()
        wgu_bf[...] = wgu_f32[...].astype(BF16)
        wd_bf[...] = wd_f32[...].astype(BF16)

        @pl.when(nx_ref[i] >= 0)
        def _():
            for cp in weight_copies(nx_ref[i]):
                cp.start(priority=1)

    def sub_tile(r0, n_rows, masked):
        rows = slice(r0, r0 + n_rows)
        lo, hi = _unpack_bf16_pair(xs_ref[rows, :])
        if masked:
            keep = lax.broadcasted_iota(I32, (n_rows, half), 0) < valid - r0
            lo = jnp.where(keep, lo, 0.0)
            hi = jnp.where(keep, hi, 0.0)
        xt = jnp.concatenate([lo.astype(BF16), hi.astype(BF16)], axis=1)
        d = functools.partial(jnp.dot, preferred_element_type=F32)
        nb = 256

        def up(c):
            gate = d(xt, wgu_bf[:, c:c + nb]) + bgu_ref[:, c:c + nb]
            lin = d(xt, wgu_bf[:, d_exp + c:d_exp + c + nb]) + bgu_ref[:, d_exp + c:d_exp + c + nb]
            return gate, lin

        def activate(gate, lin):
            gate = jnp.minimum(gate, SWIGLU_LIMIT)
            lin = jnp.clip(lin, -SWIGLU_LIMIT, SWIGLU_LIMIT)
            return (gate * jax.nn.sigmoid(SWIGLU_ALPHA * gate) * (lin + 1.0)).astype(BF16)

        chunks = list(range(0, d_exp, nb))
        acts = []
        pending = up(chunks[0])
        for n in range(len(chunks)):
            nxt = up(chunks[n + 1]) if n + 1 < len(chunks) else None
            acts.append(activate(*pending))
            pending = nxt
        out = d(jnp.concatenate(acts, axis=1), wd_bf[...]) + bd_ref[...]
        y_ref[rows, :] = _pack_bf16_pair(out[:, :half], out[:, half:])

    pl.when(valid == tm)(functools.partial(sub_tile, 0, tm, False))
    for n_sub in range(1, tm // FFN_SUB_ROWS + 1):
        n_rows = n_sub * FFN_SUB_ROWS
        pl.when((valid > n_rows - FFN_SUB_ROWS) & (valid <= n_rows) & (valid < tm))(
            functools.partial(sub_tile, 0, n_rows, True))


def _ffn_call(tile_expert, tile_valid, tile_next, n_used, xs, w_gu, b_gu, w_down, b_down, tm):
    R, half = xs.shape
    E, D, two_f = w_gu.shape
    d_exp = w_down.shape[1]
    rows = lambda i, te, tv, nx, nu: (jnp.minimum(i, nu[0] - 1), 0)
    grid_spec = pltpu.PrefetchScalarGridSpec(
        num_scalar_prefetch=4,
        grid=(R // tm,),
        in_specs=[
            pl.BlockSpec((tm, half), rows),
            pl.BlockSpec(memory_space=pl.ANY),
            pl.BlockSpec((None, 1, two_f), lambda i, te, tv, nx, nu: (te[i], 0, 0)),
            pl.BlockSpec(memory_space=pl.ANY),
            pl.BlockSpec((None, 1, D), lambda i, te, tv, nx, nu: (te[i], 0, 0)),
        ],
        out_specs=pl.BlockSpec((tm, half), rows),
        scratch_shapes=[
            pltpu.VMEM((D, two_f), F32), pltpu.VMEM((d_exp, D), F32),
            pltpu.VMEM((D, two_f), BF16), pltpu.VMEM((d_exp, D), BF16),
            pltpu.SemaphoreType.DMA((2,)),
        ],
    )
    return pl.pallas_call(
        _ffn_kernel,
        grid_spec=grid_spec,
        out_shape=jax.ShapeDtypeStruct((R, half), U32),
        compiler_params=pltpu.CompilerParams(
            dimension_semantics=("arbitrary",), vmem_limit_bytes=VMEM_LIMIT_BYTES),
        name="moe_ffn",
    )(tile_expert, tile_valid, tile_next, n_used, xs, w_gu, b_gu.reshape(E, 1, two_f), w_down,
      b_down.reshape(E, 1, D))


def _final_kernel(x_ref, yp_ref, wts_ref, mod_ref, gfin_ref, *rest):
    o_ref = rest[-1]
    x = x_ref[...]
    half = x.shape[1] // 2
    w = wts_ref[...]
    lo = jnp.zeros((x.shape[0], half), F32)
    hi = jnp.zeros((x.shape[0], half), F32)
    for k in range(TOP_K):
        l, h = _unpack_bf16_pair(yp_ref[k])
        lo = lo + w[:, k:k + 1] * l
        hi = hi + w[:, k:k + 1] * h
    gate = mod_ref[5:6, :]
    x_lo = x[:, :half] + gate[:, :half] * lo
    x_hi = x[:, half:] + gate[:, half:] * hi
    ms = (jnp.sum(x_lo * x_lo, axis=-1, keepdims=True) + jnp.sum(x_hi * x_hi, axis=-1, keepdims=True)) / x.shape[1]
    inv = lax.rsqrt(ms + RMS_EPS)
    o_ref[:, :half] = x_lo * inv * gfin_ref[:, :half]
    o_ref[:, half:] = x_hi * inv * gfin_ref[:, half:]


def _final_call(x1, yp, wts, mod, g_final, tokens_per_batch, group, n_groups, prev_out):
    T, D = x1.shape
    T_all = T * n_groups
    tq = min(512, tokens_per_batch)
    per_b = tokens_per_batch // tq
    first = group * (T // tq)
    in_specs = [
        pl.BlockSpec((tq, D), lambda i: (i, 0)),
        pl.BlockSpec((TOP_K, tq, D // 2), lambda i: (0, i, 0)),
        pl.BlockSpec((tq, ROUTE_LANES), lambda i: (i, 0)),
        pl.BlockSpec((None, N_MOD, D), lambda i: ((first + i) // per_b, 0, 0)),
        pl.BlockSpec((1, D), lambda i: (0, 0)),
    ]
    args = [x1, yp, wts, mod, g_final.reshape(1, D)]
    aliases = {}
    if prev_out is not None:
        in_specs.append(pl.BlockSpec(memory_space=pl.ANY))
        args.append(prev_out)
        aliases = {len(args) - 1: 0}
    return pl.pallas_call(
        _final_kernel,
        grid=(T // tq,),
        in_specs=in_specs,
        out_specs=pl.BlockSpec((tq, D), lambda i: (first + i, 0)),
        out_shape=jax.ShapeDtypeStruct((T_all, D), F32),
        input_output_aliases=aliases,
        compiler_params=pltpu.CompilerParams(
            dimension_semantics=("arbitrary",), vmem_limit_bytes=VMEM_LIMIT_BYTES),
        name="moe_combine_final",
    )(*args)


def _group_layout(counts, n_tiles, tm):
    padded = ((counts + tm - 1) // tm) * tm
    ends = jnp.cumsum(padded)
    starts = ends - padded
    tile_row = jnp.arange(n_tiles, dtype=I32) * tm
    te = jnp.minimum(jnp.sum(tile_row[:, None] >= ends[None, :], axis=1), N_EXPERTS - 1).astype(I32)
    eids = jnp.arange(N_EXPERTS, dtype=I32)
    mine = te[:, None] == eids[None, :]
    lookup = lambda table: jnp.sum(jnp.where(mine, table[None, :], 0), axis=1)
    tv = jnp.clip(lookup(counts) - (tile_row - lookup(starts)), 0, tm).astype(I32)
    later = (eids[None, :] > eids[:, None]) & (counts[None, :] > 0)
    nxt = jnp.min(jnp.where(later, eids[None, :], N_EXPERTS), axis=1)
    nx = lookup(jnp.where(nxt < N_EXPERTS, nxt, -1)).astype(I32)
    n_used = jnp.sum(tv > 0).astype(I32).reshape(1)
    return starts, te, tv, nx, n_used


def kernel(x, c, positions, w_ada, b_ada, g_mix, w_in, w_dw, b_dw, g_conv_ln, b_conv_ln, g_ret_norm,
           w_out, g_ffn, w_router, b_router, w_gu, b_gu, w_down, b_down, g_final):
    B, S, D = x.shape
    T = B * S
    assert w_ada.shape[0] == 1, "single-layer block: the final norm directly follows layer 0"
    xt = x
    for l in range(1):
        mod = _mod_call(c, w_ada[l], b_ada[l]).reshape(B, N_MOD, D)
        n_groups = MOE_TOKEN_GROUPS if B % MOE_TOKEN_GROUPS == 0 else 1
        Bg = B // n_groups
        Tg = Bg * S
        tm = FFN_TILE_ROWS if Tg * TOP_K >= FFN_TILE_ROWS * N_EXPERTS * 4 else FFN_SUB_ROWS
        n_tiles = (Tg * TOP_K) // tm + N_EXPERTS
        x1s, scattered = [], []
        pos_km = jnp.zeros((TOP_K * Tg,), I32)
        for g in range(n_groups):
            x1, h2p, idx, wts, rank, counts = _mixer_call(
                xt, positions, mod, g_mix[l], w_in[l], w_dw[l], b_dw[l], g_conv_ln[l], b_conv_ln[l],
                g_ret_norm[l], w_out[l], g_ffn[l], w_router[l], b_router[l], g, n_groups, pos_km)
            x1 = x1.reshape(Tg, D)
            starts, te, tv, nx, n_used = _group_layout(counts[:, 0], n_tiles, tm)
            pos_km = rank
            for e in range(N_EXPERTS):
                pos_km = pos_km + jnp.where(idx == e, starts[e], 0)
            pos_km = pos_km[:TOP_K].astype(I32).reshape(-1)
            x1s.append(x1)
            scattered.append((_sc_scatter_call(h2p, pos_km, n_tiles * tm), pos_km, wts, te, tv, nx, n_used))
        out = None
        for g, (xs, pos_km, wts, te, tv, nx, n_used) in enumerate(scattered):
            ys = _ffn_call(te, tv, nx, n_used, xs, w_gu[l], b_gu[l], w_down[l], b_down[l], tm)
            yp = _sc_gather_call(ys, pos_km)
            out = _final_call(x1s[g], yp.reshape(TOP_K, Tg, D // 2), wts, mod, g_final, S, g, n_groups, out)
        xt = out
    return xt.reshape(B, S, D)
```

```python
import functools

import numpy as np
import jax
import jax.numpy as jnp
from jax import lax
from jax.experimental import pallas as pl
from jax.experimental.pallas import tpu as pltpu
from jax.experimental.pallas import tpu_sc as plsc

F32 = jnp.float32
BF16 = jnp.bfloat16
U32 = jnp.uint32
I32 = jnp.int32

CONV_WIDTH = 31
CONV_HALO = 32
RET_HEADS = 4
RET_DIM = 128
RET_CHUNK = 128
ROPE_BASE = 10000.0
N_EXPERTS = 32
TOP_K = 4
SWIGLU_LIMIT = 7.0
SWIGLU_ALPHA = 1.702
RMS_EPS = 1e-6
LN_EPS = 1e-5
N_MOD = 6

VMEM_LIMIT_BYTES = 56 * 1024 * 1024
MOE_TOKEN_GROUPS = 2
FFN_TILE_ROWS = 1024
FFN_SUB_ROWS = 256


def _split_bf16(a):
    hi = a.astype(BF16)
    lo = (a - hi.astype(F32)).astype(BF16)
    return hi, lo


def _dot3(a, b_hi, b_lo):
    a_hi, a_lo = _split_bf16(a)
    d = functools.partial(jnp.dot, preferred_element_type=F32)
    return d(a_hi, b_hi) + (d(a_hi, b_lo) + d(a_lo, b_hi))


def _pack_bf16_pair(lo, hi):
    lo_bits = lax.bitcast_convert_type(lo.astype(BF16).astype(F32), U32)
    hi_bits = lax.bitcast_convert_type(hi.astype(BF16).astype(F32), U32)
    return (lo_bits >> 16) | (hi_bits & jnp.uint32(0xFFFF0000))


def _unpack_bf16_pair(p):
    lo = lax.bitcast_convert_type(p << 16, F32)
    hi = lax.bitcast_convert_type(p & jnp.uint32(0xFFFF0000), F32)
    return lo, hi


def _mod_kernel(c_ref, whi_ref, wlo_ref, b_ref, o_ref):
    c = c_ref[...]
    c_act = c * jax.nn.sigmoid(c)
    o_ref[...] = _dot3(c_act, whi_ref[...], wlo_ref[...]) + b_ref[...]


def _mod_call(c, w_ada, b_ada):
    B, D = c.shape
    n = w_ada.shape[1]
    bn = 1024
    w_hi, w_lo = _split_bf16(w_ada)
    return pl.pallas_call(
        _mod_kernel,
        grid=(n // bn,),
        in_specs=[
            pl.BlockSpec((B, D), lambda j: (0, 0)),
            pl.BlockSpec((D, bn), lambda j: (0, j)),
            pl.BlockSpec((D, bn), lambda j: (0, j)),
            pl.BlockSpec((1, bn), lambda j: (0, j)),
        ],
        out_specs=pl.BlockSpec((B, bn), lambda j: (0, j)),
        out_shape=jax.ShapeDtypeStruct((B, n), F32),
        name="adaln_mod",
    )(c, w_hi, w_lo, b_ada.reshape(1, n))


def _retention_tables():
    h = np.arange(RET_HEADS, dtype=np.float32)
    log_gamma = np.log(1.0 - np.power(2.0, -5.0 - h)).astype(np.float32)
    idx = np.arange(RET_CHUNK, dtype=np.float32)
    diff = idx[:, None] - idx[None, :]
    causal = diff >= 0
    mask = np.where(causal[None], np.exp(log_gamma[:, None, None] * np.where(causal, diff, 0.0)[None]), 0.0)
    q_decay = np.exp(log_gamma[:, None] * (idx + 1.0))[..., None]
    k_decay = np.exp(log_gamma[:, None] * (RET_CHUNK - 1.0 - idx))[..., None]
    chunk_decay = np.exp(log_gamma * RET_CHUNK)
    return (mask.astype(np.float32), q_decay.astype(np.float32), k_decay.astype(np.float32),
            [float(v) for v in chunk_decay.astype(np.float32)])


def _mixer_kernel(chunk_decay, ts, x_ref, pos_ref, mod_ref, gmix_ref, win_ref, wdw_ref, bdw_ref,
                  gcl_ref, bcl_ref, gret_ref, wout_ref, rope_ref, dmask_ref, qdec_ref, kdec_ref,
                  gffn_ref, wr2_ref, wrhi_ref, br_ref, utri_ref, after_ref,
                  o_ref, h2p_ref, idx_ref, wts_ref, rank_ref, cnt_ref,
                  proj_ref, uext_ref, state_ref, cat_ref, conv_ref, carry_ref):
    del after_ref
    s = pl.program_id(1)
    conv_ch = wdw_ref.shape[1]
    ret_w = RET_HEADS * RET_DIM

    @pl.when(s == 0)
    def _():
        uext_ref[0:CONV_HALO, :] = jnp.zeros((CONV_HALO, conv_ch), F32)
        state_ref[...] = jnp.zeros_like(state_ref)

    x = x_ref[...]
    row_gain = gmix_ref[...] * (1.0 + mod_ref[1:2, :])
    h = (x * lax.rsqrt(jnp.mean(x * x, axis=-1, keepdims=True) + RMS_EPS) * row_gain + mod_ref[0:1, :]).astype(BF16)
    proj_ref[...] = jnp.dot(h, win_ref[...], preferred_element_type=F32)

    a = proj_ref[:, 0:conv_ch]
    b = proj_ref[:, conv_ch:2 * conv_ch]
    uext_ref[CONV_HALO:CONV_HALO + ts, :] = a * jax.nn.sigmoid(b)
    cb = 64
    lead = CONV_HALO - (CONV_WIDTH - 1)
    span = cb + CONV_HALO

    def conv_block(r0):
        for c0 in range(0, conv_ch, 128):
            xw = uext_ref[r0:r0 + span, c0:c0 + 128]
            acc = jnp.zeros((cb, 128), F32) + bdw_ref[:, c0:c0 + 128]
            for r in range(8):
                xr = xw if r == 0 else pltpu.roll(xw, span - r, 0)
                for q in range((lead + CONV_WIDTH - 1 - r) // 8 + 1):
                    j = 8 * q + r - lead
                    if 0 <= j < CONV_WIDTH:
                        acc = acc + wdw_ref[j:j + 1, c0:c0 + 128] * xr[8 * q:8 * q + cb]
            conv_ref[r0:r0 + cb, c0:c0 + 128] = acc
    rb = 2 * cb

    def conv_norm(r0):
        acc = conv_ref[r0:r0 + rb, 0:conv_ch]
        mu = jnp.mean(acc, axis=-1, keepdims=True)
        d = acc - mu
        var = jnp.mean(d * d, axis=-1, keepdims=True)
        ln = d * lax.rsqrt(var + LN_EPS) * gcl_ref[...] + bcl_ref[...]
        cat_ref[r0:r0 + rb, 0:conv_ch] = (ln * jax.nn.sigmoid(ln)).astype(BF16)

    hts = ts // 2
    low = lax.broadcasted_iota(I32, (hts, RET_DIM), 1) < RET_DIM // 2
    posf = jnp.where(low, pos_ref[0:hts, :].astype(F32), pos_ref[hts:ts, :].astype(F32))
    ang = posf * rope_ref[...]
    cos_p = jnp.cos(ang)
    sin_p = jnp.sin(ang)
    cos_s = pltpu.roll(cos_p, RET_DIM // 2, 1)
    sin_s = pltpu.roll(sin_p, RET_DIM // 2, 1)
    cos2 = jnp.concatenate([jnp.where(low, cos_p, cos_s), jnp.where(low, cos_s, cos_p)], axis=0)
    sin2 = jnp.concatenate([jnp.where(low, -sin_p, sin_s), jnp.where(low, -sin_s, sin_p)], axis=0)
    q0 = 2 * conv_ch
    k0 = q0 + ret_w
    v0 = k0 + ret_w
    g0 = v0 + ret_w

    def retention_unit(hd, n):
        c0 = hd * RET_DIM
        r0 = n * RET_CHUNK
        rows = slice(r0, r0 + RET_CHUNK)
        cs = cos2[rows]
        sn = sin2[rows]
        q = proj_ref[rows, q0 + c0:q0 + c0 + RET_DIM]
        k = proj_ref[rows, k0 + c0:k0 + c0 + RET_DIM]
        v = proj_ref[rows, v0 + c0:v0 + c0 + RET_DIM].astype(BF16)
        g = proj_ref[rows, g0 + c0:g0 + c0 + RET_DIM]
        qr = q * cs + pltpu.roll(q, RET_DIM // 2, 1) * sn
        kr = k * cs + pltpu.roll(k, RET_DIM // 2, 1) * sn
        st = state_ref[hd]
        scores = lax.dot_general(qr.astype(BF16), kr.astype(BF16), (((1,), (1,)), ((), ())),
                                 preferred_element_type=F32) * dmask_ref[hd]
        inner = jnp.dot(scores.astype(BF16), v, preferred_element_type=F32)
        cross = jnp.dot((qr * qdec_ref[hd]).astype(BF16), st.astype(BF16), preferred_element_type=F32)
        kv = lax.dot_general((kr * kdec_ref[hd]).astype(BF16), v, (((0,), (0,)), ((), ())),
                             preferred_element_type=F32)
        state_ref[hd] = chunk_decay[hd] * st + kv
        r = inner + cross
        mu = jnp.mean(r, axis=-1, keepdims=True)
        d = r - mu
        var = jnp.mean(d * d, axis=-1, keepdims=True)
        rn = d * lax.rsqrt(var + LN_EPS) * gret_ref[:, c0:c0 + RET_DIM]
        cat_ref[rows, conv_ch + c0:conv_ch + c0 + RET_DIM] = (g * jax.nn.sigmoid(g) * rn).astype(BF16)

    units = [(hd, n) for n in range(ts // RET_CHUNK) for hd in range(RET_HEADS)]
    n_conv = ts // cb
    per = -(-len(units) // n_conv)
    for i in range(n_conv):
        conv_block(i * cb)
        if i % 2 == 1:
            conv_norm((i - 1) * cb)
        for hd, n in units[i * per:(i + 1) * per]:
            retention_unit(hd, n)
    uext_ref[0:CONV_HALO, :] = uext_ref[ts:ts + CONV_HALO, :]

    out = jnp.dot(cat_ref[...], wout_ref[...], preferred_element_type=F32)
    x1 = x + mod_ref[2:3, :] * out
    o_ref[...] = x1

    @pl.when((pl.program_id(0) == 0) & (s == 0))
    def _():
        carry_ref[...] = jnp.zeros_like(carry_ref)

    _route_tile(x1, mod_ref, gffn_ref, wr2_ref, wrhi_ref, br_ref, utri_ref,
                h2p_ref, idx_ref, wts_ref, rank_ref, cnt_ref, carry_ref)


def _mixer_call(x, positions, mod, g_mix, w_in, w_dw, b_dw, g_conv_ln, b_conv_ln, g_ret_norm, w_out,
                g_ffn, w_router, b_router, group, n_groups, after):
    S, D = x.shape[1:]
    B = x.shape[0] // n_groups
    b0 = group * B
    in_cols = w_in.shape[1]
    conv_ch = w_dw.shape[1]
    ts = min(512, S)
    mask, q_decay, k_decay, chunk_decay = _retention_tables()
    key_scale = np.float32(RET_DIM ** -0.5)
    half = RET_DIM // 2
    inv_freq = (ROPE_BASE ** (-np.arange(half, dtype=np.float32) / half)).astype(np.float32)
    rope = np.concatenate([inv_freq, inv_freq])[None, :]
    w_dw_p = jnp.zeros((CONV_HALO, conv_ch), F32).at[:CONV_WIDTH].set(w_dw)
    wr_hi, wr_lo = _split_bf16(w_router.T)
    utri = jnp.asarray(np.triu(np.ones((ts, ts), np.float32), 1), BF16)
    n_s = S // ts
    T = B * S
    full = lambda shape: pl.BlockSpec(shape, lambda b, s: (0,) * len(shape))
    return pl.pallas_call(
        functools.partial(_mixer_kernel, chunk_decay, ts),
        grid=(B, S // ts),
        in_specs=[
            pl.BlockSpec((None, ts, D), lambda b, s: (b0 + b, s, 0)),
            pl.BlockSpec((None, ts, 1), lambda b, s: (b0 + b, s, 0)),
            pl.BlockSpec((None, N_MOD, D), lambda b, s: (b0 + b, 0, 0)),
            full((1, D)),
            full((D, in_cols)),
            full((CONV_HALO, conv_ch)),
            full((1, conv_ch)),
            full((1, conv_ch)),
            full((1, conv_ch)),
            full((1, RET_HEADS * RET_DIM)),
            full((conv_ch + RET_HEADS * RET_DIM, D)),
            full((1, RET_DIM)),
            full((RET_HEADS, RET_CHUNK, RET_CHUNK)),
            full((RET_HEADS, RET_CHUNK, 1)),
            full((RET_HEADS, RET_CHUNK, 1)),
            full((1, D)),
            full((2 * N_EXPERTS, D)),
            full((N_EXPERTS, D)),
            full((N_EXPERTS, 1)),
            full((ts, ts)),
            pl.BlockSpec(memory_space=pl.ANY),
        ],
        out_specs=[
            pl.BlockSpec((None, ts, D), lambda b, s: (b, s, 0)),
            pl.BlockSpec((ts, D // 2), lambda b, s: (b * n_s + s, 0)),
            pl.BlockSpec((ROUTE_ROWS, ts), lambda b, s: (0, b * n_s + s)),
            pl.BlockSpec((ROUTE_ROWS, ts), lambda b, s: (0, b * n_s + s)),
            pl.BlockSpec((ROUTE_ROWS, ts), lambda b, s: (0, b * n_s + s)),
            full((N_EXPERTS, 1)),
        ],
        out_shape=[
            jax.ShapeDtypeStruct((B, S, D), F32),
            jax.ShapeDtypeStruct((T, D // 2), U32),
            jax.ShapeDtypeStruct((ROUTE_ROWS, T), I32),
            jax.ShapeDtypeStruct((ROUTE_ROWS, T), F32),
            jax.ShapeDtypeStruct((ROUTE_ROWS, T), I32),
            jax.ShapeDtypeStruct((N_EXPERTS, 1), I32),
        ],
        scratch_shapes=[
            pltpu.VMEM((ts, in_cols), F32),
            pltpu.VMEM((CONV_HALO + ts, conv_ch), F32),
            pltpu.VMEM((RET_HEADS, RET_DIM, RET_DIM), F32),
            pltpu.VMEM((ts, conv_ch + RET_HEADS * RET_DIM), BF16),
            pltpu.VMEM((ts, conv_ch), F32),
            pltpu.VMEM((N_EXPERTS, 1), F32),
        ],
        compiler_params=pltpu.CompilerParams(
            dimension_semantics=("arbitrary", "arbitrary"), vmem_limit_bytes=VMEM_LIMIT_BYTES),
        name="hybrid_mixer",
    )(x, positions.reshape(-1, S, 1), mod, g_mix.reshape(1, D), w_in.astype(BF16), w_dw_p,
      b_dw.reshape(1, -1), g_conv_ln.reshape(1, -1), b_conv_ln.reshape(1, -1), g_ret_norm.reshape(1, -1),
      w_out.astype(BF16), jnp.asarray(rope), jnp.asarray(mask * key_scale), jnp.asarray(q_decay),
      jnp.asarray(k_decay * key_scale), g_ffn.reshape(1, D), jnp.concatenate([wr_hi, wr_lo], axis=0), wr_hi,
      b_router.reshape(N_EXPERTS, 1), utri, after)


def _route_tile(x, mod_ref, gffn_ref, wr2_ref, wrhi_ref, br_ref, utri_ref,
                h2p_ref, idx_ref, wts_ref, rank_ref, cnt_ref, carry_ref):
    tr = x.shape[0]
    half = x.shape[1] // 2
    y = x * lax.rsqrt(jnp.mean(x * x, axis=-1, keepdims=True) + RMS_EPS) * gffn_ref[...]
    h2 = y * (1.0 + mod_ref[4:5, :]) + mod_ref[3:4, :]
    h2p_ref[...] = _pack_bf16_pair(h2[:, :half], h2[:, half:])

    h_hi, h_lo = _split_bf16(h2)
    nt = (((1,), (1,)), ((), ()))
    r = lax.dot_general(wr2_ref[...], h_hi, nt, preferred_element_type=F32)
    r2 = lax.dot_general(wrhi_ref[...], h_lo, nt, preferred_element_type=F32)
    l = r[:N_EXPERTS] + (r[N_EXPERTS:] + r2) + br_ref[...]
    eid = lax.broadcasted_iota(I32, l.shape, 0)
    vals, sels, idxs = [], [], []
    for _ in range(TOP_K):
        m = jnp.max(l, axis=0, keepdims=True)
        ik = jnp.min(jnp.where(l == m, eid, N_EXPERTS), axis=0, keepdims=True)
        sel = eid == ik
        vals.append(m)
        sels.append(sel)
        idxs.append(ik)
        l = jnp.where(sel, -jnp.inf, l)
    exps = [jnp.exp(v - vals[0]) for v in vals]
    denom = exps[0] + exps[1] + exps[2] + exps[3]
    member = jnp.zeros(l.shape, F32)
    for sel in sels:
        member = member + sel.astype(F32)
    before = jnp.dot(member.astype(BF16), utri_ref[...], preferred_element_type=F32) + carry_ref[...]
    ranks = [jnp.sum(jnp.where(sel, before, 0.0), axis=0, keepdims=True) for sel in sels]
    carry_ref[...] = carry_ref[...] + jnp.sum(member, axis=1, keepdims=True)
    cnt_ref[...] = carry_ref[...].astype(I32)

    def rows(pieces, n_rows, dtype):
        rid = lax.broadcasted_iota(I32, (n_rows, tr), 0)
        out = jnp.zeros((n_rows, tr), dtype)
        for k, p in enumerate(pieces):
            out = jnp.where(rid == k, p.astype(dtype), out)
        return out

    idx_ref[...] = rows(idxs, idx_ref.shape[0], I32)
    rank_ref[...] = rows(ranks, rank_ref.shape[0], I32)
    wts_ref[...] = rows([e / denom for e in exps], wts_ref.shape[0], F32)


ROUTE_ROWS = 8
ROUTE_LANES = 128


SC_ROWS = 128


V7X_SC_CORES = 2
V7X_SC_SUBCORES = 16


def _sc_workers():
    return V7X_SC_CORES, V7X_SC_SUBCORES


def _sc_mesh():
    return plsc.VectorSubcoreMesh(core_axis_name="c", subcore_axis_name="s",
                                  num_cores=V7X_SC_CORES, num_subcores=V7X_SC_SUBCORES)


def _sc_scatter_call(h2p, pos_km, n_rows):
    T, W = h2p.shape
    nc, ns = _sc_workers()
    n = SC_ROWS
    per_w = T // (nc * ns * n)

    def body(h2p_hbm, pos_hbm, xs_hbm, i0, i1, i2, i3, rows_v, sem):
        wid = lax.axis_index("s") * nc + lax.axis_index("c")
        idx_refs = (i0, i1, i2, i3)

        @pl.loop(0, per_w)
        def _(j):
            t0 = (wid * per_w + j) * n
            pltpu.sync_copy(h2p_hbm.at[pl.ds(t0, n)], rows_v)
            for k in range(TOP_K):
                pltpu.sync_copy(pos_hbm.at[pl.ds(k * T + t0, n)], idx_refs[k])
            copies = [pltpu.async_copy(rows_v, xs_hbm.at[idx_refs[k]], sem) for k in range(TOP_K)]
            for cp in copies:
                cp.wait()

    return pl.kernel(
        body,
        out_type=jax.ShapeDtypeStruct((n_rows, W), h2p.dtype),
        mesh=_sc_mesh(),
        scratch_types=[pltpu.VMEM((n,), I32)] * TOP_K + [pltpu.VMEM((n, W), h2p.dtype), pltpu.SemaphoreType.DMA],
        name="moe_scatter_rows",
    )(h2p, pos_km)


def _sc_gather_call(ys, pos_km):
    P = pos_km.shape[0]
    W = ys.shape[1]
    nc, ns = _sc_workers()
    n = SC_ROWS
    per_w = P // (nc * ns * n)

    def body(ys_hbm, pos_hbm, yp_hbm, idx_v, rows_v, sem):
        wid = lax.axis_index("s") * nc + lax.axis_index("c")

        @pl.loop(0, per_w)
        def _(j):
            p0 = (wid * per_w + j) * n
            pltpu.sync_copy(pos_hbm.at[pl.ds(p0, n)], idx_v)
            pltpu.async_copy(ys_hbm.at[idx_v], rows_v, sem).wait()
            pltpu.sync_copy(rows_v, yp_hbm.at[pl.ds(p0, n)])

    return pl.kernel(
        body,
        out_type=jax.ShapeDtypeStruct((P, W), ys.dtype),
        mesh=_sc_mesh(),
        scratch_types=[pltpu.VMEM((n,), I32), pltpu.VMEM((n, W), ys.dtype), pltpu.SemaphoreType.DMA],
        name="moe_gather_rows",
    )(ys, pos_km)


def _ffn_kernel(te_ref, tv_ref, nx_ref, nu_ref, xs_ref, wgu_hbm, bgu_ref, wd_hbm, bd_ref, y_ref,
                wgu_f32, wd_f32, wgu_bf, wd_bf, sems):
    del nu_ref
    i = pl.program_id(0)
    valid = tv_ref[i]
    tm, half = xs_ref.shape
    d_exp = wd_bf.shape[0]

    def weight_copies(e):
        return (pltpu.make_async_copy(wgu_hbm.at[e], wgu_f32, sems.at[0]),
                pltpu.make_async_copy(wd_hbm.at[e], wd_f32, sems.at[1]))

    @pl.when(i == 0)
    def _():
        for cp in weight_copies(te_ref[0]):
            cp.start(priority=1)

    @pl.when(((i == 0) | (te_ref[i] != te_ref[jnp.maximum(i - 1, 0)])) & (valid > 0))
    def _():
        for cp in weight_copies(te_ref[i]):
            cp.wait()
        wgu_bf[...] = wgu_f32[...].astype(BF16)
        wd_bf[...] = wd_f32[...].astype(BF16)

        @pl.when(nx_ref[i] >= 0)
        def _():
            for cp in weight_copies(nx_ref[i]):
                cp.start(priority=1)

    def sub_tile(r0, n_rows, masked):
        rows = slice(r0, r0 + n_rows)
        lo, hi = _unpack_bf16_pair(xs_ref[rows, :])
        if masked:
            keep = lax.broadcasted_iota(I32, (n_rows, half), 0) < valid - r0
            lo = jnp.where(keep, lo, 0.0)
            hi = jnp.where(keep, hi, 0.0)
        xt = jnp.concatenate([lo.astype(BF16), hi.astype(BF16)], axis=1)
        d = functools.partial(jnp.dot, preferred_element_type=F32)
        nb = 256

        def up(c):
            gate = d(xt, wgu_bf[:, c:c + nb]) + bgu_ref[:, c:c + nb]
            lin = d(xt, wgu_bf[:, d_exp + c:d_exp + c + nb]) + bgu_ref[:, d_exp + c:d_exp + c + nb]
            return gate, lin

        def activate(gate, lin):
            gate = jnp.minimum(gate, SWIGLU_LIMIT)
            lin = jnp.clip(lin, -SWIGLU_LIMIT, SWIGLU_LIMIT)
            return (gate * jax.nn.sigmoid(SWIGLU_ALPHA * gate) * (lin + 1.0)).astype(BF16)

        chunks = list(range(0, d_exp, nb))
        acts = []
        pending = up(chunks[0])
        for n in range(len(chunks)):
            nxt = up(chunks[n + 1]) if n + 1 < len(chunks) else None
            acts.append(activate(*pending))
            pending = nxt
        out = d(jnp.concatenate(acts, axis=1), wd_bf[...]) + bd_ref[...]
        y_ref[rows, :] = _pack_bf16_pair(out[:, :half], out[:, half:])

    pl.when(valid == tm)(functools.partial(sub_tile, 0, tm, False))
    for n_sub in range(1, tm // FFN_SUB_ROWS + 1):
        n_rows = n_sub * FFN_SUB_ROWS
        pl.when((valid > n_rows - FFN_SUB_ROWS) & (valid <= n_rows) & (valid < tm))(
            functools.partial(sub_tile, 0, n_rows, True))


def _ffn_call(tile_expert, tile_valid, tile_next, n_used, xs, w_gu, b_gu, w_down, b_down, tm):
    R, half = xs.shape
    E, D, two_f = w_gu.shape
    d_exp = w_down.shape[1]
    rows = lambda i, te, tv, nx, nu: (jnp.minimum(i, nu[0] - 1), 0)
    grid_spec = pltpu.PrefetchScalarGridSpec(
        num_scalar_prefetch=4,
        grid=(R // tm,),
        in_specs=[
            pl.BlockSpec((tm, half), rows),
            pl.BlockSpec(memory_space=pl.ANY),
            pl.BlockSpec((None, 1, two_f), lambda i, te, tv, nx, nu: (te[i], 0, 0)),
            pl.BlockSpec(memory_space=pl.ANY),
            pl.BlockSpec((None, 1, D), lambda i, te, tv, nx, nu: (te[i], 0, 0)),
        ],
        out_specs=pl.BlockSpec((tm, half), rows),
        scratch_shapes=[
            pltpu.VMEM((D, two_f), F32), pltpu.VMEM((d_exp, D), F32),
            pltpu.VMEM((D, two_f), BF16), pltpu.VMEM((d_exp, D), BF16),
            pltpu.SemaphoreType.DMA((2,)),
        ],
    )
    return pl.pallas_call(
        _ffn_kernel,
        grid_spec=grid_spec,
        out_shape=jax.ShapeDtypeStruct((R, half), U32),
        compiler_params=pltpu.CompilerParams(
            dimension_semantics=("arbitrary",), vmem_limit_bytes=VMEM_LIMIT_BYTES),
        name="moe_ffn",
    )(tile_expert, tile_valid, tile_next, n_used, xs, w_gu, b_gu.reshape(E, 1, two_f), w_down,
      b_down.reshape(E, 1, D))


def _final_kernel(x_ref, yp_ref, wts_ref, mod_ref, gfin_ref, *rest):
    o_ref = rest[-1]
    x = x_ref[...]
    half = x.shape[1] // 2
    w = jnp.concatenate([wts_ref[...], jnp.zeros((ROUTE_LANES - ROUTE_ROWS, x.shape[0]), F32)], axis=0).T
    lo = jnp.zeros((x.shape[0], half), F32)
    hi = jnp.zeros((x.shape[0], half), F32)
    for k in range(TOP_K):
        l, h = _unpack_bf16_pair(yp_ref[k])
        lo = lo + w[:, k:k + 1] * l
        hi = hi + w[:, k:k + 1] * h
    gate = mod_ref[5:6, :]
    x_lo = x[:, :half] + gate[:, :half] * lo
    x_hi = x[:, half:] + gate[:, half:] * hi
    ms = (jnp.sum(x_lo * x_lo, axis=-1, keepdims=True) + jnp.sum(x_hi * x_hi, axis=-1, keepdims=True)) / x.shape[1]
    inv = lax.rsqrt(ms + RMS_EPS)
    o_ref[:, :half] = x_lo * inv * gfin_ref[:, :half]
    o_ref[:, half:] = x_hi * inv * gfin_ref[:, half:]


def _final_call(x1, yp, wts, mod, g_final, tokens_per_batch, group, n_groups, prev_out):
    T, D = x1.shape
    T_all = T * n_groups
    tq = min(512, tokens_per_batch)
    per_b = tokens_per_batch // tq
    first = group * (T // tq)
    in_specs = [
        pl.BlockSpec((tq, D), lambda i: (i, 0)),
        pl.BlockSpec((TOP_K, tq, D // 2), lambda i: (0, i, 0)),
        pl.BlockSpec((ROUTE_ROWS, tq), lambda i: (0, i)),
        pl.BlockSpec((None, N_MOD, D), lambda i: ((first + i) // per_b, 0, 0)),
        pl.BlockSpec((1, D), lambda i: (0, 0)),
    ]
    args = [x1, yp, wts, mod, g_final.reshape(1, D)]
    aliases = {}
    if prev_out is not None:
        in_specs.append(pl.BlockSpec(memory_space=pl.ANY))
        args.append(prev_out)
        aliases = {len(args) - 1: 0}
    return pl.pallas_call(
        _final_kernel,
        grid=(T // tq,),
        in_specs=in_specs,
        out_specs=pl.BlockSpec((tq, D), lambda i: (first + i, 0)),
        out_shape=jax.ShapeDtypeStruct((T_all, D), F32),
        input_output_aliases=aliases,
        compiler_params=pltpu.CompilerParams(
            dimension_semantics=("arbitrary",), vmem_limit_bytes=VMEM_LIMIT_BYTES),
        name="moe_combine_final",
    )(*args)


def _group_layout(counts, n_tiles, tm):
    padded = ((counts + tm - 1) // tm) * tm
    ends = jnp.cumsum(padded)
    starts = ends - padded
    tile_row = jnp.arange(n_tiles, dtype=I32) * tm
    te = jnp.minimum(jnp.sum(tile_row[:, None] >= ends[None, :], axis=1), N_EXPERTS - 1).astype(I32)
    eids = jnp.arange(N_EXPERTS, dtype=I32)
    mine = te[:, None] == eids[None, :]
    lookup = lambda table: jnp.sum(jnp.where(mine, table[None, :], 0), axis=1)
    tv = jnp.clip(lookup(counts) - (tile_row - lookup(starts)), 0, tm).astype(I32)
    later = (eids[None, :] > eids[:, None]) & (counts[None, :] > 0)
    nxt = jnp.min(jnp.where(later, eids[None, :], N_EXPERTS), axis=1)
    nx = lookup(jnp.where(nxt < N_EXPERTS, nxt, -1)).astype(I32)
    n_used = jnp.sum(tv > 0).astype(I32).reshape(1)
    return starts, te, tv, nx, n_used


def kernel(x, c, positions, w_ada, b_ada, g_mix, w_in, w_dw, b_dw, g_conv_ln, b_conv_ln, g_ret_norm,
           w_out, g_ffn, w_router, b_router, w_gu, b_gu, w_down, b_down, g_final):
    B, S, D = x.shape
    T = B * S
    assert w_ada.shape[0] == 1, "single-layer block: the final norm directly follows layer 0"
    xt = x
    for l in range(1):
        mod = _mod_call(c, w_ada[l], b_ada[l]).reshape(B, N_MOD, D)
        n_groups = MOE_TOKEN_GROUPS if B % MOE_TOKEN_GROUPS == 0 else 1
        Bg = B // n_groups
        Tg = Bg * S
        tm = FFN_TILE_ROWS if Tg * TOP_K >= FFN_TILE_ROWS * N_EXPERTS * 4 else FFN_SUB_ROWS
        n_tiles = (Tg * TOP_K) // tm + N_EXPERTS
        x1s, scattered = [], []
        pos_km = jnp.zeros((TOP_K * Tg,), I32)
        for g in range(n_groups):
            x1, h2p, idx, wts, rank, counts = _mixer_call(
                xt, positions, mod, g_mix[l], w_in[l], w_dw[l], b_dw[l], g_conv_ln[l], b_conv_ln[l],
                g_ret_norm[l], w_out[l], g_ffn[l], w_router[l], b_router[l], g, n_groups, pos_km)
            x1 = x1.reshape(Tg, D)
            starts, te, tv, nx, n_used = _group_layout(counts[:, 0], n_tiles, tm)
            pos_km = rank
            for e in range(N_EXPERTS):
                pos_km = pos_km + jnp.where(idx == e, starts[e], 0)
            pos_km = pos_km[:TOP_K].astype(I32).reshape(-1)
            x1s.append(x1)
            scattered.append((_sc_scatter_call(h2p, pos_km, n_tiles * tm), pos_km, wts, te, tv, nx, n_used))
        out = None
        for g, (xs, pos_km, wts, te, tv, nx, n_used) in enumerate(scattered):
            ys = _ffn_call(te, tv, nx, n_used, xs, w_gu[l], b_gu[l], w_down[l], b_down[l], tm)
            yp = _sc_gather_call(ys, pos_km)
            out = _final_call(x1s[g], yp.reshape(TOP_K, Tg, D // 2), wts, mod, g_final, S, g, n_groups, out)
        xt = out
    return xt.reshape(B, S, D)
```

```python
import functools

import numpy as np
import jax
import jax.numpy as jnp
from jax import lax
from jax.experimental import pallas as pl
from jax.experimental.pallas import tpu as pltpu
from jax.experimental.pallas import tpu_sc as plsc

F32 = jnp.float32
BF16 = jnp.bfloat16
U32 = jnp.uint32
I32 = jnp.int32

CONV_WIDTH = 31
CONV_HALO = 32
RET_HEADS = 4
RET_DIM = 128
RET_CHUNK = 128
ROPE_BASE = 10000.0
N_EXPERTS = 32
TOP_K = 4
SWIGLU_LIMIT = 7.0
SWIGLU_ALPHA = 1.702
RMS_EPS = 1e-6
LN_EPS = 1e-5
N_MOD = 6

VMEM_LIMIT_BYTES = 56 * 1024 * 1024
MOE_TOKEN_GROUPS = 2
FFN_TILE_ROWS = 1024
FFN_SUB_ROWS = 256


def _split_bf16(a):
    hi = a.astype(BF16)
    lo = (a - hi.astype(F32)).astype(BF16)
    return hi, lo


def _dot3(a, b_hi, b_lo):
    a_hi, a_lo = _split_bf16(a)
    d = functools.partial(jnp.dot, preferred_element_type=F32)
    return d(a_hi, b_hi) + (d(a_hi, b_lo) + d(a_lo, b_hi))


def _pack_bf16_pair(lo, hi):
    lo_bits = lax.bitcast_convert_type(lo.astype(BF16).astype(F32), U32)
    hi_bits = lax.bitcast_convert_type(hi.astype(BF16).astype(F32), U32)
    return (lo_bits >> 16) | (hi_bits & jnp.uint32(0xFFFF0000))


def _unpack_bf16_pair(p):
    lo = lax.bitcast_convert_type(p << 16, F32)
    hi = lax.bitcast_convert_type(p & jnp.uint32(0xFFFF0000), F32)
    return lo, hi


def _mod_kernel(c_ref, whi_ref, wlo_ref, b_ref, o_ref):
    c = c_ref[...]
    c_act = c * jax.nn.sigmoid(c)
    o_ref[...] = _dot3(c_act, whi_ref[...], wlo_ref[...]) + b_ref[...]


def _mod_call(c, w_ada, b_ada):
    B, D = c.shape
    n = w_ada.shape[1]
    bn = 1024
    w_hi, w_lo = _split_bf16(w_ada)
    return pl.pallas_call(
        _mod_kernel,
        grid=(n // bn,),
        in_specs=[
            pl.BlockSpec((B, D), lambda j: (0, 0)),
            pl.BlockSpec((D, bn), lambda j: (0, j)),
            pl.BlockSpec((D, bn), lambda j: (0, j)),
            pl.BlockSpec((1, bn), lambda j: (0, j)),
        ],
        out_specs=pl.BlockSpec((B, bn), lambda j: (0, j)),
        out_shape=jax.ShapeDtypeStruct((B, n), F32),
        name="adaln_mod",
    )(c, w_hi, w_lo, b_ada.reshape(1, n))


def _retention_tables():
    h = np.arange(RET_HEADS, dtype=np.float32)
    log_gamma = np.log(1.0 - np.power(2.0, -5.0 - h)).astype(np.float32)
    idx = np.arange(RET_CHUNK, dtype=np.float32)
    diff = idx[:, None] - idx[None, :]
    causal = diff >= 0
    mask = np.where(causal[None], np.exp(log_gamma[:, None, None] * np.where(causal, diff, 0.0)[None]), 0.0)
    q_decay = np.exp(log_gamma[:, None] * (idx + 1.0))[..., None]
    k_decay = np.exp(log_gamma[:, None] * (RET_CHUNK - 1.0 - idx))[..., None]
    chunk_decay = np.exp(log_gamma * RET_CHUNK)
    return (mask.astype(np.float32), q_decay.astype(np.float32), k_decay.astype(np.float32),
            [float(v) for v in chunk_decay.astype(np.float32)])


def _mixer_kernel(chunk_decay, ts, x_ref, pos_ref, mod_ref, gmix_ref, win_ref, wdw_ref, bdw_ref,
                  gcl_ref, bcl_ref, gret_ref, wout_ref, rope_ref, dmask_ref, qdec_ref, kdec_ref,
                  gffn_ref, wr2_ref, wrhi_ref, br_ref, utri_ref, after_ref,
                  o_ref, h2p_ref, idx_ref, wts_ref, rank_ref, cnt_ref,
                  proj_ref, uext_ref, state_ref, cat_ref, conv_ref, carry_ref):
    del after_ref
    s = pl.program_id(1)
    conv_ch = wdw_ref.shape[1]
    ret_w = RET_HEADS * RET_DIM

    @pl.when(s == 0)
    def _():
        uext_ref[0:CONV_HALO, :] = jnp.zeros((CONV_HALO, conv_ch), F32)
        state_ref[...] = jnp.zeros_like(state_ref)

    x = x_ref[...]
    row_gain = gmix_ref[...] * (1.0 + mod_ref[1:2, :])
    h = (x * lax.rsqrt(jnp.mean(x * x, axis=-1, keepdims=True) + RMS_EPS) * row_gain + mod_ref[0:1, :]).astype(BF16)
    proj_ref[...] = jnp.dot(h, win_ref[...], preferred_element_type=F32)

    a = proj_ref[:, 0:conv_ch]
    b = proj_ref[:, conv_ch:2 * conv_ch]
    uext_ref[CONV_HALO:CONV_HALO + ts, :] = a * jax.nn.sigmoid(b)
    cb = 64
    lead = CONV_HALO - (CONV_WIDTH - 1)
    span = cb + CONV_HALO

    def conv_block(r0):
        for c0 in range(0, conv_ch, 128):
            xw = uext_ref[r0:r0 + span, c0:c0 + 128]
            acc = jnp.zeros((cb, 128), F32) + bdw_ref[:, c0:c0 + 128]
            for r in range(8):
                xr = xw if r == 0 else pltpu.roll(xw, span - r, 0)
                for q in range((lead + CONV_WIDTH - 1 - r) // 8 + 1):
                    j = 8 * q + r - lead
                    if 0 <= j < CONV_WIDTH:
                        acc = acc + wdw_ref[j:j + 1, c0:c0 + 128] * xr[8 * q:8 * q + cb]
            conv_ref[r0:r0 + cb, c0:c0 + 128] = acc
    rb = 2 * cb

    def conv_norm(r0):
        acc = conv_ref[r0:r0 + rb, 0:conv_ch]
        mu = jnp.mean(acc, axis=-1, keepdims=True)
        d = acc - mu
        var = jnp.mean(d * d, axis=-1, keepdims=True)
        ln = d * lax.rsqrt(var + LN_EPS) * gcl_ref[...] + bcl_ref[...]
        cat_ref[r0:r0 + rb, 0:conv_ch] = (ln * jax.nn.sigmoid(ln)).astype(BF16)

    hts = ts // 2
    low = lax.broadcasted_iota(I32, (hts, RET_DIM), 1) < RET_DIM // 2
    posf = jnp.where(low, pos_ref[0:hts, :].astype(F32), pos_ref[hts:ts, :].astype(F32))
    ang = posf * rope_ref[...]
    cos_p = jnp.cos(ang)
    sin_p = jnp.sin(ang)
    cos_s = pltpu.roll(cos_p, RET_DIM // 2, 1)
    sin_s = pltpu.roll(sin_p, RET_DIM // 2, 1)
    cos2 = jnp.concatenate([jnp.where(low, cos_p, cos_s), jnp.where(low, cos_s, cos_p)], axis=0)
    sin2 = jnp.concatenate([jnp.where(low, -sin_p, sin_s), jnp.where(low, -sin_s, sin_p)], axis=0)
    q0 = 2 * conv_ch
    k0 = q0 + ret_w
    v0 = k0 + ret_w
    g0 = v0 + ret_w

    def retention_unit(hd, n):
        c0 = hd * RET_DIM
        r0 = n * RET_CHUNK
        rows = slice(r0, r0 + RET_CHUNK)
        cs = cos2[rows]
        sn = sin2[rows]
        q = proj_ref[rows, q0 + c0:q0 + c0 + RET_DIM]
        k = proj_ref[rows, k0 + c0:k0 + c0 + RET_DIM]
        v = proj_ref[rows, v0 + c0:v0 + c0 + RET_DIM].astype(BF16)
        g = proj_ref[rows, g0 + c0:g0 + c0 + RET_DIM]
        qr = q * cs + pltpu.roll(q, RET_DIM // 2, 1) * sn
        kr = k * cs + pltpu.roll(k, RET_DIM // 2, 1) * sn
        st = state_ref[hd]
        scores = lax.dot_general(qr.astype(BF16), kr.astype(BF16), (((1,), (1,)), ((), ())),
                                 preferred_element_type=F32) * dmask_ref[hd]
        inner = jnp.dot(scores.astype(BF16), v, preferred_element_type=F32)
        cross = jnp.dot((qr * qdec_ref[hd]).astype(BF16), st.astype(BF16), preferred_element_type=F32)
        kv = lax.dot_general((kr * kdec_ref[hd]).astype(BF16), v, (((0,), (0,)), ((), ())),
                             preferred_element_type=F32)
        state_ref[hd] = chunk_decay[hd] * st + kv
        r = inner + cross
        mu = jnp.mean(r, axis=-1, keepdims=True)
        d = r - mu
        var = jnp.mean(d * d, axis=-1, keepdims=True)
        rn = d * lax.rsqrt(var + LN_EPS) * gret_ref[:, c0:c0 + RET_DIM]
        cat_ref[rows, conv_ch + c0:conv_ch + c0 + RET_DIM] = (g * jax.nn.sigmoid(g) * rn).astype(BF16)

    units = [(hd, n) for n in range(ts // RET_CHUNK) for hd in range(RET_HEADS)]
    n_conv = ts // cb
    per = -(-len(units) // n_conv)
    for i in range(n_conv):
        conv_block(i * cb)
        if i % 2 == 1:
            conv_norm((i - 1) * cb)
        for hd, n in units[i * per:(i + 1) * per]:
            retention_unit(hd, n)
    uext_ref[0:CONV_HALO, :] = uext_ref[ts:ts + CONV_HALO, :]

    out = jnp.dot(cat_ref[...], wout_ref[...], preferred_element_type=F32)
    x1 = x + mod_ref[2:3, :] * out
    o_ref[...] = x1

    @pl.when((pl.program_id(0) == 0) & (s == 0))
    def _():
        carry_ref[...] = jnp.zeros_like(carry_ref)

    _route_tile(x1, mod_ref, gffn_ref, wr2_ref, wrhi_ref, br_ref, utri_ref,
                h2p_ref, idx_ref, wts_ref, rank_ref, cnt_ref, carry_ref)


def _mixer_call(x, positions, mod, g_mix, w_in, w_dw, b_dw, g_conv_ln, b_conv_ln, g_ret_norm, w_out,
                g_ffn, w_router, b_router, group, n_groups, after):
    S, D = x.shape[1:]
    B = x.shape[0] // n_groups
    b0 = group * B
    in_cols = w_in.shape[1]
    conv_ch = w_dw.shape[1]
    ts = min(512, S)
    mask, q_decay, k_decay, chunk_decay = _retention_tables()
    key_scale = np.float32(RET_DIM ** -0.5)
    half = RET_DIM // 2
    inv_freq = (ROPE_BASE ** (-np.arange(half, dtype=np.float32) / half)).astype(np.float32)
    rope = np.concatenate([inv_freq, inv_freq])[None, :]
    w_dw_p = jnp.zeros((CONV_HALO, conv_ch), F32).at[:CONV_WIDTH].set(w_dw)
    wr_hi, wr_lo = _split_bf16(w_router.T)
    utri = jnp.asarray(np.triu(np.ones((ts, ts), np.float32), 1), BF16)
    n_s = S // ts
    T = B * S
    full = lambda shape: pl.BlockSpec(shape, lambda b, s: (0,) * len(shape))
    return pl.pallas_call(
        functools.partial(_mixer_kernel, chunk_decay, ts),
        grid=(B, S // ts),
        in_specs=[
            pl.BlockSpec((None, ts, D), lambda b, s: (b0 + b, s, 0)),
            pl.BlockSpec((None, ts, 1), lambda b, s: (b0 + b, s, 0)),
            pl.BlockSpec((None, N_MOD, D), lambda b, s: (b0 + b, 0, 0)),
            full((1, D)),
            full((D, in_cols)),
            full((CONV_HALO, conv_ch)),
            full((1, conv_ch)),
            full((1, conv_ch)),
            full((1, conv_ch)),
            full((1, RET_HEADS * RET_DIM)),
            full((conv_ch + RET_HEADS * RET_DIM, D)),
            full((1, RET_DIM)),
            full((RET_HEADS, RET_CHUNK, RET_CHUNK)),
            full((RET_HEADS, RET_CHUNK, 1)),
            full((RET_HEADS, RET_CHUNK, 1)),
            full((1, D)),
            full((2 * N_EXPERTS, D)),
            full((N_EXPERTS, D)),
            full((N_EXPERTS, 1)),
            full((ts, ts)),
            pl.BlockSpec(memory_space=pl.ANY),
        ],
        out_specs=[
            pl.BlockSpec((None, ts, D), lambda b, s: (b, s, 0)),
            pl.BlockSpec((ts, D // 2), lambda b, s: (b * n_s + s, 0)),
            pl.BlockSpec((ROUTE_ROWS, ts), lambda b, s: (0, b * n_s + s)),
            pl.BlockSpec((ROUTE_ROWS, ts), lambda b, s: (0, b * n_s + s)),
            pl.BlockSpec((ROUTE_ROWS, ts), lambda b, s: (0, b * n_s + s)),
            full((N_EXPERTS, 1)),
        ],
        out_shape=[
            jax.ShapeDtypeStruct((B, S, D), F32),
            jax.ShapeDtypeStruct((T, D // 2), U32),
            jax.ShapeDtypeStruct((ROUTE_ROWS, T), I32),
            jax.ShapeDtypeStruct((ROUTE_ROWS, T), F32),
            jax.ShapeDtypeStruct((ROUTE_ROWS, T), I32),
            jax.ShapeDtypeStruct((N_EXPERTS, 1), I32),
        ],
        scratch_shapes=[
            pltpu.VMEM((ts, in_cols), F32),
            pltpu.VMEM((CONV_HALO + ts, conv_ch), F32),
            pltpu.VMEM((RET_HEADS, RET_DIM, RET_DIM), F32),
            pltpu.VMEM((ts, conv_ch + RET_HEADS * RET_DIM), BF16),
            pltpu.VMEM((ts, conv_ch), F32),
            pltpu.VMEM((N_EXPERTS, 1), F32),
        ],
        compiler_params=pltpu.CompilerParams(
            dimension_semantics=("arbitrary", "arbitrary"), vmem_limit_bytes=VMEM_LIMIT_BYTES),
        name="hybrid_mixer",
    )(x, positions.reshape(-1, S, 1), mod, g_mix.reshape(1, D), w_in.astype(BF16), w_dw_p,
      b_dw.reshape(1, -1), g_conv_ln.reshape(1, -1), b_conv_ln.reshape(1, -1), g_ret_norm.reshape(1, -1),
      w_out.astype(BF16), jnp.asarray(rope), jnp.asarray(mask * key_scale), jnp.asarray(q_decay),
      jnp.asarray(k_decay * key_scale), g_ffn.reshape(1, D), jnp.concatenate([wr_hi, wr_lo], axis=0), wr_hi,
      b_router.reshape(N_EXPERTS, 1), utri, after)


def _route_tile(x, mod_ref, gffn_ref, wr2_ref, wrhi_ref, br_ref, utri_ref,
                h2p_ref, idx_ref, wts_ref, rank_ref, cnt_ref, carry_ref):
    tr = x.shape[0]
    half = x.shape[1] // 2
    y = x * lax.rsqrt(jnp.mean(x * x, axis=-1, keepdims=True) + RMS_EPS) * gffn_ref[...]
    h2 = y * (1.0 + mod_ref[4:5, :]) + mod_ref[3:4, :]
    h2p_ref[...] = _pack_bf16_pair(h2[:, :half], h2[:, half:])

    h_hi, h_lo = _split_bf16(h2)
    nt = (((1,), (1,)), ((), ()))
    r = lax.dot_general(wr2_ref[...], h_hi, nt, preferred_element_type=F32)
    r2 = lax.dot_general(wrhi_ref[...], h_lo, nt, preferred_element_type=F32)
    l = r[:N_EXPERTS] + (r[N_EXPERTS:] + r2) + br_ref[...]
    eid = lax.broadcasted_iota(I32, l.shape, 0)
    vals, sels, idxs = [], [], []
    for _ in range(TOP_K):
        m = jnp.max(l, axis=0, keepdims=True)
        ik = jnp.min(jnp.where(l == m, eid, N_EXPERTS), axis=0, keepdims=True)
        sel = eid == ik
        vals.append(m)
        sels.append(sel)
        idxs.append(ik)
        l = jnp.where(sel, -jnp.inf, l)
    exps = [jnp.exp(v - vals[0]) for v in vals]
    denom = exps[0] + exps[1] + exps[2] + exps[3]
    member = jnp.zeros(l.shape, F32)
    for sel in sels:
        member = member + sel.astype(F32)
    before = jnp.dot(member.astype(BF16), utri_ref[...], preferred_element_type=F32) + carry_ref[...]
    ranks = [jnp.sum(jnp.where(sel, before, 0.0), axis=0, keepdims=True) for sel in sels]
    carry_ref[...] = carry_ref[...] + jnp.sum(member, axis=1, keepdims=True)
    cnt_ref[...] = carry_ref[...].astype(I32)

    def rows(pieces, n_rows, dtype):
        rid = lax.broadcasted_iota(I32, (n_rows, tr), 0)
        out = jnp.zeros((n_rows, tr), dtype)
        for k, p in enumerate(pieces):
            out = jnp.where(rid == k, p.astype(dtype), out)
        return out

    idx_ref[...] = rows(idxs, idx_ref.shape[0], I32)
    rank_ref[...] = rows(ranks, rank_ref.shape[0], I32)
    wts_ref[...] = rows([e / denom for e in exps], wts_ref.shape[0], F32)


ROUTE_ROWS = 8
ROUTE_LANES = 128


def _pos_kernel(starts_ref, idx_ref, rank_ref, pos_ref):
    idx = idx_ref[...]
    pos = rank_ref[...]
    for e in range(N_EXPERTS):
        pos = pos + jnp.where(idx == e, starts_ref[e], 0)
    pos_ref[...] = pos


def _pos_call(starts, idx, rank):
    rows, T = idx.shape
    tp = min(4096, T)
    spec = pl.BlockSpec((rows, tp), lambda i, st: (0, i))
    return pl.pallas_call(
        _pos_kernel,
        grid_spec=pltpu.PrefetchScalarGridSpec(
            num_scalar_prefetch=1, grid=(T // tp,), in_specs=[spec, spec], out_specs=spec),
        out_shape=jax.ShapeDtypeStruct((rows, T), I32),
        name="moe_sorted_pos",
    )(starts.astype(I32), idx, rank)


SC_ROWS = 128


V7X_SC_CORES = 2
V7X_SC_SUBCORES = 16


def _sc_workers():
    return V7X_SC_CORES, V7X_SC_SUBCORES


def _sc_mesh():
    return plsc.VectorSubcoreMesh(core_axis_name="c", subcore_axis_name="s",
                                  num_cores=V7X_SC_CORES, num_subcores=V7X_SC_SUBCORES)


def _sc_scatter_call(h2p, pos_km, n_rows):
    T, W = h2p.shape
    nc, ns = _sc_workers()
    n = SC_ROWS
    per_w = T // (nc * ns * n)

    def body(h2p_hbm, pos_hbm, xs_hbm, i0, i1, i2, i3, rows_v, sem):
        wid = lax.axis_index("s") * nc + lax.axis_index("c")
        idx_refs = (i0, i1, i2, i3)

        @pl.loop(0, per_w)
        def _(j):
            t0 = (wid * per_w + j) * n
            pltpu.sync_copy(h2p_hbm.at[pl.ds(t0, n)], rows_v)
            for k in range(TOP_K):
                pltpu.sync_copy(pos_hbm.at[pl.ds(k * T + t0, n)], idx_refs[k])
            copies = [pltpu.async_copy(rows_v, xs_hbm.at[idx_refs[k]], sem) for k in range(TOP_K)]
            for cp in copies:
                cp.wait()

    return pl.kernel(
        body,
        out_type=jax.ShapeDtypeStruct((n_rows, W), h2p.dtype),
        mesh=_sc_mesh(),
        scratch_types=[pltpu.VMEM((n,), I32)] * TOP_K + [pltpu.VMEM((n, W), h2p.dtype), pltpu.SemaphoreType.DMA],
        name="moe_scatter_rows",
    )(h2p, pos_km)


def _sc_gather_call(ys, pos_km):
    P = pos_km.shape[0]
    W = ys.shape[1]
    nc, ns = _sc_workers()
    n = SC_ROWS
    per_w = P // (nc * ns * n)

    def body(ys_hbm, pos_hbm, yp_hbm, idx_v, rows_v, sem):
        wid = lax.axis_index("s") * nc + lax.axis_index("c")

        @pl.loop(0, per_w)
        def _(j):
            p0 = (wid * per_w + j) * n
            pltpu.sync_copy(pos_hbm.at[pl.ds(p0, n)], idx_v)
            pltpu.async_copy(ys_hbm.at[idx_v], rows_v, sem).wait()
            pltpu.sync_copy(rows_v, yp_hbm.at[pl.ds(p0, n)])

    return pl.kernel(
        body,
        out_type=jax.ShapeDtypeStruct((P, W), ys.dtype),
        mesh=_sc_mesh(),
        scratch_types=[pltpu.VMEM((n,), I32), pltpu.VMEM((n, W), ys.dtype), pltpu.SemaphoreType.DMA],
        name="moe_gather_rows",
    )(ys, pos_km)


def _ffn_kernel(te_ref, tv_ref, nx_ref, nu_ref, xs_ref, wgu_hbm, bgu_ref, wd_hbm, bd_ref, y_ref,
                wgu_f32, wd_f32, wgu_bf, wd_bf, sems):
    del nu_ref
    i = pl.program_id(0)
    valid = tv_ref[i]
    tm, half = xs_ref.shape
    d_exp = wd_bf.shape[0]

    def weight_copies(e):
        return (pltpu.make_async_copy(wgu_hbm.at[e], wgu_f32, sems.at[0]),
                pltpu.make_async_copy(wd_hbm.at[e], wd_f32, sems.at[1]))

    @pl.when(i == 0)
    def _():
        for cp in weight_copies(te_ref[0]):
            cp.start(priority=1)

    @pl.when(((i == 0) | (te_ref[i] != te_ref[jnp.maximum(i - 1, 0)])) & (valid > 0))
    def _():
        for cp in weight_copies(te_ref[i]):
            cp.wait()
        wgu_bf[...] = wgu_f32[...].astype(BF16)
        wd_bf[...] = wd_f32[...].astype(BF16)

        @pl.when(nx_ref[i] >= 0)
        def _():
            for cp in weight_copies(nx_ref[i]):
                cp.start(priority=1)

    def sub_tile(r0, n_rows, masked):
        rows = slice(r0, r0 + n_rows)
        lo, hi = _unpack_bf16_pair(xs_ref[rows, :])
        if masked:
            keep = lax.broadcasted_iota(I32, (n_rows, half), 0) < valid - r0
            lo = jnp.where(keep, lo, 0.0)
            hi = jnp.where(keep, hi, 0.0)
        xt = jnp.concatenate([lo.astype(BF16), hi.astype(BF16)], axis=1)
        d = functools.partial(jnp.dot, preferred_element_type=F32)
        nb = 256

        def up(c):
            gate = d(xt, wgu_bf[:, c:c + nb]) + bgu_ref[:, c:c + nb]
            lin = d(xt, wgu_bf[:, d_exp + c:d_exp + c + nb]) + bgu_ref[:, d_exp + c:d_exp + c + nb]
            return gate, lin

        def activate(gate, lin):
            gate = jnp.minimum(gate, SWIGLU_LIMIT)
            lin = jnp.clip(lin, -SWIGLU_LIMIT, SWIGLU_LIMIT)
            return (gate * jax.nn.sigmoid(SWIGLU_ALPHA * gate) * (lin + 1.0)).astype(BF16)

        chunks = list(range(0, d_exp, nb))
        acts = []
        pending = up(chunks[0])
        for n in range(len(chunks)):
            nxt = up(chunks[n + 1]) if n + 1 < len(chunks) else None
            acts.append(activate(*pending))
            pending = nxt
        out = d(jnp.concatenate(acts, axis=1), wd_bf[...]) + bd_ref[...]
        y_ref[rows, :] = _pack_bf16_pair(out[:, :half], out[:, half:])

    pl.when(valid == tm)(functools.partial(sub_tile, 0, tm, False))
    for n_sub in range(1, tm // FFN_SUB_ROWS + 1):
        n_rows = n_sub * FFN_SUB_ROWS
        pl.when((valid > n_rows - FFN_SUB_ROWS) & (valid <= n_rows) & (valid < tm))(
            functools.partial(sub_tile, 0, n_rows, True))


def _ffn_call(tile_expert, tile_valid, tile_next, n_used, xs, w_gu, b_gu, w_down, b_down, tm):
    R, half = xs.shape
    E, D, two_f = w_gu.shape
    d_exp = w_down.shape[1]
    rows = lambda i, te, tv, nx, nu: (jnp.minimum(i, nu[0] - 1), 0)
    grid_spec = pltpu.PrefetchScalarGridSpec(
        num_scalar_prefetch=4,
        grid=(R // tm,),
        in_specs=[
            pl.BlockSpec((tm, half), rows),
            pl.BlockSpec(memory_space=pl.ANY),
            pl.BlockSpec((None, 1, two_f), lambda i, te, tv, nx, nu: (te[i], 0, 0)),
            pl.BlockSpec(memory_space=pl.ANY),
            pl.BlockSpec((None, 1, D), lambda i, te, tv, nx, nu: (te[i], 0, 0)),
        ],
        out_specs=pl.BlockSpec((tm, half), rows),
        scratch_shapes=[
            pltpu.VMEM((D, two_f), F32), pltpu.VMEM((d_exp, D), F32),
            pltpu.VMEM((D, two_f), BF16), pltpu.VMEM((d_exp, D), BF16),
            pltpu.SemaphoreType.DMA((2,)),
        ],
    )
    return pl.pallas_call(
        _ffn_kernel,
        grid_spec=grid_spec,
        out_shape=jax.ShapeDtypeStruct((R, half), U32),
        compiler_params=pltpu.CompilerParams(
            dimension_semantics=("arbitrary",), vmem_limit_bytes=VMEM_LIMIT_BYTES),
        name="moe_ffn",
    )(tile_expert, tile_valid, tile_next, n_used, xs, w_gu, b_gu.reshape(E, 1, two_f), w_down,
      b_down.reshape(E, 1, D))


def _final_kernel(x_ref, yp_ref, wts_ref, mod_ref, gfin_ref, *rest):
    o_ref = rest[-1]
    x = x_ref[...]
    half = x.shape[1] // 2
    w = jnp.concatenate([wts_ref[...], jnp.zeros((ROUTE_LANES - ROUTE_ROWS, x.shape[0]), F32)], axis=0).T
    lo = jnp.zeros((x.shape[0], half), F32)
    hi = jnp.zeros((x.shape[0], half), F32)
    for k in range(TOP_K):
        l, h = _unpack_bf16_pair(yp_ref[k])
        lo = lo + w[:, k:k + 1] * l
        hi = hi + w[:, k:k + 1] * h
    gate = mod_ref[5:6, :]
    x_lo = x[:, :half] + gate[:, :half] * lo
    x_hi = x[:, half:] + gate[:, half:] * hi
    ms = (jnp.sum(x_lo * x_lo, axis=-1, keepdims=True) + jnp.sum(x_hi * x_hi, axis=-1, keepdims=True)) / x.shape[1]
    inv = lax.rsqrt(ms + RMS_EPS)
    o_ref[:, :half] = x_lo * inv * gfin_ref[:, :half]
    o_ref[:, half:] = x_hi * inv * gfin_ref[:, half:]


def _final_call(x1, yp, wts, mod, g_final, tokens_per_batch, group, n_groups, prev_out):
    T, D = x1.shape
    T_all = T * n_groups
    tq = min(512, tokens_per_batch)
    per_b = tokens_per_batch // tq
    first = group * (T // tq)
    in_specs = [
        pl.BlockSpec((tq, D), lambda i: (i, 0)),
        pl.BlockSpec((TOP_K, tq, D // 2), lambda i: (0, i, 0)),
        pl.BlockSpec((ROUTE_ROWS, tq), lambda i: (0, i)),
        pl.BlockSpec((None, N_MOD, D), lambda i: ((first + i) // per_b, 0, 0)),
        pl.BlockSpec((1, D), lambda i: (0, 0)),
    ]
    args = [x1, yp, wts, mod, g_final.reshape(1, D)]
    aliases = {}
    if prev_out is not None:
        in_specs.append(pl.BlockSpec(memory_space=pl.ANY))
        args.append(prev_out)
        aliases = {len(args) - 1: 0}
    return pl.pallas_call(
        _final_kernel,
        grid=(T // tq,),
        in_specs=in_specs,
        out_specs=pl.BlockSpec((tq, D), lambda i: (first + i, 0)),
        out_shape=jax.ShapeDtypeStruct((T_all, D), F32),
        input_output_aliases=aliases,
        compiler_params=pltpu.CompilerParams(
            dimension_semantics=("arbitrary",), vmem_limit_bytes=VMEM_LIMIT_BYTES),
        name="moe_combine_final",
    )(*args)


def _group_layout(counts, n_tiles, tm):
    padded = ((counts + tm - 1) // tm) * tm
    ends = jnp.cumsum(padded)
    starts = ends - padded
    tile_row = jnp.arange(n_tiles, dtype=I32) * tm
    te = jnp.minimum(jnp.sum(tile_row[:, None] >= ends[None, :], axis=1), N_EXPERTS - 1).astype(I32)
    eids = jnp.arange(N_EXPERTS, dtype=I32)
    mine = te[:, None] == eids[None, :]
    lookup = lambda table: jnp.sum(jnp.where(mine, table[None, :], 0), axis=1)
    tv = jnp.clip(lookup(counts) - (tile_row - lookup(starts)), 0, tm).astype(I32)
    later = (eids[None, :] > eids[:, None]) & (counts[None, :] > 0)
    nxt = jnp.min(jnp.where(later, eids[None, :], N_EXPERTS), axis=1)
    nx = lookup(jnp.where(nxt < N_EXPERTS, nxt, -1)).astype(I32)
    n_used = jnp.sum(tv > 0).astype(I32).reshape(1)
    return starts, te, tv, nx, n_used


def kernel(x, c, positions, w_ada, b_ada, g_mix, w_in, w_dw, b_dw, g_conv_ln, b_conv_ln, g_ret_norm,
           w_out, g_ffn, w_router, b_router, w_gu, b_gu, w_down, b_down, g_final):
    B, S, D = x.shape
    T = B * S
    assert w_ada.shape[0] == 1, "single-layer block: the final norm directly follows layer 0"
    xt = x
    for l in range(1):
        mod = _mod_call(c, w_ada[l], b_ada[l]).reshape(B, N_MOD, D)
        n_groups = MOE_TOKEN_GROUPS if B % MOE_TOKEN_GROUPS == 0 else 1
        Bg = B // n_groups
        Tg = Bg * S
        tm = FFN_TILE_ROWS if Tg * TOP_K >= FFN_TILE_ROWS * N_EXPERTS * 4 else FFN_SUB_ROWS
        n_tiles = (Tg * TOP_K) // tm + N_EXPERTS
        x1s, scattered = [], []
        pos_km = jnp.zeros((TOP_K * Tg,), I32)
        for g in range(n_groups):
            x1, h2p, idx, wts, rank, counts = _mixer_call(
                xt, positions, mod, g_mix[l], w_in[l], w_dw[l], b_dw[l], g_conv_ln[l], b_conv_ln[l],
                g_ret_norm[l], w_out[l], g_ffn[l], w_router[l], b_router[l], g, n_groups, pos_km)
            x1 = x1.reshape(Tg, D)
            starts, te, tv, nx, n_used = _group_layout(counts[:, 0], n_tiles, tm)
            pos_km = _pos_call(starts, idx, rank)[:TOP_K].reshape(-1)
            x1s.append(x1)
            scattered.append((_sc_scatter_call(h2p, pos_km, n_tiles * tm), pos_km, wts, te, tv, nx, n_used))
        out = None
        for g, (xs, pos_km, wts, te, tv, nx, n_used) in enumerate(scattered):
            ys = _ffn_call(te, tv, nx, n_used, xs, w_gu[l], b_gu[l], w_down[l], b_down[l], tm)
            yp = _sc_gather_call(ys, pos_km)
            out = _final_call(x1s[g], yp.reshape(TOP_K, Tg, D // 2), wts, mod, g_final, S, g, n_groups, out)
        xt = out
    return xt.reshape(B, S, D)
```

```python
import functools

import numpy as np
import jax
import jax.numpy as jnp
from jax import lax
from jax.experimental import pallas as pl
from jax.experimental.pallas import tpu as pltpu
from jax.experimental.pallas import tpu_sc as plsc

F32 = jnp.float32
BF16 = jnp.bfloat16
U32 = jnp.uint32
I32 = jnp.int32

CONV_WIDTH = 31
CONV_HALO = 32
RET_HEADS = 4
RET_DIM = 128
RET_CHUNK = 128
ROPE_BASE = 10000.0
N_EXPERTS = 32
TOP_K = 4
SWIGLU_LIMIT = 7.0
SWIGLU_ALPHA = 1.702
RMS_EPS = 1e-6
LN_EPS = 1e-5
N_MOD = 6

VMEM_LIMIT_BYTES = 56 * 1024 * 1024
MOE_TOKEN_GROUPS = 2
FFN_TILE_ROWS = 1024
FFN_SUB_ROWS = 256


def _split_bf16(a):
    hi = a.astype(BF16)
    lo = (a - hi.astype(F32)).astype(BF16)
    return hi, lo


def _dot3(a, b_hi, b_lo):
    a_hi, a_lo = _split_bf16(a)
    d = functools.partial(jnp.dot, preferred_element_type=F32)
    return d(a_hi, b_hi) + (d(a_hi, b_lo) + d(a_lo, b_hi))


def _pack_bf16_pair(lo, hi):
    lo_bits = lax.bitcast_convert_type(lo.astype(BF16).astype(F32), U32)
    hi_bits = lax.bitcast_convert_type(hi.astype(BF16).astype(F32), U32)
    return (lo_bits >> 16) | (hi_bits & jnp.uint32(0xFFFF0000))


def _unpack_bf16_pair(p):
    lo = lax.bitcast_convert_type(p << 16, F32)
    hi = lax.bitcast_convert_type(p & jnp.uint32(0xFFFF0000), F32)
    return lo, hi


def _mod_kernel(c_ref, whi_ref, wlo_ref, b_ref, o_ref):
    c = c_ref[...]
    c_act = c * jax.nn.sigmoid(c)
    o_ref[...] = _dot3(c_act, whi_ref[...], wlo_ref[...]) + b_ref[...]


def _mod_call(c, w_ada, b_ada):
    B, D = c.shape
    n = w_ada.shape[1]
    bn = 1024
    w_hi, w_lo = _split_bf16(w_ada)
    return pl.pallas_call(
        _mod_kernel,
        grid=(n // bn,),
        in_specs=[
            pl.BlockSpec((B, D), lambda j: (0, 0)),
            pl.BlockSpec((D, bn), lambda j: (0, j)),
            pl.BlockSpec((D, bn), lambda j: (0, j)),
            pl.BlockSpec((1, bn), lambda j: (0, j)),
        ],
        out_specs=pl.BlockSpec((B, bn), lambda j: (0, j)),
        out_shape=jax.ShapeDtypeStruct((B, n), F32),
        name="adaln_mod",
    )(c, w_hi, w_lo, b_ada.reshape(1, n))


def _retention_tables():
    h = np.arange(RET_HEADS, dtype=np.float32)
    log_gamma = np.log(1.0 - np.power(2.0, -5.0 - h)).astype(np.float32)
    idx = np.arange(RET_CHUNK, dtype=np.float32)
    diff = idx[:, None] - idx[None, :]
    causal = diff >= 0
    mask = np.where(causal[None], np.exp(log_gamma[:, None, None] * np.where(causal, diff, 0.0)[None]), 0.0)
    q_decay = np.exp(log_gamma[:, None] * (idx + 1.0))[..., None]
    k_decay = np.exp(log_gamma[:, None] * (RET_CHUNK - 1.0 - idx))[..., None]
    chunk_decay = np.exp(log_gamma * RET_CHUNK)
    return (mask.astype(np.float32), q_decay.astype(np.float32), k_decay.astype(np.float32),
            [float(v) for v in chunk_decay.astype(np.float32)])


def _mixer_kernel(chunk_decay, ts, x_ref, pos_ref, mod_ref, gmix_ref, win_ref, wdw_ref, bdw_ref,
                  gcl_ref, bcl_ref, gret_ref, wout_ref, rope_ref, dmask_ref, qdec_ref, kdec_ref,
                  gffn_ref, wr2_ref, wrhi_ref, br_ref, utri_ref, after_ref,
                  o_ref, h2p_ref, idx_ref, wts_ref, rank_ref, cnt_ref,
                  proj_ref, uext_ref, state_ref, cat_ref, conv_ref, carry_ref):
    del after_ref
    s = pl.program_id(1)
    conv_ch = wdw_ref.shape[1]
    ret_w = RET_HEADS * RET_DIM

    @pl.when(s == 0)
    def _():
        uext_ref[0:CONV_HALO, :] = jnp.zeros((CONV_HALO, conv_ch), F32)
        state_ref[...] = jnp.zeros_like(state_ref)

    x = x_ref[...]
    row_gain = gmix_ref[...] * (1.0 + mod_ref[1:2, :])
    h = (x * lax.rsqrt(jnp.mean(x * x, axis=-1, keepdims=True) + RMS_EPS) * row_gain + mod_ref[0:1, :]).astype(BF16)
    proj_ref[...] = jnp.dot(h, win_ref[...], preferred_element_type=F32)

    a = proj_ref[:, 0:conv_ch]
    b = proj_ref[:, conv_ch:2 * conv_ch]
    uext_ref[CONV_HALO:CONV_HALO + ts, :] = a * jax.nn.sigmoid(b)
    cb = 64
    lead = CONV_HALO - (CONV_WIDTH - 1)
    span = cb + CONV_HALO

    def conv_block(r0):
        for c0 in range(0, conv_ch, 128):
            xw = uext_ref[r0:r0 + span, c0:c0 + 128]
            acc = jnp.zeros((cb, 128), F32) + bdw_ref[:, c0:c0 + 128]
            for r in range(8):
                xr = xw if r == 0 else pltpu.roll(xw, span - r, 0)
                for q in range((lead + CONV_WIDTH - 1 - r) // 8 + 1):
                    j = 8 * q + r - lead
                    if 0 <= j < CONV_WIDTH:
                        acc = acc + wdw_ref[j:j + 1, c0:c0 + 128] * xr[8 * q:8 * q + cb]
            conv_ref[r0:r0 + cb, c0:c0 + 128] = acc
    rb = 2 * cb

    def conv_norm(r0):
        acc = conv_ref[r0:r0 + rb, 0:conv_ch]
        mu = jnp.mean(acc, axis=-1, keepdims=True)
        d = acc - mu
        var = jnp.mean(d * d, axis=-1, keepdims=True)
        ln = d * lax.rsqrt(var + LN_EPS) * gcl_ref[...] + bcl_ref[...]
        cat_ref[r0:r0 + rb, 0:conv_ch] = (ln * jax.nn.sigmoid(ln)).astype(BF16)

    hts = ts // 2
    low = lax.broadcasted_iota(I32, (hts, RET_DIM), 1) < RET_DIM // 2
    pos_t = jnp.broadcast_to(pos_ref[...].astype(F32), (RET_DIM, ts)).T
    posf = jnp.where(low, pos_t[0:hts], pos_t[hts:ts])
    ang = posf * rope_ref[...]
    cos_p = jnp.cos(ang)
    sin_p = jnp.sin(ang)
    cos_s = pltpu.roll(cos_p, RET_DIM // 2, 1)
    sin_s = pltpu.roll(sin_p, RET_DIM // 2, 1)
    cos2 = jnp.concatenate([jnp.where(low, cos_p, cos_s), jnp.where(low, cos_s, cos_p)], axis=0)
    sin2 = jnp.concatenate([jnp.where(low, -sin_p, sin_s), jnp.where(low, -sin_s, sin_p)], axis=0)
    q0 = 2 * conv_ch
    k0 = q0 + ret_w
    v0 = k0 + ret_w
    g0 = v0 + ret_w

    def retention_unit(hd, n):
        c0 = hd * RET_DIM
        r0 = n * RET_CHUNK
        rows = slice(r0, r0 + RET_CHUNK)
        cs = cos2[rows]
        sn = sin2[rows]
        q = proj_ref[rows, q0 + c0:q0 + c0 + RET_DIM]
        k = proj_ref[rows, k0 + c0:k0 + c0 + RET_DIM]
        v = proj_ref[rows, v0 + c0:v0 + c0 + RET_DIM].astype(BF16)
        g = proj_ref[rows, g0 + c0:g0 + c0 + RET_DIM]
        qr = q * cs + pltpu.roll(q, RET_DIM // 2, 1) * sn
        kr = k * cs + pltpu.roll(k, RET_DIM // 2, 1) * sn
        st = state_ref[hd]
        scores = lax.dot_general(qr.astype(BF16), kr.astype(BF16), (((1,), (1,)), ((), ())),
                                 preferred_element_type=F32) * dmask_ref[hd]
        inner = jnp.dot(scores.astype(BF16), v, preferred_element_type=F32)
        cross = jnp.dot((qr * qdec_ref[hd]).astype(BF16), st.astype(BF16), preferred_element_type=F32)
        kv = lax.dot_general((kr * kdec_ref[hd]).astype(BF16), v, (((0,), (0,)), ((), ())),
                             preferred_element_type=F32)
        state_ref[hd] = chunk_decay[hd] * st + kv
        r = inner + cross
        mu = jnp.mean(r, axis=-1, keepdims=True)
        d = r - mu
        var = jnp.mean(d * d, axis=-1, keepdims=True)
        rn = d * lax.rsqrt(var + LN_EPS) * gret_ref[:, c0:c0 + RET_DIM]
        cat_ref[rows, conv_ch + c0:conv_ch + c0 + RET_DIM] = (g * jax.nn.sigmoid(g) * rn).astype(BF16)

    units = [(hd, n) for n in range(ts // RET_CHUNK) for hd in range(RET_HEADS)]
    n_conv = ts // cb
    per = -(-len(units) // n_conv)
    for i in range(n_conv):
        conv_block(i * cb)
        if i % 2 == 1:
            conv_norm((i - 1) * cb)
        for hd, n in units[i * per:(i + 1) * per]:
            retention_unit(hd, n)
    uext_ref[0:CONV_HALO, :] = uext_ref[ts:ts + CONV_HALO, :]

    out = jnp.dot(cat_ref[...], wout_ref[...], preferred_element_type=F32)
    x1 = x + mod_ref[2:3, :] * out
    o_ref[...] = x1

    @pl.when((pl.program_id(0) == 0) & (s == 0))
    def _():
        carry_ref[...] = jnp.zeros_like(carry_ref)

    _route_tile(x1, mod_ref, gffn_ref, wr2_ref, wrhi_ref, br_ref, utri_ref,
                h2p_ref, idx_ref, wts_ref, rank_ref, cnt_ref, carry_ref)


def _mixer_call(x, positions, mod, g_mix, w_in, w_dw, b_dw, g_conv_ln, b_conv_ln, g_ret_norm, w_out,
                g_ffn, w_router, b_router, group, n_groups, after):
    S, D = x.shape[1:]
    B = x.shape[0] // n_groups
    b0 = group * B
    in_cols = w_in.shape[1]
    conv_ch = w_dw.shape[1]
    ts = min(512, S)
    mask, q_decay, k_decay, chunk_decay = _retention_tables()
    key_scale = np.float32(RET_DIM ** -0.5)
    half = RET_DIM // 2
    inv_freq = (ROPE_BASE ** (-np.arange(half, dtype=np.float32) / half)).astype(np.float32)
    rope = np.concatenate([inv_freq, inv_freq])[None, :]
    w_dw_p = jnp.zeros((CONV_HALO, conv_ch), F32).at[:CONV_WIDTH].set(w_dw)
    wr_hi, wr_lo = _split_bf16(w_router.T)
    utri = jnp.asarray(np.triu(np.ones((ts, ts), np.float32), 1), BF16)
    n_s = S // ts
    T = B * S
    full = lambda shape: pl.BlockSpec(shape, lambda b, s: (0,) * len(shape))
    return pl.pallas_call(
        functools.partial(_mixer_kernel, chunk_decay, ts),
        grid=(B, S // ts),
        in_specs=[
            pl.BlockSpec((None, ts, D), lambda b, s: (b0 + b, s, 0)),
            pl.BlockSpec((None, 1, ts), lambda b, s: (b0 + b, 0, s)),
            pl.BlockSpec((None, N_MOD, D), lambda b, s: (b0 + b, 0, 0)),
            full((1, D)),
            full((D, in_cols)),
            full((CONV_HALO, conv_ch)),
            full((1, conv_ch)),
            full((1, conv_ch)),
            full((1, conv_ch)),
            full((1, RET_HEADS * RET_DIM)),
            full((conv_ch + RET_HEADS * RET_DIM, D)),
            full((1, RET_DIM)),
            full((RET_HEADS, RET_CHUNK, RET_CHUNK)),
            full((RET_HEADS, RET_CHUNK, 1)),
            full((RET_HEADS, RET_CHUNK, 1)),
            full((1, D)),
            full((2 * N_EXPERTS, D)),
            full((N_EXPERTS, D)),
            full((N_EXPERTS, 1)),
            full((ts, ts)),
            pl.BlockSpec(memory_space=pl.ANY),
        ],
        out_specs=[
            pl.BlockSpec((None, ts, D), lambda b, s: (b, s, 0)),
            pl.BlockSpec((ts, D // 2), lambda b, s: (b * n_s + s, 0)),
            pl.BlockSpec((ROUTE_ROWS, ts), lambda b, s: (0, b * n_s + s)),
            pl.BlockSpec((ROUTE_ROWS, ts), lambda b, s: (0, b * n_s + s)),
            pl.BlockSpec((ROUTE_ROWS, ts), lambda b, s: (0, b * n_s + s)),
            full((N_EXPERTS, 1)),
        ],
        out_shape=[
            jax.ShapeDtypeStruct((B, S, D), F32),
            jax.ShapeDtypeStruct((T, D // 2), U32),
            jax.ShapeDtypeStruct((ROUTE_ROWS, T), I32),
            jax.ShapeDtypeStruct((ROUTE_ROWS, T), F32),
            jax.ShapeDtypeStruct((ROUTE_ROWS, T), I32),
            jax.ShapeDtypeStruct((N_EXPERTS, 1), I32),
        ],
        scratch_shapes=[
            pltpu.VMEM((ts, in_cols), F32),
            pltpu.VMEM((CONV_HALO + ts, conv_ch), F32),
            pltpu.VMEM((RET_HEADS, RET_DIM, RET_DIM), F32),
            pltpu.VMEM((ts, conv_ch + RET_HEADS * RET_DIM), BF16),
            pltpu.VMEM((ts, conv_ch), F32),
            pltpu.VMEM((N_EXPERTS, 1), F32),
        ],
        compiler_params=pltpu.CompilerParams(
            dimension_semantics=("arbitrary", "arbitrary"), vmem_limit_bytes=VMEM_LIMIT_BYTES),
        name="hybrid_mixer",
    )(x, positions.reshape(-1, 1, S), mod, g_mix.reshape(1, D), w_in.astype(BF16), w_dw_p,
      b_dw.reshape(1, -1), g_conv_ln.reshape(1, -1), b_conv_ln.reshape(1, -1), g_ret_norm.reshape(1, -1),
      w_out.astype(BF16), jnp.asarray(rope), jnp.asarray(mask * key_scale), jnp.asarray(q_decay),
      jnp.asarray(k_decay * key_scale), g_ffn.reshape(1, D), jnp.concatenate([wr_hi, wr_lo], axis=0), wr_hi,
      b_router.reshape(N_EXPERTS, 1), utri, after)


def _route_tile(x, mod_ref, gffn_ref, wr2_ref, wrhi_ref, br_ref, utri_ref,
                h2p_ref, idx_ref, wts_ref, rank_ref, cnt_ref, carry_ref):
    tr = x.shape[0]
    half = x.shape[1] // 2
    y = x * lax.rsqrt(jnp.mean(x * x, axis=-1, keepdims=True) + RMS_EPS) * gffn_ref[...]
    h2 = y * (1.0 + mod_ref[4:5, :]) + mod_ref[3:4, :]
    h2p_ref[...] = _pack_bf16_pair(h2[:, :half], h2[:, half:])

    h_hi, h_lo = _split_bf16(h2)
    nt = (((1,), (1,)), ((), ()))
    r = lax.dot_general(wr2_ref[...], h_hi, nt, preferred_element_type=F32)
    r2 = lax.dot_general(wrhi_ref[...], h_lo, nt, preferred_element_type=F32)
    l = r[:N_EXPERTS] + (r[N_EXPERTS:] + r2) + br_ref[...]
    eid = lax.broadcasted_iota(I32, l.shape, 0)
    vals, sels, idxs = [], [], []
    for _ in range(TOP_K):
        m = jnp.max(l, axis=0, keepdims=True)
        ik = jnp.min(jnp.where(l == m, eid, N_EXPERTS), axis=0, keepdims=True)
        sel = eid == ik
        vals.append(m)
        sels.append(sel)
        idxs.append(ik)
        l = jnp.where(sel, -jnp.inf, l)
    exps = [jnp.exp(v - vals[0]) for v in vals]
    denom = exps[0] + exps[1] + exps[2] + exps[3]
    member = jnp.zeros(l.shape, F32)
    for sel in sels:
        member = member + sel.astype(F32)
    before = jnp.dot(member.astype(BF16), utri_ref[...], preferred_element_type=F32) + carry_ref[...]
    ranks = [jnp.sum(jnp.where(sel, before, 0.0), axis=0, keepdims=True) for sel in sels]
    carry_ref[...] = carry_ref[...] + jnp.sum(member, axis=1, keepdims=True)
    cnt_ref[...] = carry_ref[...].astype(I32)

    def rows(pieces, n_rows, dtype):
        rid = lax.broadcasted_iota(I32, (n_rows, tr), 0)
        out = jnp.zeros((n_rows, tr), dtype)
        for k, p in enumerate(pieces):
            out = jnp.where(rid == k, p.astype(dtype), out)
        return out

    idx_ref[...] = rows(idxs, idx_ref.shape[0], I32)
    rank_ref[...] = rows(ranks, rank_ref.shape[0], I32)
    wts_ref[...] = rows([e / denom for e in exps], wts_ref.shape[0], F32)


ROUTE_ROWS = 8
ROUTE_LANES = 128


def _pos_kernel(starts_ref, idx_ref, rank_ref, pos_ref):
    idx = idx_ref[...]
    pos = rank_ref[...]
    for e in range(N_EXPERTS):
        pos = pos + jnp.where(idx == e, starts_ref[e], 0)
    pos_ref[...] = pos


def _pos_call(starts, idx, rank):
    rows, T = idx.shape
    tp = min(4096, T)
    spec = pl.BlockSpec((rows, tp), lambda i, st: (0, i))
    return pl.pallas_call(
        _pos_kernel,
        grid_spec=pltpu.PrefetchScalarGridSpec(
            num_scalar_prefetch=1, grid=(T // tp,), in_specs=[spec, spec], out_specs=spec),
        out_shape=jax.ShapeDtypeStruct((rows, T), I32),
        name="moe_sorted_pos",
    )(starts.astype(I32), idx, rank)


SC_ROWS = 128


V7X_SC_CORES = 2
V7X_SC_SUBCORES = 16


def _sc_workers():
    return V7X_SC_CORES, V7X_SC_SUBCORES


def _sc_mesh():
    return plsc.VectorSubcoreMesh(core_axis_name="c", subcore_axis_name="s",
                                  num_cores=V7X_SC_CORES, num_subcores=V7X_SC_SUBCORES)


def _sc_scatter_call(h2p, pos_km, n_rows):
    T, W = h2p.shape
    nc, ns = _sc_workers()
    n = SC_ROWS
    per_w = T // (nc * ns * n)

    def body(h2p_hbm, pos_hbm, xs_hbm, i0, i1, i2, i3, rows_v, sem):
        wid = lax.axis_index("s") * nc + lax.axis_index("c")
        idx_refs = (i0, i1, i2, i3)

        @pl.loop(0, per_w)
        def _(j):
            t0 = (wid * per_w + j) * n
            pltpu.sync_copy(h2p_hbm.at[pl.ds(t0, n)], rows_v)
            for k in range(TOP_K):
                pltpu.sync_copy(pos_hbm.at[pl.ds(k * T + t0, n)], idx_refs[k])
            copies = [pltpu.async_copy(rows_v, xs_hbm.at[idx_refs[k]], sem) for k in range(TOP_K)]
            for cp in copies:
                cp.wait()

    return pl.kernel(
        body,
        out_type=jax.ShapeDtypeStruct((n_rows, W), h2p.dtype),
        mesh=_sc_mesh(),
        scratch_types=[pltpu.VMEM((n,), I32)] * TOP_K + [pltpu.VMEM((n, W), h2p.dtype), pltpu.SemaphoreType.DMA],
        name="moe_scatter_rows",
    )(h2p, pos_km)


def _sc_gather_call(ys, pos_km):
    P = pos_km.shape[0]
    W = ys.shape[1]
    nc, ns = _sc_workers()
    n = SC_ROWS
    per_w = P // (nc * ns * n)

    def body(ys_hbm, pos_hbm, yp_hbm, idx_v, rows_v, sem):
        wid = lax.axis_index("s") * nc + lax.axis_index("c")

        @pl.loop(0, per_w)
        def _(j):
            p0 = (wid * per_w + j) * n
            pltpu.sync_copy(pos_hbm.at[pl.ds(p0, n)], idx_v)
            pltpu.async_copy(ys_hbm.at[idx_v], rows_v, sem).wait()
            pltpu.sync_copy(rows_v, yp_hbm.at[pl.ds(p0, n)])

    return pl.kernel(
        body,
        out_type=jax.ShapeDtypeStruct((P, W), ys.dtype),
        mesh=_sc_mesh(),
        scratch_types=[pltpu.VMEM((n,), I32), pltpu.VMEM((n, W), ys.dtype), pltpu.SemaphoreType.DMA],
        name="moe_gather_rows",
    )(ys, pos_km)


def _ffn_kernel(te_ref, tv_ref, nx_ref, nu_ref, xs_ref, wgu_hbm, bgu_ref, wd_hbm, bd_ref, y_ref,
                wgu_f32, wd_f32, wgu_bf, wd_bf, sems):
    del nu_ref
    i = pl.program_id(0)
    valid = tv_ref[i]
    tm, half = xs_ref.shape
    d_exp = wd_bf.shape[0]

    def weight_copies(e):
        return (pltpu.make_async_copy(wgu_hbm.at[e], wgu_f32, sems.at[0]),
                pltpu.make_async_copy(wd_hbm.at[e], wd_f32, sems.at[1]))

    @pl.when(i == 0)
    def _():
        for cp in weight_copies(te_ref[0]):
            cp.start(priority=1)

    @pl.when(((i == 0) | (te_ref[i] != te_ref[jnp.maximum(i - 1, 0)])) & (valid > 0))
    def _():
        for cp in weight_copies(te_ref[i]):
            cp.wait()
        wgu_bf[...] = wgu_f32[...].astype(BF16)
        wd_bf[...] = wd_f32[...].astype(BF16)

        @pl.when(nx_ref[i] >= 0)
        def _():
            for cp in weight_copies(nx_ref[i]):
                cp.start(priority=1)

    def sub_tile(r0, n_rows, masked):
        rows = slice(r0, r0 + n_rows)
        lo, hi = _unpack_bf16_pair(xs_ref[rows, :])
        if masked:
            keep = lax.broadcasted_iota(I32, (n_rows, half), 0) < valid - r0
            lo = jnp.where(keep, lo, 0.0)
            hi = jnp.where(keep, hi, 0.0)
        xt = jnp.concatenate([lo.astype(BF16), hi.astype(BF16)], axis=1)
        d = functools.partial(jnp.dot, preferred_element_type=F32)
        nb = 256

        def up(c):
            gate = d(xt, wgu_bf[:, c:c + nb]) + bgu_ref[:, c:c + nb]
            lin = d(xt, wgu_bf[:, d_exp + c:d_exp + c + nb]) + bgu_ref[:, d_exp + c:d_exp + c + nb]
            return gate, lin

        def activate(gate, lin):
            gate = jnp.minimum(gate, SWIGLU_LIMIT)
            lin = jnp.clip(lin, -SWIGLU_LIMIT, SWIGLU_LIMIT)
            return (gate * jax.nn.sigmoid(SWIGLU_ALPHA * gate) * (lin + 1.0)).astype(BF16)

        chunks = list(range(0, d_exp, nb))
        acts = []
        pending = up(chunks[0])
        for n in range(len(chunks)):
            nxt = up(chunks[n + 1]) if n + 1 < len(chunks) else None
            acts.append(activate(*pending))
            pending = nxt
        out = d(jnp.concatenate(acts, axis=1), wd_bf[...]) + bd_ref[...]
        y_ref[rows, :] = _pack_bf16_pair(out[:, :half], out[:, half:])

    pl.when(valid == tm)(functools.partial(sub_tile, 0, tm, False))
    for n_sub in range(1, tm // FFN_SUB_ROWS + 1):
        n_rows = n_sub * FFN_SUB_ROWS
        pl.when((valid > n_rows - FFN_SUB_ROWS) & (valid <= n_rows) & (valid < tm))(
            functools.partial(sub_tile, 0, n_rows, True))


def _ffn_call(tile_expert, tile_valid, tile_next, n_used, xs, w_gu, b_gu, w_down, b_down, tm):
    R, half = xs.shape
    E, D, two_f = w_gu.shape
    d_exp = w_down.shape[1]
    rows = lambda i, te, tv, nx, nu: (jnp.minimum(i, nu[0] - 1), 0)
    grid_spec = pltpu.PrefetchScalarGridSpec(
        num_scalar_prefetch=4,
        grid=(R // tm,),
        in_specs=[
            pl.BlockSpec((tm, half), rows),
            pl.BlockSpec(memory_space=pl.ANY),
            pl.BlockSpec((None, 1, two_f), lambda i, te, tv, nx, nu: (te[i], 0, 0)),
            pl.BlockSpec(memory_space=pl.ANY),
            pl.BlockSpec((None, 1, D), lambda i, te, tv, nx, nu: (te[i], 0, 0)),
        ],
        out_specs=pl.BlockSpec((tm, half), rows),
        scratch_shapes=[
            pltpu.VMEM((D, two_f), F32), pltpu.VMEM((d_exp, D), F32),
            pltpu.VMEM((D, two_f), BF16), pltpu.VMEM((d_exp, D), BF16),
            pltpu.SemaphoreType.DMA((2,)),
        ],
    )
    return pl.pallas_call(
        _ffn_kernel,
        grid_spec=grid_spec,
        out_shape=jax.ShapeDtypeStruct((R, half), U32),
        compiler_params=pltpu.CompilerParams(
            dimension_semantics=("arbitrary",), vmem_limit_bytes=VMEM_LIMIT_BYTES),
        name="moe_ffn",
    )(tile_expert, tile_valid, tile_next, n_used, xs, w_gu, b_gu.reshape(E, 1, two_f), w_down,
      b_down.reshape(E, 1, D))


def _final_kernel(x_ref, yp_ref, wts_ref, mod_ref, gfin_ref, *rest):
    o_ref = rest[-1]
    x = x_ref[...]
    half = x.shape[1] // 2
    w = jnp.concatenate([wts_ref[...], jnp.zeros((ROUTE_LANES - ROUTE_ROWS, x.shape[0]), F32)], axis=0).T
    lo = jnp.zeros((x.shape[0], half), F32)
    hi = jnp.zeros((x.shape[0], half), F32)
    for k in range(TOP_K):
        l, h = _unpack_bf16_pair(yp_ref[k])
        lo = lo + w[:, k:k + 1] * l
        hi = hi + w[:, k:k + 1] * h
    gate = mod_ref[5:6, :]
    x_lo = x[:, :half] + gate[:, :half] * lo
    x_hi = x[:, half:] + gate[:, half:] * hi
    ms = (jnp.sum(x_lo * x_lo, axis=-1, keepdims=True) + jnp.sum(x_hi * x_hi, axis=-1, keepdims=True)) / x.shape[1]
    inv = lax.rsqrt(ms + RMS_EPS)
    o_ref[:, :half] = x_lo * inv * gfin_ref[:, :half]
    o_ref[:, half:] = x_hi * inv * gfin_ref[:, half:]


def _final_call(x1, yp, wts, mod, g_final, tokens_per_batch, group, n_groups, prev_out):
    T, D = x1.shape
    T_all = T * n_groups
    tq = min(512, tokens_per_batch)
    per_b = tokens_per_batch // tq
    first = group * (T // tq)
    in_specs = [
        pl.BlockSpec((tq, D), lambda i: (i, 0)),
        pl.BlockSpec((TOP_K, tq, D // 2), lambda i: (0, i, 0)),
        pl.BlockSpec((ROUTE_ROWS, tq), lambda i: (0, i)),
        pl.BlockSpec((None, N_MOD, D), lambda i: ((first + i) // per_b, 0, 0)),
        pl.BlockSpec((1, D), lambda i: (0, 0)),
    ]
    args = [x1, yp, wts, mod, g_final.reshape(1, D)]
    aliases = {}
    if prev_out is not None:
        in_specs.append(pl.BlockSpec(memory_space=pl.ANY))
        args.append(prev_out)
        aliases = {len(args) - 1: 0}
    return pl.pallas_call(
        _final_kernel,
        grid=(T // tq,),
        in_specs=in_specs,
        out_specs=pl.BlockSpec((tq, D), lambda i: (first + i, 0)),
        out_shape=jax.ShapeDtypeStruct((T_all, D), F32),
        input_output_aliases=aliases,
        compiler_params=pltpu.CompilerParams(
            dimension_semantics=("arbitrary",), vmem_limit_bytes=VMEM_LIMIT_BYTES),
        name="moe_combine_final",
    )(*args)


def _group_layout(counts, n_tiles, tm):
    padded = ((counts + tm - 1) // tm) * tm
    ends = jnp.cumsum(padded)
    starts = ends - padded
    tile_row = jnp.arange(n_tiles, dtype=I32) * tm
    te = jnp.minimum(jnp.sum(tile_row[:, None] >= ends[None, :], axis=1), N_EXPERTS - 1).astype(I32)
    eids = jnp.arange(N_EXPERTS, dtype=I32)
    mine = te[:, None] == eids[None, :]
    lookup = lambda table: jnp.sum(jnp.where(mine, table[None, :], 0), axis=1)
    tv = jnp.clip(lookup(counts) - (tile_row - lookup(starts)), 0, tm).astype(I32)
    later = (eids[None, :] > eids[:, None]) & (counts[None, :] > 0)
    nxt = jnp.min(jnp.where(later, eids[None, :], N_EXPERTS), axis=1)
    nx = lookup(jnp.where(nxt < N_EXPERTS, nxt, -1)).astype(I32)
    n_used = jnp.sum(tv > 0).astype(I32).reshape(1)
    return starts, te, tv, nx, n_used


def kernel(x, c, positions, w_ada, b_ada, g_mix, w_in, w_dw, b_dw, g_conv_ln, b_conv_ln, g_ret_norm,
           w_out, g_ffn, w_router, b_router, w_gu, b_gu, w_down, b_down, g_final):
    B, S, D = x.shape
    T = B * S
    assert w_ada.shape[0] == 1, "single-layer block: the final norm directly follows layer 0"
    xt = x
    for l in range(1):
        mod = _mod_call(c, w_ada[l], b_ada[l]).reshape(B, N_MOD, D)
        n_groups = MOE_TOKEN_GROUPS if B % MOE_TOKEN_GROUPS == 0 else 1
        Bg = B // n_groups
        Tg = Bg * S
        tm = FFN_TILE_ROWS if Tg * TOP_K >= FFN_TILE_ROWS * N_EXPERTS * 4 else FFN_SUB_ROWS
        n_tiles = (Tg * TOP_K) // tm + N_EXPERTS
        x1s, scattered = [], []
        pos_km = jnp.zeros((TOP_K * Tg,), I32)
        for g in range(n_groups):
            x1, h2p, idx, wts, rank, counts = _mixer_call(
                xt, positions, mod, g_mix[l], w_in[l], w_dw[l], b_dw[l], g_conv_ln[l], b_conv_ln[l],
                g_ret_norm[l], w_out[l], g_ffn[l], w_router[l], b_router[l], g, n_groups, pos_km)
            x1 = x1.reshape(Tg, D)
            starts, te, tv, nx, n_used = _group_layout(counts[:, 0], n_tiles, tm)
            pos_km = _pos_call(starts, idx, rank)[:TOP_K].reshape(-1)
            x1s.append(x1)
            scattered.append((_sc_scatter_call(h2p, pos_km, n_tiles * tm), pos_km, wts, te, tv, nx, n_used))
        out = None
        for g, (xs, pos_km, wts, te, tv, nx, n_used) in enumerate(scattered):
            ys = _ffn_call(te, tv, nx, n_used, xs, w_gu[l], b_gu[l], w_down[l], b_down[l], tm)
            yp = _sc_gather_call(ys, pos_km)
            out = _final_call(x1s[g], yp.reshape(TOP_K, Tg, D // 2), wts, mod, g_final, S, g, n_groups, out)
        xt = out
    return xt.reshape(B, S, D)
```

```python
import functools

import numpy as np
import jax
import jax.numpy as jnp
from jax import lax
from jax.experimental import pallas as pl
from jax.experimental.pallas import tpu as pltpu
from jax.experimental.pallas import tpu_sc as plsc

F32 = jnp.float32
BF16 = jnp.bfloat16
U32 = jnp.uint32
I32 = jnp.int32

CONV_WIDTH = 31
CONV_HALO = 32
RET_HEADS = 4
RET_DIM = 128
RET_CHUNK = 128
ROPE_BASE = 10000.0
N_EXPERTS = 32
TOP_K = 4
SWIGLU_LIMIT = 7.0
SWIGLU_ALPHA = 1.702
RMS_EPS = 1e-6
LN_EPS = 1e-5
N_MOD = 6

VMEM_LIMIT_BYTES = 56 * 1024 * 1024
MOE_TOKEN_GROUPS = 2
COMBINE_INPUT_BUFFERS = 3
FFN_TILE_ROWS = 1024
FFN_SUB_ROWS = 256


def _split_bf16(a):
    hi = a.astype(BF16)
    lo = (a - hi.astype(F32)).astype(BF16)
    return hi, lo


def _dot3(a, b_hi, b_lo):
    a_hi, a_lo = _split_bf16(a)
    d = functools.partial(jnp.dot, preferred_element_type=F32)
    return d(a_hi, b_hi) + (d(a_hi, b_lo) + d(a_lo, b_hi))


def _pack_bf16_pair(lo, hi):
    lo_bits = lax.bitcast_convert_type(lo.astype(BF16).astype(F32), U32)
    hi_bits = lax.bitcast_convert_type(hi.astype(BF16).astype(F32), U32)
    return (lo_bits >> 16) | (hi_bits & jnp.uint32(0xFFFF0000))


def _unpack_bf16_pair(p):
    lo = lax.bitcast_convert_type(p << 16, F32)
    hi = lax.bitcast_convert_type(p & jnp.uint32(0xFFFF0000), F32)
    return lo, hi


def _mod_kernel(c_ref, whi_ref, wlo_ref, b_ref, o_ref):
    c = c_ref[...]
    c_act = c * jax.nn.sigmoid(c)
    o_ref[...] = _dot3(c_act, whi_ref[...], wlo_ref[...]) + b_ref[...]


def _mod_call(c, w_ada, b_ada):
    B, D = c.shape
    n = w_ada.shape[1]
    bn = 1024
    w_hi, w_lo = _split_bf16(w_ada)
    return pl.pallas_call(
        _mod_kernel,
        grid=(n // bn,),
        in_specs=[
            pl.BlockSpec((B, D), lambda j: (0, 0)),
            pl.BlockSpec((D, bn), lambda j: (0, j)),
            pl.BlockSpec((D, bn), lambda j: (0, j)),
            pl.BlockSpec((1, bn), lambda j: (0, j)),
        ],
        out_specs=pl.BlockSpec((B, bn), lambda j: (0, j)),
        out_shape=jax.ShapeDtypeStruct((B, n), F32),
        name="adaln_mod",
    )(c, w_hi, w_lo, b_ada.reshape(1, n))


def _retention_tables():
    h = np.arange(RET_HEADS, dtype=np.float32)
    log_gamma = np.log(1.0 - np.power(2.0, -5.0 - h)).astype(np.float32)
    idx = np.arange(RET_CHUNK, dtype=np.float32)
    diff = idx[:, None] - idx[None, :]
    causal = diff >= 0
    mask = np.where(causal[None], np.exp(log_gamma[:, None, None] * np.where(causal, diff, 0.0)[None]), 0.0)
    q_decay = np.exp(log_gamma[:, None] * (idx + 1.0))[..., None]
    k_decay = np.exp(log_gamma[:, None] * (RET_CHUNK - 1.0 - idx))[..., None]
    chunk_decay = np.exp(log_gamma * RET_CHUNK)
    return (mask.astype(np.float32), q_decay.astype(np.float32), k_decay.astype(np.float32),
            [float(v) for v in chunk_decay.astype(np.float32)])


def _mixer_kernel(chunk_decay, ts, x_ref, pos_ref, mod_ref, gmix_ref, win_ref, wdw_ref, bdw_ref,
                  gcl_ref, bcl_ref, gret_ref, wout_ref, rope_ref, dmask_ref, qdec_ref, kdec_ref,
                  gffn_ref, wr2_ref, wrhi_ref, br_ref, utri_ref, after_ref,
                  o_ref, h2p_ref, idx_ref, wts_ref, rank_ref, cnt_ref,
                  proj_ref, uext_ref, state_ref, cat_ref, conv_ref, carry_ref):
    del after_ref
    s = pl.program_id(1)
    conv_ch = wdw_ref.shape[1]
    ret_w = RET_HEADS * RET_DIM

    @pl.when(s == 0)
    def _():
        uext_ref[0:CONV_HALO, :] = jnp.zeros((CONV_HALO, conv_ch), F32)
        state_ref[...] = jnp.zeros_like(state_ref)

    x = x_ref[...]
    row_gain = gmix_ref[...] * (1.0 + mod_ref[1:2, :])
    h = (x * lax.rsqrt(jnp.mean(x * x, axis=-1, keepdims=True) + RMS_EPS) * row_gain + mod_ref[0:1, :]).astype(BF16)
    proj_ref[...] = jnp.dot(h, win_ref[...], preferred_element_type=F32)

    a = proj_ref[:, 0:conv_ch]
    b = proj_ref[:, conv_ch:2 * conv_ch]
    uext_ref[CONV_HALO:CONV_HALO + ts, :] = a * jax.nn.sigmoid(b)
    cb = 64
    lead = CONV_HALO - (CONV_WIDTH - 1)
    span = cb + CONV_HALO

    def conv_block(r0):
        for c0 in range(0, conv_ch, 128):
            xw = uext_ref[r0:r0 + span, c0:c0 + 128]
            acc = jnp.zeros((cb, 128), F32) + bdw_ref[:, c0:c0 + 128]
            for r in range(8):
                xr = xw if r == 0 else pltpu.roll(xw, span - r, 0)
                for q in range((lead + CONV_WIDTH - 1 - r) // 8 + 1):
                    j = 8 * q + r - lead
                    if 0 <= j < CONV_WIDTH:
                        acc = acc + wdw_ref[j:j + 1, c0:c0 + 128] * xr[8 * q:8 * q + cb]
            conv_ref[r0:r0 + cb, c0:c0 + 128] = acc
    rb = 2 * cb

    def conv_norm(r0):
        acc = conv_ref[r0:r0 + rb, 0:conv_ch]
        mu = jnp.mean(acc, axis=-1, keepdims=True)
        d = acc - mu
        var = jnp.mean(d * d, axis=-1, keepdims=True)
        ln = d * lax.rsqrt(var + LN_EPS) * gcl_ref[...] + bcl_ref[...]
        cat_ref[r0:r0 + rb, 0:conv_ch] = (ln * jax.nn.sigmoid(ln)).astype(BF16)

    hts = ts // 2
    low = lax.broadcasted_iota(I32, (hts, RET_DIM), 1) < RET_DIM // 2
    pos_t = jnp.broadcast_to(pos_ref[...].astype(F32), (RET_DIM, ts)).T
    posf = jnp.where(low, pos_t[0:hts], pos_t[hts:ts])
    ang = posf * rope_ref[...]
    cos_p = jnp.cos(ang)
    sin_p = jnp.sin(ang)
    cos_s = pltpu.roll(cos_p, RET_DIM // 2, 1)
    sin_s = pltpu.roll(sin_p, RET_DIM // 2, 1)
    cos2 = jnp.concatenate([jnp.where(low, cos_p, cos_s), jnp.where(low, cos_s, cos_p)], axis=0)
    sin2 = jnp.concatenate([jnp.where(low, -sin_p, sin_s), jnp.where(low, -sin_s, sin_p)], axis=0)
    q0 = 2 * conv_ch
    k0 = q0 + ret_w
    v0 = k0 + ret_w
    g0 = v0 + ret_w

    def retention_unit(hd, n):
        c0 = hd * RET_DIM
        r0 = n * RET_CHUNK
        rows = slice(r0, r0 + RET_CHUNK)
        cs = cos2[rows]
        sn = sin2[rows]
        q = proj_ref[rows, q0 + c0:q0 + c0 + RET_DIM]
        k = proj_ref[rows, k0 + c0:k0 + c0 + RET_DIM]
        v = proj_ref[rows, v0 + c0:v0 + c0 + RET_DIM].astype(BF16)
        g = proj_ref[rows, g0 + c0:g0 + c0 + RET_DIM]
        qr = q * cs + pltpu.roll(q, RET_DIM // 2, 1) * sn
        kr = k * cs + pltpu.roll(k, RET_DIM // 2, 1) * sn
        st = state_ref[hd]
        scores = lax.dot_general(qr.astype(BF16), kr.astype(BF16), (((1,), (1,)), ((), ())),
                                 preferred_element_type=F32) * dmask_ref[hd]
        inner = jnp.dot(scores.astype(BF16), v, preferred_element_type=F32)
        cross = jnp.dot((qr * qdec_ref[hd]).astype(BF16), st.astype(BF16), preferred_element_type=F32)
        kv = lax.dot_general((kr * kdec_ref[hd]).astype(BF16), v, (((0,), (0,)), ((), ())),
                             preferred_element_type=F32)
        state_ref[hd] = chunk_decay[hd] * st + kv
        r = inner + cross
        mu = jnp.mean(r, axis=-1, keepdims=True)
        d = r - mu
        var = jnp.mean(d * d, axis=-1, keepdims=True)
        rn = d * lax.rsqrt(var + LN_EPS) * gret_ref[:, c0:c0 + RET_DIM]
        cat_ref[rows, conv_ch + c0:conv_ch + c0 + RET_DIM] = (g * jax.nn.sigmoid(g) * rn).astype(BF16)

    units = [(hd, n) for n in range(ts // RET_CHUNK) for hd in range(RET_HEADS)]
    n_conv = ts // cb
    per = -(-len(units) // n_conv)
    for i in range(n_conv):
        conv_block(i * cb)
        if i % 2 == 1:
            conv_norm((i - 1) * cb)
        for hd, n in units[i * per:(i + 1) * per]:
            retention_unit(hd, n)
    uext_ref[0:CONV_HALO, :] = uext_ref[ts:ts + CONV_HALO, :]

    out = jnp.dot(cat_ref[...], wout_ref[...], preferred_element_type=F32)
    x1 = x + mod_ref[2:3, :] * out
    o_ref[...] = x1

    @pl.when((pl.program_id(0) == 0) & (s == 0))
    def _():
        carry_ref[...] = jnp.zeros_like(carry_ref)

    _route_tile(x1, mod_ref, gffn_ref, wr2_ref, wrhi_ref, br_ref, utri_ref,
                h2p_ref, idx_ref, wts_ref, rank_ref, cnt_ref, carry_ref)


def _mixer_call(x, positions, mod, g_mix, w_in, w_dw, b_dw, g_conv_ln, b_conv_ln, g_ret_norm, w_out,
                g_ffn, w_router, b_router, group, n_groups, after):
    S, D = x.shape[1:]
    B = x.shape[0] // n_groups
    b0 = group * B
    in_cols = w_in.shape[1]
    conv_ch = w_dw.shape[1]
    ts = min(512, S)
    mask, q_decay, k_decay, chunk_decay = _retention_tables()
    key_scale = np.float32(RET_DIM ** -0.5)
    half = RET_DIM // 2
    inv_freq = (ROPE_BASE ** (-np.arange(half, dtype=np.float32) / half)).astype(np.float32)
    rope = np.concatenate([inv_freq, inv_freq])[None, :]
    w_dw_p = jnp.zeros((CONV_HALO, conv_ch), F32).at[:CONV_WIDTH].set(w_dw)
    wr_hi, wr_lo = _split_bf16(w_router.T)
    utri = jnp.asarray(np.triu(np.ones((ts, ts), np.float32), 1), BF16)
    n_s = S // ts
    T = B * S
    full = lambda shape: pl.BlockSpec(shape, lambda b, s: (0,) * len(shape))
    return pl.pallas_call(
        functools.partial(_mixer_kernel, chunk_decay, ts),
        grid=(B, S // ts),
        in_specs=[
            pl.BlockSpec((None, ts, D), lambda b, s: (b0 + b, s, 0)),
            pl.BlockSpec((None, 1, ts), lambda b, s: (b0 + b, 0, s)),
            pl.BlockSpec((None, N_MOD, D), lambda b, s: (b0 + b, 0, 0)),
            full((1, D)),
            full((D, in_cols)),
            full((CONV_HALO, conv_ch)),
            full((1, conv_ch)),
            full((1, conv_ch)),
            full((1, conv_ch)),
            full((1, RET_HEADS * RET_DIM)),
            full((conv_ch + RET_HEADS * RET_DIM, D)),
            full((1, RET_DIM)),
            full((RET_HEADS, RET_CHUNK, RET_CHUNK)),
            full((RET_HEADS, RET_CHUNK, 1)),
            full((RET_HEADS, RET_CHUNK, 1)),
            full((1, D)),
            full((2 * N_EXPERTS, D)),
            full((N_EXPERTS, D)),
            full((N_EXPERTS, 1)),
            full((ts, ts)),
            pl.BlockSpec(memory_space=pl.ANY),
        ],
        out_specs=[
            pl.BlockSpec((None, ts, D), lambda b, s: (b, s, 0)),
            pl.BlockSpec((ts, D // 2), lambda b, s: (b * n_s + s, 0)),
            pl.BlockSpec((ROUTE_ROWS, ts), lambda b, s: (0, b * n_s + s)),
            pl.BlockSpec((ROUTE_ROWS, ts), lambda b, s: (0, b * n_s + s)),
            pl.BlockSpec((ROUTE_ROWS, ts), lambda b, s: (0, b * n_s + s)),
            full((N_EXPERTS, 1)),
        ],
        out_shape=[
            jax.ShapeDtypeStruct((B, S, D), F32),
            jax.ShapeDtypeStruct((T, D // 2), U32),
            jax.ShapeDtypeStruct((ROUTE_ROWS, T), I32),
            jax.ShapeDtypeStruct((ROUTE_ROWS, T), F32),
            jax.ShapeDtypeStruct((ROUTE_ROWS, T), I32),
            jax.ShapeDtypeStruct((N_EXPERTS, 1), I32),
        ],
        scratch_shapes=[
            pltpu.VMEM((ts, in_cols), F32),
            pltpu.VMEM((CONV_HALO + ts, conv_ch), F32),
            pltpu.VMEM((RET_HEADS, RET_DIM, RET_DIM), F32),
            pltpu.VMEM((ts, conv_ch + RET_HEADS * RET_DIM), BF16),
            pltpu.VMEM((ts, conv_ch), F32),
            pltpu.VMEM((N_EXPERTS, 1), F32),
        ],
        compiler_params=pltpu.CompilerParams(
            dimension_semantics=("arbitrary", "arbitrary"), vmem_limit_bytes=VMEM_LIMIT_BYTES),
        name="hybrid_mixer",
    )(x, positions.reshape(-1, 1, S), mod, g_mix.reshape(1, D), w_in.astype(BF16), w_dw_p,
      b_dw.reshape(1, -1), g_conv_ln.reshape(1, -1), b_conv_ln.reshape(1, -1), g_ret_norm.reshape(1, -1),
      w_out.astype(BF16), jnp.asarray(rope), jnp.asarray(mask * key_scale), jnp.asarray(q_decay),
      jnp.asarray(k_decay * key_scale), g_ffn.reshape(1, D), jnp.concatenate([wr_hi, wr_lo], axis=0), wr_hi,
      b_router.reshape(N_EXPERTS, 1), utri, after)


def _route_tile(x, mod_ref, gffn_ref, wr2_ref, wrhi_ref, br_ref, utri_ref,
                h2p_ref, idx_ref, wts_ref, rank_ref, cnt_ref, carry_ref):
    tr = x.shape[0]
    half = x.shape[1] // 2
    y = x * lax.rsqrt(jnp.mean(x * x, axis=-1, keepdims=True) + RMS_EPS) * gffn_ref[...]
    h2 = y * (1.0 + mod_ref[4:5, :]) + mod_ref[3:4, :]
    h2p_ref[...] = _pack_bf16_pair(h2[:, :half], h2[:, half:])

    h_hi, h_lo = _split_bf16(h2)
    nt = (((1,), (1,)), ((), ()))
    r = lax.dot_general(wr2_ref[...], h_hi, nt, preferred_element_type=F32)
    r2 = lax.dot_general(wrhi_ref[...], h_lo, nt, preferred_element_type=F32)
    l = r[:N_EXPERTS] + (r[N_EXPERTS:] + r2) + br_ref[...]
    eid = lax.broadcasted_iota(I32, l.shape, 0)
    vals, sels, idxs = [], [], []
    for _ in range(TOP_K):
        m = jnp.max(l, axis=0, keepdims=True)
        ik = jnp.min(jnp.where(l == m, eid, N_EXPERTS), axis=0, keepdims=True)
        sel = eid == ik
        vals.append(m)
        sels.append(sel)
        idxs.append(ik)
        l = jnp.where(sel, -jnp.inf, l)
    exps = [jnp.exp(v - vals[0]) for v in vals]
    denom = exps[0] + exps[1] + exps[2] + exps[3]
    member = jnp.zeros(l.shape, F32)
    for sel in sels:
        member = member + sel.astype(F32)
    before = jnp.dot(member.astype(BF16), utri_ref[...], preferred_element_type=F32) + carry_ref[...]
    ranks = [jnp.sum(jnp.where(sel, before, 0.0), axis=0, keepdims=True) for sel in sels]
    carry_ref[...] = carry_ref[...] + jnp.sum(member, axis=1, keepdims=True)
    cnt_ref[...] = carry_ref[...].astype(I32)

    def rows(pieces, n_rows, dtype):
        rid = lax.broadcasted_iota(I32, (n_rows, tr), 0)
        out = jnp.zeros((n_rows, tr), dtype)
        for k, p in enumerate(pieces):
            out = jnp.where(rid == k, p.astype(dtype), out)
        return out

    idx_ref[...] = rows(idxs, idx_ref.shape[0], I32)
    rank_ref[...] = rows(ranks, rank_ref.shape[0], I32)
    wts_ref[...] = rows([e / denom for e in exps], wts_ref.shape[0], F32)


ROUTE_ROWS = 8
ROUTE_LANES = 128


def _pos_kernel(starts_ref, idx_ref, rank_ref, pos_ref):
    idx = idx_ref[...]
    pos = rank_ref[...]
    for e in range(N_EXPERTS):
        pos = pos + jnp.where(idx == e, starts_ref[e], 0)
    pos_ref[...] = pos


def _pos_call(starts, idx, rank):
    rows, T = idx.shape
    tp = min(4096, T)
    spec = pl.BlockSpec((rows, tp), lambda i, st: (0, i))
    return pl.pallas_call(
        _pos_kernel,
        grid_spec=pltpu.PrefetchScalarGridSpec(
            num_scalar_prefetch=1, grid=(T // tp,), in_specs=[spec, spec], out_specs=spec),
        out_shape=jax.ShapeDtypeStruct((rows, T), I32),
        name="moe_sorted_pos",
    )(starts.astype(I32), idx, rank)


SC_ROWS = 128


V7X_SC_CORES = 2
V7X_SC_SUBCORES = 16


def _sc_workers():
    return V7X_SC_CORES, V7X_SC_SUBCORES


def _sc_mesh():
    return plsc.VectorSubcoreMesh(core_axis_name="c", subcore_axis_name="s",
                                  num_cores=V7X_SC_CORES, num_subcores=V7X_SC_SUBCORES)


def _sc_scatter_call(h2p, pos_km, n_rows):
    T, W = h2p.shape
    nc, ns = _sc_workers()
    n = SC_ROWS
    per_w = T // (nc * ns * n)

    def body(h2p_hbm, pos_hbm, xs_hbm, i0, i1, i2, i3, rows_v, sem):
        wid = lax.axis_index("s") * nc + lax.axis_index("c")
        idx_refs = (i0, i1, i2, i3)

        @pl.loop(0, per_w)
        def _(j):
            t0 = (wid * per_w + j) * n
            pltpu.sync_copy(h2p_hbm.at[pl.ds(t0, n)], rows_v)
            for k in range(TOP_K):
                pltpu.sync_copy(pos_hbm.at[pl.ds(k * T + t0, n)], idx_refs[k])
            copies = [pltpu.async_copy(rows_v, xs_hbm.at[idx_refs[k]], sem) for k in range(TOP_K)]
            for cp in copies:
                cp.wait()

    return pl.kernel(
        body,
        out_type=jax.ShapeDtypeStruct((n_rows, W), h2p.dtype),
        mesh=_sc_mesh(),
        scratch_types=[pltpu.VMEM((n,), I32)] * TOP_K + [pltpu.VMEM((n, W), h2p.dtype), pltpu.SemaphoreType.DMA],
        name="moe_scatter_rows",
    )(h2p, pos_km)


def _sc_gather_call(ys, pos_km):
    P = pos_km.shape[0]
    W = ys.shape[1]
    nc, ns = _sc_workers()
    n = SC_ROWS
    per_w = P // (nc * ns * n)

    def body(ys_hbm, pos_hbm, yp_hbm, idx_v, rows_v, sem):
        wid = lax.axis_index("s") * nc + lax.axis_index("c")

        @pl.loop(0, per_w)
        def _(j):
            p0 = (wid * per_w + j) * n
            pltpu.sync_copy(pos_hbm.at[pl.ds(p0, n)], idx_v)
            pltpu.async_copy(ys_hbm.at[idx_v], rows_v, sem).wait()
            pltpu.sync_copy(rows_v, yp_hbm.at[pl.ds(p0, n)])

    return pl.kernel(
        body,
        out_type=jax.ShapeDtypeStruct((P, W), ys.dtype),
        mesh=_sc_mesh(),
        scratch_types=[pltpu.VMEM((n,), I32), pltpu.VMEM((n, W), ys.dtype), pltpu.SemaphoreType.DMA],
        name="moe_gather_rows",
    )(ys, pos_km)


def _ffn_kernel(te_ref, tv_ref, nx_ref, nu_ref, xs_ref, wgu_hbm, bgu_ref, wd_hbm, bd_ref, y_ref,
                wgu_f32, wd_f32, wgu_bf, wd_bf, sems):
    del nu_ref
    i = pl.program_id(0)
    valid = tv_ref[i]
    tm, half = xs_ref.shape
    d_exp = wd_bf.shape[0]

    def weight_copies(e):
        return (pltpu.make_async_copy(wgu_hbm.at[e], wgu_f32, sems.at[0]),
                pltpu.make_async_copy(wd_hbm.at[e], wd_f32, sems.at[1]))

    @pl.when(i == 0)
    def _():
        for cp in weight_copies(te_ref[0]):
            cp.start(priority=1)

    @pl.when(((i == 0) | (te_ref[i] != te_ref[jnp.maximum(i - 1, 0)])) & (valid > 0))
    def _():
        for cp in weight_copies(te_ref[i]):
            cp.wait()
        wgu_bf[...] = wgu_f32[...].astype(BF16)
        wd_bf[...] = wd_f32[...].astype(BF16)

        @pl.when(nx_ref[i] >= 0)
        def _():
            for cp in weight_copies(nx_ref[i]):
                cp.start(priority=1)

    def sub_tile(r0, n_rows, masked):
        rows = slice(r0, r0 + n_rows)
        lo, hi = _unpack_bf16_pair(xs_ref[rows, :])
        if masked:
            keep = lax.broadcasted_iota(I32, (n_rows, half), 0) < valid - r0
            lo = jnp.where(keep, lo, 0.0)
            hi = jnp.where(keep, hi, 0.0)
        xt = jnp.concatenate([lo.astype(BF16), hi.astype(BF16)], axis=1)
        d = functools.partial(jnp.dot, preferred_element_type=F32)
        nb = 256

        def up(c):
            gate = d(xt, wgu_bf[:, c:c + nb]) + bgu_ref[:, c:c + nb]
            lin = d(xt, wgu_bf[:, d_exp + c:d_exp + c + nb]) + bgu_ref[:, d_exp + c:d_exp + c + nb]
            return gate, lin

        def activate(gate, lin):
            gate = jnp.minimum(gate, SWIGLU_LIMIT)
            lin = jnp.clip(lin, -SWIGLU_LIMIT, SWIGLU_LIMIT)
            return (gate * jax.nn.sigmoid(SWIGLU_ALPHA * gate) * (lin + 1.0)).astype(BF16)

        chunks = list(range(0, d_exp, nb))
        acts = []
        pending = up(chunks[0])
        for n in range(len(chunks)):
            nxt = up(chunks[n + 1]) if n + 1 < len(chunks) else None
            acts.append(activate(*pending))
            pending = nxt
        out = d(jnp.concatenate(acts, axis=1), wd_bf[...]) + bd_ref[...]
        y_ref[rows, :] = _pack_bf16_pair(out[:, :half], out[:, half:])

    pl.when(valid == tm)(functools.partial(sub_tile, 0, tm, False))
    for n_sub in range(1, tm // FFN_SUB_ROWS + 1):
        n_rows = n_sub * FFN_SUB_ROWS
        pl.when((valid > n_rows - FFN_SUB_ROWS) & (valid <= n_rows) & (valid < tm))(
            functools.partial(sub_tile, 0, n_rows, True))


def _ffn_call(tile_expert, tile_valid, tile_next, n_used, xs, w_gu, b_gu, w_down, b_down, tm):
    R, half = xs.shape
    E, D, two_f = w_gu.shape
    d_exp = w_down.shape[1]
    rows = lambda i, te, tv, nx, nu: (jnp.minimum(i, nu[0] - 1), 0)
    grid_spec = pltpu.PrefetchScalarGridSpec(
        num_scalar_prefetch=4,
        grid=(R // tm,),
        in_specs=[
            pl.BlockSpec((tm, half), rows),
            pl.BlockSpec(memory_space=pl.ANY),
            pl.BlockSpec((None, 1, two_f), lambda i, te, tv, nx, nu: (te[i], 0, 0)),
            pl.BlockSpec(memory_space=pl.ANY),
            pl.BlockSpec((None, 1, D), lambda i, te, tv, nx, nu: (te[i], 0, 0)),
        ],
        out_specs=pl.BlockSpec((tm, half), rows),
        scratch_shapes=[
            pltpu.VMEM((D, two_f), F32), pltpu.VMEM((d_exp, D), F32),
            pltpu.VMEM((D, two_f), BF16), pltpu.VMEM((d_exp, D), BF16),
            pltpu.SemaphoreType.DMA((2,)),
        ],
    )
    return pl.pallas_call(
        _ffn_kernel,
        grid_spec=grid_spec,
        out_shape=jax.ShapeDtypeStruct((R, half), U32),
        compiler_params=pltpu.CompilerParams(
            dimension_semantics=("arbitrary",), vmem_limit_bytes=VMEM_LIMIT_BYTES),
        name="moe_ffn",
    )(tile_expert, tile_valid, tile_next, n_used, xs, w_gu, b_gu.reshape(E, 1, two_f), w_down,
      b_down.reshape(E, 1, D))


def _final_kernel(x_ref, yp_ref, wts_ref, mod_ref, gfin_ref, *rest):
    o_ref = rest[-1]
    x = x_ref[...]
    half = x.shape[1] // 2
    w = jnp.concatenate([wts_ref[...], jnp.zeros((ROUTE_LANES - ROUTE_ROWS, x.shape[0]), F32)], axis=0).T
    lo = jnp.zeros((x.shape[0], half), F32)
    hi = jnp.zeros((x.shape[0], half), F32)
    for k in range(TOP_K):
        l, h = _unpack_bf16_pair(yp_ref[k])
        lo = lo + w[:, k:k + 1] * l
        hi = hi + w[:, k:k + 1] * h
    gate = mod_ref[5:6, :]
    x_lo = x[:, :half] + gate[:, :half] * lo
    x_hi = x[:, half:] + gate[:, half:] * hi
    ms = (jnp.sum(x_lo * x_lo, axis=-1, keepdims=True) + jnp.sum(x_hi * x_hi, axis=-1, keepdims=True)) / x.shape[1]
    inv = lax.rsqrt(ms + RMS_EPS)
    o_ref[:, :half] = x_lo * inv * gfin_ref[:, :half]
    o_ref[:, half:] = x_hi * inv * gfin_ref[:, half:]


def _final_call(x1, yp, wts, mod, g_final, tokens_per_batch, group, n_groups, prev_out):
    T, D = x1.shape
    T_all = T * n_groups
    tq = min(512, tokens_per_batch)
    per_b = tokens_per_batch // tq
    first = group * (T // tq)
    in_specs = [
        pl.BlockSpec((tq, D), lambda i: (i, 0), pipeline_mode=pl.Buffered(COMBINE_INPUT_BUFFERS)),
        pl.BlockSpec((TOP_K, tq, D // 2), lambda i: (0, i, 0), pipeline_mode=pl.Buffered(COMBINE_INPUT_BUFFERS)),
        pl.BlockSpec((ROUTE_ROWS, tq), lambda i: (0, i)),
        pl.BlockSpec((1, N_MOD, D), lambda i: ((first + i) // per_b, 0, 0)),
        pl.BlockSpec((1, D), lambda i: (0, 0)),
    ]
    out_spec = pl.BlockSpec((tq, D), lambda i: (first + i, 0))
    args = [x1, yp, wts, mod, g_final.reshape(1, D)]
    aliases = {}
    if prev_out is not None:
        args.append(prev_out)
        aliases = {len(args) - 1: 0}

    def tile(x_ref, yp_ref, wts_ref, mod_ref, gfin_ref, o_ref):
        _final_kernel(x_ref, yp_ref, wts_ref, mod_ref.at[0], gfin_ref, o_ref)

    def outer(x_hbm, yp_hbm, wts_hbm, mod_hbm, gfin_hbm, *rest):
        pltpu.emit_pipeline(tile, grid=(T // tq,), in_specs=in_specs, out_specs=[out_spec])(
            x_hbm, yp_hbm, wts_hbm, mod_hbm, gfin_hbm, rest[-1])

    return pl.pallas_call(
        outer,
        in_specs=[pl.BlockSpec(memory_space=pl.ANY)] * len(args),
        out_specs=pl.BlockSpec(memory_space=pl.ANY),
        out_shape=jax.ShapeDtypeStruct((T_all, D), F32),
        input_output_aliases=aliases,
        compiler_params=pltpu.CompilerParams(vmem_limit_bytes=VMEM_LIMIT_BYTES),
        name="moe_combine_final",
    )(*args)


def _group_layout(counts, n_tiles, tm):
    padded = ((counts + tm - 1) // tm) * tm
    ends = jnp.cumsum(padded)
    starts = ends - padded
    tile_row = jnp.arange(n_tiles, dtype=I32) * tm
    te = jnp.minimum(jnp.sum(tile_row[:, None] >= ends[None, :], axis=1), N_EXPERTS - 1).astype(I32)
    eids = jnp.arange(N_EXPERTS, dtype=I32)
    mine = te[:, None] == eids[None, :]
    lookup = lambda table: jnp.sum(jnp.where(mine, table[None, :], 0), axis=1)
    tv = jnp.clip(lookup(counts) - (tile_row - lookup(starts)), 0, tm).astype(I32)
    later = (eids[None, :] > eids[:, None]) & (counts[None, :] > 0)
    nxt = jnp.min(jnp.where(later, eids[None, :], N_EXPERTS), axis=1)
    nx = lookup(jnp.where(nxt < N_EXPERTS, nxt, -1)).astype(I32)
    n_used = jnp.sum(tv > 0).astype(I32).reshape(1)
    return starts, te, tv, nx, n_used


def kernel(x, c, positions, w_ada, b_ada, g_mix, w_in, w_dw, b_dw, g_conv_ln, b_conv_ln, g_ret_norm,
           w_out, g_ffn, w_router, b_router, w_gu, b_gu, w_down, b_down, g_final):
    B, S, D = x.shape
    T = B * S
    assert w_ada.shape[0] == 1, "single-layer block: the final norm directly follows layer 0"
    xt = x
    for l in range(1):
        mod = _mod_call(c, w_ada[l], b_ada[l]).reshape(B, N_MOD, D)
        n_groups = MOE_TOKEN_GROUPS if B % MOE_TOKEN_GROUPS == 0 else 1
        Bg = B // n_groups
        Tg = Bg * S
        tm = FFN_TILE_ROWS if Tg * TOP_K >= FFN_TILE_ROWS * N_EXPERTS * 4 else FFN_SUB_ROWS
        n_tiles = (Tg * TOP_K) // tm + N_EXPERTS
        x1s, scattered = [], []
        pos_km = jnp.zeros((TOP_K * Tg,), I32)
        for g in range(n_groups):
            x1, h2p, idx, wts, rank, counts = _mixer_call(
                xt, positions, mod, g_mix[l], w_in[l], w_dw[l], b_dw[l], g_conv_ln[l], b_conv_ln[l],
                g_ret_norm[l], w_out[l], g_ffn[l], w_router[l], b_router[l], g, n_groups, pos_km)
            x1 = x1.reshape(Tg, D)
            starts, te, tv, nx, n_used = _group_layout(counts[:, 0], n_tiles, tm)
            pos_km = _pos_call(starts, idx, rank)[:TOP_K].reshape(-1)
            x1s.append(x1)
            scattered.append((_sc_scatter_call(h2p, pos_km, n_tiles * tm), pos_km, wts, te, tv, nx, n_used))
        out = None
        for g, (xs, pos_km, wts, te, tv, nx, n_used) in enumerate(scattered):
            ys = _ffn_call(te, tv, nx, n_used, xs, w_gu[l], b_gu[l], w_down[l], b_down[l], tm)
            yp = _sc_gather_call(ys, pos_km)
            out = _final_call(x1s[g], yp.reshape(TOP_K, Tg, D // 2), wts, mod, g_final, S, g, n_groups, out)
        xt = out
    return xt.reshape(B, S, D)
```

```python
import functools

import numpy as np
import jax
import jax.numpy as jnp
from jax import lax
from jax.experimental import pallas as pl
from jax.experimental.pallas import tpu as pltpu
from jax.experimental.pallas import tpu_sc as plsc

F32 = jnp.float32
BF16 = jnp.bfloat16
U32 = jnp.uint32
I32 = jnp.int32

CONV_WIDTH = 31
CONV_HALO = 32
RET_HEADS = 4
RET_DIM = 128
RET_CHUNK = 128
ROPE_BASE = 10000.0
N_EXPERTS = 32
TOP_K = 4
SWIGLU_LIMIT = 7.0
SWIGLU_ALPHA = 1.702
RMS_EPS = 1e-6
LN_EPS = 1e-5
N_MOD = 6

VMEM_LIMIT_BYTES = 56 * 1024 * 1024
MOE_TOKEN_GROUPS = 2
COMBINE_INPUT_BUFFERS = 4
FFN_TILE_ROWS = 1024
FFN_SUB_ROWS = 256


def _split_bf16(a):
    hi = a.astype(BF16)
    lo = (a - hi.astype(F32)).astype(BF16)
    return hi, lo


def _dot3(a, b_hi, b_lo):
    a_hi, a_lo = _split_bf16(a)
    d = functools.partial(jnp.dot, preferred_element_type=F32)
    return d(a_hi, b_hi) + (d(a_hi, b_lo) + d(a_lo, b_hi))


def _pack_bf16_pair(lo, hi):
    lo_bits = lax.bitcast_convert_type(lo.astype(BF16).astype(F32), U32)
    hi_bits = lax.bitcast_convert_type(hi.astype(BF16).astype(F32), U32)
    return (lo_bits >> 16) | (hi_bits & jnp.uint32(0xFFFF0000))


def _unpack_bf16_pair(p):
    lo = lax.bitcast_convert_type(p << 16, F32)
    hi = lax.bitcast_convert_type(p & jnp.uint32(0xFFFF0000), F32)
    return lo, hi


def _mod_kernel(c_ref, whi_ref, wlo_ref, b_ref, o_ref):
    c = c_ref[...]
    c_act = c * jax.nn.sigmoid(c)
    o_ref[...] = _dot3(c_act, whi_ref[...], wlo_ref[...]) + b_ref[...]


def _mod_call(c, w_ada, b_ada):
    B, D = c.shape
    n = w_ada.shape[1]
    bn = 1024
    w_hi, w_lo = _split_bf16(w_ada)
    return pl.pallas_call(
        _mod_kernel,
        grid=(n // bn,),
        in_specs=[
            pl.BlockSpec((B, D), lambda j: (0, 0)),
            pl.BlockSpec((D, bn), lambda j: (0, j)),
            pl.BlockSpec((D, bn), lambda j: (0, j)),
            pl.BlockSpec((1, bn), lambda j: (0, j)),
        ],
        out_specs=pl.BlockSpec((B, bn), lambda j: (0, j)),
        out_shape=jax.ShapeDtypeStruct((B, n), F32),
        name="adaln_mod",
    )(c, w_hi, w_lo, b_ada.reshape(1, n))


def _retention_tables():
    h = np.arange(RET_HEADS, dtype=np.float32)
    log_gamma = np.log(1.0 - np.power(2.0, -5.0 - h)).astype(np.float32)
    idx = np.arange(RET_CHUNK, dtype=np.float32)
    diff = idx[:, None] - idx[None, :]
    causal = diff >= 0
    mask = np.where(causal[None], np.exp(log_gamma[:, None, None] * np.where(causal, diff, 0.0)[None]), 0.0)
    q_decay = np.exp(log_gamma[:, None] * (idx + 1.0))[..., None]
    k_decay = np.exp(log_gamma[:, None] * (RET_CHUNK - 1.0 - idx))[..., None]
    chunk_decay = np.exp(log_gamma * RET_CHUNK)
    return (mask.astype(np.float32), q_decay.astype(np.float32), k_decay.astype(np.float32),
            [float(v) for v in chunk_decay.astype(np.float32)])


def _mixer_kernel(chunk_decay, ts, x_ref, pos_ref, mod_ref, gmix_ref, win_ref, wdw_ref, bdw_ref,
                  gcl_ref, bcl_ref, gret_ref, wout_ref, rope_ref, dmask_ref, qdec_ref, kdec_ref,
                  gffn_ref, wr2_ref, wrhi_ref, br_ref, utri_ref, after_ref,
                  o_ref, h2p_ref, idx_ref, wts_ref, rank_ref, cnt_ref,
                  proj_ref, uext_ref, state_ref, cat_ref, conv_ref, carry_ref):
    del after_ref
    s = pl.program_id(1)
    conv_ch = wdw_ref.shape[1]
    ret_w = RET_HEADS * RET_DIM

    @pl.when(s == 0)
    def _():
        uext_ref[0:CONV_HALO, :] = jnp.zeros((CONV_HALO, conv_ch), F32)
        state_ref[...] = jnp.zeros_like(state_ref)

    x = x_ref[...]
    row_gain = gmix_ref[...] * (1.0 + mod_ref[1:2, :])
    h = (x * lax.rsqrt(jnp.mean(x * x, axis=-1, keepdims=True) + RMS_EPS) * row_gain + mod_ref[0:1, :]).astype(BF16)
    proj_ref[...] = jnp.dot(h, win_ref[...], preferred_element_type=F32)

    a = proj_ref[:, 0:conv_ch]
    b = proj_ref[:, conv_ch:2 * conv_ch]
    uext_ref[CONV_HALO:CONV_HALO + ts, :] = a * jax.nn.sigmoid(b)
    cb = 64
    lead = CONV_HALO - (CONV_WIDTH - 1)
    span = cb + CONV_HALO

    def conv_block(r0):
        for c0 in range(0, conv_ch, 128):
            xw = uext_ref[r0:r0 + span, c0:c0 + 128]
            acc = jnp.zeros((cb, 128), F32) + bdw_ref[:, c0:c0 + 128]
            for r in range(8):
                xr = xw if r == 0 else pltpu.roll(xw, span - r, 0)
                for q in range((lead + CONV_WIDTH - 1 - r) // 8 + 1):
                    j = 8 * q + r - lead
                    if 0 <= j < CONV_WIDTH:
                        acc = acc + wdw_ref[j:j + 1, c0:c0 + 128] * xr[8 * q:8 * q + cb]
            conv_ref[r0:r0 + cb, c0:c0 + 128] = acc
    rb = 2 * cb

    def conv_norm(r0):
        acc = conv_ref[r0:r0 + rb, 0:conv_ch]
        mu = jnp.mean(acc, axis=-1, keepdims=True)
        d = acc - mu
        var = jnp.mean(d * d, axis=-1, keepdims=True)
        ln = d * lax.rsqrt(var + LN_EPS) * gcl_ref[...] + bcl_ref[...]
        cat_ref[r0:r0 + rb, 0:conv_ch] = (ln * jax.nn.sigmoid(ln)).astype(BF16)

    hts = ts // 2
    low = lax.broadcasted_iota(I32, (hts, RET_DIM), 1) < RET_DIM // 2
    pos_t = jnp.broadcast_to(pos_ref[...].astype(F32), (RET_DIM, ts)).T
    posf = jnp.where(low, pos_t[0:hts], pos_t[hts:ts])
    ang = posf * rope_ref[...]
    cos_p = jnp.cos(ang)
    sin_p = jnp.sin(ang)
    cos_s = pltpu.roll(cos_p, RET_DIM // 2, 1)
    sin_s = pltpu.roll(sin_p, RET_DIM // 2, 1)
    cos2 = jnp.concatenate([jnp.where(low, cos_p, cos_s), jnp.where(low, cos_s, cos_p)], axis=0)
    sin2 = jnp.concatenate([jnp.where(low, -sin_p, sin_s), jnp.where(low, -sin_s, sin_p)], axis=0)
    q0 = 2 * conv_ch
    k0 = q0 + ret_w
    v0 = k0 + ret_w
    g0 = v0 + ret_w

    def retention_unit(hd, n):
        c0 = hd * RET_DIM
        r0 = n * RET_CHUNK
        rows = slice(r0, r0 + RET_CHUNK)
        cs = cos2[rows]
        sn = sin2[rows]
        q = proj_ref[rows, q0 + c0:q0 + c0 + RET_DIM]
        k = proj_ref[rows, k0 + c0:k0 + c0 + RET_DIM]
        v = proj_ref[rows, v0 + c0:v0 + c0 + RET_DIM].astype(BF16)
        g = proj_ref[rows, g0 + c0:g0 + c0 + RET_DIM]
        qr = q * cs + pltpu.roll(q, RET_DIM // 2, 1) * sn
        kr = k * cs + pltpu.roll(k, RET_DIM // 2, 1) * sn
        st = state_ref[hd]
        scores = lax.dot_general(qr.astype(BF16), kr.astype(BF16), (((1,), (1,)), ((), ())),
                                 preferred_element_type=F32) * dmask_ref[hd]
        inner = jnp.dot(scores.astype(BF16), v, preferred_element_type=F32)
        cross = jnp.dot((qr * qdec_ref[hd]).astype(BF16), st.astype(BF16), preferred_element_type=F32)
        kv = lax.dot_general((kr * kdec_ref[hd]).astype(BF16), v, (((0,), (0,)), ((), ())),
                             preferred_element_type=F32)
        state_ref[hd] = chunk_decay[hd] * st + kv
        r = inner + cross
        mu = jnp.mean(r, axis=-1, keepdims=True)
        d = r - mu
        var = jnp.mean(d * d, axis=-1, keepdims=True)
        rn = d * lax.rsqrt(var + LN_EPS) * gret_ref[:, c0:c0 + RET_DIM]
        cat_ref[rows, conv_ch + c0:conv_ch + c0 + RET_DIM] = (g * jax.nn.sigmoid(g) * rn).astype(BF16)

    units = [(hd, n) for n in range(ts // RET_CHUNK) for hd in range(RET_HEADS)]
    n_conv = ts // cb
    per = -(-len(units) // n_conv)
    for i in range(n_conv):
        conv_block(i * cb)
        if i % 2 == 1:
            conv_norm((i - 1) * cb)
        for hd, n in units[i * per:(i + 1) * per]:
            retention_unit(hd, n)
    uext_ref[0:CONV_HALO, :] = uext_ref[ts:ts + CONV_HALO, :]

    out = jnp.dot(cat_ref[...], wout_ref[...], preferred_element_type=F32)
    x1 = x + mod_ref[2:3, :] * out
    o_ref[...] = x1

    @pl.when((pl.program_id(0) == 0) & (s == 0))
    def _():
        carry_ref[...] = jnp.zeros_like(carry_ref)

    _route_tile(x1, mod_ref, gffn_ref, wr2_ref, wrhi_ref, br_ref, utri_ref,
                h2p_ref, idx_ref, wts_ref, rank_ref, cnt_ref, carry_ref)


def _mixer_call(x, positions, mod, g_mix, w_in, w_dw, b_dw, g_conv_ln, b_conv_ln, g_ret_norm, w_out,
                g_ffn, w_router, b_router, group, n_groups, after):
    S, D = x.shape[1:]
    B = x.shape[0] // n_groups
    b0 = group * B
    in_cols = w_in.shape[1]
    conv_ch = w_dw.shape[1]
    ts = min(512, S)
    mask, q_decay, k_decay, chunk_decay = _retention_tables()
    key_scale = np.float32(RET_DIM ** -0.5)
    half = RET_DIM // 2
    inv_freq = (ROPE_BASE ** (-np.arange(half, dtype=np.float32) / half)).astype(np.float32)
    rope = np.concatenate([inv_freq, inv_freq])[None, :]
    w_dw_p = jnp.zeros((CONV_HALO, conv_ch), F32).at[:CONV_WIDTH].set(w_dw)
    wr_hi, wr_lo = _split_bf16(w_router.T)
    utri = jnp.asarray(np.triu(np.ones((ts, ts), np.float32), 1), BF16)
    n_s = S // ts
    T = B * S
    full = lambda shape: pl.BlockSpec(shape, lambda b, s: (0,) * len(shape))
    return pl.pallas_call(
        functools.partial(_mixer_kernel, chunk_decay, ts),
        grid=(B, S // ts),
        in_specs=[
            pl.BlockSpec((None, ts, D), lambda b, s: (b0 + b, s, 0)),
            pl.BlockSpec((None, 1, ts), lambda b, s: (b0 + b, 0, s)),
            pl.BlockSpec((None, N_MOD, D), lambda b, s: (b0 + b, 0, 0)),
            full((1, D)),
            full((D, in_cols)),
            full((CONV_HALO, conv_ch)),
            full((1, conv_ch)),
            full((1, conv_ch)),
            full((1, conv_ch)),
            full((1, RET_HEADS * RET_DIM)),
            full((conv_ch + RET_HEADS * RET_DIM, D)),
            full((1, RET_DIM)),
            full((RET_HEADS, RET_CHUNK, RET_CHUNK)),
            full((RET_HEADS, RET_CHUNK, 1)),
            full((RET_HEADS, RET_CHUNK, 1)),
            full((1, D)),
            full((2 * N_EXPERTS, D)),
            full((N_EXPERTS, D)),
            full((N_EXPERTS, 1)),
            full((ts, ts)),
            pl.BlockSpec(memory_space=pl.ANY),
        ],
        out_specs=[
            pl.BlockSpec((None, ts, D), lambda b, s: (b, s, 0)),
            pl.BlockSpec((ts, D // 2), lambda b, s: (b * n_s + s, 0)),
            pl.BlockSpec((ROUTE_ROWS, ts), lambda b, s: (0, b * n_s + s)),
            pl.BlockSpec((ROUTE_ROWS, ts), lambda b, s: (0, b * n_s + s)),
            pl.BlockSpec((ROUTE_ROWS, ts), lambda b, s: (0, b * n_s + s)),
            full((N_EXPERTS, 1)),
        ],
        out_shape=[
            jax.ShapeDtypeStruct((B, S, D), F32),
            jax.ShapeDtypeStruct((T, D // 2), U32),
            jax.ShapeDtypeStruct((ROUTE_ROWS, T), I32),
            jax.ShapeDtypeStruct((ROUTE_ROWS, T), F32),
            jax.ShapeDtypeStruct((ROUTE_ROWS, T), I32),
            jax.ShapeDtypeStruct((N_EXPERTS, 1), I32),
        ],
        scratch_shapes=[
            pltpu.VMEM((ts, in_cols), F32),
            pltpu.VMEM((CONV_HALO + ts, conv_ch), F32),
            pltpu.VMEM((RET_HEADS, RET_DIM, RET_DIM), F32),
            pltpu.VMEM((ts, conv_ch + RET_HEADS * RET_DIM), BF16),
            pltpu.VMEM((ts, conv_ch), F32),
            pltpu.VMEM((N_EXPERTS, 1), F32),
        ],
        compiler_params=pltpu.CompilerParams(
            dimension_semantics=("arbitrary", "arbitrary"), vmem_limit_bytes=VMEM_LIMIT_BYTES),
        name="hybrid_mixer",
    )(x, positions.reshape(-1, 1, S), mod, g_mix.reshape(1, D), w_in.astype(BF16), w_dw_p,
      b_dw.reshape(1, -1), g_conv_ln.reshape(1, -1), b_conv_ln.reshape(1, -1), g_ret_norm.reshape(1, -1),
      w_out.astype(BF16), jnp.asarray(rope), jnp.asarray(mask * key_scale), jnp.asarray(q_decay),
      jnp.asarray(k_decay * key_scale), g_ffn.reshape(1, D), jnp.concatenate([wr_hi, wr_lo], axis=0), wr_hi,
      b_router.reshape(N_EXPERTS, 1), utri, after)


def _route_tile(x, mod_ref, gffn_ref, wr2_ref, wrhi_ref, br_ref, utri_ref,
                h2p_ref, idx_ref, wts_ref, rank_ref, cnt_ref, carry_ref):
    tr = x.shape[0]
    half = x.shape[1] // 2
    y = x * lax.rsqrt(jnp.mean(x * x, axis=-1, keepdims=True) + RMS_EPS) * gffn_ref[...]
    h2 = y * (1.0 + mod_ref[4:5, :]) + mod_ref[3:4, :]
    h2p_ref[...] = _pack_bf16_pair(h2[:, :half], h2[:, half:])

    h_hi, h_lo = _split_bf16(h2)
    nt = (((1,), (1,)), ((), ()))
    r = lax.dot_general(wr2_ref[...], h_hi, nt, preferred_element_type=F32)
    r2 = lax.dot_general(wrhi_ref[...], h_lo, nt, preferred_element_type=F32)
    l = r[:N_EXPERTS] + (r[N_EXPERTS:] + r2) + br_ref[...]
    eid = lax.broadcasted_iota(I32, l.shape, 0)
    vals, sels, idxs = [], [], []
    for _ in range(TOP_K):
        m = jnp.max(l, axis=0, keepdims=True)
        ik = jnp.min(jnp.where(l == m, eid, N_EXPERTS), axis=0, keepdims=True)
        sel = eid == ik
        vals.append(m)
        sels.append(sel)
        idxs.append(ik)
        l = jnp.where(sel, -jnp.inf, l)
    exps = [jnp.exp(v - vals[0]) for v in vals]
    denom = exps[0] + exps[1] + exps[2] + exps[3]
    member = jnp.zeros(l.shape, F32)
    for sel in sels:
        member = member + sel.astype(F32)
    before = jnp.dot(member.astype(BF16), utri_ref[...], preferred_element_type=F32) + carry_ref[...]
    ranks = [jnp.sum(jnp.where(sel, before, 0.0), axis=0, keepdims=True) for sel in sels]
    carry_ref[...] = carry_ref[...] + jnp.sum(member, axis=1, keepdims=True)
    cnt_ref[...] = carry_ref[...].astype(I32)

    def rows(pieces, n_rows, dtype):
        rid = lax.broadcasted_iota(I32, (n_rows, tr), 0)
        out = jnp.zeros((n_rows, tr), dtype)
        for k, p in enumerate(pieces):
            out = jnp.where(rid == k, p.astype(dtype), out)
        return out

    idx_ref[...] = rows(idxs, idx_ref.shape[0], I32)
    rank_ref[...] = rows(ranks, rank_ref.shape[0], I32)
    wts_ref[...] = rows([e / denom for e in exps], wts_ref.shape[0], F32)


ROUTE_ROWS = 8
ROUTE_LANES = 128


def _pos_kernel(starts_ref, idx_ref, rank_ref, pos_ref):
    idx = idx_ref[...]
    pos = rank_ref[...]
    for e in range(N_EXPERTS):
        pos = pos + jnp.where(idx == e, starts_ref[e], 0)
    pos_ref[...] = pos


def _pos_call(starts, idx, rank):
    rows, T = idx.shape
    tp = min(4096, T)
    spec = pl.BlockSpec((rows, tp), lambda i, st: (0, i))
    return pl.pallas_call(
        _pos_kernel,
        grid_spec=pltpu.PrefetchScalarGridSpec(
            num_scalar_prefetch=1, grid=(T // tp,), in_specs=[spec, spec], out_specs=spec),
        out_shape=jax.ShapeDtypeStruct((rows, T), I32),
        name="moe_sorted_pos",
    )(starts.astype(I32), idx, rank)


SC_ROWS = 128


V7X_SC_CORES = 2
V7X_SC_SUBCORES = 16


def _sc_workers():
    return V7X_SC_CORES, V7X_SC_SUBCORES


def _sc_mesh():
    return plsc.VectorSubcoreMesh(core_axis_name="c", subcore_axis_name="s",
                                  num_cores=V7X_SC_CORES, num_subcores=V7X_SC_SUBCORES)


def _sc_scatter_call(h2p, pos_km, n_rows):
    T, W = h2p.shape
    nc, ns = _sc_workers()
    n = SC_ROWS
    per_w = T // (nc * ns * n)

    def body(h2p_hbm, pos_hbm, xs_hbm, i0, i1, i2, i3, rows_v, sem):
        wid = lax.axis_index("s") * nc + lax.axis_index("c")
        idx_refs = (i0, i1, i2, i3)

        @pl.loop(0, per_w)
        def _(j):
            t0 = (wid * per_w + j) * n
            pltpu.sync_copy(h2p_hbm.at[pl.ds(t0, n)], rows_v)
            for k in range(TOP_K):
                pltpu.sync_copy(pos_hbm.at[pl.ds(k * T + t0, n)], idx_refs[k])
            copies = [pltpu.async_copy(rows_v, xs_hbm.at[idx_refs[k]], sem) for k in range(TOP_K)]
            for cp in copies:
                cp.wait()

    return pl.kernel(
        body,
        out_type=jax.ShapeDtypeStruct((n_rows, W), h2p.dtype),
        mesh=_sc_mesh(),
        scratch_types=[pltpu.VMEM((n,), I32)] * TOP_K + [pltpu.VMEM((n, W), h2p.dtype), pltpu.SemaphoreType.DMA],
        name="moe_scatter_rows",
    )(h2p, pos_km)


def _sc_gather_call(ys, pos_km):
    P = pos_km.shape[0]
    W = ys.shape[1]
    nc, ns = _sc_workers()
    n = SC_ROWS
    per_w = P // (nc * ns * n)

    def body(ys_hbm, pos_hbm, yp_hbm, idx_v, rows_v, sem):
        wid = lax.axis_index("s") * nc + lax.axis_index("c")

        @pl.loop(0, per_w)
        def _(j):
            p0 = (wid * per_w + j) * n
            pltpu.sync_copy(pos_hbm.at[pl.ds(p0, n)], idx_v)
            pltpu.async_copy(ys_hbm.at[idx_v], rows_v, sem).wait()
            pltpu.sync_copy(rows_v, yp_hbm.at[pl.ds(p0, n)])

    return pl.kernel(
        body,
        out_type=jax.ShapeDtypeStruct((P, W), ys.dtype),
        mesh=_sc_mesh(),
        scratch_types=[pltpu.VMEM((n,), I32), pltpu.VMEM((n, W), ys.dtype), pltpu.SemaphoreType.DMA],
        name="moe_gather_rows",
    )(ys, pos_km)


def _ffn_kernel(te_ref, tv_ref, nx_ref, nu_ref, xs_ref, wgu_hbm, bgu_ref, wd_hbm, bd_ref, y_ref,
                wgu_f32, wd_f32, wgu_bf, wd_bf, sems):
    del nu_ref
    i = pl.program_id(0)
    valid = tv_ref[i]
    tm, half = xs_ref.shape
    d_exp = wd_bf.shape[0]

    def weight_copies(e):
        return (pltpu.make_async_copy(wgu_hbm.at[e], wgu_f32, sems.at[0]),
                pltpu.make_async_copy(wd_hbm.at[e], wd_f32, sems.at[1]))

    @pl.when(i == 0)
    def _():
        for cp in weight_copies(te_ref[0]):
            cp.start(priority=1)

    @pl.when(((i == 0) | (te_ref[i] != te_ref[jnp.maximum(i - 1, 0)])) & (valid > 0))
    def _():
        for cp in weight_copies(te_ref[i]):
            cp.wait()
        wgu_bf[...] = wgu_f32[...].astype(BF16)
        wd_bf[...] = wd_f32[...].astype(BF16)

        @pl.when(nx_ref[i] >= 0)
        def _():
            for cp in weight_copies(nx_ref[i]):
                cp.start(priority=1)

    def sub_tile(r0, n_rows, masked):
        rows = slice(r0, r0 + n_rows)
        lo, hi = _unpack_bf16_pair(xs_ref[rows, :])
        if masked:
            keep = lax.broadcasted_iota(I32, (n_rows, half), 0) < valid - r0
            lo = jnp.where(keep, lo, 0.0)
            hi = jnp.where(keep, hi, 0.0)
        xt = jnp.concatenate([lo.astype(BF16), hi.astype(BF16)], axis=1)
        d = functools.partial(jnp.dot, preferred_element_type=F32)
        nb = 256

        def up(c):
            gate = d(xt, wgu_bf[:, c:c + nb]) + bgu_ref[:, c:c + nb]
            lin = d(xt, wgu_bf[:, d_exp + c:d_exp + c + nb]) + bgu_ref[:, d_exp + c:d_exp + c + nb]
            return gate, lin

        def activate(gate, lin):
            gate = jnp.minimum(gate, SWIGLU_LIMIT)
            lin = jnp.clip(lin, -SWIGLU_LIMIT, SWIGLU_LIMIT)
            return (gate * jax.nn.sigmoid(SWIGLU_ALPHA * gate) * (lin + 1.0)).astype(BF16)

        chunks = list(range(0, d_exp, nb))
        acts = []
        pending = up(chunks[0])
        for n in range(len(chunks)):
            nxt = up(chunks[n + 1]) if n + 1 < len(chunks) else None
            acts.append(activate(*pending))
            pending = nxt
        out = d(jnp.concatenate(acts, axis=1), wd_bf[...]) + bd_ref[...]
        y_ref[rows, :] = _pack_bf16_pair(out[:, :half], out[:, half:])

    pl.when(valid == tm)(functools.partial(sub_tile, 0, tm, False))
    for n_sub in range(1, tm // FFN_SUB_ROWS + 1):
        n_rows = n_sub * FFN_SUB_ROWS
        pl.when((valid > n_rows - FFN_SUB_ROWS) & (valid <= n_rows) & (valid < tm))(
            functools.partial(sub_tile, 0, n_rows, True))


def _ffn_call(tile_expert, tile_valid, tile_next, n_used, xs, w_gu, b_gu, w_down, b_down, tm):
    R, half = xs.shape
    E, D, two_f = w_gu.shape
    d_exp = w_down.shape[1]
    rows = lambda i, te, tv, nx, nu: (jnp.minimum(i, nu[0] - 1), 0)
    grid_spec = pltpu.PrefetchScalarGridSpec(
        num_scalar_prefetch=4,
        grid=(R // tm,),
        in_specs=[
            pl.BlockSpec((tm, half), rows),
            pl.BlockSpec(memory_space=pl.ANY),
            pl.BlockSpec((None, 1, two_f), lambda i, te, tv, nx, nu: (te[i], 0, 0)),
            pl.BlockSpec(memory_space=pl.ANY),
            pl.BlockSpec((None, 1, D), lambda i, te, tv, nx, nu: (te[i], 0, 0)),
        ],
        out_specs=pl.BlockSpec((tm, half), rows),
        scratch_shapes=[
            pltpu.VMEM((D, two_f), F32), pltpu.VMEM((d_exp, D), F32),
            pltpu.VMEM((D, two_f), BF16), pltpu.VMEM((d_exp, D), BF16),
            pltpu.SemaphoreType.DMA((2,)),
        ],
    )
    return pl.pallas_call(
        _ffn_kernel,
        grid_spec=grid_spec,
        out_shape=jax.ShapeDtypeStruct((R, half), U32),
        compiler_params=pltpu.CompilerParams(
            dimension_semantics=("arbitrary",), vmem_limit_bytes=VMEM_LIMIT_BYTES),
        name="moe_ffn",
    )(tile_expert, tile_valid, tile_next, n_used, xs, w_gu, b_gu.reshape(E, 1, two_f), w_down,
      b_down.reshape(E, 1, D))


def _final_kernel(x_ref, yp_ref, wts_ref, mod_ref, gfin_ref, *rest):
    o_ref = rest[-1]
    x = x_ref[...]
    half = x.shape[1] // 2
    w = jnp.concatenate([wts_ref[...], jnp.zeros((ROUTE_LANES - ROUTE_ROWS, x.shape[0]), F32)], axis=0).T
    lo = jnp.zeros((x.shape[0], half), F32)
    hi = jnp.zeros((x.shape[0], half), F32)
    for k in range(TOP_K):
        l, h = _unpack_bf16_pair(yp_ref[k])
        lo = lo + w[:, k:k + 1] * l
        hi = hi + w[:, k:k + 1] * h
    gate = mod_ref[5:6, :]
    x_lo = x[:, :half] + gate[:, :half] * lo
    x_hi = x[:, half:] + gate[:, half:] * hi
    ms = (jnp.sum(x_lo * x_lo, axis=-1, keepdims=True) + jnp.sum(x_hi * x_hi, axis=-1, keepdims=True)) / x.shape[1]
    inv = lax.rsqrt(ms + RMS_EPS)
    o_ref[:, :half] = x_lo * inv * gfin_ref[:, :half]
    o_ref[:, half:] = x_hi * inv * gfin_ref[:, half:]


def _final_call(x1, yp, wts, mod, g_final, tokens_per_batch, group, n_groups, prev_out):
    T, D = x1.shape
    T_all = T * n_groups
    tq = min(512, tokens_per_batch)
    per_b = tokens_per_batch // tq
    first = group * (T // tq)
    in_specs = [
        pl.BlockSpec((tq, D), lambda i: (i, 0), pipeline_mode=pl.Buffered(COMBINE_INPUT_BUFFERS)),
        pl.BlockSpec((TOP_K, tq, D // 2), lambda i: (0, i, 0), pipeline_mode=pl.Buffered(COMBINE_INPUT_BUFFERS)),
        pl.BlockSpec((ROUTE_ROWS, tq), lambda i: (0, i)),
        pl.BlockSpec((1, N_MOD, D), lambda i: ((first + i) // per_b, 0, 0)),
        pl.BlockSpec((1, D), lambda i: (0, 0)),
    ]
    out_spec = pl.BlockSpec((tq, D), lambda i: (first + i, 0))
    args = [x1, yp, wts, mod, g_final.reshape(1, D)]
    aliases = {}
    if prev_out is not None:
        args.append(prev_out)
        aliases = {len(args) - 1: 0}

    def tile(x_ref, yp_ref, wts_ref, mod_ref, gfin_ref, o_ref):
        _final_kernel(x_ref, yp_ref, wts_ref, mod_ref.at[0], gfin_ref, o_ref)

    def outer(x_hbm, yp_hbm, wts_hbm, mod_hbm, gfin_hbm, *rest):
        pltpu.emit_pipeline(tile, grid=(T // tq,), in_specs=in_specs, out_specs=[out_spec])(
            x_hbm, yp_hbm, wts_hbm, mod_hbm, gfin_hbm, rest[-1])

    return pl.pallas_call(
        outer,
        in_specs=[pl.BlockSpec(memory_space=pl.ANY)] * len(args),
        out_specs=pl.BlockSpec(memory_space=pl.ANY),
        out_shape=jax.ShapeDtypeStruct((T_all, D), F32),
        input_output_aliases=aliases,
        compiler_params=pltpu.CompilerParams(vmem_limit_bytes=VMEM_LIMIT_BYTES),
        name="moe_combine_final",
    )(*args)


def _group_layout(counts, n_tiles, tm):
    padded = ((counts + tm - 1) // tm) * tm
    ends = jnp.cumsum(padded)
    starts = ends - padded
    tile_row = jnp.arange(n_tiles, dtype=I32) * tm
    te = jnp.minimum(jnp.sum(tile_row[:, None] >= ends[None, :], axis=1), N_EXPERTS - 1).astype(I32)
    eids = jnp.arange(N_EXPERTS, dtype=I32)
    mine = te[:, None] == eids[None, :]
    lookup = lambda table: jnp.sum(jnp.where(mine, table[None, :], 0), axis=1)
    tv = jnp.clip(lookup(counts) - (tile_row - lookup(starts)), 0, tm).astype(I32)
    later = (eids[None, :] > eids[:, None]) & (counts[None, :] > 0)
    nxt = jnp.min(jnp.where(later, eids[None, :], N_EXPERTS), axis=1)
    nx = lookup(jnp.where(nxt < N_EXPERTS, nxt, -1)).astype(I32)
    n_used = jnp.sum(tv > 0).astype(I32).reshape(1)
    return starts, te, tv, nx, n_used


def kernel(x, c, positions, w_ada, b_ada, g_mix, w_in, w_dw, b_dw, g_conv_ln, b_conv_ln, g_ret_norm,
           w_out, g_ffn, w_router, b_router, w_gu, b_gu, w_down, b_down, g_final):
    B, S, D = x.shape
    T = B * S
    assert w_ada.shape[0] == 1, "single-layer block: the final norm directly follows layer 0"
    xt = x
    for l in range(1):
        mod = _mod_call(c, w_ada[l], b_ada[l]).reshape(B, N_MOD, D)
        n_groups = MOE_TOKEN_GROUPS if B % MOE_TOKEN_GROUPS == 0 else 1
        Bg = B // n_groups
        Tg = Bg * S
        tm = FFN_TILE_ROWS if Tg * TOP_K >= FFN_TILE_ROWS * N_EXPERTS * 4 else FFN_SUB_ROWS
        n_tiles = (Tg * TOP_K) // tm + N_EXPERTS
        x1s, scattered = [], []
        pos_km = jnp.zeros((TOP_K * Tg,), I32)
        for g in range(n_groups):
            x1, h2p, idx, wts, rank, counts = _mixer_call(
                xt, positions, mod, g_mix[l], w_in[l], w_dw[l], b_dw[l], g_conv_ln[l], b_conv_ln[l],
                g_ret_norm[l], w_out[l], g_ffn[l], w_router[l], b_router[l], g, n_groups, pos_km)
            x1 = x1.reshape(Tg, D)
            starts, te, tv, nx, n_used = _group_layout(counts[:, 0], n_tiles, tm)
            pos_km = _pos_call(starts, idx, rank)[:TOP_K].reshape(-1)
            x1s.append(x1)
            scattered.append((_sc_scatter_call(h2p, pos_km, n_tiles * tm), pos_km, wts, te, tv, nx, n_used))
        out = None
        for g, (xs, pos_km, wts, te, tv, nx, n_used) in enumerate(scattered):
            ys = _ffn_call(te, tv, nx, n_used, xs, w_gu[l], b_gu[l], w_down[l], b_down[l], tm)
            yp = _sc_gather_call(ys, pos_km)
            out = _final_call(x1s[g], yp.reshape(TOP_K, Tg, D // 2), wts, mod, g_final, S, g, n_groups, out)
        xt = out
    return xt.reshape(B, S, D)
```
